```python
import math
import jax
import jax.numpy as jnp
from jax import lax
import numpy as np

D_MODEL = 1024
BATCH = 8
SEQ = 2048
DEPTH = 1
DEC_BATCH = 128
DEC_SEQ = 4
PAST_LEN = 16384
PAGE_SIZE = 128

HEAD_DIM = 64
H_GDN = 8
H_RWKV = 8
D_GDN = H_GDN * HEAD_DIM
D_RWKV = H_RWKV * HEAD_DIM
D_MIX = D_GDN + D_RWKV
CONV_W = 4
CHUNK = 64
LORA_W = 64
LORA_A = 64
LORA_G = 128
N_GROUPS = 4
EXP_PER_GROUP = 8
N_EXPERTS = N_GROUPS * EXP_PER_GROUP
TOP_K = 2
D_EXPERT = 256
EPS = 1e-6
GN_EPS = HEAD_DIM * 1e-5

GDN_QKV = 3 * D_GDN
GDN_COLS = GDN_QKV + D_GDN + 2 * H_GDN
RWKV_COLS = 3 * D_RWKV + LORA_W + LORA_A + LORA_G
D_IN = GDN_COLS + RWKV_COLS
GDN_SPLITS = (GDN_QKV, GDN_QKV + D_GDN, GDN_QKV + D_GDN + H_GDN)
RWKV_SPLITS = (D_RWKV, 2 * D_RWKV, 3 * D_RWKV, 3 * D_RWKV + LORA_W, 3 * D_RWKV + LORA_W + LORA_A)

kernel_name = "hybrid_gdn_rwkv7_hmoe_adaln_step"


def rms_norm(x, w):
    xf = x.astype(jnp.float32)
    y = xf * lax.rsqrt(jnp.mean(xf * xf, axis=-1, keepdims=True) + EPS)
    return (y * w.astype(jnp.float32)).astype(x.dtype)


def l2_normalize(x):
    xf = x.astype(jnp.float32)
    return xf * lax.rsqrt(jnp.sum(xf * xf, axis=-1, keepdims=True) + EPS)


def causal_short_conv(u, buf, w):
    t = u.shape[1]
    full = jnp.concatenate([buf.astype(u.dtype), u], axis=1)
    y = full[:, CONV_W - 1:CONV_W - 1 + t] * w[CONV_W - 1]
    for i in range(CONV_W - 1):
        y = y + full[:, i:i + t] * w[i]
    return jax.nn.silu(y), full[:, t:]


def gated_delta_rule_chunked(q, k, v, beta, g, s0):
    b, t, h, dk = q.shape
    dv = v.shape[-1]
    c = min(CHUNK, t)
    n = -(-t // c)
    pad = n * c - t

    def to_blocks(a):
        a = jnp.pad(a, [(0, 0), (0, pad)] + [(0, 0)] * (a.ndim - 2))
        a = a.reshape((b, n, c) + a.shape[2:])
        return jnp.moveaxis(a, (1, 3), (0, 2))

    qb, kb, vb, bb, gb = map(to_blocks, (q, k, v, beta, g))
    G = jnp.cumsum(gb, axis=-1)
    idx = jnp.arange(c)
    causal = idx[:, None] >= idx[None, :]
    strict = idx[:, None] > idx[None, :]
    decay = jnp.exp(jnp.where(causal, G[..., :, None] - G[..., None, :], -jnp.inf))
    kk = jnp.einsum('nbhik,nbhjk->nbhij', kb, kb)
    lmat = jnp.where(strict, bb[..., :, None] * decay * kk, 0.0)
    a_mat = lmat + jnp.eye(c, dtype=lmat.dtype)
    rhs = jnp.concatenate([bb[..., None] * vb, (bb * jnp.exp(G))[..., None] * kb], axis=-1)
    sol = lax.linalg.triangular_solve(a_mat, rhs, left_side=True, lower=True, unit_diagonal=True)
    u_v, w_k = sol[..., :dv], sol[..., dv:]
    qk = jnp.einsum('nbhik,nbhjk->nbhij', qb, kb) * decay
    q_dec = jnp.exp(G)[..., None] * qb
    k_dec = jnp.exp(G[..., -1:] - G)[..., None] * kb
    g_last = jnp.exp(G[..., -1])

    def step(s, inp):
        u_v_c, w_k_c, qk_c, q_dec_c, k_dec_c, g_last_c = inp
        u = u_v_c - jnp.einsum('bhck,bhkv->bhcv', w_k_c, s)
        o = jnp.einsum('bhck,bhkv->bhcv', q_dec_c, s) + jnp.einsum('bhij,bhjv->bhiv', qk_c, u)
        s = g_last_c[..., None, None] * s + jnp.einsum('bhck,bhcv->bhkv', k_dec_c, u)
        return s, o

    s, o = lax.scan(step, s0, (u_v, w_k, qk, q_dec, k_dec, g_last))
    o = jnp.moveaxis(o, (0, 2), (1, 3)).reshape(b, n * c, h, dv)[:, :t]
    return o, s


def gdn_mixer(cols, conv_buf, s0, conv_w, a_log, dt_bias, norm_w):
    b, t, _ = cols.shape
    qkv, z, beta_in, a_in = jnp.split(cols, GDN_SPLITS, axis=-1)
    qkv, conv_new = causal_short_conv(qkv, conv_buf, conv_w)
    qkv = qkv.astype(jnp.float32)
    q = qkv[..., :D_GDN].reshape(b, t, H_GDN, HEAD_DIM)
    k = qkv[..., D_GDN:2 * D_GDN].reshape(b, t, H_GDN, HEAD_DIM)
    v = qkv[..., 2 * D_GDN:].reshape(b, t, H_GDN, HEAD_DIM)
    q = l2_normalize(q) * HEAD_DIM ** -0.5
    k = l2_normalize(k)
    beta = jax.nn.sigmoid(beta_in.astype(jnp.float32))
    g = -jnp.exp(a_log.astype(jnp.float32)) * jax.nn.softplus(a_in.astype(jnp.float32) + dt_bias.astype(jnp.float32))
    o, s = gated_delta_rule_chunked(q, k, v, beta, g, s0.astype(jnp.float32))
    o = rms_norm(o, norm_w) * jax.nn.silu(z.astype(jnp.float32).reshape(b, t, H_GDN, HEAD_DIM))
    return o.reshape(b, t, D_GDN), conv_new, s


def rwkv7_time_mix(cols, shift_buf, s0, mu, w0, w_up, a0, a_up, g_up, k_k, k_a, r_k, ln_w, ln_b):
    b, t, _ = cols.shape
    prev = jnp.concatenate([shift_buf[:, None].astype(cols.dtype), cols[:, :-1]], axis=1)
    xs = (cols + (prev - cols) * mu).astype(jnp.float32)
    r, k, v, dw, da, dg = jnp.split(xs, RWKV_SPLITS, axis=-1)
    w_log = -jax.nn.softplus(-(w0 + jnp.tanh(dw) @ w_up)) - 0.5
    decay = jnp.exp(-jnp.exp(w_log))
    a = jax.nn.sigmoid(a0 + da @ a_up)
    gate = jax.nn.sigmoid(dg) @ g_up

    def heads(z):
        return z.reshape(b, t, H_RWKV, HEAD_DIM)

    r, k, v, decay, a = map(heads, (r, k, v, decay, a))
    kk = l2_normalize(k * k_k.reshape(H_RWKV, HEAD_DIM))
    k = k * (1.0 + (a - 1.0) * k_a.reshape(H_RWKV, HEAD_DIM))

    def step(s, inp):
        r_t, k_t, v_t, w_t, kk_t, a_t = inp
        sa = jnp.einsum('bhvk,bhk->bhv', s, -kk_t)
        s = (s * w_t[:, :, None, :] + sa[..., None] * (kk_t * a_t)[:, :, None, :]
             + v_t[..., None] * k_t[:, :, None, :])
        return s, jnp.einsum('bhvk,bhk->bhv', s, r_t)

    seq_major = tuple(jnp.moveaxis(z, 1, 0) for z in (r, k, v, decay, kk, a))
    s, y = lax.scan(step, s0.astype(jnp.float32), seq_major)
    y = jnp.moveaxis(y, 0, 1)
    mean = jnp.mean(y, axis=-1, keepdims=True)
    var = jnp.mean(jnp.square(y - mean), axis=-1, keepdims=True)
    y = (y - mean) * lax.rsqrt(var + GN_EPS) * ln_w.reshape(H_RWKV, HEAD_DIM) + ln_b.reshape(H_RWKV, HEAD_DIM)
    bonus = jnp.sum(r * k * r_k, axis=-1, keepdims=True) * v
    out = (y + bonus).reshape(b, t, D_RWKV) * gate
    return out, cols[:, -1], s


def hier_moe(h, rg_w, rg_b, re_w, re_b, w_gu, w_down):
    b, t, d = h.shape
    tok_x = h.reshape(-1, d)
    n = tok_x.shape[0]
    g_logits = (tok_x @ rg_w + rg_b).astype(jnp.float32)
    grp = jnp.argmax(g_logits, axis=-1)
    g_prob = jnp.take_along_axis(jax.nn.softmax(g_logits, axis=-1), grp[:, None], axis=-1)
    e_logits = (tok_x @ re_w + re_b).astype(jnp.float32).reshape(n, N_GROUPS, EXP_PER_GROUP)
    e_logits = jnp.take_along_axis(e_logits, grp[:, None, None], axis=1)[:, 0]
    top_p, top_i = lax.top_k(jax.nn.softmax(e_logits, axis=-1), TOP_K)
    top_p = top_p / jnp.sum(top_p, axis=-1, keepdims=True)
    weights = (g_prob * top_p).reshape(-1)
    expert = (grp[:, None] * EXP_PER_GROUP + top_i).reshape(-1)
    order = jnp.argsort(expert)
    tok = order // TOP_K
    xs = tok_x[tok]
    sizes = jnp.bincount(expert, length=N_EXPERTS).astype(jnp.int32)
    gu = lax.ragged_dot(xs, w_gu, sizes)
    gt, up = jnp.split(gu, 2, axis=-1)
    ys = lax.ragged_dot(jax.nn.silu(gt) * up, w_down, sizes)
    ys = ys * weights[order][:, None].astype(ys.dtype)
    out = jax.ops.segment_sum(ys, tok, num_segments=n)
    return out.reshape(b, t, d).astype(h.dtype)


def decoder_layer(x, c, conv_buf, s_gdn, shift_buf, s_rwkv, p):
    mod = jax.nn.silu(c) @ p['ada_w'] + p['ada_b']
    sh_m, sc_m, gt_m, sh_f, sc_f, gt_f = jnp.split(mod[:, None, :], 6, axis=-1)
    h = rms_norm(x, p['norm_mix_w']) * (1.0 + sc_m) + sh_m
    cols = h @ p['w_in']
    o_a, conv_new, s_gdn_new = gdn_mixer(cols[..., :GDN_COLS], conv_buf, s_gdn, p['gdn_conv_w'],
                                         p['gdn_a_log'], p['gdn_dt_bias'], p['gdn_norm_w'])
    o_b, shift_new, s_rwkv_new = rwkv7_time_mix(cols[..., GDN_COLS:], shift_buf, s_rwkv, p['rwkv_mu'],
                                                p['rwkv_w0'], p['rwkv_w_up'], p['rwkv_a0'], p['rwkv_a_up'],
                                                p['rwkv_g_up'], p['rwkv_k_k'], p['rwkv_k_a'], p['rwkv_r_k'],
                                                p['rwkv_ln_w'], p['rwkv_ln_b'])
    mixed = jnp.concatenate([o_a, o_b], axis=-1).astype(x.dtype) @ p['w_out']
    x = x + gt_m * mixed
    h = rms_norm(x, p['norm_ffn_w']) * (1.0 + sc_f) + sh_f
    x = x + gt_f * hier_moe(h, p['router_group_w'], p['router_group_b'], p['router_expert_w'],
                            p['router_expert_b'], p['expert_w_gate_up'], p['expert_w_down'])
    return x, conv_new, s_gdn_new, shift_new, s_rwkv_new


def setup_inputs(seed: int = 0) -> dict:
    key = jax.random.key(seed)
    ks = jax.random.split(key, 40)
    f32 = jnp.float32

    def nrm(i, shape, scale):
        return scale * jax.random.normal(ks[i], shape, f32)

    def uni(i, shape, lo, hi):
        return jax.random.uniform(ks[i], shape, f32, lo, hi)

    dt = jnp.exp(uni(17, (DEPTH, H_GDN), math.log(1e-3), math.log(1e-1)))
    return {
        'x_prompt': nrm(0, (BATCH, SEQ, D_MODEL), 1.0),
        'x_sample': nrm(1, (DEC_BATCH, DEC_SEQ, D_MODEL), 1.0),
        'state_gdn_conv': nrm(2, (DEPTH, DEC_BATCH, CONV_W - 1, GDN_QKV), 1.0),
        'state_gdn': nrm(3, (DEPTH, DEC_BATCH, H_GDN, HEAD_DIM, HEAD_DIM), 0.1),
        'state_rwkv_shift': nrm(4, (DEPTH, DEC_BATCH, RWKV_COLS), 1.0),
        'state_rwkv': nrm(5, (DEPTH, DEC_BATCH, H_RWKV, HEAD_DIM, HEAD_DIM), 0.1),
        'c_prompt': nrm(6, (BATCH, D_MODEL), 1.0),
        'c_sample': nrm(7, (DEC_BATCH, D_MODEL), 1.0),
        'ada_w': nrm(8, (DEPTH, D_MODEL, 6 * D_MODEL), 0.5 * D_MODEL ** -0.5),
        'ada_b': nrm(9, (DEPTH, 6 * D_MODEL), 0.02),
        'norm_mix_w': 1.0 + nrm(10, (DEPTH, D_MODEL), 0.02),
        'w_in': nrm(11, (DEPTH, D_MODEL, D_IN), D_MODEL ** -0.5),
        'gdn_conv_w': nrm(12, (DEPTH, CONV_W, GDN_QKV), CONV_W ** -0.5),
        'gdn_a_log': jnp.log(uni(13, (DEPTH, H_GDN), 1.0, 16.0)),
        'gdn_dt_bias': dt + jnp.log(-jnp.expm1(-dt)),
        'gdn_norm_w': 1.0 + nrm(14, (DEPTH, HEAD_DIM), 0.02),
        'rwkv_mu': uni(15, (DEPTH, RWKV_COLS), 0.0, 1.0),
        'rwkv_w0': uni(16, (DEPTH, D_RWKV), -5.0, 1.0),
        'rwkv_w_up': nrm(18, (DEPTH, LORA_W, D_RWKV), 0.5 * LORA_W ** -0.5),
        'rwkv_a0': nrm(19, (DEPTH, D_RWKV), 0.1),
        'rwkv_a_up': nrm(20, (DEPTH, LORA_A, D_RWKV), LORA_A ** -0.5),
        'rwkv_g_up': nrm(21, (DEPTH, LORA_G, D_RWKV), LORA_G ** -0.5),
        'rwkv_k_k': 0.85 + nrm(22, (DEPTH, D_RWKV), 0.02),
        'rwkv_k_a': 1.0 + nrm(23, (DEPTH, D_RWKV), 0.02),
        'rwkv_r_k': nrm(24, (DEPTH, H_RWKV, HEAD_DIM), 0.1),
        'rwkv_ln_w': 1.0 + nrm(25, (DEPTH, D_RWKV), 0.02),
        'rwkv_ln_b': nrm(26, (DEPTH, D_RWKV), 0.02),
        'w_out': nrm(27, (DEPTH, D_MIX, D_MODEL), D_MIX ** -0.5),
        'norm_ffn_w': 1.0 + nrm(28, (DEPTH, D_MODEL), 0.02),
        'router_group_w': nrm(29, (DEPTH, D_MODEL, N_GROUPS), D_MODEL ** -0.5),
        'router_group_b': nrm(30, (DEPTH, N_GROUPS), 0.01),
        'router_expert_w': nrm(31, (DEPTH, D_MODEL, N_EXPERTS), D_MODEL ** -0.5),
        'router_expert_b': nrm(32, (DEPTH, N_EXPERTS), 0.01),
        'expert_w_gate_up': nrm(33, (DEPTH, N_EXPERTS, D_MODEL, 2 * D_EXPERT), D_MODEL ** -0.5),
        'expert_w_down': nrm(34, (DEPTH, N_EXPERTS, D_EXPERT, D_MODEL), D_EXPERT ** -0.5),
        'final_norm_w': 1.0 + nrm(35, (D_MODEL,), 0.02),
    }


def reference(x_prompt, x_sample, state_gdn_conv, state_gdn, state_rwkv_shift, state_rwkv,
              c_prompt, c_sample, ada_w, ada_b, norm_mix_w, w_in, gdn_conv_w, gdn_a_log, gdn_dt_bias,
              gdn_norm_w, rwkv_mu, rwkv_w0, rwkv_w_up, rwkv_a0, rwkv_a_up, rwkv_g_up, rwkv_k_k, rwkv_k_a,
              rwkv_r_k, rwkv_ln_w, rwkv_ln_b, w_out, norm_ffn_w, router_group_w, router_group_b,
              router_expert_w, router_expert_b, expert_w_gate_up, expert_w_down, final_norm_w):
    hp, hs = x_prompt, x_sample
    conv_p, sgdn_p, shift_p, srwkv_p = [], [], [], []
    conv_s, sgdn_s, shift_s, srwkv_s = [], [], [], []
    for l in range(DEPTH):
        p = {
            'ada_w': ada_w[l], 'ada_b': ada_b[l], 'norm_mix_w': norm_mix_w[l], 'w_in': w_in[l],
            'gdn_conv_w': gdn_conv_w[l], 'gdn_a_log': gdn_a_log[l], 'gdn_dt_bias': gdn_dt_bias[l],
            'gdn_norm_w': gdn_norm_w[l], 'rwkv_mu': rwkv_mu[l], 'rwkv_w0': rwkv_w0[l],
            'rwkv_w_up': rwkv_w_up[l], 'rwkv_a0': rwkv_a0[l], 'rwkv_a_up': rwkv_a_up[l],
            'rwkv_g_up': rwkv_g_up[l], 'rwkv_k_k': rwkv_k_k[l], 'rwkv_k_a': rwkv_k_a[l],
            'rwkv_r_k': rwkv_r_k[l], 'rwkv_ln_w': rwkv_ln_w[l], 'rwkv_ln_b': rwkv_ln_b[l],
            'w_out': w_out[l], 'norm_ffn_w': norm_ffn_w[l], 'router_group_w': router_group_w[l],
            'router_group_b': router_group_b[l], 'router_expert_w': router_expert_w[l],
            'router_expert_b': router_expert_b[l], 'expert_w_gate_up': expert_w_gate_up[l],
            'expert_w_down': expert_w_down[l],
        }
        zc = jnp.zeros((BATCH, CONV_W - 1, GDN_QKV), hp.dtype)
        zs = jnp.zeros((BATCH, H_GDN, HEAD_DIM, HEAD_DIM), jnp.float32)
        zsh = jnp.zeros((BATCH, RWKV_COLS), hp.dtype)
        zr = jnp.zeros((BATCH, H_RWKV, HEAD_DIM, HEAD_DIM), jnp.float32)
        hp, cp, gp, shp, rp = decoder_layer(hp, c_prompt, zc, zs, zsh, zr, p)
        hs, cs, gs, shs, rs = decoder_layer(hs, c_sample, state_gdn_conv[l], state_gdn[l],
                                            state_rwkv_shift[l], state_rwkv[l], p)
        conv_p.append(cp); sgdn_p.append(gp); shift_p.append(shp); srwkv_p.append(rp)
        conv_s.append(cs); sgdn_s.append(gs); shift_s.append(shs); srwkv_s.append(rs)
    y_prompt = rms_norm(hp, final_norm_w)
    y_sample = rms_norm(hs, final_norm_w)
    return (y_prompt, y_sample,
            jnp.stack(conv_p), jnp.stack(sgdn_p), jnp.stack(shift_p), jnp.stack(srwkv_p),
            jnp.stack(conv_s), jnp.stack(sgdn_s), jnp.stack(shift_s), jnp.stack(srwkv_s))
```

```python
import functools

import jax
import jax.numpy as jnp
from jax import lax
from jax.experimental import pallas as pl
from jax.experimental.pallas import tpu as pltpu

F32 = jnp.float32
BF16 = jnp.bfloat16
HI = lax.Precision.HIGHEST

D_MODEL = 1024
HEAD_DIM = 64
N_HEADS = 8
D_MIX_HALF = N_HEADS * HEAD_DIM
CONV_W = 4
LORA_W = 64
LORA_A = 64
LORA_G = 128
N_GROUPS = 4
EXP_PER_GROUP = 8
N_EXPERTS = N_GROUPS * EXP_PER_GROUP
D_EXPERT = 256
EPS = 1e-6
GN_EPS = HEAD_DIM * 1e-5

LANES = 128
N_PAIRS = D_MIX_HALF // LANES
GDN_QKV = 3 * D_MIX_HALF
RWKV_COLS = 3 * D_MIX_HALF + LORA_W + LORA_A + LORA_G
LORA_COLS = LORA_W + LORA_A + LORA_G

COL_Q, COL_K, COL_V, COL_Z, COL_B, COL_A = (i * N_PAIRS for i in range(6))
COL_RW = 6 * N_PAIRS
N_COLS = 6 * D_MIX_HALF + RWKV_COLS
CHUNK = 64

VMEM_LIMIT = 48 * 1024 * 1024


def _dot(a, b, prec=HI):
    return jnp.dot(a, b, preferred_element_type=F32, precision=prec)


def _dot_nt(a, b, prec=HI):
    return lax.dot_general(a, b, (((1,), (1,)), ((), ())), preferred_element_type=F32, precision=prec)


def _dot_tn(a, b, prec=HI):
    return lax.dot_general(a, b, (((0,), (0,)), ((), ())), preferred_element_type=F32, precision=prec)


def _sigmoid(x):
    return 1.0 / (1.0 + jnp.exp(-x))


def _silu(x):
    return x * _sigmoid(x)


def _softplus(x):
    return jnp.maximum(x, 0.0) + jnp.log1p(jnp.exp(-jnp.abs(x)))


def _iota2(shape, dim):
    return lax.broadcasted_iota(jnp.int32, shape, dim)


def _head_block_ones():
    r = _iota2((LANES, LANES), 0)
    c = _iota2((LANES, LANES), 1)
    sh = HEAD_DIM.bit_length() - 1
    return jnp.where((r >> sh) == (c >> sh), 1.0, 0.0).astype(F32)


def _tri_ones(n):
    r = _iota2((n, n), 0)
    c = _iota2((n, n), 1)
    return jnp.where(r >= c, 1.0, 0.0).astype(F32)


def _unit_lower_inverse(low, n):
    r = _iota2((n, n), 0)
    c = _iota2((n, n), 1)
    inv = jnp.where(r == c, 1.0, 0.0) - jnp.where((r >> 1) == (c >> 1), low, 0.0)
    s = 2
    while s < n:
        sh = s.bit_length()
        off = jnp.where(((r >> sh) == (c >> sh)) & ((r & (2 * s - 1)) >= s) & ((c & (2 * s - 1)) < s), low, 0.0)
        inv = inv - _dot(inv, _dot(off, inv))
        s *= 2
    return inv


def _ada_kernel(c_ref, w_ref, b_ref, o_ref):
    o_ref[...] = _dot(_silu(c_ref[...]), w_ref[...]) + b_ref[...]


def _ada(c_all, ada_w, ada_b):
    n, d = c_all.shape
    nout = ada_w.shape[1]
    tn = 1536
    return pl.pallas_call(
        _ada_kernel,
        grid=(nout // tn,),
        in_specs=[pl.BlockSpec((n, d), lambda j: (0, 0)),
                  pl.BlockSpec((d, tn), lambda j: (0, j)),
                  pl.BlockSpec((1, tn), lambda j: (0, j))],
        out_specs=pl.BlockSpec((n, tn), lambda j: (0, j)),
        out_shape=jax.ShapeDtypeStruct((n, nout), F32),
        compiler_params=pltpu.CompilerParams(dimension_semantics=("arbitrary",), vmem_limit_bytes=VMEM_LIMIT),
        name="ada",
    )(c_all, ada_w, ada_b.reshape(1, nout))


def _modulated_norm(x, nw, sc, sh):
    ms = jnp.mean(x * x, axis=-1, keepdims=True)
    return (x * lax.rsqrt(ms + EPS) * nw) * (1.0 + sc) + sh


def _inproj_kernel(x_ref, sc_ref, sh_ref, nw_ref, w_ref, o_ref):
    bb, tt, d = x_ref.shape
    h = _modulated_norm(x_ref[...], nw_ref[...], sc_ref[...], sh_ref[...])
    h = h.reshape(bb * tt, d).astype(BF16)
    o = jnp.dot(h, w_ref[...], preferred_element_type=F32)
    o_ref[...] = o.reshape(bb, tt, o.shape[-1])


def _inproj(x, sc, sh, nw, w_cols, bb, tt):
    b, t, d = x.shape
    tn = N_COLS // 2
    grid = (2, b // bb, t // tt)
    return pl.pallas_call(
        _inproj_kernel,
        grid=grid,
        in_specs=[pl.BlockSpec((bb, tt, d), lambda n, i, j: (i, j, 0)),
                  pl.BlockSpec((bb, 1, d), lambda n, i, j: (i, 0, 0)),
                  pl.BlockSpec((bb, 1, d), lambda n, i, j: (i, 0, 0)),
                  pl.BlockSpec((1, 1, d), lambda n, i, j: (0, 0, 0)),
                  pl.BlockSpec((d, tn), lambda n, i, j: (0, n))],
        out_specs=pl.BlockSpec((bb, tt, tn), lambda n, i, j: (i, j, n)),
        out_shape=jax.ShapeDtypeStruct((b, t, N_COLS), F32),
        compiler_params=pltpu.CompilerParams(dimension_semantics=("arbitrary",) * 3, vmem_limit_bytes=VMEM_LIMIT),
        name="inproj",
    )(x, sc, sh, nw, w_cols)


def _pad_rows(a, n):
    if a.shape[0] == n:
        return a
    return jnp.concatenate([a, jnp.zeros((n - a.shape[0], a.shape[1]), a.dtype)], axis=0)


def _gdn_kernel(q_ref, k_ref, v_ref, z_ref, b_ref, a_ref, cq_ref, ck_ref, cv_ref, wq_ref, wk_ref, wv_ref,
                alog_ref, dtb_ref, nw_ref, s0_ref, o_ref, sout_ref, s_scr, hist_scr, *, chunk, t_valid):
    cd = q_ref.shape[1]
    c = pl.program_id(2)
    hist_lo = 8 - (CONV_W - 1)

    @pl.when(c == 0)
    def _():
        s_scr[...] = s0_ref[0]
        for n, cref in enumerate((cq_ref, ck_ref, cv_ref)):
            hist_scr[n, hist_lo:8, :] = cref[0]

    def conv(n, u_ref, w_ref):
        u = u_ref[0]
        hist_scr[n, 8:8 + cd, :] = u
        y = u * w_ref[CONV_W - 1:CONV_W, :]
        for i in range(CONV_W - 1):
            y = y + hist_scr[n, hist_lo + i:hist_lo + i + cd, :] * w_ref[i:i + 1, :]
        hist_scr[n, hist_lo:8, :] = hist_scr[n, hist_lo + cd:8 + cd, :]
        return _pad_rows(_silu(y), chunk)

    q = conv(0, q_ref, wq_ref)
    k = conv(1, k_ref, wk_ref)
    v = conv(2, v_ref, wv_ref)
    ones_h = _head_block_ones()
    q = q * lax.rsqrt(_dot(q * q, ones_h) + EPS) * (HEAD_DIM ** -0.5)
    k = k * lax.rsqrt(_dot(k * k, ones_h) + EPS)
    beta = _sigmoid(_pad_rows(b_ref[0], chunk))
    g = -jnp.exp(alog_ref[...]) * _softplus(_pad_rows(a_ref[0], chunk) + dtb_ref[...])
    if t_valid is not None:
        valid = (c * cd + _iota2((chunk, LANES), 0)) < t_valid
        q, k, v = (jnp.where(valid, a, 0.0) for a in (q, k, v))
        beta = jnp.where(valid, beta, 0.0)
        g = jnp.where(valid, g, 0.0)

    gcum = _dot(_tri_ones(chunk), g)
    row = _iota2((chunk, chunk), 0)
    col = _iota2((chunk, chunk), 1)
    causal = row >= col
    strict = row > col
    outs = []
    for j in range(2):
        lo = j * HEAD_DIM
        sl = slice(lo, lo + HEAD_DIM)
        qh, kh, vh, bh, gh = q[:, sl], k[:, sl], v[:, sl], beta[:, sl], gcum[:, sl]
        g_i = gcum[:, lo:lo + chunk]
        decay = jnp.where(causal, jnp.exp(jnp.minimum(g_i - g_i.T, 0.0)), 0.0)
        b_i = beta[:, lo:lo + chunk]
        kk = _dot_nt(kh, kh)
        t_inv = _unit_lower_inverse(jnp.where(strict, b_i * decay * kk, 0.0), chunk)
        eg = jnp.exp(gh)
        sol = _dot(t_inv, jnp.concatenate([bh * vh, bh * eg * kh], axis=1))
        u_v, w_k = sol[:, :HEAD_DIM], sol[:, HEAD_DIM:]
        s = s_scr[j]
        u = u_v - _dot(w_k, s)
        qk = _dot_nt(qh, kh) * decay
        outs.append(_dot(eg * qh, s) + _dot(qk, u))
        g_last = gh[chunk - 1:chunk, :]
        k_dec = jnp.exp(g_last - gh) * kh
        s_scr[j] = jnp.exp(g_last) * s + _dot_tn(k_dec, u)

    o = jnp.concatenate(outs, axis=1)
    o = o * lax.rsqrt(_dot(o * o, ones_h) * (1.0 / HEAD_DIM) + EPS) * nw_ref[...]
    o = o * _silu(_pad_rows(z_ref[0], chunk))
    o_ref[0] = o[:cd]

    @pl.when(c == pl.num_programs(2) - 1)
    def _():
        sout_ref[0] = s_scr[...]


def _gdn(cols, conv0, s0, conv_w, alog_rep, dtb_rep, nw_rep, cd, chunk, t_valid):
    b, t, _ = cols.shape
    nch = t // cd

    def col_spec(base):
        return pl.BlockSpec((1, cd, LANES), lambda i, h, c: (i, c, base + h))

    def conv_spec(base):
        return pl.BlockSpec((1, CONV_W - 1, LANES), lambda i, h, c: (i, 0, base + h))

    def w_spec(base):
        return pl.BlockSpec((CONV_W, LANES), lambda i, h, c: (0, base + h))

    vec_spec = pl.BlockSpec((1, LANES), lambda i, h, c: (0, h))
    state_spec = pl.BlockSpec((1, 2, HEAD_DIM, HEAD_DIM), lambda i, h, c: (i, h, 0, 0))
    kern = functools.partial(_gdn_kernel, chunk=chunk, t_valid=t_valid)
    return pl.pallas_call(
        kern,
        grid=(b, N_PAIRS, nch),
        in_specs=[col_spec(COL_Q), col_spec(COL_K), col_spec(COL_V), col_spec(COL_Z), col_spec(COL_B),
                  col_spec(COL_A),
                  conv_spec(0), conv_spec(N_PAIRS), conv_spec(2 * N_PAIRS),
                  w_spec(0), w_spec(N_PAIRS), w_spec(2 * N_PAIRS),
                  vec_spec, vec_spec, pl.BlockSpec((1, LANES), lambda i, h, c: (0, 0)),
                  state_spec],
        out_specs=[pl.BlockSpec((1, cd, LANES), lambda i, h, c: (i, c, h)), state_spec],
        out_shape=[jax.ShapeDtypeStruct((b, t, D_MIX_HALF), F32),
                   jax.ShapeDtypeStruct((b, N_HEADS, HEAD_DIM, HEAD_DIM), F32)],
        scratch_shapes=[pltpu.VMEM((2, HEAD_DIM, HEAD_DIM), F32), pltpu.VMEM((3, 8 + cd, LANES), F32)],
        compiler_params=pltpu.CompilerParams(dimension_semantics=("arbitrary",) * 3, vmem_limit_bytes=VMEM_LIMIT),
        name="gdn",
    )(cols, cols, cols, cols, cols, cols, conv0, conv0, conv0, conv_w, conv_w, conv_w,
      alog_rep, dtb_rep, nw_rep, s0)


def _rwkv_kernel(r_ref, k_ref, v_ref, l_ref, pr_ref, pk_ref, pv_ref, pl_ref, mr_ref, mk_ref, mv_ref, ml_ref,
                 w0_ref, a0_ref, kk_ref, ka_ref, rk_ref, lnw_ref, lnb_ref, wup_ref, aup_ref, gup_ref, s0_ref,
                 o_ref, sout_ref, s_scr, hist_scr, histl_scr, *, chunk, t_valid):
    cd = r_ref.shape[1]
    c = pl.program_id(2)

    @pl.when(c == 0)
    def _():
        s_scr[...] = s0_ref[0]
        for n, pref in enumerate((pr_ref, pk_ref, pv_ref)):
            hist_scr[n, 7:8, :] = pref[0]
        histl_scr[7:8, :] = pl_ref[0]

    def shift(u_ref, mu_ref, read, write):
        u = u_ref[0]
        write(slice(8, 8 + cd), u)
        prev = read(slice(7, 7 + cd))
        write(slice(7, 8), read(slice(7 + cd, 8 + cd)))
        return _pad_rows(u + (prev - u) * mu_ref[...], chunk)

    def mixed(n, u_ref, mu_ref):
        def read(s):
            return hist_scr[n, s, :]

        def write(s, val):
            hist_scr[n, s, :] = val
        return shift(u_ref, mu_ref, read, write)

    def read_l(s):
        return histl_scr[s, :]

    def write_l(s, val):
        histl_scr[s, :] = val

    r = mixed(0, r_ref, mr_ref)
    k = mixed(1, k_ref, mk_ref)
    v = mixed(2, v_ref, mv_ref)
    xl = shift(l_ref, ml_ref, read_l, write_l)
    dw = xl[:, :LORA_W]
    da = xl[:, LORA_W:LORA_W + LORA_A]
    dg = xl[:, LORA_W + LORA_A:]
    w_log = -_softplus(-(w0_ref[...] + _dot(jnp.tanh(dw), wup_ref[...]))) - 0.5
    lw = -jnp.exp(w_log)
    a = _sigmoid(a0_ref[...] + _dot(da, aup_ref[...]))
    gate = _dot(_sigmoid(dg), gup_ref[...])
    ones_h = _head_block_ones()
    kk = k * kk_ref[...]
    kk = kk * lax.rsqrt(_dot(kk * kk, ones_h) + EPS)
    k = k * (1.0 + (a - 1.0) * ka_ref[...])
    if t_valid is not None:
        valid = (c * cd + _iota2((chunk, LANES), 0)) < t_valid
        r, k, v, kk, lw = (jnp.where(valid, x, 0.0) for x in (r, k, v, kk, lw))

    cum = _dot(_tri_ones(chunk), lw)
    e_in = jnp.exp(cum)
    e_out = jnp.exp(-cum)
    a_t = -kk * jnp.exp(cum - lw)
    b_t = kk * a * e_out
    k_t = k * e_out
    r_t = r * e_in
    cum_last = cum[chunk - 1:chunk, :]
    e_rest = jnp.exp(cum_last - cum)
    b_c = kk * a * e_rest
    k_c = k * e_rest
    row = _iota2((chunk, chunk), 0)
    col = _iota2((chunk, chunk), 1)
    causal = row >= col
    strict = row > col
    ys = []
    for j in range(2):
        sl = slice(j * HEAD_DIM, (j + 1) * HEAD_DIM)
        ar = jnp.concatenate([a_t[:, sl], r_t[:, sl]], axis=0)
        bk = jnp.concatenate([b_t[:, sl], k_t[:, sl]], axis=0)
        m = _dot_nt(ar, bk)
        l_ab = jnp.where(strict, m[:chunk, :chunk], 0.0)
        l_ak = jnp.where(strict, m[:chunk, chunk:], 0.0)
        m_rb = jnp.where(causal, m[chunk:, :chunk], 0.0)
        m_rk = jnp.where(causal, m[chunk:, chunk:], 0.0)
        t_inv = _unit_lower_inverse(-l_ab, chunk)
        s = s_scr[j]
        ars = _dot_nt(ar, s)
        vh = v[:, sl]
        u = _dot(t_inv, ars[:chunk] + _dot(l_ak, vh))
        ys.append(ars[chunk:] + _dot(m_rb, u) + _dot(m_rk, vh))
        s_scr[j] = s * jnp.exp(cum_last[:, sl]) + _dot_tn(u, b_c[:, sl]) + _dot_tn(vh, k_c[:, sl])

    y = jnp.concatenate(ys, axis=1)
    mean = _dot(y, ones_h) * (1.0 / HEAD_DIM)
    dy = y - mean
    var = _dot(dy * dy, ones_h) * (1.0 / HEAD_DIM)
    y = dy * lax.rsqrt(var + GN_EPS) * lnw_ref[...] + lnb_ref[...]
    bonus = _dot(r * k * rk_ref[...], ones_h) * v
    o_ref[0] = ((y + bonus) * gate)[:cd]

    @pl.when(c == pl.num_programs(2) - 1)
    def _():
        sout_ref[0] = s_scr[...]


def _rwkv(cols, shift0, s0, mu, vecs, w_up, a_up, g_up, cd, chunk, t_valid):
    b, t, _ = cols.shape
    nch = t // cd
    lora_blk = (COL_RW * LANES + 3 * D_MIX_HALF) // LORA_COLS
    lora_blk_local = (3 * D_MIX_HALF) // LORA_COLS

    def col_spec(base):
        return pl.BlockSpec((1, cd, LANES), lambda i, h, c: (i, c, COL_RW + base + h))

    def prev_spec(base):
        return pl.BlockSpec((1, 1, LANES), lambda i, h, c: (i, 0, base + h))

    def mu_spec(base):
        return pl.BlockSpec((1, LANES), lambda i, h, c: (0, base + h))

    vec_spec = pl.BlockSpec((1, LANES), lambda i, h, c: (0, h))
    state_spec = pl.BlockSpec((1, 2, HEAD_DIM, HEAD_DIM), lambda i, h, c: (i, h, 0, 0))
    kern = functools.partial(_rwkv_kernel, chunk=chunk, t_valid=t_valid)
    return pl.pallas_call(
        kern,
        grid=(b, N_PAIRS, nch),
        in_specs=[col_spec(0), col_spec(N_PAIRS), col_spec(2 * N_PAIRS),
                  pl.BlockSpec((1, cd, LORA_COLS), lambda i, h, c: (i, c, lora_blk)),
                  prev_spec(0), prev_spec(N_PAIRS), prev_spec(2 * N_PAIRS),
                  pl.BlockSpec((1, 1, LORA_COLS), lambda i, h, c: (i, 0, lora_blk_local)),
                  mu_spec(0), mu_spec(N_PAIRS), mu_spec(2 * N_PAIRS),
                  pl.BlockSpec((1, LORA_COLS), lambda i, h, c: (0, lora_blk_local))]
                 + [vec_spec] * 7
                 + [pl.BlockSpec((LORA_W, LANES), lambda i, h, c: (0, h)),
                    pl.BlockSpec((LORA_A, LANES), lambda i, h, c: (0, h)),
                    pl.BlockSpec((LORA_G, LANES), lambda i, h, c: (0, h)),
                    state_spec],
        out_specs=[pl.BlockSpec((1, cd, LANES), lambda i, h, c: (i, c, h)), state_spec],
        out_shape=[jax.ShapeDtypeStruct((b, t, D_MIX_HALF), F32),
                   jax.ShapeDtypeStruct((b, N_HEADS, HEAD_DIM, HEAD_DIM), F32)],
        scratch_shapes=[pltpu.VMEM((2, HEAD_DIM, HEAD_DIM), F32), pltpu.VMEM((3, 8 + cd, LANES), F32),
                        pltpu.VMEM((8 + cd, LORA_COLS), F32)],
        compiler_params=pltpu.CompilerParams(dimension_semantics=("arbitrary",) * 3, vmem_limit_bytes=VMEM_LIMIT),
        name="rwkv",
    )(cols, cols, cols, cols, shift0, shift0, shift0, shift0, mu, mu, mu, mu, *vecs, w_up, a_up, g_up, s0)


def _outproj_kernel(oa_ref, ob_ref, x_ref, gt_ref, sc_ref, sh_ref, nw_ref, wa_ref, wb_ref, wr_ref, br_ref,
                    x1_ref, h_ref, wt_ref):
    bb, tt, d = x_ref.shape
    n = bb * tt
    oa = oa_ref[...].reshape(n, D_MIX_HALF).astype(BF16)
    ob = ob_ref[...].reshape(n, D_MIX_HALF).astype(BF16)
    mixed = (jnp.dot(oa, wa_ref[...], preferred_element_type=F32)
             + jnp.dot(ob, wb_ref[...], preferred_element_type=F32))
    x1 = x_ref[...] + gt_ref[...] * mixed.reshape(bb, tt, d)
    x1_ref[...] = x1
    h = _modulated_norm(x1, nw_ref[...], sc_ref[...], sh_ref[...]).reshape(n, d)
    h_ref[...] = h.reshape(bb, tt, d).astype(BF16)

    logits = _dot(h, wr_ref[...]) + br_ref[...]
    lane_i = _iota2((n, LANES), 1)
    lane = lane_i.astype(F32)
    lane_grp = (lane_i >> (EXP_PER_GROUP.bit_length() - 1)).astype(F32)
    neg = jnp.float32(-jnp.inf)
    big = jnp.float32(LANES)
    is_grp = (lane_i >= N_EXPERTS) & (lane_i < N_EXPERTS + N_GROUPS)
    gl = jnp.where(is_grp, logits, neg)
    gmax = jnp.max(gl, axis=-1, keepdims=True)
    grp = jnp.min(jnp.where(gl == gmax, lane, big), axis=-1, keepdims=True) - N_EXPERTS
    g_prob = 1.0 / jnp.sum(jnp.where(is_grp, jnp.exp(gl - gmax), 0.0), axis=-1, keepdims=True)
    in_grp = (lane_i < N_EXPERTS) & (lane_grp == grp)
    el = jnp.where(in_grp, logits, neg)
    m1 = jnp.max(el, axis=-1, keepdims=True)
    i1 = jnp.min(jnp.where(el == m1, lane, big), axis=-1, keepdims=True)
    el2 = jnp.where(lane == i1, neg, el)
    m2 = jnp.max(el2, axis=-1, keepdims=True)
    i2 = jnp.min(jnp.where(el2 == m2, lane, big), axis=-1, keepdims=True)
    e2 = jnp.exp(m2 - m1)
    p1 = g_prob / (1.0 + e2)
    p2 = g_prob * e2 / (1.0 + e2)
    wt = jnp.where(lane == i1, p1, 0.0) + jnp.where(lane == i2, p2, 0.0)
    wt_ref[...] = wt.reshape(bb, tt, LANES)


def _outproj(oa, ob, x, gt, sc, sh, nw, wa, wb, wr, br, bb, tt):
    b, t, d = x.shape
    grid = (b // bb, t // tt)

    def tok_spec(w):
        return pl.BlockSpec((bb, tt, w), lambda i, j: (i, j, 0))

    mod_spec = pl.BlockSpec((bb, 1, d), lambda i, j: (i, 0, 0))

    def full2(shape):
        return pl.BlockSpec(shape, lambda i, j: (0, 0))

    return pl.pallas_call(
        _outproj_kernel,
        grid=grid,
        in_specs=[tok_spec(D_MIX_HALF), tok_spec(D_MIX_HALF), tok_spec(d), mod_spec, mod_spec, mod_spec,
                  pl.BlockSpec((1, 1, d), lambda i, j: (0, 0, 0)),
                  full2((D_MIX_HALF, d)), full2((D_MIX_HALF, d)), full2((d, LANES)), full2((1, LANES))],
        out_specs=[tok_spec(d), tok_spec(d), tok_spec(LANES)],
        out_shape=[jax.ShapeDtypeStruct((b, t, d), F32), jax.ShapeDtypeStruct((b, t, d), BF16),
                   jax.ShapeDtypeStruct((b, t, LANES), F32)],
        compiler_params=pltpu.CompilerParams(dimension_semantics=("arbitrary",) * 2, vmem_limit_bytes=VMEM_LIMIT),
        name="outproj",
    )(oa, ob, x, gt, sc, sh, nw, wa, wb, wr, br)


def _moe_kernel(h_ref, wt_ref, x1_ref, gt_ref, fw_ref, wgu_ref, wd_ref, y_ref, acc_scr):
    bb, tt, d = x1_ref.shape
    n = bb * tt
    e = pl.program_id(2)

    @pl.when(e == 0)
    def _():
        acc_scr[...] = jnp.zeros_like(acc_scr)

    h = h_ref[...].reshape(n, d)
    gu = jnp.dot(h, wgu_ref[0], preferred_element_type=F32)
    act = (_silu(gu[:, :D_EXPERT]) * gu[:, D_EXPERT:]).astype(BF16)
    y = jnp.dot(act, wd_ref[0], preferred_element_type=F32)
    wt = wt_ref[...].reshape(n, LANES)
    w_e = jnp.sum(jnp.where(_iota2((n, LANES), 1) == e, wt, 0.0), axis=-1, keepdims=True)
    acc_scr[...] += w_e * y

    @pl.when(e == pl.num_programs(2) - 1)
    def _():
        x2 = x1_ref[...] + gt_ref[...] * acc_scr[...].reshape(bb, tt, d)
        ms = jnp.mean(x2 * x2, axis=-1, keepdims=True)
        y_ref[...] = x2 * lax.rsqrt(ms + EPS) * fw_ref[...]


def _moe(h, wt, x1, gt, fw, w_gu, w_down, bb, tt):
    b, t, d = x1.shape
    grid = (b // bb, t // tt, N_EXPERTS)

    def tok_spec(w):
        return pl.BlockSpec((bb, tt, w), lambda i, j, e: (i, j, 0))

    return pl.pallas_call(
        _moe_kernel,
        grid=grid,
        in_specs=[tok_spec(d), tok_spec(LANES), tok_spec(d),
                  pl.BlockSpec((bb, 1, d), lambda i, j, e: (i, 0, 0)),
                  pl.BlockSpec((1, 1, d), lambda i, j, e: (0, 0, 0)),
                  pl.BlockSpec((1, d, 2 * D_EXPERT), lambda i, j, e: (e, 0, 0)),
                  pl.BlockSpec((1, D_EXPERT, d), lambda i, j, e: (e, 0, 0))],
        out_specs=tok_spec(d),
        out_shape=jax.ShapeDtypeStruct((b, t, d), F32),
        scratch_shapes=[pltpu.VMEM((bb * tt, d), F32)],
        compiler_params=pltpu.CompilerParams(dimension_semantics=("arbitrary",) * 3, vmem_limit_bytes=VMEM_LIMIT),
        name="moe",
    )(h, wt, x1, gt, fw, w_gu, w_down)


def _layer(x, mod, conv0, s_gdn0, shift0, s_rwkv0, p, *, t_valid, bb, tt, cd, chunk):
    sh_m, sc_m, gt_m, sh_f, sc_f, gt_f = mod
    cols = _inproj(x, sc_m, sh_m, p['norm_mix_w'], p['w_cols'], bb, tt)
    o_a, s_gdn = _gdn(cols, conv0, s_gdn0, p['gdn_conv_w'], p['alog_rep'], p['dtb_rep'], p['gdn_nw_rep'],
                      cd, chunk, t_valid)
    o_b, s_rwkv = _rwkv(cols, shift0, s_rwkv0, p['rwkv_mu'], p['rwkv_vecs'], p['rwkv_w_up'], p['rwkv_a_up'],
                        p['rwkv_g_up'], cd, chunk, t_valid)
    x1, h2, wt = _outproj(o_a, o_b, x, gt_m, sc_f, sh_f, p['norm_ffn_w'], p['w_out_a'], p['w_out_b'],
                          p['w_router'], p['b_router'], bb, tt)
    y = _moe(h2, wt, x1, gt_f, p['final_norm_w'], p['w_gu'], p['w_down'], bb, tt)
    t_last = x.shape[1] if t_valid is None else t_valid
    conv_new = cols[:, t_last - (CONV_W - 1):t_last, :GDN_QKV]
    shift_new = cols[:, t_last - 1, COL_RW * LANES:]
    return y, conv_new, s_gdn, shift_new, s_rwkv


def kernel(x_prompt, x_sample, state_gdn_conv, state_gdn, state_rwkv_shift, state_rwkv, c_prompt, c_sample, ada_w, ada_b, norm_mix_w, w_in, gdn_conv_w, gdn_a_log, gdn_dt_bias, gdn_norm_w, rwkv_mu, rwkv_w0, rwkv_w_up, rwkv_a0, rwkv_a_up, rwkv_g_up, rwkv_k_k, rwkv_k_a, rwkv_r_k, rwkv_ln_w, rwkv_ln_b, w_out, norm_ffn_w, router_group_w, router_group_b, router_expert_w, router_expert_b, expert_w_gate_up, expert_w_down, final_norm_w):
    depth = ada_w.shape[0]
    assert depth == 1
    l = 0
    b_p, t_p, d = x_prompt.shape
    b_s, t_s, _ = x_sample.shape
    gdn_cols = GDN_QKV + D_MIX_HALF + 2 * N_HEADS

    w = w_in[l]
    rep = jnp.repeat(jnp.arange(N_HEADS), HEAD_DIM)
    beta0 = GDN_QKV + D_MIX_HALF
    w_cols = jnp.concatenate([w[:, :beta0], w[:, beta0 + rep], w[:, beta0 + N_HEADS + rep], w[:, gdn_cols:]],
                             axis=1).astype(BF16)
    row = lambda a: a.reshape(1, -1)
    w_router = jnp.zeros((d, LANES), F32)
    w_router = w_router.at[:, :N_EXPERTS].set(router_expert_w[l]).at[:, N_EXPERTS:N_EXPERTS + N_GROUPS].set(
        router_group_w[l])
    b_router = jnp.zeros((1, LANES), F32)
    b_router = b_router.at[0, :N_EXPERTS].set(router_expert_b[l]).at[0, N_EXPERTS:N_EXPERTS + N_GROUPS].set(
        router_group_b[l])
    p = {
        'norm_mix_w': norm_mix_w[l].reshape(1, 1, d), 'w_cols': w_cols,
        'gdn_conv_w': gdn_conv_w[l], 'alog_rep': row(gdn_a_log[l][rep]), 'dtb_rep': row(gdn_dt_bias[l][rep]),
        'gdn_nw_rep': row(jnp.tile(gdn_norm_w[l], 2)),
        'rwkv_mu': row(rwkv_mu[l]),
        'rwkv_vecs': tuple(row(a) for a in (rwkv_w0[l], rwkv_a0[l], rwkv_k_k[l], rwkv_k_a[l], rwkv_r_k[l],
                                            rwkv_ln_w[l], rwkv_ln_b[l])),
        'rwkv_w_up': rwkv_w_up[l], 'rwkv_a_up': rwkv_a_up[l], 'rwkv_g_up': rwkv_g_up[l],
        'w_out_a': w_out[l][:D_MIX_HALF].astype(BF16), 'w_out_b': w_out[l][D_MIX_HALF:].astype(BF16),
        'norm_ffn_w': norm_ffn_w[l].reshape(1, 1, d), 'w_router': w_router, 'b_router': b_router,
        'w_gu': expert_w_gate_up[l].astype(BF16), 'w_down': expert_w_down[l].astype(BF16),
        'final_norm_w': final_norm_w.reshape(1, 1, d),
    }

    mod = _ada(jnp.concatenate([c_prompt, c_sample], axis=0), ada_w[l], ada_b[l])
    mod_p = tuple(m.reshape(b_p, 1, d) for m in jnp.split(mod[:b_p], 6, axis=-1))
    mod_s = tuple(m.reshape(b_s, 1, d) for m in jnp.split(mod[b_p:], 6, axis=-1))

    zc = jnp.zeros((b_p, CONV_W - 1, GDN_QKV), F32)
    zs = jnp.zeros((b_p, N_HEADS, HEAD_DIM, HEAD_DIM), F32)
    zsh = jnp.zeros((b_p, 1, RWKV_COLS), F32)
    y_p, conv_p, sg_p, shift_p, sr_p = _layer(x_prompt, mod_p, zc, zs, zsh, zs, p, t_valid=None,
                                              bb=1, tt=512, cd=CHUNK, chunk=CHUNK)

    t_pad = 8
    x_s = jnp.pad(x_sample, ((0, 0), (0, t_pad - t_s), (0, 0)))
    y_s, conv_s, sg_s, shift_s, sr_s = _layer(x_s, mod_s, state_gdn_conv[l], state_gdn[l],
                                              state_rwkv_shift[l].reshape(b_s, 1, RWKV_COLS), state_rwkv[l], p,
                                              t_valid=t_s, bb=64, tt=t_pad, cd=t_pad, chunk=CHUNK)
    y_s = y_s[:, :t_s]
    return (y_p, y_s, conv_p[None], sg_p[None], shift_p[None], sr_p[None],
            conv_s[None], sg_s[None], shift_s[None], sr_s[None])
```

```python
import functools

import jax
import jax.numpy as jnp
from jax import lax
from jax.experimental import pallas as pl
from jax.experimental.pallas import tpu as pltpu

F32 = jnp.float32
BF16 = jnp.bfloat16
HI = lax.Precision.HIGHEST

D_MODEL = 1024
HEAD_DIM = 64
N_HEADS = 8
D_MIX_HALF = N_HEADS * HEAD_DIM
CONV_W = 4
LORA_W = 64
LORA_A = 64
LORA_G = 128
N_GROUPS = 4
EXP_PER_GROUP = 8
N_EXPERTS = N_GROUPS * EXP_PER_GROUP
D_EXPERT = 256
EPS = 1e-6
GN_EPS = HEAD_DIM * 1e-5

LANES = 128
SUBLANES = 8
GDN_QKV = 3 * D_MIX_HALF
RWKV_RKV = 3 * D_MIX_HALF
RWKV_COLS = RWKV_RKV + LORA_W + LORA_A + LORA_G
LORA_COLS = LORA_W + LORA_A + LORA_G

COL_Z = GDN_QKV
COL_B = COL_Z + D_MIX_HALF
COL_A = COL_B + D_MIX_HALF
COL_RW = COL_A + D_MIX_HALF
COL_LORA = COL_RW + RWKV_RKV
N_COLS = COL_LORA + LORA_COLS
CHUNK = 64

VMEM_LIMIT = 48 * 1024 * 1024


def _dot(a, b, prec=HI):
    return jnp.dot(a, b, preferred_element_type=F32, precision=prec)


def _bf(a):
    return a.astype(BF16)


def _mm(a, b):
    return jnp.dot(_bf(a), _bf(b), preferred_element_type=F32)


def _mm_nt(a, b):
    return lax.dot_general(_bf(a), _bf(b), (((1,), (1,)), ((), ())), preferred_element_type=F32)


def _mm_tn(a, b):
    return lax.dot_general(_bf(a), _bf(b), (((0,), (0,)), ((), ())), preferred_element_type=F32)


def _split2(a):
    hi = a.astype(BF16)
    return hi, (a - hi.astype(F32)).astype(BF16)


def _sigmoid(x):
    return 1.0 / (1.0 + jnp.exp(-x))


def _silu(x):
    return x * _sigmoid(x)


def _softplus(x):
    return jnp.maximum(x, 0.0) + jnp.log1p(jnp.exp(-jnp.abs(x)))


def _iota2(shape, dim):
    return lax.broadcasted_iota(jnp.int32, shape, dim)


def _head_block_ones():
    r = _iota2((LANES, LANES), 0)
    c = _iota2((LANES, LANES), 1)
    sh = HEAD_DIM.bit_length() - 1
    return jnp.where((r >> sh) == (c >> sh), 1.0, 0.0).astype(F32)


def _head_sums(x):
    ones = _bf(_head_block_ones())
    hi, lo = _split2(x)
    parts = []
    for p in range(x.shape[1] // LANES):
        s = slice(p * LANES, (p + 1) * LANES)
        parts.append(jnp.dot(hi[:, s], ones, preferred_element_type=F32)
                     + jnp.dot(lo[:, s], ones, preferred_element_type=F32))
    return jnp.concatenate(parts, axis=1)


def _cumsum_rows(x):
    n = x.shape[0]
    r = _iota2((n, n), 0)
    c = _iota2((n, n), 1)
    tri = jnp.where(r >= c, 1.0, 0.0).astype(BF16)
    x1 = x.astype(BF16)
    rem = x - x1.astype(F32)
    x2 = rem.astype(BF16)
    x3 = (rem - x2.astype(F32)).astype(BF16)
    return (jnp.dot(tri, x1, preferred_element_type=F32) + jnp.dot(tri, x2, preferred_element_type=F32)
            + jnp.dot(tri, x3, preferred_element_type=F32))


def _unit_lower_inverses(lows, n, mm):
    r = _iota2((n, n), 0)
    c = _iota2((n, n), 1)
    eye = jnp.where(r == c, 1.0, 0.0)
    pair = (r >> 1) == (c >> 1)
    invs = [eye - jnp.where(pair, low, 0.0) for low in lows]
    s = 2
    while s < n:
        sh = s.bit_length()
        sel = ((r >> sh) == (c >> sh)) & ((r & (2 * s - 1)) >= s) & ((c & (2 * s - 1)) < s)
        prods = [mm(jnp.where(sel, low, 0.0), inv) for low, inv in zip(lows, invs)]
        invs = [inv - mm(inv, prod) for inv, prod in zip(invs, prods)]
        s *= 2
    return invs


def _pad_rows(a, n):
    if a.shape[0] == n:
        return a
    return jnp.concatenate([a, jnp.zeros((n - a.shape[0], a.shape[1]), a.dtype)], axis=0)


def _for_rows(n, body):
    if n == 1:
        body(0)
    else:
        def step(i, carry):
            body(i)
            return carry
        lax.fori_loop(0, n, step, 0)


def _ada_kernel(c_ref, w_ref, b_ref, o_ref):
    o_ref[...] = _dot(_silu(c_ref[...]), w_ref[...]) + b_ref[...]


def _ada(c_all, ada_w, ada_b):
    n, d = c_all.shape
    nout = ada_w.shape[1]
    tn = 1536
    return pl.pallas_call(
        _ada_kernel,
        grid=(nout // tn,),
        in_specs=[pl.BlockSpec((n, d), lambda j: (0, 0)),
                  pl.BlockSpec((d, tn), lambda j: (0, j)),
                  pl.BlockSpec((1, tn), lambda j: (0, j))],
        out_specs=pl.BlockSpec((n, tn), lambda j: (0, j)),
        out_shape=jax.ShapeDtypeStruct((n, nout), F32),
        compiler_params=pltpu.CompilerParams(dimension_semantics=("arbitrary",), vmem_limit_bytes=VMEM_LIMIT),
        name="ada",
    )(c_all, ada_w, ada_b.reshape(1, nout))


def _modulated_norm(x, nw, sc, sh):
    ms = jnp.mean(x * x, axis=-1, keepdims=True)
    return (x * lax.rsqrt(ms + EPS) * nw) * (1.0 + sc) + sh


def _inproj_kernel(x_ref, sc_ref, sh_ref, nw_ref, w_ref, o_ref):
    bb, tt, d = x_ref.shape
    h = _modulated_norm(x_ref[...], nw_ref[...], sc_ref[...], sh_ref[...])
    h = h.reshape(bb * tt, d).astype(BF16)
    o = jnp.dot(h, w_ref[...], preferred_element_type=F32)
    o_ref[...] = o.reshape(bb, tt, o.shape[-1])


def _inproj(x, sc, sh, nw, w_cols, bb, tt):
    b, t, d = x.shape
    tn = N_COLS // 2
    grid = (2, b // bb, t // tt)
    return pl.pallas_call(
        _inproj_kernel,
        grid=grid,
        in_specs=[pl.BlockSpec((bb, tt, d), lambda n, i, j: (i, j, 0)),
                  pl.BlockSpec((bb, 1, d), lambda n, i, j: (i, 0, 0)),
                  pl.BlockSpec((bb, 1, d), lambda n, i, j: (i, 0, 0)),
                  pl.BlockSpec((1, 1, d), lambda n, i, j: (0, 0, 0)),
                  pl.BlockSpec((d, tn), lambda n, i, j: (0, n))],
        out_specs=pl.BlockSpec((bb, tt, tn), lambda n, i, j: (i, j, n)),
        out_shape=jax.ShapeDtypeStruct((b, t, N_COLS), F32),
        compiler_params=pltpu.CompilerParams(dimension_semantics=("arbitrary",) * 3, vmem_limit_bytes=VMEM_LIMIT),
        name="inproj",
    )(x, sc, sh, nw, w_cols)


def _gdn_kernel(qkv_ref, z_ref, b_ref, a_ref, c0_ref, cw_ref, alog_ref, dtb_ref, nw_ref, s0_ref,
                o_ref, s_ref, hist_scr, *, chunk, t_valid):
    bsz, cd, _ = qkv_ref.shape
    c = pl.program_id(1)
    hist_lo = SUBLANES - (CONV_W - 1)

    @pl.when(c == 0)
    def _():
        s_ref[...] = s0_ref[...]
        hist_scr[:, hist_lo:SUBLANES, :] = c0_ref[...]

    row = _iota2((chunk, chunk), 0)
    col = _iota2((chunk, chunk), 1)
    causal = row >= col
    strict = row > col

    def one_row(bi):
        u = qkv_ref[bi]
        hist_scr[bi, SUBLANES:SUBLANES + cd, :] = u
        y = u * cw_ref[CONV_W - 1:CONV_W, :]
        for i in range(CONV_W - 1):
            y = y + hist_scr[bi, hist_lo + i:hist_lo + i + cd, :] * cw_ref[i:i + 1, :]
        hist_scr[bi, hist_lo:SUBLANES, :] = hist_scr[bi, hist_lo + cd:SUBLANES + cd, :]
        qkv = _pad_rows(_silu(y), chunk)
        q, k, v = (qkv[:, i * D_MIX_HALF:(i + 1) * D_MIX_HALF] for i in range(3))
        q = q * lax.rsqrt(_head_sums(q * q) + EPS) * (HEAD_DIM ** -0.5)
        k = k * lax.rsqrt(_head_sums(k * k) + EPS)
        beta = _sigmoid(_pad_rows(b_ref[bi], chunk))
        g = -jnp.exp(alog_ref[...]) * _softplus(_pad_rows(a_ref[bi], chunk) + dtb_ref[...])
        if t_valid is not None:
            valid = (c * cd + _iota2((chunk, D_MIX_HALF), 0)) < t_valid
            q, k, v = (jnp.where(valid, x, 0.0) for x in (q, k, v))
            beta = jnp.where(valid, beta, 0.0)
            g = jnp.where(valid, g, 0.0)

        gcum = _cumsum_rows(g)
        eg = jnp.exp(gcum)
        g_last = gcum[chunk - 1:chunk, :]
        rhs_v = beta * v
        rhs_k = beta * eg * k
        q_dec = eg * q
        k_dec = jnp.exp(g_last - gcum) * k
        eg_last = jnp.exp(g_last)
        heads = range(N_HEADS)
        sls = [slice(h * HEAD_DIM, (h + 1) * HEAD_DIM) for h in heads]
        g_is = [gcum[:, h * HEAD_DIM:h * HEAD_DIM + chunk] for h in heads]
        decays = [jnp.where(causal, jnp.exp(jnp.minimum(g_i - g_i.T, 0.0)), 0.0) for g_i in g_is]
        qk_kks = [_mm_nt(jnp.concatenate([k[:, sl], q[:, sl]], axis=0), k[:, sl]) for sl in sls]
        lows = [jnp.where(strict, beta[:, h * HEAD_DIM:h * HEAD_DIM + chunk] * decays[h] * qk_kks[h][:chunk], 0.0)
                for h in heads]
        t_invs = _unit_lower_inverses(lows, chunk, _mm)
        sols = [_mm(t_invs[h], jnp.concatenate([rhs_v[:, sls[h]], rhs_k[:, sls[h]]], axis=1)) for h in heads]
        states = [s_ref[bi, h] for h in heads]
        wss = [_mm(jnp.concatenate([sols[h][:, HEAD_DIM:], q_dec[:, sls[h]]], axis=0), states[h])
               for h in heads]
        u_news = [sols[h][:, :HEAD_DIM] - wss[h][:chunk] for h in heads]
        outs = [wss[h][chunk:] + _mm(qk_kks[h][chunk:] * decays[h], u_news[h]) for h in heads]
        s_news = [eg_last[:, sls[h]] * states[h] + _mm_tn(k_dec[:, sls[h]], u_news[h]) for h in heads]
        for h in heads:
            s_ref[bi, h] = s_news[h]

        o = jnp.concatenate(outs, axis=1)
        o = o * lax.rsqrt(_head_sums(o * o) * (1.0 / HEAD_DIM) + EPS) * nw_ref[...]
        o = o * _silu(_pad_rows(z_ref[bi], chunk))
        o_ref[bi] = o[:cd]

    _for_rows(bsz, one_row)


def _gdn(cols, conv0, s0, conv_w, alog_rep, dtb_rep, nw_rep, bsz, cd, chunk, t_valid):
    b, t, _ = cols.shape

    def col_spec(width, start):
        return pl.BlockSpec((bsz, cd, width), lambda i, c: (i, c, start // width))

    vec_spec = pl.BlockSpec((1, D_MIX_HALF), lambda i, c: (0, 0))
    state_spec = pl.BlockSpec((bsz, N_HEADS, HEAD_DIM, HEAD_DIM), lambda i, c: (i, 0, 0, 0))
    kern = functools.partial(_gdn_kernel, chunk=chunk, t_valid=t_valid)
    return pl.pallas_call(
        kern,
        grid=(b // bsz, t // cd),
        in_specs=[col_spec(GDN_QKV, 0), col_spec(D_MIX_HALF, COL_Z), col_spec(D_MIX_HALF, COL_B),
                  col_spec(D_MIX_HALF, COL_A),
                  pl.BlockSpec((bsz, CONV_W - 1, GDN_QKV), lambda i, c: (i, 0, 0)),
                  pl.BlockSpec((CONV_W, GDN_QKV), lambda i, c: (0, 0)),
                  vec_spec, vec_spec, vec_spec, state_spec],
        out_specs=[pl.BlockSpec((bsz, cd, D_MIX_HALF), lambda i, c: (i, c, 0)), state_spec],
        out_shape=[jax.ShapeDtypeStruct((b, t, D_MIX_HALF), F32),
                   jax.ShapeDtypeStruct((b, N_HEADS, HEAD_DIM, HEAD_DIM), F32)],
        scratch_shapes=[pltpu.VMEM((bsz, SUBLANES + cd, GDN_QKV), F32)],
        compiler_params=pltpu.CompilerParams(dimension_semantics=("arbitrary",) * 2, vmem_limit_bytes=VMEM_LIMIT),
        name="gdn",
    )(cols, cols, cols, cols, conv0, conv_w, alog_rep, dtb_rep, nw_rep, s0)


def _rwkv_kernel(rkv_ref, l_ref, p_ref, pl_ref, mu_ref, mul_ref, w0_ref, a0_ref, kk_ref, ka_ref, rk_ref,
                 lnw_ref, lnb_ref, wup_ref, aup_ref, gup_ref, s0_ref, o_ref, s_ref, hist_scr, histl_scr,
                 *, chunk, t_valid):
    bsz, cd, _ = rkv_ref.shape
    c = pl.program_id(1)
    prev_row = SUBLANES - 1

    @pl.when(c == 0)
    def _():
        s_ref[...] = s0_ref[...]
        hist_scr[:, prev_row:SUBLANES, :] = p_ref[...]
        histl_scr[:, prev_row:SUBLANES, :] = pl_ref[...]

    row = _iota2((chunk, chunk), 0)
    col = _iota2((chunk, chunk), 1)
    causal = row >= col
    strict = row > col

    def shifted(bi, u_ref, scr, m_ref):
        u = u_ref[bi]
        scr[bi, SUBLANES:SUBLANES + cd, :] = u
        prev = scr[bi, prev_row:prev_row + cd, :]
        scr[bi, prev_row:SUBLANES, :] = scr[bi, prev_row + cd:SUBLANES + cd, :]
        return _pad_rows(u + (prev - u) * m_ref[...], chunk)

    def one_row(bi):
        rkv = shifted(bi, rkv_ref, hist_scr, mu_ref)
        r, k, v = (rkv[:, i * D_MIX_HALF:(i + 1) * D_MIX_HALF] for i in range(3))
        xl = shifted(bi, l_ref, histl_scr, mul_ref)
        dw = xl[:, :LORA_W]
        da = xl[:, LORA_W:LORA_W + LORA_A]
        dg = xl[:, LORA_W + LORA_A:]
        w_log = -_softplus(-(w0_ref[...] + _mm(jnp.tanh(dw), wup_ref[...]))) - 0.5
        lw = -jnp.exp(w_log)
        a = _sigmoid(a0_ref[...] + _mm(da, aup_ref[...]))
        gate = _mm(_sigmoid(dg), gup_ref[...])
        kk = k * kk_ref[...]
        kk = kk * lax.rsqrt(_head_sums(kk * kk) + EPS)
        k = k * (1.0 + (a - 1.0) * ka_ref[...])
        if t_valid is not None:
            valid = (c * cd + _iota2((chunk, D_MIX_HALF), 0)) < t_valid
            r, k, v, kk, lw = (jnp.where(valid, x, 0.0) for x in (r, k, v, kk, lw))

        cum = _cumsum_rows(lw)
        e_out = jnp.exp(-cum)
        a_t = -kk * jnp.exp(cum - lw)
        b_t = kk * a * e_out
        k_t = k * e_out
        r_t = r * jnp.exp(cum)
        cum_last = cum[chunk - 1:chunk, :]
        e_rest = jnp.exp(cum_last - cum)
        b_c = kk * a * e_rest
        k_c = k * e_rest
        e_last = jnp.exp(cum_last)
        heads = range(N_HEADS)
        sls = [slice(h * HEAD_DIM, (h + 1) * HEAD_DIM) for h in heads]
        ars = [jnp.concatenate([a_t[:, sl], r_t[:, sl]], axis=0) for sl in sls]
        bks = [jnp.concatenate([b_t[:, sl], k_t[:, sl]], axis=0) for sl in sls]
        ms = [_mm_nt(ars[h], bks[h]) for h in heads]
        t_invs = _unit_lower_inverses([jnp.where(strict, -m[:chunk, :chunk], 0.0) for m in ms], chunk, _mm)
        states = [s_ref[bi, h] for h in heads]
        arss = [_mm_nt(ars[h], states[h]) for h in heads]
        vhs = [v[:, sl] for sl in sls]
        akvs = [_mm(jnp.where(strict, ms[h][:chunk, chunk:], 0.0), vhs[h]) for h in heads]
        us = [_mm(t_invs[h], arss[h][:chunk] + akvs[h]) for h in heads]
        uvs = [jnp.concatenate([us[h], vhs[h]], axis=0) for h in heads]
        causal2 = _iota2((chunk, 2 * chunk), 0) >= (_iota2((chunk, 2 * chunk), 1) & (chunk - 1))
        ys = [arss[h][chunk:] + _mm(jnp.where(causal2, ms[h][chunk:], 0.0), uvs[h]) for h in heads]
        s_news = [states[h] * e_last[:, sls[h]]
                  + _mm_tn(uvs[h], jnp.concatenate([b_c[:, sls[h]], k_c[:, sls[h]]], axis=0)) for h in heads]
        for h in heads:
            s_ref[bi, h] = s_news[h]

        y = jnp.concatenate(ys, axis=1)
        mean = _head_sums(y) * (1.0 / HEAD_DIM)
        dy = y - mean
        var = _head_sums(dy * dy) * (1.0 / HEAD_DIM)
        y = dy * lax.rsqrt(var + GN_EPS) * lnw_ref[...] + lnb_ref[...]
        bonus = _head_sums(r * k * rk_ref[...]) * v
        o_ref[bi] = ((y + bonus) * gate)[:cd]

    _for_rows(bsz, one_row)


def _rwkv(cols, shift0, s0, mu, vecs, w_up, a_up, g_up, bsz, cd, chunk, t_valid):
    b, t, _ = cols.shape

    def full2(shape):
        return pl.BlockSpec(shape, lambda i, c: (0, 0))

    state_spec = pl.BlockSpec((bsz, N_HEADS, HEAD_DIM, HEAD_DIM), lambda i, c: (i, 0, 0, 0))
    kern = functools.partial(_rwkv_kernel, chunk=chunk, t_valid=t_valid)
    return pl.pallas_call(
        kern,
        grid=(b // bsz, t // cd),
        in_specs=[pl.BlockSpec((bsz, cd, RWKV_RKV), lambda i, c: (i, c, COL_RW // RWKV_RKV)),
                  pl.BlockSpec((bsz, cd, LORA_COLS), lambda i, c: (i, c, COL_LORA // LORA_COLS)),
                  pl.BlockSpec((bsz, 1, RWKV_RKV), lambda i, c: (i, 0, 0)),
                  pl.BlockSpec((bsz, 1, LORA_COLS), lambda i, c: (i, 0, RWKV_RKV // LORA_COLS)),
                  pl.BlockSpec((1, RWKV_RKV), lambda i, c: (0, 0)),
                  pl.BlockSpec((1, LORA_COLS), lambda i, c: (0, RWKV_RKV // LORA_COLS))]
                 + [full2((1, D_MIX_HALF))] * 7
                 + [full2((LORA_W, D_MIX_HALF)), full2((LORA_A, D_MIX_HALF)), full2((LORA_G, D_MIX_HALF)),
                    state_spec],
        out_specs=[pl.BlockSpec((bsz, cd, D_MIX_HALF), lambda i, c: (i, c, 0)), state_spec],
        out_shape=[jax.ShapeDtypeStruct((b, t, D_MIX_HALF), F32),
                   jax.ShapeDtypeStruct((b, N_HEADS, HEAD_DIM, HEAD_DIM), F32)],
        scratch_shapes=[pltpu.VMEM((bsz, SUBLANES + cd, RWKV_RKV), F32),
                        pltpu.VMEM((bsz, SUBLANES + cd, LORA_COLS), F32)],
        compiler_params=pltpu.CompilerParams(dimension_semantics=("arbitrary",) * 2, vmem_limit_bytes=VMEM_LIMIT),
        name="rwkv",
    )(cols, cols, shift0, shift0, mu, mu, *vecs, w_up, a_up, g_up, s0)


def _outproj_kernel(oa_ref, ob_ref, x_ref, gt_ref, sc_ref, sh_ref, nw_ref, wa_ref, wb_ref, wr_ref, br_ref,
                    x1_ref, h_ref, wt_ref):
    bb, tt, d = x_ref.shape
    n = bb * tt
    oa = oa_ref[...].reshape(n, D_MIX_HALF).astype(BF16)
    ob = ob_ref[...].reshape(n, D_MIX_HALF).astype(BF16)
    mixed = (jnp.dot(oa, wa_ref[...], preferred_element_type=F32)
             + jnp.dot(ob, wb_ref[...], preferred_element_type=F32))
    x1 = x_ref[...] + gt_ref[...] * mixed.reshape(bb, tt, d)
    x1_ref[...] = x1
    h = _modulated_norm(x1, nw_ref[...], sc_ref[...], sh_ref[...]).reshape(n, d)
    h_ref[...] = h.reshape(bb, tt, d).astype(BF16)

    logits = _dot(h, wr_ref[...]) + br_ref[...]
    lane_i = _iota2((n, LANES), 1)
    lane = lane_i.astype(F32)
    lane_grp = (lane_i >> (EXP_PER_GROUP.bit_length() - 1)).astype(F32)
    neg = jnp.float32(-jnp.inf)
    big = jnp.float32(LANES)
    is_grp = (lane_i >= N_EXPERTS) & (lane_i < N_EXPERTS + N_GROUPS)
    gl = jnp.where(is_grp, logits, neg)
    gmax = jnp.max(gl, axis=-1, keepdims=True)
    grp = jnp.min(jnp.where(gl == gmax, lane, big), axis=-1, keepdims=True) - N_EXPERTS
    g_prob = 1.0 / jnp.sum(jnp.where(is_grp, jnp.exp(gl - gmax), 0.0), axis=-1, keepdims=True)
    in_grp = (lane_i < N_EXPERTS) & (lane_grp == grp)
    el = jnp.where(in_grp, logits, neg)
    m1 = jnp.max(el, axis=-1, keepdims=True)
    i1 = jnp.min(jnp.where(el == m1, lane, big), axis=-1, keepdims=True)
    el2 = jnp.where(lane == i1, neg, el)
    m2 = jnp.max(el2, axis=-1, keepdims=True)
    i2 = jnp.min(jnp.where(el2 == m2, lane, big), axis=-1, keepdims=True)
    e2 = jnp.exp(m2 - m1)
    p1 = g_prob / (1.0 + e2)
    p2 = g_prob * e2 / (1.0 + e2)
    wt = jnp.where(lane == i1, p1, 0.0) + jnp.where(lane == i2, p2, 0.0)
    wt_ref[...] = wt.reshape(bb, tt, LANES)


def _outproj(oa, ob, x, gt, sc, sh, nw, wa, wb, wr, br, bb, tt):
    b, t, d = x.shape
    grid = (b // bb, t // tt)

    def tok_spec(w):
        return pl.BlockSpec((bb, tt, w), lambda i, j: (i, j, 0))

    mod_spec = pl.BlockSpec((bb, 1, d), lambda i, j: (i, 0, 0))

    def full2(shape):
        return pl.BlockSpec(shape, lambda i, j: (0, 0))

    return pl.pallas_call(
        _outproj_kernel,
        grid=grid,
        in_specs=[tok_spec(D_MIX_HALF), tok_spec(D_MIX_HALF), tok_spec(d), mod_spec, mod_spec, mod_spec,
                  pl.BlockSpec((1, 1, d), lambda i, j: (0, 0, 0)),
                  full2((D_MIX_HALF, d)), full2((D_MIX_HALF, d)), full2((d, LANES)), full2((1, LANES))],
        out_specs=[tok_spec(d), tok_spec(d), tok_spec(LANES)],
        out_shape=[jax.ShapeDtypeStruct((b, t, d), F32), jax.ShapeDtypeStruct((b, t, d), BF16),
                   jax.ShapeDtypeStruct((b, t, LANES), F32)],
        compiler_params=pltpu.CompilerParams(dimension_semantics=("arbitrary",) * 2, vmem_limit_bytes=VMEM_LIMIT),
        name="outproj",
    )(oa, ob, x, gt, sc, sh, nw, wa, wb, wr, br)


def _moe_kernel(h_ref, wt_ref, x1_ref, gt_ref, fw_ref, wgu_ref, wd_ref, y_ref, acc_scr):
    bb, tt, d = x1_ref.shape
    n = bb * tt
    e = pl.program_id(2)

    @pl.when(e == 0)
    def _():
        acc_scr[...] = jnp.zeros_like(acc_scr)

    h = h_ref[...].reshape(n, d)
    gu = jnp.dot(h, wgu_ref[0], preferred_element_type=F32)
    act = (_silu(gu[:, :D_EXPERT]) * gu[:, D_EXPERT:]).astype(BF16)
    y = jnp.dot(act, wd_ref[0], preferred_element_type=F32)
    wt = wt_ref[...].reshape(n, LANES)
    w_e = jnp.sum(jnp.where(_iota2((n, LANES), 1) == e, wt, 0.0), axis=-1, keepdims=True)
    acc_scr[...] += w_e * y

    @pl.when(e == pl.num_programs(2) - 1)
    def _():
        x2 = x1_ref[...] + gt_ref[...] * acc_scr[...].reshape(bb, tt, d)
        ms = jnp.mean(x2 * x2, axis=-1, keepdims=True)
        y_ref[...] = x2 * lax.rsqrt(ms + EPS) * fw_ref[...]


def _moe(h, wt, x1, gt, fw, w_gu, w_down, bb, tt):
    b, t, d = x1.shape
    grid = (b // bb, t // tt, N_EXPERTS)

    def tok_spec(w):
        return pl.BlockSpec((bb, tt, w), lambda i, j, e: (i, j, 0))

    return pl.pallas_call(
        _moe_kernel,
        grid=grid,
        in_specs=[tok_spec(d), tok_spec(LANES), tok_spec(d),
                  pl.BlockSpec((bb, 1, d), lambda i, j, e: (i, 0, 0)),
                  pl.BlockSpec((1, 1, d), lambda i, j, e: (0, 0, 0)),
                  pl.BlockSpec((1, d, 2 * D_EXPERT), lambda i, j, e: (e, 0, 0)),
                  pl.BlockSpec((1, D_EXPERT, d), lambda i, j, e: (e, 0, 0))],
        out_specs=tok_spec(d),
        out_shape=jax.ShapeDtypeStruct((b, t, d), F32),
        scratch_shapes=[pltpu.VMEM((bb * tt, d), F32)],
        compiler_params=pltpu.CompilerParams(dimension_semantics=("arbitrary",) * 3, vmem_limit_bytes=VMEM_LIMIT),
        name="moe",
    )(h, wt, x1, gt, fw, w_gu, w_down)


def _layer(x, mod, conv0, s_gdn0, shift0, s_rwkv0, p, *, t_valid, bb, tt, bsz, cd, chunk):
    sh_m, sc_m, gt_m, sh_f, sc_f, gt_f = mod
    cols = _inproj(x, sc_m, sh_m, p['norm_mix_w'], p['w_cols'], bb, tt)
    o_a, s_gdn = _gdn(cols, conv0, s_gdn0, p['gdn_conv_w'], p['alog_rep'], p['dtb_rep'], p['gdn_nw_rep'],
                      bsz, cd, chunk, t_valid)
    o_b, s_rwkv = _rwkv(cols, shift0, s_rwkv0, p['rwkv_mu'], p['rwkv_vecs'], p['rwkv_w_up'], p['rwkv_a_up'],
                        p['rwkv_g_up'], bsz, cd, chunk, t_valid)
    x1, h2, wt = _outproj(o_a, o_b, x, gt_m, sc_f, sh_f, p['norm_ffn_w'], p['w_out_a'], p['w_out_b'],
                          p['w_router'], p['b_router'], bb, tt)
    y = _moe(h2, wt, x1, gt_f, p['final_norm_w'], p['w_gu'], p['w_down'], bb, tt)
    t_last = x.shape[1] if t_valid is None else t_valid
    conv_new = cols[:, t_last - (CONV_W - 1):t_last, :GDN_QKV]
    shift_new = cols[:, t_last - 1, COL_RW:]
    return y, conv_new, s_gdn, shift_new, s_rwkv


def kernel(x_prompt, x_sample, state_gdn_conv, state_gdn, state_rwkv_shift, state_rwkv, c_prompt, c_sample, ada_w, ada_b, norm_mix_w, w_in, gdn_conv_w, gdn_a_log, gdn_dt_bias, gdn_norm_w, rwkv_mu, rwkv_w0, rwkv_w_up, rwkv_a0, rwkv_a_up, rwkv_g_up, rwkv_k_k, rwkv_k_a, rwkv_r_k, rwkv_ln_w, rwkv_ln_b, w_out, norm_ffn_w, router_group_w, router_group_b, router_expert_w, router_expert_b, expert_w_gate_up, expert_w_down, final_norm_w):
    depth = ada_w.shape[0]
    assert depth == 1
    l = 0
    b_p, t_p, d = x_prompt.shape
    b_s, t_s, _ = x_sample.shape
    gdn_cols = GDN_QKV + D_MIX_HALF + 2 * N_HEADS

    w = w_in[l]
    rep = jnp.repeat(jnp.arange(N_HEADS), HEAD_DIM)
    beta0 = GDN_QKV + D_MIX_HALF
    w_cols = jnp.concatenate([w[:, :beta0], w[:, beta0 + rep], w[:, beta0 + N_HEADS + rep], w[:, gdn_cols:]],
                             axis=1).astype(BF16)
    row = lambda a: a.reshape(1, -1)
    w_router = jnp.zeros((d, LANES), F32)
    w_router = w_router.at[:, :N_EXPERTS].set(router_expert_w[l]).at[:, N_EXPERTS:N_EXPERTS + N_GROUPS].set(
        router_group_w[l])
    b_router = jnp.zeros((1, LANES), F32)
    b_router = b_router.at[0, :N_EXPERTS].set(router_expert_b[l]).at[0, N_EXPERTS:N_EXPERTS + N_GROUPS].set(
        router_group_b[l])
    p = {
        'norm_mix_w': norm_mix_w[l].reshape(1, 1, d), 'w_cols': w_cols,
        'gdn_conv_w': gdn_conv_w[l], 'alog_rep': row(gdn_a_log[l][rep]), 'dtb_rep': row(gdn_dt_bias[l][rep]),
        'gdn_nw_rep': row(jnp.tile(gdn_norm_w[l], N_HEADS)),
        'rwkv_mu': row(rwkv_mu[l]),
        'rwkv_vecs': tuple(row(a) for a in (rwkv_w0[l], rwkv_a0[l], rwkv_k_k[l], rwkv_k_a[l], rwkv_r_k[l],
                                            rwkv_ln_w[l], rwkv_ln_b[l])),
        'rwkv_w_up': rwkv_w_up[l], 'rwkv_a_up': rwkv_a_up[l], 'rwkv_g_up': rwkv_g_up[l],
        'w_out_a': w_out[l][:D_MIX_HALF].astype(BF16), 'w_out_b': w_out[l][D_MIX_HALF:].astype(BF16),
        'norm_ffn_w': norm_ffn_w[l].reshape(1, 1, d), 'w_router': w_router, 'b_router': b_router,
        'w_gu': expert_w_gate_up[l].astype(BF16), 'w_down': expert_w_down[l].astype(BF16),
        'final_norm_w': final_norm_w.reshape(1, 1, d),
    }

    mod = _ada(jnp.concatenate([c_prompt, c_sample], axis=0), ada_w[l], ada_b[l])
    mod_p = tuple(m.reshape(b_p, 1, d) for m in jnp.split(mod[:b_p], 6, axis=-1))
    mod_s = tuple(m.reshape(b_s, 1, d) for m in jnp.split(mod[b_p:], 6, axis=-1))

    zc = jnp.zeros((b_p, CONV_W - 1, GDN_QKV), F32)
    zs = jnp.zeros((b_p, N_HEADS, HEAD_DIM, HEAD_DIM), F32)
    zsh = jnp.zeros((b_p, 1, RWKV_COLS), F32)
    y_p, conv_p, sg_p, shift_p, sr_p = _layer(x_prompt, mod_p, zc, zs, zsh, zs, p, t_valid=None,
                                              bb=1, tt=512, bsz=1, cd=CHUNK, chunk=CHUNK)

    t_pad = SUBLANES
    x_s = jnp.pad(x_sample, ((0, 0), (0, t_pad - t_s), (0, 0)))
    y_s, conv_s, sg_s, shift_s, sr_s = _layer(x_s, mod_s, state_gdn_conv[l], state_gdn[l],
                                              state_rwkv_shift[l].reshape(b_s, 1, RWKV_COLS), state_rwkv[l], p,
                                              t_valid=t_s, bb=64, tt=t_pad, bsz=8, cd=t_pad, chunk=t_pad)
    y_s = y_s[:, :t_s]
    return (y_p, y_s, conv_p[None], sg_p[None], shift_p[None], sr_p[None],
            conv_s[None], sg_s[None], shift_s[None], sr_s[None])
```

```python
import functools

import jax
import jax.numpy as jnp
from jax import lax
from jax.experimental import pallas as pl
from jax.experimental.pallas import tpu as pltpu

F32 = jnp.float32
BF16 = jnp.bfloat16
HI = lax.Precision.HIGHEST

D_MODEL = 1024
HEAD_DIM = 64
N_HEADS = 8
D_MIX_HALF = N_HEADS * HEAD_DIM
CONV_W = 4
LORA_W = 64
LORA_A = 64
LORA_G = 128
N_GROUPS = 4
EXP_PER_GROUP = 8
N_EXPERTS = N_GROUPS * EXP_PER_GROUP
D_EXPERT = 256
EPS = 1e-6
GN_EPS = HEAD_DIM * 1e-5

LANES = 128
SUBLANES = 8
GDN_QKV = 3 * D_MIX_HALF
RWKV_RKV = 3 * D_MIX_HALF
RWKV_COLS = RWKV_RKV + LORA_W + LORA_A + LORA_G
LORA_COLS = LORA_W + LORA_A + LORA_G

COL_Z = GDN_QKV
COL_B = COL_Z + D_MIX_HALF
COL_A = COL_B + D_MIX_HALF
COL_RW = COL_A + D_MIX_HALF
COL_LORA = COL_RW + RWKV_RKV
N_COLS = COL_LORA + LORA_COLS
CHUNK = 64

VMEM_LIMIT = 48 * 1024 * 1024


def _dot(a, b, prec=HI):
    return jnp.dot(a, b, preferred_element_type=F32, precision=prec)


def _bf(a):
    return a.astype(BF16)


def _mm(a, b):
    return jnp.dot(_bf(a), _bf(b), preferred_element_type=F32)


def _mm_nt(a, b):
    return lax.dot_general(_bf(a), _bf(b), (((1,), (1,)), ((), ())), preferred_element_type=F32)


def _mm_tn(a, b):
    return lax.dot_general(_bf(a), _bf(b), (((0,), (0,)), ((), ())), preferred_element_type=F32)


def _split2(a):
    hi = a.astype(BF16)
    return hi, (a - hi.astype(F32)).astype(BF16)


def _sigmoid(x):
    return 1.0 / (1.0 + jnp.exp(-x))


def _silu(x):
    return x * _sigmoid(x)


def _softplus(x):
    return jnp.maximum(x, 0.0) + jnp.log1p(jnp.exp(-jnp.abs(x)))


def _iota2(shape, dim):
    return lax.broadcasted_iota(jnp.int32, shape, dim)


def _head_block_ones():
    r = _iota2((LANES, LANES), 0)
    c = _iota2((LANES, LANES), 1)
    sh = HEAD_DIM.bit_length() - 1
    return jnp.where((r >> sh) == (c >> sh), 1.0, 0.0).astype(F32)


def _head_sums(x):
    ones = _bf(_head_block_ones())
    hi, lo = _split2(x)
    parts = []
    for p in range(x.shape[1] // LANES):
        s = slice(p * LANES, (p + 1) * LANES)
        parts.append(jnp.dot(hi[:, s], ones, preferred_element_type=F32)
                     + jnp.dot(lo[:, s], ones, preferred_element_type=F32))
    return jnp.concatenate(parts, axis=1)


def _cumsum_rows(x):
    n = x.shape[0]
    r = _iota2((n, n), 0)
    c = _iota2((n, n), 1)
    tri = jnp.where(r >= c, 1.0, 0.0).astype(BF16)
    x1 = x.astype(BF16)
    rem = x - x1.astype(F32)
    x2 = rem.astype(BF16)
    x3 = (rem - x2.astype(F32)).astype(BF16)
    return (jnp.dot(tri, x1, preferred_element_type=F32) + jnp.dot(tri, x2, preferred_element_type=F32)
            + jnp.dot(tri, x3, preferred_element_type=F32))


def _unit_lower_inverses(lows, n, mm):
    r = _iota2((n, n), 0)
    c = _iota2((n, n), 1)
    eye = jnp.where(r == c, 1.0, 0.0)
    pair = (r >> 1) == (c >> 1)
    invs = [eye - jnp.where(pair, low, 0.0) for low in lows]
    s = 2
    while s < n:
        sh = s.bit_length()
        sel = ((r >> sh) == (c >> sh)) & ((r & (2 * s - 1)) >= s) & ((c & (2 * s - 1)) < s)
        prods = [mm(jnp.where(sel, low, 0.0), inv) for low, inv in zip(lows, invs)]
        invs = [inv - mm(inv, prod) for inv, prod in zip(invs, prods)]
        s *= 2
    return invs


def _pad_rows(a, n):
    if a.shape[0] == n:
        return a
    return jnp.concatenate([a, jnp.zeros((n - a.shape[0], a.shape[1]), a.dtype)], axis=0)


def _for_row_groups(n, group, body):
    if n == group:
        body(list(range(n)))
    else:
        def step(i, carry):
            body([i * group + j for j in range(group)])
            return carry
        lax.fori_loop(0, n // group, step, 0)


def _ada_kernel(c_ref, w_ref, b_ref, o_ref):
    o_ref[...] = _dot(_silu(c_ref[...]), w_ref[...]) + b_ref[...]


def _ada(c_all, ada_w, ada_b):
    n, d = c_all.shape
    nout = ada_w.shape[1]
    tn = 1536
    return pl.pallas_call(
        _ada_kernel,
        grid=(nout // tn,),
        in_specs=[pl.BlockSpec((n, d), lambda j: (0, 0)),
                  pl.BlockSpec((d, tn), lambda j: (0, j)),
                  pl.BlockSpec((1, tn), lambda j: (0, j))],
        out_specs=pl.BlockSpec((n, tn), lambda j: (0, j)),
        out_shape=jax.ShapeDtypeStruct((n, nout), F32),
        compiler_params=pltpu.CompilerParams(dimension_semantics=("arbitrary",), vmem_limit_bytes=VMEM_LIMIT),
        name="ada",
    )(c_all, ada_w, ada_b.reshape(1, nout))


def _modulated_norm(x, nw, sc, sh):
    ms = jnp.mean(x * x, axis=-1, keepdims=True)
    return (x * lax.rsqrt(ms + EPS) * nw) * (1.0 + sc) + sh


def _inproj_kernel(x_ref, sc_ref, sh_ref, nw_ref, w_ref, o_ref):
    bb, tt, d = x_ref.shape
    h = _modulated_norm(x_ref[...], nw_ref[...], sc_ref[...], sh_ref[...])
    h = h.reshape(bb * tt, d).astype(BF16)
    o = jnp.dot(h, w_ref[...], preferred_element_type=F32)
    o_ref[...] = o.reshape(bb, tt, o.shape[-1])


def _inproj(x, sc, sh, nw, w_cols, bb, tt):
    b, t, d = x.shape
    tn = N_COLS // 2
    grid = (2, b // bb, t // tt)
    return pl.pallas_call(
        _inproj_kernel,
        grid=grid,
        in_specs=[pl.BlockSpec((bb, tt, d), lambda n, i, j: (i, j, 0)),
                  pl.BlockSpec((bb, 1, d), lambda n, i, j: (i, 0, 0)),
                  pl.BlockSpec((bb, 1, d), lambda n, i, j: (i, 0, 0)),
                  pl.BlockSpec((1, 1, d), lambda n, i, j: (0, 0, 0)),
                  pl.BlockSpec((d, tn), lambda n, i, j: (0, n))],
        out_specs=pl.BlockSpec((bb, tt, tn), lambda n, i, j: (i, j, n)),
        out_shape=jax.ShapeDtypeStruct((b, t, N_COLS), F32),
        compiler_params=pltpu.CompilerParams(dimension_semantics=("arbitrary",) * 3, vmem_limit_bytes=VMEM_LIMIT),
        name="inproj",
    )(x, sc, sh, nw, w_cols)


def _gdn_kernel(qkv_ref, z_ref, b_ref, a_ref, c0_ref, cw_ref, alog_ref, dtb_ref, nw_ref, s0_ref,
                o_ref, s_ref, hist_scr, *, chunk, t_valid, rows_per_group):
    bsz, cd, _ = qkv_ref.shape
    c = pl.program_id(1)
    hist_lo = SUBLANES - (CONV_W - 1)

    @pl.when(c == 0)
    def _():
        s_ref[...] = s0_ref[...]
        hist_scr[:, hist_lo:SUBLANES, :] = c0_ref[...]

    row = _iota2((chunk, chunk), 0)
    col = _iota2((chunk, chunk), 1)
    causal = row >= col
    strict = row > col

    def prep(bi):
        u = qkv_ref[bi]
        hist_scr[bi, SUBLANES:SUBLANES + cd, :] = u
        y = u * cw_ref[CONV_W - 1:CONV_W, :]
        for i in range(CONV_W - 1):
            y = y + hist_scr[bi, hist_lo + i:hist_lo + i + cd, :] * cw_ref[i:i + 1, :]
        hist_scr[bi, hist_lo:SUBLANES, :] = hist_scr[bi, hist_lo + cd:SUBLANES + cd, :]
        qkv = _pad_rows(_silu(y), chunk)
        q, k, v = (qkv[:, i * D_MIX_HALF:(i + 1) * D_MIX_HALF] for i in range(3))
        q = q * lax.rsqrt(_head_sums(q * q) + EPS) * (HEAD_DIM ** -0.5)
        k = k * lax.rsqrt(_head_sums(k * k) + EPS)
        beta = _sigmoid(_pad_rows(b_ref[bi], chunk))
        g = -jnp.exp(alog_ref[...]) * _softplus(_pad_rows(a_ref[bi], chunk) + dtb_ref[...])
        if t_valid is not None:
            valid = (c * cd + _iota2((chunk, D_MIX_HALF), 0)) < t_valid
            q, k, v = (jnp.where(valid, x, 0.0) for x in (q, k, v))
            beta = jnp.where(valid, beta, 0.0)
            g = jnp.where(valid, g, 0.0)
        gcum = _cumsum_rows(g)
        eg = jnp.exp(gcum)
        g_last = gcum[chunk - 1:chunk, :]
        return dict(q=q, k=k, beta=beta, gcum=gcum, rhs_v=beta * v, rhs_k=beta * eg * k, q_dec=eg * q,
                    k_dec=jnp.exp(g_last - gcum) * k, eg_last=jnp.exp(g_last))

    def group(bis):
        pre = [prep(bi) for bi in bis]
        chains = [(r, h) for r in range(len(bis)) for h in range(N_HEADS)]
        n = range(len(chains))

        def head(name, i, width=HEAD_DIM):
            r, h = chains[i]
            return pre[r][name][:, h * HEAD_DIM:h * HEAD_DIM + width]

        g_is = [head('gcum', i, chunk) for i in n]
        decays = [jnp.where(causal, jnp.exp(jnp.minimum(g_i - g_i.T, 0.0)), 0.0) for g_i in g_is]
        qk_kks = [_mm_nt(jnp.concatenate([head('k', i), head('q', i)], axis=0), head('k', i)) for i in n]
        lows = [jnp.where(strict, head('beta', i, chunk) * decays[i] * qk_kks[i][:chunk], 0.0) for i in n]
        t_invs = _unit_lower_inverses(lows, chunk, _mm)
        sols = [_mm(t_invs[i], jnp.concatenate([head('rhs_v', i), head('rhs_k', i)], axis=1)) for i in n]
        states = [s_ref[bis[r], h] for r, h in chains]
        wss = [_mm(jnp.concatenate([sols[i][:, HEAD_DIM:], head('q_dec', i)], axis=0), states[i])
               for i in n]
        u_news = [sols[i][:, :HEAD_DIM] - wss[i][:chunk] for i in n]
        outs = [wss[i][chunk:] + _mm(qk_kks[i][chunk:] * decays[i], u_news[i]) for i in n]
        s_news = [head('eg_last', i) * states[i] + _mm_tn(head('k_dec', i), u_news[i]) for i in n]
        for i, (r, h) in enumerate(chains):
            s_ref[bis[r], h] = s_news[i]
        for r, bi in enumerate(bis):
            o = jnp.concatenate(outs[r * N_HEADS:(r + 1) * N_HEADS], axis=1)
            o = o * lax.rsqrt(_head_sums(o * o) * (1.0 / HEAD_DIM) + EPS) * nw_ref[...]
            o = o * _silu(_pad_rows(z_ref[bi], chunk))
            o_ref[bi] = o[:cd]

    _for_row_groups(bsz, rows_per_group, group)


def _gdn(cols, conv0, s0, conv_w, alog_rep, dtb_rep, nw_rep, bsz, rows_per_group, cd, chunk, t_valid):
    b, t, _ = cols.shape

    def col_spec(width, start):
        return pl.BlockSpec((bsz, cd, width), lambda i, c: (i, c, start // width))

    vec_spec = pl.BlockSpec((1, D_MIX_HALF), lambda i, c: (0, 0))
    state_spec = pl.BlockSpec((bsz, N_HEADS, HEAD_DIM, HEAD_DIM), lambda i, c: (i, 0, 0, 0))
    kern = functools.partial(_gdn_kernel, chunk=chunk, t_valid=t_valid, rows_per_group=rows_per_group)
    return pl.pallas_call(
        kern,
        grid=(b // bsz, t // cd),
        in_specs=[col_spec(GDN_QKV, 0), col_spec(D_MIX_HALF, COL_Z), col_spec(D_MIX_HALF, COL_B),
                  col_spec(D_MIX_HALF, COL_A),
                  pl.BlockSpec((bsz, CONV_W - 1, GDN_QKV), lambda i, c: (i, 0, 0)),
                  pl.BlockSpec((CONV_W, GDN_QKV), lambda i, c: (0, 0)),
                  vec_spec, vec_spec, vec_spec, state_spec],
        out_specs=[pl.BlockSpec((bsz, cd, D_MIX_HALF), lambda i, c: (i, c, 0)), state_spec],
        out_shape=[jax.ShapeDtypeStruct((b, t, D_MIX_HALF), F32),
                   jax.ShapeDtypeStruct((b, N_HEADS, HEAD_DIM, HEAD_DIM), F32)],
        scratch_shapes=[pltpu.VMEM((bsz, SUBLANES + cd, GDN_QKV), F32)],
        compiler_params=pltpu.CompilerParams(dimension_semantics=("arbitrary",) * 2, vmem_limit_bytes=VMEM_LIMIT),
        name="gdn",
    )(cols, cols, cols, cols, conv0, conv_w, alog_rep, dtb_rep, nw_rep, s0)


def _rwkv_kernel(rkv_ref, l_ref, p_ref, pl_ref, mu_ref, mul_ref, w0_ref, a0_ref, kk_ref, ka_ref, rk_ref,
                 lnw_ref, lnb_ref, wup_ref, aup_ref, gup_ref, s0_ref, o_ref, s_ref, hist_scr, histl_scr,
                 *, chunk, t_valid, rows_per_group):
    bsz, cd, _ = rkv_ref.shape
    c = pl.program_id(1)
    prev_row = SUBLANES - 1

    @pl.when(c == 0)
    def _():
        s_ref[...] = s0_ref[...]
        hist_scr[:, prev_row:SUBLANES, :] = p_ref[...]
        histl_scr[:, prev_row:SUBLANES, :] = pl_ref[...]

    row = _iota2((chunk, chunk), 0)
    col = _iota2((chunk, chunk), 1)
    causal = row >= col
    strict = row > col

    def shifted(bi, u_ref, scr, m_ref):
        u = u_ref[bi]
        scr[bi, SUBLANES:SUBLANES + cd, :] = u
        prev = scr[bi, prev_row:prev_row + cd, :]
        scr[bi, prev_row:SUBLANES, :] = scr[bi, prev_row + cd:SUBLANES + cd, :]
        return _pad_rows(u + (prev - u) * m_ref[...], chunk)

    def prep(bi):
        rkv = shifted(bi, rkv_ref, hist_scr, mu_ref)
        r, k, v = (rkv[:, i * D_MIX_HALF:(i + 1) * D_MIX_HALF] for i in range(3))
        xl = shifted(bi, l_ref, histl_scr, mul_ref)
        dw = xl[:, :LORA_W]
        da = xl[:, LORA_W:LORA_W + LORA_A]
        dg = xl[:, LORA_W + LORA_A:]
        w_log = -_softplus(-(w0_ref[...] + _mm(jnp.tanh(dw), wup_ref[...]))) - 0.5
        lw = -jnp.exp(w_log)
        a = _sigmoid(a0_ref[...] + _mm(da, aup_ref[...]))
        gate = _mm(_sigmoid(dg), gup_ref[...])
        kk = k * kk_ref[...]
        kk = kk * lax.rsqrt(_head_sums(kk * kk) + EPS)
        k = k * (1.0 + (a - 1.0) * ka_ref[...])
        if t_valid is not None:
            valid = (c * cd + _iota2((chunk, D_MIX_HALF), 0)) < t_valid
            r, k, v, kk, lw = (jnp.where(valid, x, 0.0) for x in (r, k, v, kk, lw))
        cum = _cumsum_rows(lw)
        e_out = jnp.exp(-cum)
        cum_last = cum[chunk - 1:chunk, :]
        e_rest = jnp.exp(cum_last - cum)
        return dict(r=r, k=k, v=v, gate=gate, a_t=-kk * jnp.exp(cum - lw), b_t=kk * a * e_out, k_t=k * e_out,
                    r_t=r * jnp.exp(cum), b_c=kk * a * e_rest, k_c=k * e_rest, e_last=jnp.exp(cum_last))

    def group(bis):
        pre = [prep(bi) for bi in bis]
        chains = [(r, h) for r in range(len(bis)) for h in range(N_HEADS)]
        n = range(len(chains))

        def head(name, i):
            r, h = chains[i]
            return pre[r][name][:, h * HEAD_DIM:(h + 1) * HEAD_DIM]

        ars = [jnp.concatenate([head('a_t', i), head('r_t', i)], axis=0) for i in n]
        bks = [jnp.concatenate([head('b_t', i), head('k_t', i)], axis=0) for i in n]
        ms = [_mm_nt(ars[i], bks[i]) for i in n]
        t_invs = _unit_lower_inverses([jnp.where(strict, -m[:chunk, :chunk], 0.0) for m in ms], chunk, _mm)
        states = [s_ref[bis[r], h] for r, h in chains]
        arss = [_mm_nt(ars[i], states[i]) for i in n]
        vhs = [head('v', i) for i in n]
        akvs = [_mm(jnp.where(strict, ms[i][:chunk, chunk:], 0.0), vhs[i]) for i in n]
        us = [_mm(t_invs[i], arss[i][:chunk] + akvs[i]) for i in n]
        uvs = [jnp.concatenate([us[i], vhs[i]], axis=0) for i in n]
        causal2 = _iota2((chunk, 2 * chunk), 0) >= (_iota2((chunk, 2 * chunk), 1) & (chunk - 1))
        ys = [arss[i][chunk:] + _mm(jnp.where(causal2, ms[i][chunk:], 0.0), uvs[i]) for i in n]
        s_news = [states[i] * head('e_last', i)
                  + _mm_tn(uvs[i], jnp.concatenate([head('b_c', i), head('k_c', i)], axis=0)) for i in n]
        for i, (r, h) in enumerate(chains):
            s_ref[bis[r], h] = s_news[i]
        for r, bi in enumerate(bis):
            p = pre[r]
            y = jnp.concatenate(ys[r * N_HEADS:(r + 1) * N_HEADS], axis=1)
            mean = _head_sums(y) * (1.0 / HEAD_DIM)
            dy = y - mean
            var = _head_sums(dy * dy) * (1.0 / HEAD_DIM)
            y = dy * lax.rsqrt(var + GN_EPS) * lnw_ref[...] + lnb_ref[...]
            bonus = _head_sums(p['r'] * p['k'] * rk_ref[...]) * p['v']
            o_ref[bi] = ((y + bonus) * p['gate'])[:cd]

    _for_row_groups(bsz, rows_per_group, group)


def _rwkv(cols, shift0, s0, mu, vecs, w_up, a_up, g_up, bsz, rows_per_group, cd, chunk, t_valid):
    b, t, _ = cols.shape

    def full2(shape):
        return pl.BlockSpec(shape, lambda i, c: (0, 0))

    state_spec = pl.BlockSpec((bsz, N_HEADS, HEAD_DIM, HEAD_DIM), lambda i, c: (i, 0, 0, 0))
    kern = functools.partial(_rwkv_kernel, chunk=chunk, t_valid=t_valid, rows_per_group=rows_per_group)
    return pl.pallas_call(
        kern,
        grid=(b // bsz, t // cd),
        in_specs=[pl.BlockSpec((bsz, cd, RWKV_RKV), lambda i, c: (i, c, COL_RW // RWKV_RKV)),
                  pl.BlockSpec((bsz, cd, LORA_COLS), lambda i, c: (i, c, COL_LORA // LORA_COLS)),
                  pl.BlockSpec((bsz, 1, RWKV_RKV), lambda i, c: (i, 0, 0)),
                  pl.BlockSpec((bsz, 1, LORA_COLS), lambda i, c: (i, 0, RWKV_RKV // LORA_COLS)),
                  pl.BlockSpec((1, RWKV_RKV), lambda i, c: (0, 0)),
                  pl.BlockSpec((1, LORA_COLS), lambda i, c: (0, RWKV_RKV // LORA_COLS))]
                 + [full2((1, D_MIX_HALF))] * 7
                 + [full2((LORA_W, D_MIX_HALF)), full2((LORA_A, D_MIX_HALF)), full2((LORA_G, D_MIX_HALF)),
                    state_spec],
        out_specs=[pl.BlockSpec((bsz, cd, D_MIX_HALF), lambda i, c: (i, c, 0)), state_spec],
        out_shape=[jax.ShapeDtypeStruct((b, t, D_MIX_HALF), F32),
                   jax.ShapeDtypeStruct((b, N_HEADS, HEAD_DIM, HEAD_DIM), F32)],
        scratch_shapes=[pltpu.VMEM((bsz, SUBLANES + cd, RWKV_RKV), F32),
                        pltpu.VMEM((bsz, SUBLANES + cd, LORA_COLS), F32)],
        compiler_params=pltpu.CompilerParams(dimension_semantics=("arbitrary",) * 2, vmem_limit_bytes=VMEM_LIMIT),
        name="rwkv",
    )(cols, cols, shift0, shift0, mu, mu, *vecs, w_up, a_up, g_up, s0)


def _outproj_kernel(oa_ref, ob_ref, x_ref, gt_ref, sc_ref, sh_ref, nw_ref, wa_ref, wb_ref, wr_ref, br_ref,
                    x1_ref, h_ref, wt_ref):
    bb, tt, d = x_ref.shape
    n = bb * tt
    oa = oa_ref[...].reshape(n, D_MIX_HALF).astype(BF16)
    ob = ob_ref[...].reshape(n, D_MIX_HALF).astype(BF16)
    mixed = (jnp.dot(oa, wa_ref[...], preferred_element_type=F32)
             + jnp.dot(ob, wb_ref[...], preferred_element_type=F32))
    x1 = x_ref[...] + gt_ref[...] * mixed.reshape(bb, tt, d)
    x1_ref[...] = x1
    h = _modulated_norm(x1, nw_ref[...], sc_ref[...], sh_ref[...]).reshape(n, d)
    h_ref[...] = h.reshape(bb, tt, d).astype(BF16)

    logits = _dot(h, wr_ref[...]) + br_ref[...]
    lane_i = _iota2((n, LANES), 1)
    lane = lane_i.astype(F32)
    lane_grp = (lane_i >> (EXP_PER_GROUP.bit_length() - 1)).astype(F32)
    neg = jnp.float32(-jnp.inf)
    big = jnp.float32(LANES)
    is_grp = (lane_i >= N_EXPERTS) & (lane_i < N_EXPERTS + N_GROUPS)
    gl = jnp.where(is_grp, logits, neg)
    gmax = jnp.max(gl, axis=-1, keepdims=True)
    grp = jnp.min(jnp.where(gl == gmax, lane, big), axis=-1, keepdims=True) - N_EXPERTS
    g_prob = 1.0 / jnp.sum(jnp.where(is_grp, jnp.exp(gl - gmax), 0.0), axis=-1, keepdims=True)
    in_grp = (lane_i < N_EXPERTS) & (lane_grp == grp)
    el = jnp.where(in_grp, logits, neg)
    m1 = jnp.max(el, axis=-1, keepdims=True)
    i1 = jnp.min(jnp.where(el == m1, lane, big), axis=-1, keepdims=True)
    el2 = jnp.where(lane == i1, neg, el)
    m2 = jnp.max(el2, axis=-1, keepdims=True)
    i2 = jnp.min(jnp.where(el2 == m2, lane, big), axis=-1, keepdims=True)
    e2 = jnp.exp(m2 - m1)
    p1 = g_prob / (1.0 + e2)
    p2 = g_prob * e2 / (1.0 + e2)
    wt = jnp.where(lane == i1, p1, 0.0) + jnp.where(lane == i2, p2, 0.0)
    wt_ref[...] = wt.reshape(bb, tt, LANES)


def _outproj(oa, ob, x, gt, sc, sh, nw, wa, wb, wr, br, bb, tt):
    b, t, d = x.shape
    grid = (b // bb, t // tt)

    def tok_spec(w):
        return pl.BlockSpec((bb, tt, w), lambda i, j: (i, j, 0))

    mod_spec = pl.BlockSpec((bb, 1, d), lambda i, j: (i, 0, 0))

    def full2(shape):
        return pl.BlockSpec(shape, lambda i, j: (0, 0))

    return pl.pallas_call(
        _outproj_kernel,
        grid=grid,
        in_specs=[tok_spec(D_MIX_HALF), tok_spec(D_MIX_HALF), tok_spec(d), mod_spec, mod_spec, mod_spec,
                  pl.BlockSpec((1, 1, d), lambda i, j: (0, 0, 0)),
                  full2((D_MIX_HALF, d)), full2((D_MIX_HALF, d)), full2((d, LANES)), full2((1, LANES))],
        out_specs=[tok_spec(d), tok_spec(d), tok_spec(LANES)],
        out_shape=[jax.ShapeDtypeStruct((b, t, d), F32), jax.ShapeDtypeStruct((b, t, d), BF16),
                   jax.ShapeDtypeStruct((b, t, LANES), F32)],
        compiler_params=pltpu.CompilerParams(dimension_semantics=("arbitrary",) * 2, vmem_limit_bytes=VMEM_LIMIT),
        name="outproj",
    )(oa, ob, x, gt, sc, sh, nw, wa, wb, wr, br)


def _moe_kernel(h_ref, wt_ref, x1_ref, gt_ref, fw_ref, wgu_ref, wd_ref, y_ref, acc_scr):
    bb, tt, d = x1_ref.shape
    n = bb * tt
    e = pl.program_id(2)

    @pl.when(e == 0)
    def _():
        acc_scr[...] = jnp.zeros_like(acc_scr)

    h = h_ref[...].reshape(n, d)
    gu = jnp.dot(h, wgu_ref[0], preferred_element_type=F32)
    act = (_silu(gu[:, :D_EXPERT]) * gu[:, D_EXPERT:]).astype(BF16)
    y = jnp.dot(act, wd_ref[0], preferred_element_type=F32)
    wt = wt_ref[...].reshape(n, LANES)
    w_e = jnp.sum(jnp.where(_iota2((n, LANES), 1) == e, wt, 0.0), axis=-1, keepdims=True)
    acc_scr[...] += w_e * y

    @pl.when(e == pl.num_programs(2) - 1)
    def _():
        x2 = x1_ref[...] + gt_ref[...] * acc_scr[...].reshape(bb, tt, d)
        ms = jnp.mean(x2 * x2, axis=-1, keepdims=True)
        y_ref[...] = x2 * lax.rsqrt(ms + EPS) * fw_ref[...]


def _moe(h, wt, x1, gt, fw, w_gu, w_down, bb, tt):
    b, t, d = x1.shape
    grid = (b // bb, t // tt, N_EXPERTS)

    def tok_spec(w):
        return pl.BlockSpec((bb, tt, w), lambda i, j, e: (i, j, 0))

    return pl.pallas_call(
        _moe_kernel,
        grid=grid,
        in_specs=[tok_spec(d), tok_spec(LANES), tok_spec(d),
                  pl.BlockSpec((bb, 1, d), lambda i, j, e: (i, 0, 0)),
                  pl.BlockSpec((1, 1, d), lambda i, j, e: (0, 0, 0)),
                  pl.BlockSpec((1, d, 2 * D_EXPERT), lambda i, j, e: (e, 0, 0)),
                  pl.BlockSpec((1, D_EXPERT, d), lambda i, j, e: (e, 0, 0))],
        out_specs=tok_spec(d),
        out_shape=jax.ShapeDtypeStruct((b, t, d), F32),
        scratch_shapes=[pltpu.VMEM((bb * tt, d), F32)],
        compiler_params=pltpu.CompilerParams(dimension_semantics=("arbitrary",) * 3, vmem_limit_bytes=VMEM_LIMIT),
        name="moe",
    )(h, wt, x1, gt, fw, w_gu, w_down)


def _layer(x, mod, conv0, s_gdn0, shift0, s_rwkv0, p, *, t_valid, bb, tt, bsz, rows_per_group, cd, chunk):
    sh_m, sc_m, gt_m, sh_f, sc_f, gt_f = mod
    cols = _inproj(x, sc_m, sh_m, p['norm_mix_w'], p['w_cols'], bb, tt)
    o_a, s_gdn = _gdn(cols, conv0, s_gdn0, p['gdn_conv_w'], p['alog_rep'], p['dtb_rep'], p['gdn_nw_rep'],
                      bsz, rows_per_group, cd, chunk, t_valid)
    o_b, s_rwkv = _rwkv(cols, shift0, s_rwkv0, p['rwkv_mu'], p['rwkv_vecs'], p['rwkv_w_up'], p['rwkv_a_up'],
                        p['rwkv_g_up'], bsz, rows_per_group, cd, chunk, t_valid)
    x1, h2, wt = _outproj(o_a, o_b, x, gt_m, sc_f, sh_f, p['norm_ffn_w'], p['w_out_a'], p['w_out_b'],
                          p['w_router'], p['b_router'], bb, tt)
    y = _moe(h2, wt, x1, gt_f, p['final_norm_w'], p['w_gu'], p['w_down'], bb, tt)
    t_last = x.shape[1] if t_valid is None else t_valid
    conv_new = cols[:, t_last - (CONV_W - 1):t_last, :GDN_QKV]
    shift_new = cols[:, t_last - 1, COL_RW:]
    return y, conv_new, s_gdn, shift_new, s_rwkv


def kernel(x_prompt, x_sample, state_gdn_conv, state_gdn, state_rwkv_shift, state_rwkv, c_prompt, c_sample, ada_w, ada_b, norm_mix_w, w_in, gdn_conv_w, gdn_a_log, gdn_dt_bias, gdn_norm_w, rwkv_mu, rwkv_w0, rwkv_w_up, rwkv_a0, rwkv_a_up, rwkv_g_up, rwkv_k_k, rwkv_k_a, rwkv_r_k, rwkv_ln_w, rwkv_ln_b, w_out, norm_ffn_w, router_group_w, router_group_b, router_expert_w, router_expert_b, expert_w_gate_up, expert_w_down, final_norm_w):
    depth = ada_w.shape[0]
    assert depth == 1
    l = 0
    b_p, t_p, d = x_prompt.shape
    b_s, t_s, _ = x_sample.shape
    gdn_cols = GDN_QKV + D_MIX_HALF + 2 * N_HEADS

    w = w_in[l]
    rep = jnp.repeat(jnp.arange(N_HEADS), HEAD_DIM)
    beta0 = GDN_QKV + D_MIX_HALF
    w_cols = jnp.concatenate([w[:, :beta0], w[:, beta0 + rep], w[:, beta0 + N_HEADS + rep], w[:, gdn_cols:]],
                             axis=1).astype(BF16)
    row = lambda a: a.reshape(1, -1)
    w_router = jnp.zeros((d, LANES), F32)
    w_router = w_router.at[:, :N_EXPERTS].set(router_expert_w[l]).at[:, N_EXPERTS:N_EXPERTS + N_GROUPS].set(
        router_group_w[l])
    b_router = jnp.zeros((1, LANES), F32)
    b_router = b_router.at[0, :N_EXPERTS].set(router_expert_b[l]).at[0, N_EXPERTS:N_EXPERTS + N_GROUPS].set(
        router_group_b[l])
    p = {
        'norm_mix_w': norm_mix_w[l].reshape(1, 1, d), 'w_cols': w_cols,
        'gdn_conv_w': gdn_conv_w[l], 'alog_rep': row(gdn_a_log[l][rep]), 'dtb_rep': row(gdn_dt_bias[l][rep]),
        'gdn_nw_rep': row(jnp.tile(gdn_norm_w[l], N_HEADS)),
        'rwkv_mu': row(rwkv_mu[l]),
        'rwkv_vecs': tuple(row(a) for a in (rwkv_w0[l], rwkv_a0[l], rwkv_k_k[l], rwkv_k_a[l], rwkv_r_k[l],
                                            rwkv_ln_w[l], rwkv_ln_b[l])),
        'rwkv_w_up': rwkv_w_up[l], 'rwkv_a_up': rwkv_a_up[l], 'rwkv_g_up': rwkv_g_up[l],
        'w_out_a': w_out[l][:D_MIX_HALF].astype(BF16), 'w_out_b': w_out[l][D_MIX_HALF:].astype(BF16),
        'norm_ffn_w': norm_ffn_w[l].reshape(1, 1, d), 'w_router': w_router, 'b_router': b_router,
        'w_gu': expert_w_gate_up[l].astype(BF16), 'w_down': expert_w_down[l].astype(BF16),
        'final_norm_w': final_norm_w.reshape(1, 1, d),
    }

    mod = _ada(jnp.concatenate([c_prompt, c_sample], axis=0), ada_w[l], ada_b[l])
    mod_p = tuple(m.reshape(b_p, 1, d) for m in jnp.split(mod[:b_p], 6, axis=-1))
    mod_s = tuple(m.reshape(b_s, 1, d) for m in jnp.split(mod[b_p:], 6, axis=-1))

    zc = jnp.zeros((b_p, CONV_W - 1, GDN_QKV), F32)
    zs = jnp.zeros((b_p, N_HEADS, HEAD_DIM, HEAD_DIM), F32)
    zsh = jnp.zeros((b_p, 1, RWKV_COLS), F32)
    y_p, conv_p, sg_p, shift_p, sr_p = _layer(x_prompt, mod_p, zc, zs, zsh, zs, p, t_valid=None,
                                              bb=1, tt=512, bsz=4, rows_per_group=4, cd=CHUNK, chunk=CHUNK)

    t_pad = SUBLANES
    x_s = jnp.pad(x_sample, ((0, 0), (0, t_pad - t_s), (0, 0)))
    y_s, conv_s, sg_s, shift_s, sr_s = _layer(x_s, mod_s, state_gdn_conv[l], state_gdn[l],
                                              state_rwkv_shift[l].reshape(b_s, 1, RWKV_COLS), state_rwkv[l], p,
                                              t_valid=t_s, bb=64, tt=t_pad, bsz=8, rows_per_group=4, cd=t_pad,
                                              chunk=t_pad)
    y_s = y_s[:, :t_s]
    return (y_p, y_s, conv_p[None], sg_p[None], shift_p[None], sr_p[None],
            conv_s[None], sg_s[None], shift_s[None], sr_s[None])
```

```python
import functools

import jax
import jax.numpy as jnp
from jax import lax
from jax.experimental import pallas as pl
from jax.experimental.pallas import tpu as pltpu

F32 = jnp.float32
BF16 = jnp.bfloat16
HI = lax.Precision.HIGHEST

D_MODEL = 1024
HEAD_DIM = 64
N_HEADS = 8
D_MIX_HALF = N_HEADS * HEAD_DIM
CONV_W = 4
LORA_W = 64
LORA_A = 64
LORA_G = 128
N_GROUPS = 4
EXP_PER_GROUP = 8
N_EXPERTS = N_GROUPS * EXP_PER_GROUP
D_EXPERT = 256
EPS = 1e-6
GN_EPS = HEAD_DIM * 1e-5

LANES = 128
SUBLANES = 8
GDN_QKV = 3 * D_MIX_HALF
RWKV_RKV = 3 * D_MIX_HALF
RWKV_COLS = RWKV_RKV + LORA_W + LORA_A + LORA_G
LORA_COLS = LORA_W + LORA_A + LORA_G

COL_Z = GDN_QKV
COL_B = COL_Z + D_MIX_HALF
COL_A = COL_B + D_MIX_HALF
COL_RW = COL_A + D_MIX_HALF
COL_LORA = COL_RW + RWKV_RKV
N_COLS = COL_LORA + LORA_COLS
CHUNK = 64

VMEM_LIMIT = 48 * 1024 * 1024


def _dot(a, b, prec=HI):
    return jnp.dot(a, b, preferred_element_type=F32, precision=prec)


def _bf(a):
    return a.astype(BF16)


def _mm(a, b):
    return jnp.dot(_bf(a), _bf(b), preferred_element_type=F32)


def _mm_nt(a, b):
    return lax.dot_general(_bf(a), _bf(b), (((1,), (1,)), ((), ())), preferred_element_type=F32)


def _mm_tn(a, b):
    return lax.dot_general(_bf(a), _bf(b), (((0,), (0,)), ((), ())), preferred_element_type=F32)


def _split2(a):
    hi = a.astype(BF16)
    return hi, (a - hi.astype(F32)).astype(BF16)


def _sigmoid(x):
    return 1.0 / (1.0 + jnp.exp(-x))


def _silu(x):
    return x * _sigmoid(x)


def _softplus(x):
    return jnp.maximum(x, 0.0) + jnp.log1p(jnp.exp(-jnp.abs(x)))


def _iota2(shape, dim):
    return lax.broadcasted_iota(jnp.int32, shape, dim)


def _head_block_ones():
    r = _iota2((LANES, LANES), 0)
    c = _iota2((LANES, LANES), 1)
    sh = HEAD_DIM.bit_length() - 1
    return jnp.where((r >> sh) == (c >> sh), 1.0, 0.0).astype(F32)


def _head_sums(x):
    ones = _bf(_head_block_ones())
    hi, lo = _split2(x)
    parts = []
    for p in range(x.shape[1] // LANES):
        s = slice(p * LANES, (p + 1) * LANES)
        parts.append(jnp.dot(hi[:, s], ones, preferred_element_type=F32)
                     + jnp.dot(lo[:, s], ones, preferred_element_type=F32))
    return jnp.concatenate(parts, axis=1)


def _cumsum_rows(x):
    n = x.shape[0]
    r = _iota2((n, n), 0)
    c = _iota2((n, n), 1)
    tri = jnp.where(r >= c, 1.0, 0.0).astype(BF16)
    x1 = x.astype(BF16)
    rem = x - x1.astype(F32)
    x2 = rem.astype(BF16)
    x3 = (rem - x2.astype(F32)).astype(BF16)
    return (jnp.dot(tri, x1, preferred_element_type=F32) + jnp.dot(tri, x2, preferred_element_type=F32)
            + jnp.dot(tri, x3, preferred_element_type=F32))


def _unit_lower_inverses(lows, n, mm):
    r = _iota2((n, n), 0)
    c = _iota2((n, n), 1)
    eye = jnp.where(r == c, 1.0, 0.0)
    pair = (r >> 1) == (c >> 1)
    invs = [eye - jnp.where(pair, low, 0.0) for low in lows]
    s = 2
    while s < n:
        sh = s.bit_length()
        sel = ((r >> sh) == (c >> sh)) & ((r & (2 * s - 1)) >= s) & ((c & (2 * s - 1)) < s)
        prods = [mm(jnp.where(sel, low, 0.0), inv) for low, inv in zip(lows, invs)]
        invs = [inv - mm(inv, prod) for inv, prod in zip(invs, prods)]
        s *= 2
    return invs


def _pad_rows(a, n):
    if a.shape[0] == n:
        return a
    return jnp.concatenate([a, jnp.zeros((n - a.shape[0], a.shape[1]), a.dtype)], axis=0)


def _for_row_groups(n, group, body):
    if n == group:
        body(list(range(n)))
    else:
        def step(i, carry):
            body([i * group + j for j in range(group)])
            return carry
        lax.fori_loop(0, n // group, step, 0)


def _ada_kernel(c_ref, w_ref, b_ref, o_ref):
    o_ref[...] = _dot(_silu(c_ref[...]), w_ref[...]) + b_ref[...]


def _ada(c_all, ada_w, ada_b):
    n, d = c_all.shape
    nout = ada_w.shape[1]
    tn = 1536
    return pl.pallas_call(
        _ada_kernel,
        grid=(nout // tn,),
        in_specs=[pl.BlockSpec((n, d), lambda j: (0, 0)),
                  pl.BlockSpec((d, tn), lambda j: (0, j)),
                  pl.BlockSpec((1, tn), lambda j: (0, j))],
        out_specs=pl.BlockSpec((n, tn), lambda j: (0, j)),
        out_shape=jax.ShapeDtypeStruct((n, nout), F32),
        compiler_params=pltpu.CompilerParams(dimension_semantics=("arbitrary",), vmem_limit_bytes=VMEM_LIMIT),
        name="ada",
    )(c_all, ada_w, ada_b.reshape(1, nout))


def _modulated_norm(x, nw, sc, sh):
    ms = jnp.mean(x * x, axis=-1, keepdims=True)
    return (x * lax.rsqrt(ms + EPS) * nw) * (1.0 + sc) + sh


def _inproj_kernel(x_ref, sc_ref, sh_ref, nw_ref, w_ref, o_ref):
    bb, tt, d = x_ref.shape
    h = _modulated_norm(x_ref[...], nw_ref[...], sc_ref[...], sh_ref[...])
    h = h.reshape(bb * tt, d).astype(BF16)
    o = jnp.dot(h, w_ref[...], preferred_element_type=F32)
    o_ref[...] = o.reshape(bb, tt, o.shape[-1])


def _inproj(x, sc, sh, nw, w_cols, bb, tt):
    b, t, d = x.shape
    tn = N_COLS // 2
    grid = (2, b // bb, t // tt)
    return pl.pallas_call(
        _inproj_kernel,
        grid=grid,
        in_specs=[pl.BlockSpec((bb, tt, d), lambda n, i, j: (i, j, 0)),
                  pl.BlockSpec((bb, 1, d), lambda n, i, j: (i, 0, 0)),
                  pl.BlockSpec((bb, 1, d), lambda n, i, j: (i, 0, 0)),
                  pl.BlockSpec((1, 1, d), lambda n, i, j: (0, 0, 0)),
                  pl.BlockSpec((d, tn), lambda n, i, j: (0, n))],
        out_specs=pl.BlockSpec((bb, tt, tn), lambda n, i, j: (i, j, n)),
        out_shape=jax.ShapeDtypeStruct((b, t, N_COLS), F32),
        compiler_params=pltpu.CompilerParams(dimension_semantics=("arbitrary",) * 3, vmem_limit_bytes=VMEM_LIMIT),
        name="inproj",
    )(x, sc, sh, nw, w_cols)


def _gdn_kernel(qkv_ref, z_ref, b_ref, a_ref, c0_ref, cw_ref, alog_ref, dtb_ref, nw_ref, s0_ref,
                o_ref, s_ref, hist_scr, *, chunk, t_valid, rows_per_group):
    bsz, cd, _ = qkv_ref.shape
    c = pl.program_id(1)
    hist_lo = SUBLANES - (CONV_W - 1)

    @pl.when(c == 0)
    def _():
        s_ref[...] = s0_ref[...]
        hist_scr[:, hist_lo:SUBLANES, :] = c0_ref[...]

    row = _iota2((chunk, chunk), 0)
    col = _iota2((chunk, chunk), 1)
    causal = row >= col
    strict = row > col

    def prep(bi):
        u = qkv_ref[bi]
        hist_scr[bi, SUBLANES:SUBLANES + cd, :] = u
        y = u * cw_ref[CONV_W - 1:CONV_W, :]
        for i in range(CONV_W - 1):
            y = y + hist_scr[bi, hist_lo + i:hist_lo + i + cd, :] * cw_ref[i:i + 1, :]
        hist_scr[bi, hist_lo:SUBLANES, :] = hist_scr[bi, hist_lo + cd:SUBLANES + cd, :]
        qkv = _pad_rows(_silu(y), chunk)
        q, k, v = (qkv[:, i * D_MIX_HALF:(i + 1) * D_MIX_HALF] for i in range(3))
        q = q * lax.rsqrt(_head_sums(q * q) + EPS) * (HEAD_DIM ** -0.5)
        k = k * lax.rsqrt(_head_sums(k * k) + EPS)
        beta = _sigmoid(_pad_rows(b_ref[bi], chunk))
        g = -jnp.exp(alog_ref[...]) * _softplus(_pad_rows(a_ref[bi], chunk) + dtb_ref[...])
        if t_valid is not None:
            valid = (c * cd + _iota2((chunk, D_MIX_HALF), 0)) < t_valid
            q, k, v = (jnp.where(valid, x, 0.0) for x in (q, k, v))
            beta = jnp.where(valid, beta, 0.0)
            g = jnp.where(valid, g, 0.0)
        gcum = _cumsum_rows(g)
        eg = jnp.exp(gcum)
        g_last = gcum[chunk - 1:chunk, :]
        return dict(q=q, k=k, beta=beta, gcum=gcum, rhs_v=beta * v, rhs_k=beta * eg * k, q_dec=eg * q,
                    k_dec=jnp.exp(g_last - gcum) * k, eg_last=jnp.exp(g_last))

    def group(bis):
        pre = [prep(bi) for bi in bis]
        chains = [(r, h) for r in range(len(bis)) for h in range(N_HEADS)]
        n = range(len(chains))

        def head(name, i, width=HEAD_DIM):
            r, h = chains[i]
            return pre[r][name][:, h * HEAD_DIM:h * HEAD_DIM + width]

        g_is = [head('gcum', i, chunk) for i in n]
        decays = [jnp.where(causal, jnp.exp(jnp.minimum(g_i - g_i.T, 0.0)), 0.0) for g_i in g_is]
        qk_kks = [_mm_nt(jnp.concatenate([head('k', i), head('q', i)], axis=0), head('k', i)) for i in n]
        lows = [jnp.where(strict, head('beta', i, chunk) * decays[i] * qk_kks[i][:chunk], 0.0) for i in n]
        t_invs = _unit_lower_inverses(lows, chunk, _mm)
        sols = [_mm(t_invs[i], jnp.concatenate([head('rhs_v', i), head('rhs_k', i)], axis=1)) for i in n]
        states = [s_ref[bis[r], h] for r, h in chains]
        wss = [_mm(jnp.concatenate([sols[i][:, HEAD_DIM:], head('q_dec', i)], axis=0), states[i])
               for i in n]
        u_news = [sols[i][:, :HEAD_DIM] - wss[i][:chunk] for i in n]
        outs = [wss[i][chunk:] + _mm(qk_kks[i][chunk:] * decays[i], u_news[i]) for i in n]
        s_news = [head('eg_last', i) * states[i] + _mm_tn(head('k_dec', i), u_news[i]) for i in n]
        for i, (r, h) in enumerate(chains):
            s_ref[bis[r], h] = s_news[i]
        for r, bi in enumerate(bis):
            o = jnp.concatenate(outs[r * N_HEADS:(r + 1) * N_HEADS], axis=1)
            o = o * lax.rsqrt(_head_sums(o * o) * (1.0 / HEAD_DIM) + EPS) * nw_ref[...]
            o = o * _silu(_pad_rows(z_ref[bi], chunk))
            o_ref[bi] = o[:cd]

    _for_row_groups(bsz, rows_per_group, group)


def _gdn(cols, conv0, s0, conv_w, alog_rep, dtb_rep, nw_rep, bsz, rows_per_group, cd, chunk, t_valid):
    b, t, _ = cols.shape

    def col_spec(width, start):
        return pl.BlockSpec((bsz, cd, width), lambda i, c: (i, c, start // width))

    vec_spec = pl.BlockSpec((1, D_MIX_HALF), lambda i, c: (0, 0))
    state_spec = pl.BlockSpec((bsz, N_HEADS, HEAD_DIM, HEAD_DIM), lambda i, c: (i, 0, 0, 0))
    kern = functools.partial(_gdn_kernel, chunk=chunk, t_valid=t_valid, rows_per_group=rows_per_group)
    return pl.pallas_call(
        kern,
        grid=(b // bsz, t // cd),
        in_specs=[col_spec(GDN_QKV, 0), col_spec(D_MIX_HALF, COL_Z), col_spec(D_MIX_HALF, COL_B),
                  col_spec(D_MIX_HALF, COL_A),
                  pl.BlockSpec((bsz, CONV_W - 1, GDN_QKV), lambda i, c: (i, 0, 0)),
                  pl.BlockSpec((CONV_W, GDN_QKV), lambda i, c: (0, 0)),
                  vec_spec, vec_spec, vec_spec, state_spec],
        out_specs=[pl.BlockSpec((bsz, cd, D_MIX_HALF), lambda i, c: (i, c, 0)), state_spec],
        out_shape=[jax.ShapeDtypeStruct((b, t, D_MIX_HALF), F32),
                   jax.ShapeDtypeStruct((b, N_HEADS, HEAD_DIM, HEAD_DIM), F32)],
        scratch_shapes=[pltpu.VMEM((bsz, SUBLANES + cd, GDN_QKV), F32)],
        compiler_params=pltpu.CompilerParams(dimension_semantics=("arbitrary",) * 2, vmem_limit_bytes=VMEM_LIMIT),
        name="gdn",
    )(cols, cols, cols, cols, conv0, conv_w, alog_rep, dtb_rep, nw_rep, s0)


def _rwkv_kernel(rkv_ref, l_ref, p_ref, pl_ref, mu_ref, mul_ref, w0_ref, a0_ref, kk_ref, ka_ref, rk_ref,
                 lnw_ref, lnb_ref, wup_ref, aup_ref, gup_ref, s0_ref, o_ref, s_ref, hist_scr, histl_scr,
                 *, chunk, t_valid, rows_per_group):
    bsz, cd, _ = rkv_ref.shape
    c = pl.program_id(1)
    prev_row = SUBLANES - 1

    @pl.when(c == 0)
    def _():
        s_ref[...] = s0_ref[...]
        hist_scr[:, prev_row:SUBLANES, :] = p_ref[...]
        histl_scr[:, prev_row:SUBLANES, :] = pl_ref[...]

    row = _iota2((chunk, chunk), 0)
    col = _iota2((chunk, chunk), 1)
    causal = row >= col
    strict = row > col

    def shifted(bi, u_ref, scr, m_ref):
        u = u_ref[bi]
        scr[bi, SUBLANES:SUBLANES + cd, :] = u
        prev = scr[bi, prev_row:prev_row + cd, :]
        scr[bi, prev_row:SUBLANES, :] = scr[bi, prev_row + cd:SUBLANES + cd, :]
        return _pad_rows(u + (prev - u) * m_ref[...], chunk)

    def prep(bi):
        rkv = shifted(bi, rkv_ref, hist_scr, mu_ref)
        r, k, v = (rkv[:, i * D_MIX_HALF:(i + 1) * D_MIX_HALF] for i in range(3))
        xl = shifted(bi, l_ref, histl_scr, mul_ref)
        dw = xl[:, :LORA_W]
        da = xl[:, LORA_W:LORA_W + LORA_A]
        dg = xl[:, LORA_W + LORA_A:]
        w_log = -_softplus(-(w0_ref[...] + _mm(jnp.tanh(dw), wup_ref[...]))) - 0.5
        lw = -jnp.exp(w_log)
        a = _sigmoid(a0_ref[...] + _mm(da, aup_ref[...]))
        gate = _mm(_sigmoid(dg), gup_ref[...])
        kk = k * kk_ref[...]
        kk = kk * lax.rsqrt(_head_sums(kk * kk) + EPS)
        k = k * (1.0 + (a - 1.0) * ka_ref[...])
        if t_valid is not None:
            valid = (c * cd + _iota2((chunk, D_MIX_HALF), 0)) < t_valid
            r, k, v, kk, lw = (jnp.where(valid, x, 0.0) for x in (r, k, v, kk, lw))
        cum = _cumsum_rows(lw)
        e_out = jnp.exp(-cum)
        cum_last = cum[chunk - 1:chunk, :]
        e_rest = jnp.exp(cum_last - cum)
        return dict(r=r, k=k, v=v, gate=gate, a_t=-kk * jnp.exp(cum - lw), b_t=kk * a * e_out, k_t=k * e_out,
                    r_t=r * jnp.exp(cum), b_c=kk * a * e_rest, k_c=k * e_rest, e_last=jnp.exp(cum_last))

    def group(bis):
        pre = [prep(bi) for bi in bis]
        chains = [(r, h) for r in range(len(bis)) for h in range(N_HEADS)]
        n = range(len(chains))

        def head(name, i):
            r, h = chains[i]
            return pre[r][name][:, h * HEAD_DIM:(h + 1) * HEAD_DIM]

        ars = [jnp.concatenate([head('a_t', i), head('r_t', i)], axis=0) for i in n]
        bks = [jnp.concatenate([head('b_t', i), head('k_t', i)], axis=0) for i in n]
        ms = [_mm_nt(ars[i], bks[i]) for i in n]
        t_invs = _unit_lower_inverses([jnp.where(strict, -m[:chunk, :chunk], 0.0) for m in ms], chunk, _mm)
        states = [s_ref[bis[r], h] for r, h in chains]
        arss = [_mm_nt(ars[i], states[i]) for i in n]
        vhs = [head('v', i) for i in n]
        akvs = [_mm(jnp.where(strict, ms[i][:chunk, chunk:], 0.0), vhs[i]) for i in n]
        us = [_mm(t_invs[i], arss[i][:chunk] + akvs[i]) for i in n]
        uvs = [jnp.concatenate([us[i], vhs[i]], axis=0) for i in n]
        causal2 = _iota2((chunk, 2 * chunk), 0) >= (_iota2((chunk, 2 * chunk), 1) & (chunk - 1))
        ys = [arss[i][chunk:] + _mm(jnp.where(causal2, ms[i][chunk:], 0.0), uvs[i]) for i in n]
        s_news = [states[i] * head('e_last', i)
                  + _mm_tn(uvs[i], jnp.concatenate([head('b_c', i), head('k_c', i)], axis=0)) for i in n]
        for i, (r, h) in enumerate(chains):
            s_ref[bis[r], h] = s_news[i]
        for r, bi in enumerate(bis):
            p = pre[r]
            y = jnp.concatenate(ys[r * N_HEADS:(r + 1) * N_HEADS], axis=1)
            mean = _head_sums(y) * (1.0 / HEAD_DIM)
            dy = y - mean
            var = _head_sums(dy * dy) * (1.0 / HEAD_DIM)
            y = dy * lax.rsqrt(var + GN_EPS) * lnw_ref[...] + lnb_ref[...]
            bonus = _head_sums(p['r'] * p['k'] * rk_ref[...]) * p['v']
            o_ref[bi] = ((y + bonus) * p['gate'])[:cd]

    _for_row_groups(bsz, rows_per_group, group)


def _rwkv(cols, shift0, s0, mu, vecs, w_up, a_up, g_up, bsz, rows_per_group, cd, chunk, t_valid):
    b, t, _ = cols.shape

    def full2(shape):
        return pl.BlockSpec(shape, lambda i, c: (0, 0))

    state_spec = pl.BlockSpec((bsz, N_HEADS, HEAD_DIM, HEAD_DIM), lambda i, c: (i, 0, 0, 0))
    kern = functools.partial(_rwkv_kernel, chunk=chunk, t_valid=t_valid, rows_per_group=rows_per_group)
    return pl.pallas_call(
        kern,
        grid=(b // bsz, t // cd),
        in_specs=[pl.BlockSpec((bsz, cd, RWKV_RKV), lambda i, c: (i, c, COL_RW // RWKV_RKV)),
                  pl.BlockSpec((bsz, cd, LORA_COLS), lambda i, c: (i, c, COL_LORA // LORA_COLS)),
                  pl.BlockSpec((bsz, 1, RWKV_RKV), lambda i, c: (i, 0, 0)),
                  pl.BlockSpec((bsz, 1, LORA_COLS), lambda i, c: (i, 0, RWKV_RKV // LORA_COLS)),
                  pl.BlockSpec((1, RWKV_RKV), lambda i, c: (0, 0)),
                  pl.BlockSpec((1, LORA_COLS), lambda i, c: (0, RWKV_RKV // LORA_COLS))]
                 + [full2((1, D_MIX_HALF))] * 7
                 + [full2((LORA_W, D_MIX_HALF)), full2((LORA_A, D_MIX_HALF)), full2((LORA_G, D_MIX_HALF)),
                    state_spec],
        out_specs=[pl.BlockSpec((bsz, cd, D_MIX_HALF), lambda i, c: (i, c, 0)), state_spec],
        out_shape=[jax.ShapeDtypeStruct((b, t, D_MIX_HALF), F32),
                   jax.ShapeDtypeStruct((b, N_HEADS, HEAD_DIM, HEAD_DIM), F32)],
        scratch_shapes=[pltpu.VMEM((bsz, SUBLANES + cd, RWKV_RKV), F32),
                        pltpu.VMEM((bsz, SUBLANES + cd, LORA_COLS), F32)],
        compiler_params=pltpu.CompilerParams(dimension_semantics=("arbitrary",) * 2, vmem_limit_bytes=VMEM_LIMIT),
        name="rwkv",
    )(cols, cols, shift0, shift0, mu, mu, *vecs, w_up, a_up, g_up, s0)


def _outproj_kernel(oa_ref, ob_ref, x_ref, gt_ref, sc_ref, sh_ref, nw_ref, wa_ref, wb_ref, wr_ref, br_ref,
                    x1_ref, h_ref, wt_ref):
    bb, tt, d = x_ref.shape
    n = bb * tt
    oa = oa_ref[...].reshape(n, D_MIX_HALF).astype(BF16)
    ob = ob_ref[...].reshape(n, D_MIX_HALF).astype(BF16)
    mixed = (jnp.dot(oa, wa_ref[...], preferred_element_type=F32)
             + jnp.dot(ob, wb_ref[...], preferred_element_type=F32))
    x1 = x_ref[...] + gt_ref[...] * mixed.reshape(bb, tt, d)
    x1_ref[...] = x1
    h = _modulated_norm(x1, nw_ref[...], sc_ref[...], sh_ref[...]).reshape(n, d)
    h_ref[...] = h.reshape(bb, tt, d).astype(BF16)

    logits = _dot(h, wr_ref[...]) + br_ref[...]
    lane_i = _iota2((n, LANES), 1)
    lane = lane_i.astype(F32)
    lane_grp = (lane_i >> (EXP_PER_GROUP.bit_length() - 1)).astype(F32)
    neg = jnp.float32(-jnp.inf)
    big = jnp.float32(LANES)
    is_grp = (lane_i >= N_EXPERTS) & (lane_i < N_EXPERTS + N_GROUPS)
    gl = jnp.where(is_grp, logits, neg)
    gmax = jnp.max(gl, axis=-1, keepdims=True)
    grp = jnp.min(jnp.where(gl == gmax, lane, big), axis=-1, keepdims=True) - N_EXPERTS
    g_prob = 1.0 / jnp.sum(jnp.where(is_grp, jnp.exp(gl - gmax), 0.0), axis=-1, keepdims=True)
    in_grp = (lane_i < N_EXPERTS) & (lane_grp == grp)
    el = jnp.where(in_grp, logits, neg)
    m1 = jnp.max(el, axis=-1, keepdims=True)
    i1 = jnp.min(jnp.where(el == m1, lane, big), axis=-1, keepdims=True)
    el2 = jnp.where(lane == i1, neg, el)
    m2 = jnp.max(el2, axis=-1, keepdims=True)
    i2 = jnp.min(jnp.where(el2 == m2, lane, big), axis=-1, keepdims=True)
    e2 = jnp.exp(m2 - m1)
    p1 = g_prob / (1.0 + e2)
    p2 = g_prob * e2 / (1.0 + e2)
    wt = jnp.where(lane == i1, p1, 0.0) + jnp.where(lane == i2, p2, 0.0)
    wt_ref[...] = wt.reshape(bb, tt, LANES)


def _outproj(oa, ob, x, gt, sc, sh, nw, wa, wb, wr, br, bb, tt):
    b, t, d = x.shape
    grid = (b // bb, t // tt)

    def tok_spec(w):
        return pl.BlockSpec((bb, tt, w), lambda i, j: (i, j, 0))

    mod_spec = pl.BlockSpec((bb, 1, d), lambda i, j: (i, 0, 0))

    def full2(shape):
        return pl.BlockSpec(shape, lambda i, j: (0, 0))

    return pl.pallas_call(
        _outproj_kernel,
        grid=grid,
        in_specs=[tok_spec(D_MIX_HALF), tok_spec(D_MIX_HALF), tok_spec(d), mod_spec, mod_spec, mod_spec,
                  pl.BlockSpec((1, 1, d), lambda i, j: (0, 0, 0)),
                  full2((D_MIX_HALF, d)), full2((D_MIX_HALF, d)), full2((d, LANES)), full2((1, LANES))],
        out_specs=[tok_spec(d), tok_spec(d), tok_spec(LANES)],
        out_shape=[jax.ShapeDtypeStruct((b, t, d), F32), jax.ShapeDtypeStruct((b, t, d), BF16),
                   jax.ShapeDtypeStruct((b, t, LANES), F32)],
        compiler_params=pltpu.CompilerParams(dimension_semantics=("arbitrary",) * 2, vmem_limit_bytes=VMEM_LIMIT),
        name="outproj",
    )(oa, ob, x, gt, sc, sh, nw, wa, wb, wr, br)


MOE_ROWS = 128


def _moe_kernel(h_ref, wt_ref, x1_ref, gt_ref, fw_ref, wgu_ref, wd_ref, y_ref,
                xg_scr, gtw_scr, yw_scr, cnt_scr, off_scr):
    bb, tt, d = x1_ref.shape
    n = bb * tt
    npos = 2 * n
    e = pl.program_id(2)

    @pl.when(e == 0)
    def _():
        wt_t = wt_ref[...].reshape(n, LANES).T
        member = wt_t > 0.0
        ones = jnp.where(member, 1.0, 0.0)
        upper = jnp.where(_iota2((n, n), 0) < _iota2((n, n), 1), 1.0, 0.0).astype(BF16)
        rank = jnp.dot(_bf(ones), upper, preferred_element_type=F32)
        cnt = jnp.broadcast_to(jnp.sum(ones, axis=1, keepdims=True), (LANES, LANES))
        lower = jnp.where(_iota2((LANES, LANES), 0) > _iota2((LANES, LANES), 1), 1.0, 0.0)
        off = _dot(lower, cnt)
        cnt_scr[...] = cnt
        off_scr[...] = off
        pos = off[:, :1] + rank
        pos1 = jnp.min(jnp.where(member, pos, jnp.float32(4 * n)), axis=0, keepdims=True)
        pos2 = jnp.max(jnp.where(member, pos, -1.0), axis=0, keepdims=True)
        w1 = jnp.sum(jnp.where(member & (pos == pos1), wt_t, 0.0), axis=0, keepdims=True)
        w2 = jnp.sum(jnp.where(member & (pos == pos2), wt_t, 0.0), axis=0, keepdims=True)
        w2 = jnp.where(pos2 != pos1, w2, 0.0)
        h = h_ref[...].reshape(n, d)

        def gather_block(b, carry):
            p = (b * MOE_ROWS + _iota2((MOE_ROWS, n), 0)).astype(F32)
            onehot = jnp.where((pos1 == p) | (pos2 == p), 1.0, 0.0).astype(BF16)
            rows = pl.ds(pl.multiple_of(b * MOE_ROWS, MOE_ROWS), MOE_ROWS)
            xg_scr[rows, :] = jnp.dot(onehot, h, preferred_element_type=F32).astype(BF16)
            return carry

        lax.fori_loop(0, npos // MOE_ROWS, gather_block, 0)
        z_t = jnp.concatenate([pos1, pos2, w1, w2, jnp.zeros((LANES - 4, n), F32)], axis=0).T
        p1c, p2c, w1c, w2c = (z_t[:, i:i + 1] for i in range(4))
        for b in range(npos // LANES):
            p = (b * LANES + _iota2((n, LANES), 1)).astype(F32)
            gtw_scr[:, b * LANES:(b + 1) * LANES] = (jnp.where(p1c == p, w1c, 0.0)
                                                     + jnp.where(p2c == p, w2c, 0.0)).astype(BF16)
        yw_scr[...] = jnp.zeros_like(yw_scr)

    first = off_scr[pl.ds(e, 1), 0:1].astype(jnp.int32)[0, 0]
    last = (off_scr[pl.ds(e, 1), 0:1] + cnt_scr[pl.ds(e, 1), 0:1]).astype(jnp.int32)[0, 0]

    def expert_block(b, carry):
        rows = pl.ds(pl.multiple_of(b * MOE_ROWS, MOE_ROWS), MOE_ROWS)
        gu = jnp.dot(xg_scr[rows, :], wgu_ref[0], preferred_element_type=F32)
        act = (_silu(gu[:, :D_EXPERT]) * gu[:, D_EXPERT:]).astype(BF16)
        y = jnp.dot(act, wd_ref[0], preferred_element_type=F32).astype(BF16)
        p = b * MOE_ROWS + _iota2((MOE_ROWS, 1), 0)
        yw_scr[rows, :] = jnp.where((p >= first) & (p < last), y, yw_scr[rows, :])
        return carry

    lax.fori_loop(first // MOE_ROWS, (last + MOE_ROWS - 1) // MOE_ROWS, expert_block, 0)

    @pl.when(e == pl.num_programs(2) - 1)
    def _():
        moe = jnp.dot(gtw_scr[...], yw_scr[...], preferred_element_type=F32)
        x2 = x1_ref[...] + gt_ref[...] * moe.reshape(bb, tt, d)
        ms = jnp.mean(x2 * x2, axis=-1, keepdims=True)
        y_ref[...] = x2 * lax.rsqrt(ms + EPS) * fw_ref[...]


def _moe(h, wt, x1, gt, fw, w_gu, w_down, bb, tt):
    b, t, d = x1.shape
    grid = (b // bb, t // tt, N_EXPERTS)

    def tok_spec(w):
        return pl.BlockSpec((bb, tt, w), lambda i, j, e: (i, j, 0))

    return pl.pallas_call(
        _moe_kernel,
        grid=grid,
        in_specs=[tok_spec(d), tok_spec(LANES), tok_spec(d),
                  pl.BlockSpec((bb, 1, d), lambda i, j, e: (i, 0, 0)),
                  pl.BlockSpec((1, 1, d), lambda i, j, e: (0, 0, 0)),
                  pl.BlockSpec((1, d, 2 * D_EXPERT), lambda i, j, e: (e, 0, 0)),
                  pl.BlockSpec((1, D_EXPERT, d), lambda i, j, e: (e, 0, 0))],
        out_specs=tok_spec(d),
        out_shape=jax.ShapeDtypeStruct((b, t, d), F32),
        scratch_shapes=[pltpu.VMEM((2 * bb * tt, d), BF16), pltpu.VMEM((bb * tt, 2 * bb * tt), BF16),
                        pltpu.VMEM((2 * bb * tt, d), BF16), pltpu.VMEM((LANES, LANES), F32),
                        pltpu.VMEM((LANES, LANES), F32)],
        compiler_params=pltpu.CompilerParams(dimension_semantics=("arbitrary",) * 3, vmem_limit_bytes=VMEM_LIMIT),
        name="moe",
    )(h, wt, x1, gt, fw, w_gu, w_down)


def _layer(x, mod, conv0, s_gdn0, shift0, s_rwkv0, p, *, t_valid, bb, tt, moe_tile, bsz, rows_per_group, cd,
           chunk):
    sh_m, sc_m, gt_m, sh_f, sc_f, gt_f = mod
    cols = _inproj(x, sc_m, sh_m, p['norm_mix_w'], p['w_cols'], bb, tt)
    o_a, s_gdn = _gdn(cols, conv0, s_gdn0, p['gdn_conv_w'], p['alog_rep'], p['dtb_rep'], p['gdn_nw_rep'],
                      bsz, rows_per_group, cd, chunk, t_valid)
    o_b, s_rwkv = _rwkv(cols, shift0, s_rwkv0, p['rwkv_mu'], p['rwkv_vecs'], p['rwkv_w_up'], p['rwkv_a_up'],
                        p['rwkv_g_up'], bsz, rows_per_group, cd, chunk, t_valid)
    x1, h2, wt = _outproj(o_a, o_b, x, gt_m, sc_f, sh_f, p['norm_ffn_w'], p['w_out_a'], p['w_out_b'],
                          p['w_router'], p['b_router'], bb, tt)
    y = _moe(h2, wt, x1, gt_f, p['final_norm_w'], p['w_gu'], p['w_down'], *moe_tile)
    t_last = x.shape[1] if t_valid is None else t_valid
    conv_new = cols[:, t_last - (CONV_W - 1):t_last, :GDN_QKV]
    shift_new = cols[:, t_last - 1, COL_RW:]
    return y, conv_new, s_gdn, shift_new, s_rwkv


def kernel(x_prompt, x_sample, state_gdn_conv, state_gdn, state_rwkv_shift, state_rwkv, c_prompt, c_sample, ada_w, ada_b, norm_mix_w, w_in, gdn_conv_w, gdn_a_log, gdn_dt_bias, gdn_norm_w, rwkv_mu, rwkv_w0, rwkv_w_up, rwkv_a0, rwkv_a_up, rwkv_g_up, rwkv_k_k, rwkv_k_a, rwkv_r_k, rwkv_ln_w, rwkv_ln_b, w_out, norm_ffn_w, router_group_w, router_group_b, router_expert_w, router_expert_b, expert_w_gate_up, expert_w_down, final_norm_w):
    depth = ada_w.shape[0]
    assert depth == 1
    l = 0
    b_p, t_p, d = x_prompt.shape
    b_s, t_s, _ = x_sample.shape
    gdn_cols = GDN_QKV + D_MIX_HALF + 2 * N_HEADS

    w = w_in[l]
    rep = jnp.repeat(jnp.arange(N_HEADS), HEAD_DIM)
    beta0 = GDN_QKV + D_MIX_HALF
    w_cols = jnp.concatenate([w[:, :beta0], w[:, beta0 + rep], w[:, beta0 + N_HEADS + rep], w[:, gdn_cols:]],
                             axis=1).astype(BF16)
    row = lambda a: a.reshape(1, -1)
    w_router = jnp.zeros((d, LANES), F32)
    w_router = w_router.at[:, :N_EXPERTS].set(router_expert_w[l]).at[:, N_EXPERTS:N_EXPERTS + N_GROUPS].set(
        router_group_w[l])
    b_router = jnp.zeros((1, LANES), F32)
    b_router = b_router.at[0, :N_EXPERTS].set(router_expert_b[l]).at[0, N_EXPERTS:N_EXPERTS + N_GROUPS].set(
        router_group_b[l])
    p = {
        'norm_mix_w': norm_mix_w[l].reshape(1, 1, d), 'w_cols': w_cols,
        'gdn_conv_w': gdn_conv_w[l], 'alog_rep': row(gdn_a_log[l][rep]), 'dtb_rep': row(gdn_dt_bias[l][rep]),
        'gdn_nw_rep': row(jnp.tile(gdn_norm_w[l], N_HEADS)),
        'rwkv_mu': row(rwkv_mu[l]),
        'rwkv_vecs': tuple(row(a) for a in (rwkv_w0[l], rwkv_a0[l], rwkv_k_k[l], rwkv_k_a[l], rwkv_r_k[l],
                                            rwkv_ln_w[l], rwkv_ln_b[l])),
        'rwkv_w_up': rwkv_w_up[l], 'rwkv_a_up': rwkv_a_up[l], 'rwkv_g_up': rwkv_g_up[l],
        'w_out_a': w_out[l][:D_MIX_HALF].astype(BF16), 'w_out_b': w_out[l][D_MIX_HALF:].astype(BF16),
        'norm_ffn_w': norm_ffn_w[l].reshape(1, 1, d), 'w_router': w_router, 'b_router': b_router,
        'w_gu': expert_w_gate_up[l].astype(BF16), 'w_down': expert_w_down[l].astype(BF16),
        'final_norm_w': final_norm_w.reshape(1, 1, d),
    }

    mod = _ada(jnp.concatenate([c_prompt, c_sample], axis=0), ada_w[l], ada_b[l])
    mod_p = tuple(m.reshape(b_p, 1, d) for m in jnp.split(mod[:b_p], 6, axis=-1))
    mod_s = tuple(m.reshape(b_s, 1, d) for m in jnp.split(mod[b_p:], 6, axis=-1))

    zc = jnp.zeros((b_p, CONV_W - 1, GDN_QKV), F32)
    zs = jnp.zeros((b_p, N_HEADS, HEAD_DIM, HEAD_DIM), F32)
    zsh = jnp.zeros((b_p, 1, RWKV_COLS), F32)
    y_p, conv_p, sg_p, shift_p, sr_p = _layer(x_prompt, mod_p, zc, zs, zsh, zs, p, t_valid=None,
                                              bb=1, tt=512, moe_tile=(1, 1024), bsz=4, rows_per_group=4,
                                              cd=CHUNK, chunk=CHUNK)

    t_pad = SUBLANES
    x_s = jnp.pad(x_sample, ((0, 0), (0, t_pad - t_s), (0, 0)))
    y_s, conv_s, sg_s, shift_s, sr_s = _layer(x_s, mod_s, state_gdn_conv[l], state_gdn[l],
                                              state_rwkv_shift[l].reshape(b_s, 1, RWKV_COLS), state_rwkv[l], p,
                                              t_valid=t_s, bb=64, tt=t_pad, moe_tile=(b_s, t_pad), bsz=8,
                                              rows_per_group=4, cd=t_pad,
                                              chunk=t_pad)
    y_s = y_s[:, :t_s]
    return (y_p, y_s, conv_p[None], sg_p[None], shift_p[None], sr_p[None],
            conv_s[None], sg_s[None], shift_s[None], sr_s[None])
```

```python
import functools

import jax
import jax.numpy as jnp
from jax import lax
from jax.experimental import pallas as pl
from jax.experimental.pallas import tpu as pltpu

F32 = jnp.float32
BF16 = jnp.bfloat16
HI = lax.Precision.HIGHEST

D_MODEL = 1024
HEAD_DIM = 64
N_HEADS = 8
D_MIX_HALF = N_HEADS * HEAD_DIM
CONV_W = 4
LORA_W = 64
LORA_A = 64
LORA_G = 128
N_GROUPS = 4
EXP_PER_GROUP = 8
N_EXPERTS = N_GROUPS * EXP_PER_GROUP
D_EXPERT = 256
EPS = 1e-6
GN_EPS = HEAD_DIM * 1e-5

LANES = 128
SUBLANES = 8
BF16_ROWS = 16
GDN_QKV = 3 * D_MIX_HALF
RWKV_RKV = 3 * D_MIX_HALF
RWKV_COLS = RWKV_RKV + LORA_W + LORA_A + LORA_G
LORA_COLS = LORA_W + LORA_A + LORA_G

COL_Z = GDN_QKV
COL_B = COL_Z + D_MIX_HALF
COL_A = COL_B + D_MIX_HALF
COL_RW = COL_A + D_MIX_HALF
COL_LORA = COL_RW + RWKV_RKV
N_COLS = COL_LORA + LORA_COLS
CHUNK = 64

VMEM_LIMIT = 48 * 1024 * 1024
MOE_VMEM_LIMIT = 58 * 1024 * 1024


def _dot(a, b, prec=HI):
    return jnp.dot(a, b, preferred_element_type=F32, precision=prec)


def _bf(a):
    return a.astype(BF16)


def _mm(a, b):
    return jnp.dot(_bf(a), _bf(b), preferred_element_type=F32)


def _mm_nt(a, b):
    return lax.dot_general(_bf(a), _bf(b), (((1,), (1,)), ((), ())), preferred_element_type=F32)


def _mm_tn(a, b):
    return lax.dot_general(_bf(a), _bf(b), (((0,), (0,)), ((), ())), preferred_element_type=F32)


def _split2(a):
    hi = a.astype(BF16)
    return hi, (a - hi.astype(F32)).astype(BF16)


def _sigmoid(x):
    return 1.0 / (1.0 + jnp.exp(-x))


def _silu(x):
    return x * _sigmoid(x)


def _softplus(x):
    return jnp.maximum(x, 0.0) + jnp.log1p(jnp.exp(-jnp.abs(x)))


def _iota2(shape, dim):
    return lax.broadcasted_iota(jnp.int32, shape, dim)


def _head_block_ones():
    r = _iota2((LANES, LANES), 0)
    c = _iota2((LANES, LANES), 1)
    sh = HEAD_DIM.bit_length() - 1
    return jnp.where((r >> sh) == (c >> sh), 1.0, 0.0).astype(F32)


def _head_sums(x):
    ones = _bf(_head_block_ones())
    hi, lo = _split2(x)
    parts = []
    for p in range(x.shape[1] // LANES):
        s = slice(p * LANES, (p + 1) * LANES)
        parts.append(jnp.dot(hi[:, s], ones, preferred_element_type=F32)
                     + jnp.dot(lo[:, s], ones, preferred_element_type=F32))
    return jnp.concatenate(parts, axis=1)


def _cumsum_rows(x):
    n = x.shape[0]
    r = _iota2((n, n), 0)
    c = _iota2((n, n), 1)
    tri = jnp.where(r >= c, 1.0, 0.0).astype(BF16)
    x1 = x.astype(BF16)
    rem = x - x1.astype(F32)
    x2 = rem.astype(BF16)
    x3 = (rem - x2.astype(F32)).astype(BF16)
    return (jnp.dot(tri, x1, preferred_element_type=F32) + jnp.dot(tri, x2, preferred_element_type=F32)
            + jnp.dot(tri, x3, preferred_element_type=F32))


def _unit_lower_inverses(lows, n, mm):
    r = _iota2((n, n), 0)
    c = _iota2((n, n), 1)
    eye = jnp.where(r == c, 1.0, 0.0)
    pair = (r >> 1) == (c >> 1)
    invs = [eye - jnp.where(pair, low, 0.0) for low in lows]
    s = 2
    while s < n:
        sh = s.bit_length()
        sel = ((r >> sh) == (c >> sh)) & ((r & (2 * s - 1)) >= s) & ((c & (2 * s - 1)) < s)
        prods = [mm(jnp.where(sel, low, 0.0), inv) for low, inv in zip(lows, invs)]
        invs = [inv - mm(inv, prod) for inv, prod in zip(invs, prods)]
        s *= 2
    return invs


def _pad_rows(a, n):
    if a.shape[0] == n:
        return a
    return jnp.concatenate([a, jnp.zeros((n - a.shape[0], a.shape[1]), a.dtype)], axis=0)


def _for_row_groups(n, group, body):
    if n == group:
        body(list(range(n)))
    else:
        def step(i, carry):
            body([i * group + j for j in range(group)])
            return carry
        lax.fori_loop(0, n // group, step, 0)


def _ada_kernel(c_ref, w_ref, b_ref, o_ref):
    o_ref[...] = _dot(_silu(c_ref[...]), w_ref[...]) + b_ref[...]


def _ada(c_all, ada_w, ada_b):
    n, d = c_all.shape
    nout = ada_w.shape[1]
    tn = 1536
    return pl.pallas_call(
        _ada_kernel,
        grid=(nout // tn,),
        in_specs=[pl.BlockSpec((n, d), lambda j: (0, 0)),
                  pl.BlockSpec((d, tn), lambda j: (0, j)),
                  pl.BlockSpec((1, tn), lambda j: (0, j))],
        out_specs=pl.BlockSpec((n, tn), lambda j: (0, j)),
        out_shape=jax.ShapeDtypeStruct((n, nout), F32),
        compiler_params=pltpu.CompilerParams(dimension_semantics=("arbitrary",), vmem_limit_bytes=VMEM_LIMIT),
        name="ada",
    )(c_all, ada_w, ada_b.reshape(1, nout))


def _modulated_norm(x, nw, sc, sh):
    ms = jnp.mean(x * x, axis=-1, keepdims=True)
    return (x * lax.rsqrt(ms + EPS) * nw) * (1.0 + sc) + sh


def _inproj_kernel(x_ref, sc_ref, sh_ref, nw_ref, w_ref, o_ref):
    bb, tt, d = x_ref.shape
    h = _modulated_norm(x_ref[...], nw_ref[...], sc_ref[...], sh_ref[...])
    h = h.reshape(bb * tt, d).astype(BF16)
    o = jnp.dot(h, w_ref[...], preferred_element_type=F32)
    o_ref[...] = o.reshape(bb, tt, o.shape[-1])


def _inproj(x, sc, sh, nw, w_cols, bb, tt):
    b, t, d = x.shape
    tn = N_COLS // 2
    grid = (2, b // bb, t // tt)
    return pl.pallas_call(
        _inproj_kernel,
        grid=grid,
        in_specs=[pl.BlockSpec((bb, tt, d), lambda n, i, j: (i, j, 0)),
                  pl.BlockSpec((bb, 1, d), lambda n, i, j: (i, 0, 0)),
                  pl.BlockSpec((bb, 1, d), lambda n, i, j: (i, 0, 0)),
                  pl.BlockSpec((1, 1, d), lambda n, i, j: (0, 0, 0)),
                  pl.BlockSpec((d, tn), lambda n, i, j: (0, n))],
        out_specs=pl.BlockSpec((bb, tt, tn), lambda n, i, j: (i, j, n)),
        out_shape=jax.ShapeDtypeStruct((b, t, N_COLS), F32),
        compiler_params=pltpu.CompilerParams(dimension_semantics=("arbitrary",) * 3, vmem_limit_bytes=VMEM_LIMIT),
        name="inproj",
    )(x, sc, sh, nw, w_cols)


def _gdn_kernel(qkv_ref, z_ref, b_ref, a_ref, c0_ref, cw_ref, alog_ref, dtb_ref, nw_ref, s0_ref,
                o_ref, s_ref, hist_scr, *, chunk, t_valid, rows_per_group):
    bsz, cd, _ = qkv_ref.shape
    c = pl.program_id(1)
    hist_lo = SUBLANES - (CONV_W - 1)

    @pl.when(c == 0)
    def _():
        s_ref[...] = s0_ref[...]
        hist_scr[:, hist_lo:SUBLANES, :] = c0_ref[...]

    row = _iota2((chunk, chunk), 0)
    col = _iota2((chunk, chunk), 1)
    causal = row >= col
    strict = row > col

    def prep(bi):
        u = qkv_ref[bi]
        hist_scr[bi, SUBLANES:SUBLANES + cd, :] = u
        y = u * cw_ref[CONV_W - 1:CONV_W, :]
        for i in range(CONV_W - 1):
            y = y + hist_scr[bi, hist_lo + i:hist_lo + i + cd, :] * cw_ref[i:i + 1, :]
        hist_scr[bi, hist_lo:SUBLANES, :] = hist_scr[bi, hist_lo + cd:SUBLANES + cd, :]
        qkv = _pad_rows(_silu(y), chunk)
        q, k, v = (qkv[:, i * D_MIX_HALF:(i + 1) * D_MIX_HALF] for i in range(3))
        q = q * lax.rsqrt(_head_sums(q * q) + EPS) * (HEAD_DIM ** -0.5)
        k = k * lax.rsqrt(_head_sums(k * k) + EPS)
        beta = _sigmoid(_pad_rows(b_ref[bi], chunk))
        g = -jnp.exp(alog_ref[...]) * _softplus(_pad_rows(a_ref[bi], chunk) + dtb_ref[...])
        if t_valid is not None:
            valid = (c * cd + _iota2((chunk, D_MIX_HALF), 0)) < t_valid
            q, k, v = (jnp.where(valid, x, 0.0) for x in (q, k, v))
            beta = jnp.where(valid, beta, 0.0)
            g = jnp.where(valid, g, 0.0)
        gcum = _cumsum_rows(g)
        eg = jnp.exp(gcum)
        g_last = gcum[chunk - 1:chunk, :]
        return dict(q=q, k=k, beta=beta, gcum=gcum, rhs_v=beta * v, rhs_k=beta * eg * k, q_dec=eg * q,
                    k_dec=jnp.exp(g_last - gcum) * k, eg_last=jnp.exp(g_last))

    def group(bis):
        pre = [prep(bi) for bi in bis]
        chains = [(r, h) for r in range(len(bis)) for h in range(N_HEADS)]
        n = range(len(chains))

        def head(name, i, width=HEAD_DIM):
            r, h = chains[i]
            return pre[r][name][:, h * HEAD_DIM:h * HEAD_DIM + width]

        g_is = [head('gcum', i, chunk) for i in n]
        decays = [jnp.where(causal, jnp.exp(jnp.minimum(g_i - g_i.T, 0.0)), 0.0) for g_i in g_is]
        qk_kks = [_mm_nt(jnp.concatenate([head('k', i), head('q', i)], axis=0), head('k', i)) for i in n]
        lows = [jnp.where(strict, head('beta', i, chunk) * decays[i] * qk_kks[i][:chunk], 0.0) for i in n]
        t_invs = _unit_lower_inverses(lows, chunk, _mm)
        sols = [_mm(t_invs[i], jnp.concatenate([head('rhs_v', i), head('rhs_k', i)], axis=1)) for i in n]
        states = [s_ref[bis[r], h] for r, h in chains]
        wss = [_mm(jnp.concatenate([sols[i][:, HEAD_DIM:], head('q_dec', i)], axis=0), states[i])
               for i in n]
        u_news = [sols[i][:, :HEAD_DIM] - wss[i][:chunk] for i in n]
        outs = [wss[i][chunk:] + _mm(qk_kks[i][chunk:] * decays[i], u_news[i]) for i in n]
        s_news = [head('eg_last', i) * states[i] + _mm_tn(head('k_dec', i), u_news[i]) for i in n]
        for i, (r, h) in enumerate(chains):
            s_ref[bis[r], h] = s_news[i]
        for r, bi in enumerate(bis):
            o = jnp.concatenate(outs[r * N_HEADS:(r + 1) * N_HEADS], axis=1)
            o = o * lax.rsqrt(_head_sums(o * o) * (1.0 / HEAD_DIM) + EPS) * nw_ref[...]
            o = o * _silu(_pad_rows(z_ref[bi], chunk))
            o_ref[bi] = o[:cd]

    _for_row_groups(bsz, rows_per_group, group)


def _gdn(cols, conv0, s0, conv_w, alog_rep, dtb_rep, nw_rep, bsz, rows_per_group, cd, chunk, t_valid):
    b, t, _ = cols.shape

    def col_spec(width, start):
        return pl.BlockSpec((bsz, cd, width), lambda i, c: (i, c, start // width))

    vec_spec = pl.BlockSpec((1, D_MIX_HALF), lambda i, c: (0, 0))
    state_spec = pl.BlockSpec((bsz, N_HEADS, HEAD_DIM, HEAD_DIM), lambda i, c: (i, 0, 0, 0))
    kern = functools.partial(_gdn_kernel, chunk=chunk, t_valid=t_valid, rows_per_group=rows_per_group)
    return pl.pallas_call(
        kern,
        grid=(b // bsz, t // cd),
        in_specs=[col_spec(GDN_QKV, 0), col_spec(D_MIX_HALF, COL_Z), col_spec(D_MIX_HALF, COL_B),
                  col_spec(D_MIX_HALF, COL_A),
                  pl.BlockSpec((bsz, CONV_W - 1, GDN_QKV), lambda i, c: (i, 0, 0)),
                  pl.BlockSpec((CONV_W, GDN_QKV), lambda i, c: (0, 0)),
                  vec_spec, vec_spec, vec_spec, state_spec],
        out_specs=[pl.BlockSpec((bsz, cd, D_MIX_HALF), lambda i, c: (i, c, 0)), state_spec],
        out_shape=[jax.ShapeDtypeStruct((b, t, D_MIX_HALF), F32),
                   jax.ShapeDtypeStruct((b, N_HEADS, HEAD_DIM, HEAD_DIM), F32)],
        scratch_shapes=[pltpu.VMEM((bsz, SUBLANES + cd, GDN_QKV), F32)],
        compiler_params=pltpu.CompilerParams(dimension_semantics=("arbitrary",) * 2, vmem_limit_bytes=VMEM_LIMIT),
        name="gdn",
    )(cols, cols, cols, cols, conv0, conv_w, alog_rep, dtb_rep, nw_rep, s0)


def _rwkv_kernel(rkv_ref, l_ref, p_ref, pl_ref, mu_ref, mul_ref, w0_ref, a0_ref, kk_ref, ka_ref, rk_ref,
                 lnw_ref, lnb_ref, wup_ref, aup_ref, gup_ref, s0_ref, o_ref, s_ref, hist_scr, histl_scr,
                 *, chunk, t_valid, rows_per_group):
    bsz, cd, _ = rkv_ref.shape
    c = pl.program_id(1)
    prev_row = SUBLANES - 1

    @pl.when(c == 0)
    def _():
        s_ref[...] = s0_ref[...]
        hist_scr[:, prev_row:SUBLANES, :] = p_ref[...]
        histl_scr[:, prev_row:SUBLANES, :] = pl_ref[...]

    row = _iota2((chunk, chunk), 0)
    col = _iota2((chunk, chunk), 1)
    causal = row >= col
    strict = row > col

    def shifted(bi, u_ref, scr, m_ref):
        u = u_ref[bi]
        scr[bi, SUBLANES:SUBLANES + cd, :] = u
        prev = scr[bi, prev_row:prev_row + cd, :]
        scr[bi, prev_row:SUBLANES, :] = scr[bi, prev_row + cd:SUBLANES + cd, :]
        return _pad_rows(u + (prev - u) * m_ref[...], chunk)

    def prep(bi):
        rkv = shifted(bi, rkv_ref, hist_scr, mu_ref)
        r, k, v = (rkv[:, i * D_MIX_HALF:(i + 1) * D_MIX_HALF] for i in range(3))
        xl = shifted(bi, l_ref, histl_scr, mul_ref)
        dw = xl[:, :LORA_W]
        da = xl[:, LORA_W:LORA_W + LORA_A]
        dg = xl[:, LORA_W + LORA_A:]
        w_log = -_softplus(-(w0_ref[...] + _mm(jnp.tanh(dw), wup_ref[...]))) - 0.5
        lw = -jnp.exp(w_log)
        a = _sigmoid(a0_ref[...] + _mm(da, aup_ref[...]))
        gate = _mm(_sigmoid(dg), gup_ref[...])
        kk = k * kk_ref[...]
        kk = kk * lax.rsqrt(_head_sums(kk * kk) + EPS)
        k = k * (1.0 + (a - 1.0) * ka_ref[...])
        if t_valid is not None:
            valid = (c * cd + _iota2((chunk, D_MIX_HALF), 0)) < t_valid
            r, k, v, kk, lw = (jnp.where(valid, x, 0.0) for x in (r, k, v, kk, lw))
        cum = _cumsum_rows(lw)
        e_out = jnp.exp(-cum)
        cum_last = cum[chunk - 1:chunk, :]
        e_rest = jnp.exp(cum_last - cum)
        return dict(r=r, k=k, v=v, gate=gate, a_t=-kk * jnp.exp(cum - lw), b_t=kk * a * e_out, k_t=k * e_out,
                    r_t=r * jnp.exp(cum), b_c=kk * a * e_rest, k_c=k * e_rest, e_last=jnp.exp(cum_last))

    def group(bis):
        pre = [prep(bi) for bi in bis]
        chains = [(r, h) for r in range(len(bis)) for h in range(N_HEADS)]
        n = range(len(chains))

        def head(name, i):
            r, h = chains[i]
            return pre[r][name][:, h * HEAD_DIM:(h + 1) * HEAD_DIM]

        ars = [jnp.concatenate([head('a_t', i), head('r_t', i)], axis=0) for i in n]
        bks = [jnp.concatenate([head('b_t', i), head('k_t', i)], axis=0) for i in n]
        ms = [_mm_nt(ars[i], bks[i]) for i in n]
        t_invs = _unit_lower_inverses([jnp.where(strict, -m[:chunk, :chunk], 0.0) for m in ms], chunk, _mm)
        states = [s_ref[bis[r], h] for r, h in chains]
        arss = [_mm_nt(ars[i], states[i]) for i in n]
        vhs = [head('v', i) for i in n]
        akvs = [_mm(jnp.where(strict, ms[i][:chunk, chunk:], 0.0), vhs[i]) for i in n]
        us = [_mm(t_invs[i], arss[i][:chunk] + akvs[i]) for i in n]
        uvs = [jnp.concatenate([us[i], vhs[i]], axis=0) for i in n]
        causal2 = _iota2((chunk, 2 * chunk), 0) >= (_iota2((chunk, 2 * chunk), 1) & (chunk - 1))
        ys = [arss[i][chunk:] + _mm(jnp.where(causal2, ms[i][chunk:], 0.0), uvs[i]) for i in n]
        s_news = [states[i] * head('e_last', i)
                  + _mm_tn(uvs[i], jnp.concatenate([head('b_c', i), head('k_c', i)], axis=0)) for i in n]
        for i, (r, h) in enumerate(chains):
            s_ref[bis[r], h] = s_news[i]
        for r, bi in enumerate(bis):
            p = pre[r]
            y = jnp.concatenate(ys[r * N_HEADS:(r + 1) * N_HEADS], axis=1)
            mean = _head_sums(y) * (1.0 / HEAD_DIM)
            dy = y - mean
            var = _head_sums(dy * dy) * (1.0 / HEAD_DIM)
            y = dy * lax.rsqrt(var + GN_EPS) * lnw_ref[...] + lnb_ref[...]
            bonus = _head_sums(p['r'] * p['k'] * rk_ref[...]) * p['v']
            o_ref[bi] = ((y + bonus) * p['gate'])[:cd]

    _for_row_groups(bsz, rows_per_group, group)


def _rwkv(cols, shift0, s0, mu, vecs, w_up, a_up, g_up, bsz, rows_per_group, cd, chunk, t_valid):
    b, t, _ = cols.shape

    def full2(shape):
        return pl.BlockSpec(shape, lambda i, c: (0, 0))

    state_spec = pl.BlockSpec((bsz, N_HEADS, HEAD_DIM, HEAD_DIM), lambda i, c: (i, 0, 0, 0))
    kern = functools.partial(_rwkv_kernel, chunk=chunk, t_valid=t_valid, rows_per_group=rows_per_group)
    return pl.pallas_call(
        kern,
        grid=(b // bsz, t // cd),
        in_specs=[pl.BlockSpec((bsz, cd, RWKV_RKV), lambda i, c: (i, c, COL_RW // RWKV_RKV)),
                  pl.BlockSpec((bsz, cd, LORA_COLS), lambda i, c: (i, c, COL_LORA // LORA_COLS)),
                  pl.BlockSpec((bsz, 1, RWKV_RKV), lambda i, c: (i, 0, 0)),
                  pl.BlockSpec((bsz, 1, LORA_COLS), lambda i, c: (i, 0, RWKV_RKV // LORA_COLS)),
                  pl.BlockSpec((1, RWKV_RKV), lambda i, c: (0, 0)),
                  pl.BlockSpec((1, LORA_COLS), lambda i, c: (0, RWKV_RKV // LORA_COLS))]
                 + [full2((1, D_MIX_HALF))] * 7
                 + [full2((LORA_W, D_MIX_HALF)), full2((LORA_A, D_MIX_HALF)), full2((LORA_G, D_MIX_HALF)),
                    state_spec],
        out_specs=[pl.BlockSpec((bsz, cd, D_MIX_HALF), lambda i, c: (i, c, 0)), state_spec],
        out_shape=[jax.ShapeDtypeStruct((b, t, D_MIX_HALF), F32),
                   jax.ShapeDtypeStruct((b, N_HEADS, HEAD_DIM, HEAD_DIM), F32)],
        scratch_shapes=[pltpu.VMEM((bsz, SUBLANES + cd, RWKV_RKV), F32),
                        pltpu.VMEM((bsz, SUBLANES + cd, LORA_COLS), F32)],
        compiler_params=pltpu.CompilerParams(dimension_semantics=("arbitrary",) * 2, vmem_limit_bytes=VMEM_LIMIT),
        name="rwkv",
    )(cols, cols, shift0, shift0, mu, mu, *vecs, w_up, a_up, g_up, s0)


def _outproj_kernel(oa_ref, ob_ref, x_ref, gt_ref, sc_ref, sh_ref, nw_ref, wa_ref, wb_ref, wr_ref, br_ref,
                    x1_ref, h_ref, wt_ref):
    bb, tt, d = x_ref.shape
    n = bb * tt
    oa = oa_ref[...].reshape(n, D_MIX_HALF).astype(BF16)
    ob = ob_ref[...].reshape(n, D_MIX_HALF).astype(BF16)
    mixed = (jnp.dot(oa, wa_ref[...], preferred_element_type=F32)
             + jnp.dot(ob, wb_ref[...], preferred_element_type=F32))
    x1 = x_ref[...] + gt_ref[...] * mixed.reshape(bb, tt, d)
    x1_ref[...] = x1
    h = _modulated_norm(x1, nw_ref[...], sc_ref[...], sh_ref[...]).reshape(n, d)
    h_ref[...] = h.reshape(bb, tt, d).astype(BF16)

    logits = _dot(h, wr_ref[...]) + br_ref[...]
    lane_i = _iota2((n, LANES), 1)
    lane = lane_i.astype(F32)
    lane_grp = (lane_i >> (EXP_PER_GROUP.bit_length() - 1)).astype(F32)
    neg = jnp.float32(-jnp.inf)
    big = jnp.float32(LANES)
    is_grp = (lane_i >= N_EXPERTS) & (lane_i < N_EXPERTS + N_GROUPS)
    gl = jnp.where(is_grp, logits, neg)
    gmax = jnp.max(gl, axis=-1, keepdims=True)
    grp = jnp.min(jnp.where(gl == gmax, lane, big), axis=-1, keepdims=True) - N_EXPERTS
    g_prob = 1.0 / jnp.sum(jnp.where(is_grp, jnp.exp(gl - gmax), 0.0), axis=-1, keepdims=True)
    in_grp = (lane_i < N_EXPERTS) & (lane_grp == grp)
    el = jnp.where(in_grp, logits, neg)
    m1 = jnp.max(el, axis=-1, keepdims=True)
    i1 = jnp.min(jnp.where(el == m1, lane, big), axis=-1, keepdims=True)
    el2 = jnp.where(lane == i1, neg, el)
    m2 = jnp.max(el2, axis=-1, keepdims=True)
    i2 = jnp.min(jnp.where(el2 == m2, lane, big), axis=-1, keepdims=True)
    e2 = jnp.exp(m2 - m1)
    p1 = g_prob / (1.0 + e2)
    p2 = g_prob * e2 / (1.0 + e2)
    wt = jnp.where(lane == i1, p1, 0.0) + jnp.where(lane == i2, p2, 0.0)
    wt_ref[...] = wt.reshape(bb, tt, LANES)


def _outproj(oa, ob, x, gt, sc, sh, nw, wa, wb, wr, br, bb, tt):
    b, t, d = x.shape
    grid = (b // bb, t // tt)

    def tok_spec(w):
        return pl.BlockSpec((bb, tt, w), lambda i, j: (i, j, 0))

    mod_spec = pl.BlockSpec((bb, 1, d), lambda i, j: (i, 0, 0))

    def full2(shape):
        return pl.BlockSpec(shape, lambda i, j: (0, 0))

    return pl.pallas_call(
        _outproj_kernel,
        grid=grid,
        in_specs=[tok_spec(D_MIX_HALF), tok_spec(D_MIX_HALF), tok_spec(d), mod_spec, mod_spec, mod_spec,
                  pl.BlockSpec((1, 1, d), lambda i, j: (0, 0, 0)),
                  full2((D_MIX_HALF, d)), full2((D_MIX_HALF, d)), full2((d, LANES)), full2((1, LANES))],
        out_specs=[tok_spec(d), tok_spec(d), tok_spec(LANES)],
        out_shape=[jax.ShapeDtypeStruct((b, t, d), F32), jax.ShapeDtypeStruct((b, t, d), BF16),
                   jax.ShapeDtypeStruct((b, t, LANES), F32)],
        compiler_params=pltpu.CompilerParams(dimension_semantics=("arbitrary",) * 2, vmem_limit_bytes=VMEM_LIMIT),
        name="outproj",
    )(oa, ob, x, gt, sc, sh, nw, wa, wb, wr, br)


MOE_ROWS = 128
MOE_EXPERTS_PER_STEP = 4


def _moe_kernel(h_ref, wt_ref, x1_ref, gt_ref, fw_ref, wgu_ref, wd_ref, y_ref,
                xg_scr, gtw_scr, yw_scr, cnt_scr, off_scr):
    bb, tt, d = x1_ref.shape
    n = bb * tt
    npos = 2 * n
    e = pl.program_id(2)

    @pl.when(e == 0)
    def _():
        wt_t = wt_ref[...].reshape(n, LANES).T
        member = wt_t > 0.0
        ones = jnp.where(member, 1.0, 0.0)
        upper = jnp.where(_iota2((n, n), 0) < _iota2((n, n), 1), 1.0, 0.0).astype(BF16)
        rank = jnp.dot(_bf(ones), upper, preferred_element_type=F32)
        cnt = jnp.broadcast_to(jnp.sum(ones, axis=1, keepdims=True), (LANES, LANES))
        lower = jnp.where(_iota2((LANES, LANES), 0) > _iota2((LANES, LANES), 1), 1.0, 0.0)
        off = _dot(lower, cnt)
        cnt_scr[...] = cnt
        off_scr[...] = off
        pos = off[:, :1] + rank
        pos1 = jnp.min(jnp.where(member, pos, jnp.float32(4 * n)), axis=0, keepdims=True)
        pos2 = jnp.max(jnp.where(member, pos, -1.0), axis=0, keepdims=True)
        w1 = jnp.sum(jnp.where(member & (pos == pos1), wt_t, 0.0), axis=0, keepdims=True)
        w2 = jnp.sum(jnp.where(member & (pos == pos2), wt_t, 0.0), axis=0, keepdims=True)
        w2 = jnp.where(pos2 != pos1, w2, 0.0)
        h = h_ref[...].reshape(n, d)

        def gather_block(b, carry):
            p = (b * MOE_ROWS + _iota2((MOE_ROWS, n), 0)).astype(F32)
            onehot = jnp.where((pos1 == p) | (pos2 == p), 1.0, 0.0).astype(BF16)
            rows = pl.ds(pl.multiple_of(b * MOE_ROWS, MOE_ROWS), MOE_ROWS)
            xg_scr[rows, :] = jnp.dot(onehot, h, preferred_element_type=F32).astype(BF16)
            return carry

        lax.fori_loop(0, npos // MOE_ROWS, gather_block, 0)
        z_t = jnp.concatenate([pos1, pos2, w1, w2, jnp.zeros((LANES - 4, n), F32)], axis=0).T
        p1c, p2c, w1c, w2c = (z_t[:, i:i + 1] for i in range(4))
        for b in range(npos // LANES):
            p = (b * LANES + _iota2((n, LANES), 1)).astype(F32)
            gtw_scr[:, b * LANES:(b + 1) * LANES] = (jnp.where(p1c == p, w1c, 0.0)
                                                     + jnp.where(p2c == p, w2c, 0.0)).astype(BF16)
        xg_scr[npos:, :] = jnp.zeros((MOE_ROWS, d), BF16)
        yw_scr[...] = jnp.zeros_like(yw_scr)

    for j in range(wgu_ref.shape[0]):
        ee = e * wgu_ref.shape[0] + j
        first = off_scr[pl.ds(ee, 1), 0:1].astype(jnp.int32)[0, 0]
        last = (off_scr[pl.ds(ee, 1), 0:1] + cnt_scr[pl.ds(ee, 1), 0:1]).astype(jnp.int32)[0, 0]
        start = (first // BF16_ROWS) * BF16_ROWS

        def expert_window(w, carry, j=j, first=first, last=last, start=start):
            r0 = start + w * MOE_ROWS
            rows = pl.ds(pl.multiple_of(r0, BF16_ROWS), MOE_ROWS)
            gu = jnp.dot(xg_scr[rows, :], wgu_ref[j], preferred_element_type=F32)
            act = (_silu(gu[:, :D_EXPERT]) * gu[:, D_EXPERT:]).astype(BF16)
            y = jnp.dot(act, wd_ref[j], preferred_element_type=F32).astype(BF16)
            p = r0 + _iota2((MOE_ROWS, 1), 0)
            yw_scr[rows, :] = jnp.where((p >= first) & (p < last), y, yw_scr[rows, :])
            return carry

        lax.fori_loop(0, (last - start + MOE_ROWS - 1) // MOE_ROWS, expert_window, 0)

    @pl.when(e == pl.num_programs(2) - 1)
    def _():
        moe = jnp.dot(gtw_scr[...], yw_scr[:npos, :], preferred_element_type=F32)
        x2 = x1_ref[...] + gt_ref[...] * moe.reshape(bb, tt, d)
        ms = jnp.mean(x2 * x2, axis=-1, keepdims=True)
        y_ref[...] = x2 * lax.rsqrt(ms + EPS) * fw_ref[...]


def _moe(h, wt, x1, gt, fw, w_gu, w_down, bb, tt):
    b, t, d = x1.shape
    es = MOE_EXPERTS_PER_STEP
    grid = (b // bb, t // tt, N_EXPERTS // es)
    npos = 2 * bb * tt

    def tok_spec(w):
        return pl.BlockSpec((bb, tt, w), lambda i, j, e: (i, j, 0))

    return pl.pallas_call(
        _moe_kernel,
        grid=grid,
        in_specs=[tok_spec(d), tok_spec(LANES), tok_spec(d),
                  pl.BlockSpec((bb, 1, d), lambda i, j, e: (i, 0, 0)),
                  pl.BlockSpec((1, 1, d), lambda i, j, e: (0, 0, 0)),
                  pl.BlockSpec((es, d, 2 * D_EXPERT), lambda i, j, e: (e, 0, 0)),
                  pl.BlockSpec((es, D_EXPERT, d), lambda i, j, e: (e, 0, 0))],
        out_specs=tok_spec(d),
        out_shape=jax.ShapeDtypeStruct((b, t, d), F32),
        scratch_shapes=[pltpu.VMEM((npos + MOE_ROWS, d), BF16), pltpu.VMEM((bb * tt, npos), BF16),
                        pltpu.VMEM((npos + MOE_ROWS, d), BF16), pltpu.VMEM((LANES, LANES), F32),
                        pltpu.VMEM((LANES, LANES), F32)],
        compiler_params=pltpu.CompilerParams(dimension_semantics=("arbitrary",) * 3,
                                             vmem_limit_bytes=MOE_VMEM_LIMIT),
        name="moe",
    )(h, wt, x1, gt, fw, w_gu, w_down)


def _layer(x, mod, conv0, s_gdn0, shift0, s_rwkv0, p, *, t_valid, bb, tt, moe_tile, bsz, rows_per_group, cd,
           chunk):
    sh_m, sc_m, gt_m, sh_f, sc_f, gt_f = mod
    cols = _inproj(x, sc_m, sh_m, p['norm_mix_w'], p['w_cols'], bb, tt)
    o_a, s_gdn = _gdn(cols, conv0, s_gdn0, p['gdn_conv_w'], p['alog_rep'], p['dtb_rep'], p['gdn_nw_rep'],
                      bsz, rows_per_group, cd, chunk, t_valid)
    o_b, s_rwkv = _rwkv(cols, shift0, s_rwkv0, p['rwkv_mu'], p['rwkv_vecs'], p['rwkv_w_up'], p['rwkv_a_up'],
                        p['rwkv_g_up'], bsz, rows_per_group, cd, chunk, t_valid)
    x1, h2, wt = _outproj(o_a, o_b, x, gt_m, sc_f, sh_f, p['norm_ffn_w'], p['w_out_a'], p['w_out_b'],
                          p['w_router'], p['b_router'], bb, tt)
    y = _moe(h2, wt, x1, gt_f, p['final_norm_w'], p['w_gu'], p['w_down'], *moe_tile)
    t_last = x.shape[1] if t_valid is None else t_valid
    conv_new = cols[:, t_last - (CONV_W - 1):t_last, :GDN_QKV]
    shift_new = cols[:, t_last - 1, COL_RW:]
    return y, conv_new, s_gdn, shift_new, s_rwkv


def kernel(x_prompt, x_sample, state_gdn_conv, state_gdn, state_rwkv_shift, state_rwkv, c_prompt, c_sample, ada_w, ada_b, norm_mix_w, w_in, gdn_conv_w, gdn_a_log, gdn_dt_bias, gdn_norm_w, rwkv_mu, rwkv_w0, rwkv_w_up, rwkv_a0, rwkv_a_up, rwkv_g_up, rwkv_k_k, rwkv_k_a, rwkv_r_k, rwkv_ln_w, rwkv_ln_b, w_out, norm_ffn_w, router_group_w, router_group_b, router_expert_w, router_expert_b, expert_w_gate_up, expert_w_down, final_norm_w):
    depth = ada_w.shape[0]
    assert depth == 1
    l = 0
    b_p, t_p, d = x_prompt.shape
    b_s, t_s, _ = x_sample.shape
    gdn_cols = GDN_QKV + D_MIX_HALF + 2 * N_HEADS

    w = w_in[l]
    rep = jnp.repeat(jnp.arange(N_HEADS), HEAD_DIM)
    beta0 = GDN_QKV + D_MIX_HALF
    w_cols = jnp.concatenate([w[:, :beta0], w[:, beta0 + rep], w[:, beta0 + N_HEADS + rep], w[:, gdn_cols:]],
                             axis=1).astype(BF16)
    row = lambda a: a.reshape(1, -1)
    w_router = jnp.zeros((d, LANES), F32)
    w_router = w_router.at[:, :N_EXPERTS].set(router_expert_w[l]).at[:, N_EXPERTS:N_EXPERTS + N_GROUPS].set(
        router_group_w[l])
    b_router = jnp.zeros((1, LANES), F32)
    b_router = b_router.at[0, :N_EXPERTS].set(router_expert_b[l]).at[0, N_EXPERTS:N_EXPERTS + N_GROUPS].set(
        router_group_b[l])
    p = {
        'norm_mix_w': norm_mix_w[l].reshape(1, 1, d), 'w_cols': w_cols,
        'gdn_conv_w': gdn_conv_w[l], 'alog_rep': row(gdn_a_log[l][rep]), 'dtb_rep': row(gdn_dt_bias[l][rep]),
        'gdn_nw_rep': row(jnp.tile(gdn_norm_w[l], N_HEADS)),
        'rwkv_mu': row(rwkv_mu[l]),
        'rwkv_vecs': tuple(row(a) for a in (rwkv_w0[l], rwkv_a0[l], rwkv_k_k[l], rwkv_k_a[l], rwkv_r_k[l],
                                            rwkv_ln_w[l], rwkv_ln_b[l])),
        'rwkv_w_up': rwkv_w_up[l], 'rwkv_a_up': rwkv_a_up[l], 'rwkv_g_up': rwkv_g_up[l],
        'w_out_a': w_out[l][:D_MIX_HALF].astype(BF16), 'w_out_b': w_out[l][D_MIX_HALF:].astype(BF16),
        'norm_ffn_w': norm_ffn_w[l].reshape(1, 1, d), 'w_router': w_router, 'b_router': b_router,
        'w_gu': expert_w_gate_up[l].astype(BF16), 'w_down': expert_w_down[l].astype(BF16),
        'final_norm_w': final_norm_w.reshape(1, 1, d),
    }

    mod = _ada(jnp.concatenate([c_prompt, c_sample], axis=0), ada_w[l], ada_b[l])
    mod_p = tuple(m.reshape(b_p, 1, d) for m in jnp.split(mod[:b_p], 6, axis=-1))
    mod_s = tuple(m.reshape(b_s, 1, d) for m in jnp.split(mod[b_p:], 6, axis=-1))

    zc = jnp.zeros((b_p, CONV_W - 1, GDN_QKV), F32)
    zs = jnp.zeros((b_p, N_HEADS, HEAD_DIM, HEAD_DIM), F32)
    zsh = jnp.zeros((b_p, 1, RWKV_COLS), F32)
    y_p, conv_p, sg_p, shift_p, sr_p = _layer(x_prompt, mod_p, zc, zs, zsh, zs, p, t_valid=None,
                                              bb=1, tt=512, moe_tile=(1, 1024), bsz=4, rows_per_group=4,
                                              cd=CHUNK, chunk=CHUNK)

    t_pad = SUBLANES
    x_s = jnp.pad(x_sample, ((0, 0), (0, t_pad - t_s), (0, 0)))
    y_s, conv_s, sg_s, shift_s, sr_s = _layer(x_s, mod_s, state_gdn_conv[l], state_gdn[l],
                                              state_rwkv_shift[l].reshape(b_s, 1, RWKV_COLS), state_rwkv[l], p,
                                              t_valid=t_s, bb=64, tt=t_pad, moe_tile=(b_s, t_pad), bsz=8,
                                              rows_per_group=4, cd=t_pad,
                                              chunk=t_pad)
    y_s = y_s[:, :t_s]
    return (y_p, y_s, conv_p[None], sg_p[None], shift_p[None], sr_p[None],
            conv_s[None], sg_s[None], shift_s[None], sr_s[None])
```

```python
import functools

import jax
import jax.numpy as jnp
from jax import lax
from jax.experimental import pallas as pl
from jax.experimental.pallas import tpu as pltpu

F32 = jnp.float32
BF16 = jnp.bfloat16
HI = lax.Precision.HIGHEST

D_MODEL = 1024
HEAD_DIM = 64
N_HEADS = 8
D_MIX_HALF = N_HEADS * HEAD_DIM
CONV_W = 4
LORA_W = 64
LORA_A = 64
LORA_G = 128
N_GROUPS = 4
EXP_PER_GROUP = 8
N_EXPERTS = N_GROUPS * EXP_PER_GROUP
D_EXPERT = 256
EPS = 1e-6
GN_EPS = HEAD_DIM * 1e-5

LANES = 128
SUBLANES = 8
BF16_ROWS = 16
GDN_QKV = 3 * D_MIX_HALF
RWKV_RKV = 3 * D_MIX_HALF
RWKV_COLS = RWKV_RKV + LORA_W + LORA_A + LORA_G
LORA_COLS = LORA_W + LORA_A + LORA_G

COL_Z = GDN_QKV
COL_B = COL_Z + D_MIX_HALF
COL_A = COL_B + D_MIX_HALF
COL_RW = COL_A + D_MIX_HALF
COL_LORA = COL_RW + RWKV_RKV
N_COLS = COL_LORA + LORA_COLS
CHUNK = 64

VMEM_LIMIT = 48 * 1024 * 1024
MOE_VMEM_LIMIT = 58 * 1024 * 1024


def _dot(a, b, prec=HI):
    return jnp.dot(a, b, preferred_element_type=F32, precision=prec)


def _bf(a):
    return a.astype(BF16)


def _mm(a, b):
    return jnp.dot(_bf(a), _bf(b), preferred_element_type=F32)


def _mm_nt(a, b):
    return lax.dot_general(_bf(a), _bf(b), (((1,), (1,)), ((), ())), preferred_element_type=F32)


def _mm_tn(a, b):
    return lax.dot_general(_bf(a), _bf(b), (((0,), (0,)), ((), ())), preferred_element_type=F32)


def _split2(a):
    hi = a.astype(BF16)
    return hi, (a - hi.astype(F32)).astype(BF16)


def _sigmoid(x):
    return 1.0 / (1.0 + jnp.exp(-x))


def _silu(x):
    return x * _sigmoid(x)


def _softplus(x):
    return jnp.maximum(x, 0.0) + jnp.log1p(jnp.exp(-jnp.abs(x)))


def _iota2(shape, dim):
    return lax.broadcasted_iota(jnp.int32, shape, dim)


def _head_block_ones():
    r = _iota2((LANES, LANES), 0)
    c = _iota2((LANES, LANES), 1)
    sh = HEAD_DIM.bit_length() - 1
    return jnp.where((r >> sh) == (c >> sh), 1.0, 0.0).astype(F32)


def _head_sums(x):
    ones = _bf(_head_block_ones())
    hi, lo = _split2(x)
    parts = []
    for p in range(x.shape[1] // LANES):
        s = slice(p * LANES, (p + 1) * LANES)
        parts.append(jnp.dot(hi[:, s], ones, preferred_element_type=F32)
                     + jnp.dot(lo[:, s], ones, preferred_element_type=F32))
    return jnp.concatenate(parts, axis=1)


def _cumsum_rows(x):
    n = x.shape[0]
    r = _iota2((n, n), 0)
    c = _iota2((n, n), 1)
    tri = jnp.where(r >= c, 1.0, 0.0).astype(BF16)
    x1 = x.astype(BF16)
    rem = x - x1.astype(F32)
    x2 = rem.astype(BF16)
    x3 = (rem - x2.astype(F32)).astype(BF16)
    return (jnp.dot(tri, x1, preferred_element_type=F32) + jnp.dot(tri, x2, preferred_element_type=F32)
            + jnp.dot(tri, x3, preferred_element_type=F32))


def _unit_lower_inverses(lows, n, mm):
    r = _iota2((n, n), 0)
    c = _iota2((n, n), 1)
    eye = jnp.where(r == c, 1.0, 0.0)
    pair = (r >> 1) == (c >> 1)
    invs = [eye - jnp.where(pair, low, 0.0) for low in lows]
    s = 2
    while s < n:
        sh = s.bit_length()
        sel = ((r >> sh) == (c >> sh)) & ((r & (2 * s - 1)) >= s) & ((c & (2 * s - 1)) < s)
        prods = [mm(jnp.where(sel, low, 0.0), inv) for low, inv in zip(lows, invs)]
        invs = [inv - mm(inv, prod) for inv, prod in zip(invs, prods)]
        s *= 2
    return invs


def _pad_rows(a, n):
    if a.shape[0] == n:
        return a
    return jnp.concatenate([a, jnp.zeros((n - a.shape[0], a.shape[1]), a.dtype)], axis=0)


def _for_row_groups(n, group, body):
    if n == group:
        body(list(range(n)))
    else:
        def step(i, carry):
            body([i * group + j for j in range(group)])
            return carry
        lax.fori_loop(0, n // group, step, 0)


def _ada_kernel(c_ref, w_ref, b_ref, o_ref):
    o_ref[...] = _dot(_silu(c_ref[...]), w_ref[...]) + b_ref[...]


def _ada(c_all, ada_w, ada_b):
    n, d = c_all.shape
    nout = ada_w.shape[1]
    tn = 1536
    return pl.pallas_call(
        _ada_kernel,
        grid=(nout // tn,),
        in_specs=[pl.BlockSpec((n, d), lambda j: (0, 0)),
                  pl.BlockSpec((d, tn), lambda j: (0, j)),
                  pl.BlockSpec((1, tn), lambda j: (0, j))],
        out_specs=pl.BlockSpec((n, tn), lambda j: (0, j)),
        out_shape=jax.ShapeDtypeStruct((n, nout), F32),
        compiler_params=pltpu.CompilerParams(dimension_semantics=("arbitrary",), vmem_limit_bytes=VMEM_LIMIT),
        name="ada",
    )(c_all, ada_w, ada_b.reshape(1, nout))


def _modulated_norm(x, nw, sc, sh):
    ms = jnp.mean(x * x, axis=-1, keepdims=True)
    return (x * lax.rsqrt(ms + EPS) * nw) * (1.0 + sc) + sh


def _inproj_kernel(x_ref, sc_ref, sh_ref, nw_ref, w_ref, o_ref):
    bb, tt, d = x_ref.shape
    h = _modulated_norm(x_ref[...], nw_ref[...], sc_ref[...], sh_ref[...])
    h = h.reshape(bb * tt, d).astype(BF16)
    o = jnp.dot(h, w_ref[...], preferred_element_type=F32)
    o_ref[...] = o.reshape(bb, tt, o.shape[-1])


def _inproj(x, sc, sh, nw, w_cols, bb, tt):
    b, t, d = x.shape
    tn = N_COLS // 2
    grid = (2, b // bb, t // tt)
    return pl.pallas_call(
        _inproj_kernel,
        grid=grid,
        in_specs=[pl.BlockSpec((bb, tt, d), lambda n, i, j: (i, j, 0)),
                  pl.BlockSpec((bb, 1, d), lambda n, i, j: (i, 0, 0)),
                  pl.BlockSpec((bb, 1, d), lambda n, i, j: (i, 0, 0)),
                  pl.BlockSpec((1, 1, d), lambda n, i, j: (0, 0, 0)),
                  pl.BlockSpec((d, tn), lambda n, i, j: (0, n))],
        out_specs=pl.BlockSpec((bb, tt, tn), lambda n, i, j: (i, j, n)),
        out_shape=jax.ShapeDtypeStruct((b, t, N_COLS), F32),
        compiler_params=pltpu.CompilerParams(dimension_semantics=("arbitrary",) * 3, vmem_limit_bytes=VMEM_LIMIT),
        name="inproj",
    )(x, sc, sh, nw, w_cols)


def _gdn_kernel(qkv_ref, z_ref, b_ref, a_ref, c0_ref, cw_ref, alog_ref, dtb_ref, nw_ref, s0_ref,
                o_ref, s_ref, hist_scr, *, chunk, t_valid, rows_per_group):
    bsz, cd, _ = qkv_ref.shape
    c = pl.program_id(1)
    hist_lo = SUBLANES - (CONV_W - 1)

    @pl.when(c == 0)
    def _():
        s_ref[...] = s0_ref[...]
        hist_scr[:, hist_lo:SUBLANES, :] = c0_ref[...]

    row = _iota2((chunk, chunk), 0)
    col = _iota2((chunk, chunk), 1)
    causal = row >= col
    strict = row > col

    def prep(bi):
        u = qkv_ref[bi]
        hist_scr[bi, SUBLANES:SUBLANES + cd, :] = u
        y = u * cw_ref[CONV_W - 1:CONV_W, :]
        for i in range(CONV_W - 1):
            y = y + hist_scr[bi, hist_lo + i:hist_lo + i + cd, :] * cw_ref[i:i + 1, :]
        hist_scr[bi, hist_lo:SUBLANES, :] = hist_scr[bi, hist_lo + cd:SUBLANES + cd, :]
        qkv = _pad_rows(_silu(y), chunk)
        q, k, v = (qkv[:, i * D_MIX_HALF:(i + 1) * D_MIX_HALF] for i in range(3))
        q = q * lax.rsqrt(_head_sums(q * q) + EPS) * (HEAD_DIM ** -0.5)
        k = k * lax.rsqrt(_head_sums(k * k) + EPS)
        beta = _sigmoid(_pad_rows(b_ref[bi], chunk))
        g = -jnp.exp(alog_ref[...]) * _softplus(_pad_rows(a_ref[bi], chunk) + dtb_ref[...])
        if t_valid is not None:
            valid = (c * cd + _iota2((chunk, D_MIX_HALF), 0)) < t_valid
            q, k, v = (jnp.where(valid, x, 0.0) for x in (q, k, v))
            beta = jnp.where(valid, beta, 0.0)
            g = jnp.where(valid, g, 0.0)
        gcum = _cumsum_rows(g)
        eg = jnp.exp(gcum)
        g_last = gcum[chunk - 1:chunk, :]
        return dict(q=q, k=k, beta=beta, gcum=gcum, rhs_v=beta * v, rhs_k=beta * eg * k, q_dec=eg * q,
                    k_dec=jnp.exp(g_last - gcum) * k, eg_last=jnp.exp(g_last))

    def group(bis):
        pre = [prep(bi) for bi in bis]
        chains = [(r, h) for r in range(len(bis)) for h in range(N_HEADS)]
        n = range(len(chains))

        def head(name, i, width=HEAD_DIM):
            r, h = chains[i]
            return pre[r][name][:, h * HEAD_DIM:h * HEAD_DIM + width]

        g_is = [head('gcum', i, chunk) for i in n]
        decays = [jnp.where(causal, jnp.exp(jnp.minimum(g_i - g_i.T, 0.0)), 0.0) for g_i in g_is]
        qk_kks = [_mm_nt(jnp.concatenate([head('k', i), head('q', i)], axis=0), head('k', i)) for i in n]
        lows = [jnp.where(strict, head('beta', i, chunk) * decays[i] * qk_kks[i][:chunk], 0.0) for i in n]
        t_invs = _unit_lower_inverses(lows, chunk, _mm)
        sols = [_mm(t_invs[i], jnp.concatenate([head('rhs_v', i), head('rhs_k', i)], axis=1)) for i in n]
        states = [s_ref[bis[r], h] for r, h in chains]
        wss = [_mm(jnp.concatenate([sols[i][:, HEAD_DIM:], head('q_dec', i)], axis=0), states[i])
               for i in n]
        u_news = [sols[i][:, :HEAD_DIM] - wss[i][:chunk] for i in n]
        outs = [wss[i][chunk:] + _mm(qk_kks[i][chunk:] * decays[i], u_news[i]) for i in n]
        s_news = [head('eg_last', i) * states[i] + _mm_tn(head('k_dec', i), u_news[i]) for i in n]
        for i, (r, h) in enumerate(chains):
            s_ref[bis[r], h] = s_news[i]
        for r, bi in enumerate(bis):
            o = jnp.concatenate(outs[r * N_HEADS:(r + 1) * N_HEADS], axis=1)
            o = o * lax.rsqrt(_head_sums(o * o) * (1.0 / HEAD_DIM) + EPS) * nw_ref[...]
            o = o * _silu(_pad_rows(z_ref[bi], chunk))
            o_ref[bi] = o[:cd]

    _for_row_groups(bsz, rows_per_group, group)


def _gdn(cols, conv0, s0, conv_w, alog_rep, dtb_rep, nw_rep, bsz, rows_per_group, cd, chunk, t_valid):
    b, t, _ = cols.shape

    def col_spec(width, start):
        return pl.BlockSpec((bsz, cd, width), lambda i, c: (i, c, start // width))

    vec_spec = pl.BlockSpec((1, D_MIX_HALF), lambda i, c: (0, 0))
    state_spec = pl.BlockSpec((bsz, N_HEADS, HEAD_DIM, HEAD_DIM), lambda i, c: (i, 0, 0, 0))
    kern = functools.partial(_gdn_kernel, chunk=chunk, t_valid=t_valid, rows_per_group=rows_per_group)
    return pl.pallas_call(
        kern,
        grid=(b // bsz, t // cd),
        in_specs=[col_spec(GDN_QKV, 0), col_spec(D_MIX_HALF, COL_Z), col_spec(D_MIX_HALF, COL_B),
                  col_spec(D_MIX_HALF, COL_A),
                  pl.BlockSpec((bsz, CONV_W - 1, GDN_QKV), lambda i, c: (i, 0, 0)),
                  pl.BlockSpec((CONV_W, GDN_QKV), lambda i, c: (0, 0)),
                  vec_spec, vec_spec, vec_spec, state_spec],
        out_specs=[pl.BlockSpec((bsz, cd, D_MIX_HALF), lambda i, c: (i, c, 0)), state_spec],
        out_shape=[jax.ShapeDtypeStruct((b, t, D_MIX_HALF), F32),
                   jax.ShapeDtypeStruct((b, N_HEADS, HEAD_DIM, HEAD_DIM), F32)],
        scratch_shapes=[pltpu.VMEM((bsz, SUBLANES + cd, GDN_QKV), F32)],
        compiler_params=pltpu.CompilerParams(dimension_semantics=("arbitrary",) * 2, vmem_limit_bytes=VMEM_LIMIT),
        name="gdn",
    )(cols, cols, cols, cols, conv0, conv_w, alog_rep, dtb_rep, nw_rep, s0)


def _rwkv_kernel(rkv_ref, l_ref, p_ref, pl_ref, mu_ref, mul_ref, w0_ref, a0_ref, kk_ref, ka_ref, rk_ref,
                 lnw_ref, lnb_ref, wup_ref, aup_ref, gup_ref, s0_ref, o_ref, s_ref, hist_scr, histl_scr,
                 *, chunk, t_valid, rows_per_group):
    bsz, cd, _ = rkv_ref.shape
    c = pl.program_id(1)
    prev_row = SUBLANES - 1

    @pl.when(c == 0)
    def _():
        s_ref[...] = s0_ref[...]
        hist_scr[:, prev_row:SUBLANES, :] = p_ref[...]
        histl_scr[:, prev_row:SUBLANES, :] = pl_ref[...]

    row = _iota2((chunk, chunk), 0)
    col = _iota2((chunk, chunk), 1)
    causal = row >= col
    strict = row > col

    def shifted(bi, u_ref, scr, m_ref):
        u = u_ref[bi]
        scr[bi, SUBLANES:SUBLANES + cd, :] = u
        prev = scr[bi, prev_row:prev_row + cd, :]
        scr[bi, prev_row:SUBLANES, :] = scr[bi, prev_row + cd:SUBLANES + cd, :]
        return _pad_rows(u + (prev - u) * m_ref[...], chunk)

    def prep(bi):
        rkv = shifted(bi, rkv_ref, hist_scr, mu_ref)
        r, k, v = (rkv[:, i * D_MIX_HALF:(i + 1) * D_MIX_HALF] for i in range(3))
        xl = shifted(bi, l_ref, histl_scr, mul_ref)
        dw = xl[:, :LORA_W]
        da = xl[:, LORA_W:LORA_W + LORA_A]
        dg = xl[:, LORA_W + LORA_A:]
        w_log = -_softplus(-(w0_ref[...] + _mm(jnp.tanh(dw), wup_ref[...]))) - 0.5
        lw = -jnp.exp(w_log)
        a = _sigmoid(a0_ref[...] + _mm(da, aup_ref[...]))
        gate = _mm(_sigmoid(dg), gup_ref[...])
        kk = k * kk_ref[...]
        kk = kk * lax.rsqrt(_head_sums(kk * kk) + EPS)
        k = k * (1.0 + (a - 1.0) * ka_ref[...])
        if t_valid is not None:
            valid = (c * cd + _iota2((chunk, D_MIX_HALF), 0)) < t_valid
            r, k, v, kk, lw = (jnp.where(valid, x, 0.0) for x in (r, k, v, kk, lw))
        cum = _cumsum_rows(lw)
        e_out = jnp.exp(-cum)
        cum_last = cum[chunk - 1:chunk, :]
        e_rest = jnp.exp(cum_last - cum)
        return dict(r=r, k=k, v=v, gate=gate, a_t=-kk * jnp.exp(cum - lw), b_t=kk * a * e_out, k_t=k * e_out,
                    r_t=r * jnp.exp(cum), b_c=kk * a * e_rest, k_c=k * e_rest, e_last=jnp.exp(cum_last))

    def group(bis):
        pre = [prep(bi) for bi in bis]
        chains = [(r, h) for r in range(len(bis)) for h in range(N_HEADS)]
        n = range(len(chains))

        def head(name, i):
            r, h = chains[i]
            return pre[r][name][:, h * HEAD_DIM:(h + 1) * HEAD_DIM]

        ars = [jnp.concatenate([head('a_t', i), head('r_t', i)], axis=0) for i in n]
        bks = [jnp.concatenate([head('b_t', i), head('k_t', i)], axis=0) for i in n]
        ms = [_mm_nt(ars[i], bks[i]) for i in n]
        t_invs = _unit_lower_inverses([jnp.where(strict, -m[:chunk, :chunk], 0.0) for m in ms], chunk, _mm)
        states = [s_ref[bis[r], h] for r, h in chains]
        arss = [_mm_nt(ars[i], states[i]) for i in n]
        vhs = [head('v', i) for i in n]
        akvs = [_mm(jnp.where(strict, ms[i][:chunk, chunk:], 0.0), vhs[i]) for i in n]
        us = [_mm(t_invs[i], arss[i][:chunk] + akvs[i]) for i in n]
        uvs = [jnp.concatenate([us[i], vhs[i]], axis=0) for i in n]
        causal2 = _iota2((chunk, 2 * chunk), 0) >= (_iota2((chunk, 2 * chunk), 1) & (chunk - 1))
        ys = [arss[i][chunk:] + _mm(jnp.where(causal2, ms[i][chunk:], 0.0), uvs[i]) for i in n]
        s_news = [states[i] * head('e_last', i)
                  + _mm_tn(uvs[i], jnp.concatenate([head('b_c', i), head('k_c', i)], axis=0)) for i in n]
        for i, (r, h) in enumerate(chains):
            s_ref[bis[r], h] = s_news[i]
        for r, bi in enumerate(bis):
            p = pre[r]
            y = jnp.concatenate(ys[r * N_HEADS:(r + 1) * N_HEADS], axis=1)
            mean = _head_sums(y) * (1.0 / HEAD_DIM)
            dy = y - mean
            var = _head_sums(dy * dy) * (1.0 / HEAD_DIM)
            y = dy * lax.rsqrt(var + GN_EPS) * lnw_ref[...] + lnb_ref[...]
            bonus = _head_sums(p['r'] * p['k'] * rk_ref[...]) * p['v']
            o_ref[bi] = ((y + bonus) * p['gate'])[:cd]

    _for_row_groups(bsz, rows_per_group, group)


def _rwkv(cols, shift0, s0, mu, vecs, w_up, a_up, g_up, bsz, rows_per_group, cd, chunk, t_valid):
    b, t, _ = cols.shape

    def full2(shape):
        return pl.BlockSpec(shape, lambda i, c: (0, 0))

    state_spec = pl.BlockSpec((bsz, N_HEADS, HEAD_DIM, HEAD_DIM), lambda i, c: (i, 0, 0, 0))
    kern = functools.partial(_rwkv_kernel, chunk=chunk, t_valid=t_valid, rows_per_group=rows_per_group)
    return pl.pallas_call(
        kern,
        grid=(b // bsz, t // cd),
        in_specs=[pl.BlockSpec((bsz, cd, RWKV_RKV), lambda i, c: (i, c, COL_RW // RWKV_RKV)),
                  pl.BlockSpec((bsz, cd, LORA_COLS), lambda i, c: (i, c, COL_LORA // LORA_COLS)),
                  pl.BlockSpec((bsz, 1, RWKV_RKV), lambda i, c: (i, 0, 0)),
                  pl.BlockSpec((bsz, 1, LORA_COLS), lambda i, c: (i, 0, RWKV_RKV // LORA_COLS)),
                  pl.BlockSpec((1, RWKV_RKV), lambda i, c: (0, 0)),
                  pl.BlockSpec((1, LORA_COLS), lambda i, c: (0, RWKV_RKV // LORA_COLS))]
                 + [full2((1, D_MIX_HALF))] * 7
                 + [full2((LORA_W, D_MIX_HALF)), full2((LORA_A, D_MIX_HALF)), full2((LORA_G, D_MIX_HALF)),
                    state_spec],
        out_specs=[pl.BlockSpec((bsz, cd, D_MIX_HALF), lambda i, c: (i, c, 0)), state_spec],
        out_shape=[jax.ShapeDtypeStruct((b, t, D_MIX_HALF), F32),
                   jax.ShapeDtypeStruct((b, N_HEADS, HEAD_DIM, HEAD_DIM), F32)],
        scratch_shapes=[pltpu.VMEM((bsz, SUBLANES + cd, RWKV_RKV), F32),
                        pltpu.VMEM((bsz, SUBLANES + cd, LORA_COLS), F32)],
        compiler_params=pltpu.CompilerParams(dimension_semantics=("arbitrary",) * 2, vmem_limit_bytes=VMEM_LIMIT),
        name="rwkv",
    )(cols, cols, shift0, shift0, mu, mu, *vecs, w_up, a_up, g_up, s0)


def _outproj_kernel(oa_ref, ob_ref, x_ref, gt_ref, sc_ref, sh_ref, nw_ref, wa_ref, wb_ref, wr_ref, br_ref,
                    x1_ref, h_ref, wt_ref):
    bb, tt, d = x_ref.shape
    n = bb * tt
    oa = oa_ref[...].reshape(n, D_MIX_HALF).astype(BF16)
    ob = ob_ref[...].reshape(n, D_MIX_HALF).astype(BF16)
    mixed = (jnp.dot(oa, wa_ref[...], preferred_element_type=F32)
             + jnp.dot(ob, wb_ref[...], preferred_element_type=F32))
    x1 = x_ref[...] + gt_ref[...] * mixed.reshape(bb, tt, d)
    x1_ref[...] = x1
    h = _modulated_norm(x1, nw_ref[...], sc_ref[...], sh_ref[...]).reshape(n, d)
    h_ref[...] = h.reshape(bb, tt, d).astype(BF16)

    logits = _dot(h, wr_ref[...]) + br_ref[...]
    lane_i = _iota2((n, LANES), 1)
    lane = lane_i.astype(F32)
    lane_grp = (lane_i >> (EXP_PER_GROUP.bit_length() - 1)).astype(F32)
    neg = jnp.float32(-jnp.inf)
    big = jnp.float32(LANES)
    is_grp = (lane_i >= N_EXPERTS) & (lane_i < N_EXPERTS + N_GROUPS)
    gl = jnp.where(is_grp, logits, neg)
    gmax = jnp.max(gl, axis=-1, keepdims=True)
    grp = jnp.min(jnp.where(gl == gmax, lane, big), axis=-1, keepdims=True) - N_EXPERTS
    g_prob = 1.0 / jnp.sum(jnp.where(is_grp, jnp.exp(gl - gmax), 0.0), axis=-1, keepdims=True)
    in_grp = (lane_i < N_EXPERTS) & (lane_grp == grp)
    el = jnp.where(in_grp, logits, neg)
    m1 = jnp.max(el, axis=-1, keepdims=True)
    i1 = jnp.min(jnp.where(el == m1, lane, big), axis=-1, keepdims=True)
    el2 = jnp.where(lane == i1, neg, el)
    m2 = jnp.max(el2, axis=-1, keepdims=True)
    i2 = jnp.min(jnp.where(el2 == m2, lane, big), axis=-1, keepdims=True)
    e2 = jnp.exp(m2 - m1)
    p1 = g_prob / (1.0 + e2)
    p2 = g_prob * e2 / (1.0 + e2)
    wt = jnp.where(lane == i1, p1, 0.0) + jnp.where(lane == i2, p2, 0.0)
    wt_ref[...] = wt.reshape(bb, tt, LANES)


def _outproj(oa, ob, x, gt, sc, sh, nw, wa, wb, wr, br, bb, tt):
    b, t, d = x.shape
    grid = (b // bb, t // tt)

    def tok_spec(w):
        return pl.BlockSpec((bb, tt, w), lambda i, j: (i, j, 0))

    mod_spec = pl.BlockSpec((bb, 1, d), lambda i, j: (i, 0, 0))

    def full2(shape):
        return pl.BlockSpec(shape, lambda i, j: (0, 0))

    return pl.pallas_call(
        _outproj_kernel,
        grid=grid,
        in_specs=[tok_spec(D_MIX_HALF), tok_spec(D_MIX_HALF), tok_spec(d), mod_spec, mod_spec, mod_spec,
                  pl.BlockSpec((1, 1, d), lambda i, j: (0, 0, 0)),
                  full2((D_MIX_HALF, d)), full2((D_MIX_HALF, d)), full2((d, LANES)), full2((1, LANES))],
        out_specs=[tok_spec(d), tok_spec(d), tok_spec(LANES)],
        out_shape=[jax.ShapeDtypeStruct((b, t, d), F32), jax.ShapeDtypeStruct((b, t, d), BF16),
                   jax.ShapeDtypeStruct((b, t, LANES), F32)],
        compiler_params=pltpu.CompilerParams(dimension_semantics=("arbitrary",) * 2, vmem_limit_bytes=VMEM_LIMIT),
        name="outproj",
    )(oa, ob, x, gt, sc, sh, nw, wa, wb, wr, br)


MOE_ROWS = 128
MOE_EXPERTS_PER_STEP = 4


def _moe_kernel(h_ref, wt_ref, x1_ref, gt_ref, fw_ref, wgu_ref, wd_ref, y_ref,
                xg_scr, gtw_scr, yw_scr, cnt_scr, off_scr):
    bb, tt, d = x1_ref.shape
    n = bb * tt
    npos = 2 * n
    e = pl.program_id(2)

    @pl.when(e == 0)
    def _():
        wt_t = wt_ref[...].reshape(n, LANES).T
        member = wt_t > 0.0
        ones = jnp.where(member, 1.0, 0.0)
        upper = jnp.where(_iota2((n, n), 0) < _iota2((n, n), 1), 1.0, 0.0).astype(BF16)
        rank = jnp.dot(_bf(ones), upper, preferred_element_type=F32)
        cnt = jnp.broadcast_to(jnp.sum(ones, axis=1, keepdims=True), (LANES, LANES))
        lower = jnp.where(_iota2((LANES, LANES), 0) > _iota2((LANES, LANES), 1), 1.0, 0.0)
        off = _dot(lower, cnt)
        cnt_scr[...] = cnt
        off_scr[...] = off
        pos = off[:, :1] + rank
        pos1 = jnp.min(jnp.where(member, pos, jnp.float32(4 * n)), axis=0, keepdims=True)
        pos2 = jnp.max(jnp.where(member, pos, -1.0), axis=0, keepdims=True)
        w1 = jnp.sum(jnp.where(member & (pos == pos1), wt_t, 0.0), axis=0, keepdims=True)
        w2 = jnp.sum(jnp.where(member & (pos == pos2), wt_t, 0.0), axis=0, keepdims=True)
        w2 = jnp.where(pos2 != pos1, w2, 0.0)
        h = h_ref[...].reshape(n, d)

        def onehot_block(b, carry):
            p = (b * MOE_ROWS + _iota2((MOE_ROWS, n), 0)).astype(F32)
            rows = pl.ds(pl.multiple_of(b * MOE_ROWS, MOE_ROWS), MOE_ROWS)
            yw_scr[rows, :n] = jnp.where((pos1 == p) | (pos2 == p), 1.0, 0.0).astype(BF16)
            return carry

        lax.fori_loop(0, npos // MOE_ROWS, onehot_block, 0)
        xg_scr[:npos, :] = jnp.dot(yw_scr[:npos, :n], h, preferred_element_type=F32).astype(BF16)
        z_t = jnp.concatenate([pos1, pos2, w1, w2, jnp.zeros((LANES - 4, n), F32)], axis=0).T
        p1c, p2c, w1c, w2c = (z_t[:, i:i + 1] for i in range(4))
        for b in range(npos // LANES):
            p = (b * LANES + _iota2((n, LANES), 1)).astype(F32)
            gtw_scr[:, b * LANES:(b + 1) * LANES] = (jnp.where(p1c == p, w1c, 0.0)
                                                     + jnp.where(p2c == p, w2c, 0.0)).astype(BF16)
        xg_scr[npos:, :] = jnp.zeros((MOE_ROWS, d), BF16)
        yw_scr[...] = jnp.zeros_like(yw_scr)

    es = range(wgu_ref.shape[0])
    offs = [off_scr[pl.ds(e * len(es) + j, 1), 0:1] for j in es]
    firsts = [o.astype(jnp.int32)[0, 0] for o in offs]
    lasts = [(offs[j] + cnt_scr[pl.ds(e * len(es) + j, 1), 0:1]).astype(jnp.int32)[0, 0] for j in es]
    starts = [(f // BF16_ROWS) * BF16_ROWS for f in firsts]

    def window_rows(r0):
        return pl.ds(pl.multiple_of(r0, BF16_ROWS), MOE_ROWS)

    def store_rows(j, r0, y):
        p = r0 + _iota2((MOE_ROWS, 1), 0)
        rows = window_rows(r0)
        yw_scr[rows, :] = jnp.where((p >= firsts[j]) & (p < lasts[j]), y, yw_scr[rows, :])

    gus = [jnp.dot(xg_scr[window_rows(starts[j]), :], wgu_ref[j], preferred_element_type=F32) for j in es]
    acts = [(_silu(gu[:, :D_EXPERT]) * gu[:, D_EXPERT:]).astype(BF16) for gu in gus]
    ys = [jnp.dot(acts[j], wd_ref[j], preferred_element_type=F32).astype(BF16) for j in es]
    for j in es:
        store_rows(j, starts[j], ys[j])

    for j in es:
        def more_windows(w, carry, j=j):
            r0 = starts[j] + w * MOE_ROWS
            gu = jnp.dot(xg_scr[window_rows(r0), :], wgu_ref[j], preferred_element_type=F32)
            act = (_silu(gu[:, :D_EXPERT]) * gu[:, D_EXPERT:]).astype(BF16)
            store_rows(j, r0, jnp.dot(act, wd_ref[j], preferred_element_type=F32).astype(BF16))
            return carry

        lax.fori_loop(1, (lasts[j] - starts[j] + MOE_ROWS - 1) // MOE_ROWS, more_windows, 0)

    @pl.when(e == pl.num_programs(2) - 1)
    def _():
        moe = jnp.dot(gtw_scr[...], yw_scr[:npos, :], preferred_element_type=F32)
        x2 = x1_ref[...] + gt_ref[...] * moe.reshape(bb, tt, d)
        ms = jnp.mean(x2 * x2, axis=-1, keepdims=True)
        y_ref[...] = x2 * lax.rsqrt(ms + EPS) * fw_ref[...]


def _moe(h, wt, x1, gt, fw, w_gu, w_down, bb, tt):
    b, t, d = x1.shape
    es = MOE_EXPERTS_PER_STEP
    grid = (b // bb, t // tt, N_EXPERTS // es)
    npos = 2 * bb * tt

    def tok_spec(w):
        return pl.BlockSpec((bb, tt, w), lambda i, j, e: (i, j, 0))

    return pl.pallas_call(
        _moe_kernel,
        grid=grid,
        in_specs=[tok_spec(d), tok_spec(LANES), tok_spec(d),
                  pl.BlockSpec((bb, 1, d), lambda i, j, e: (i, 0, 0)),
                  pl.BlockSpec((1, 1, d), lambda i, j, e: (0, 0, 0)),
                  pl.BlockSpec((es, d, 2 * D_EXPERT), lambda i, j, e: (e, 0, 0)),
                  pl.BlockSpec((es, D_EXPERT, d), lambda i, j, e: (e, 0, 0))],
        out_specs=tok_spec(d),
        out_shape=jax.ShapeDtypeStruct((b, t, d), F32),
        scratch_shapes=[pltpu.VMEM((npos + MOE_ROWS, d), BF16), pltpu.VMEM((bb * tt, npos), BF16),
                        pltpu.VMEM((npos + MOE_ROWS, d), BF16), pltpu.VMEM((LANES, LANES), F32),
                        pltpu.VMEM((LANES, LANES), F32)],
        compiler_params=pltpu.CompilerParams(dimension_semantics=("arbitrary",) * 3,
                                             vmem_limit_bytes=MOE_VMEM_LIMIT),
        name="moe",
    )(h, wt, x1, gt, fw, w_gu, w_down)


def _layer(x, mod, conv0, s_gdn0, shift0, s_rwkv0, p, *, t_valid, bb, tt, moe_tile, bsz, rows_per_group, cd,
           chunk):
    sh_m, sc_m, gt_m, sh_f, sc_f, gt_f = mod
    cols = _inproj(x, sc_m, sh_m, p['norm_mix_w'], p['w_cols'], bb, tt)
    o_a, s_gdn = _gdn(cols, conv0, s_gdn0, p['gdn_conv_w'], p['alog_rep'], p['dtb_rep'], p['gdn_nw_rep'],
                      bsz, rows_per_group, cd, chunk, t_valid)
    o_b, s_rwkv = _rwkv(cols, shift0, s_rwkv0, p['rwkv_mu'], p['rwkv_vecs'], p['rwkv_w_up'], p['rwkv_a_up'],
                        p['rwkv_g_up'], bsz, rows_per_group, cd, chunk, t_valid)
    x1, h2, wt = _outproj(o_a, o_b, x, gt_m, sc_f, sh_f, p['norm_ffn_w'], p['w_out_a'], p['w_out_b'],
                          p['w_router'], p['b_router'], bb, tt)
    y = _moe(h2, wt, x1, gt_f, p['final_norm_w'], p['w_gu'], p['w_down'], *moe_tile)
    t_last = x.shape[1] if t_valid is None else t_valid
    conv_new = cols[:, t_last - (CONV_W - 1):t_last, :GDN_QKV]
    shift_new = cols[:, t_last - 1, COL_RW:]
    return y, conv_new, s_gdn, shift_new, s_rwkv


def kernel(x_prompt, x_sample, state_gdn_conv, state_gdn, state_rwkv_shift, state_rwkv, c_prompt, c_sample, ada_w, ada_b, norm_mix_w, w_in, gdn_conv_w, gdn_a_log, gdn_dt_bias, gdn_norm_w, rwkv_mu, rwkv_w0, rwkv_w_up, rwkv_a0, rwkv_a_up, rwkv_g_up, rwkv_k_k, rwkv_k_a, rwkv_r_k, rwkv_ln_w, rwkv_ln_b, w_out, norm_ffn_w, router_group_w, router_group_b, router_expert_w, router_expert_b, expert_w_gate_up, expert_w_down, final_norm_w):
    depth = ada_w.shape[0]
    assert depth == 1
    l = 0
    b_p, t_p, d = x_prompt.shape
    b_s, t_s, _ = x_sample.shape
    gdn_cols = GDN_QKV + D_MIX_HALF + 2 * N_HEADS

    w = w_in[l]
    rep = jnp.repeat(jnp.arange(N_HEADS), HEAD_DIM)
    beta0 = GDN_QKV + D_MIX_HALF
    w_cols = jnp.concatenate([w[:, :beta0], w[:, beta0 + rep], w[:, beta0 + N_HEADS + rep], w[:, gdn_cols:]],
                             axis=1).astype(BF16)
    row = lambda a: a.reshape(1, -1)
    w_router = jnp.zeros((d, LANES), F32)
    w_router = w_router.at[:, :N_EXPERTS].set(router_expert_w[l]).at[:, N_EXPERTS:N_EXPERTS + N_GROUPS].set(
        router_group_w[l])
    b_router = jnp.zeros((1, LANES), F32)
    b_router = b_router.at[0, :N_EXPERTS].set(router_expert_b[l]).at[0, N_EXPERTS:N_EXPERTS + N_GROUPS].set(
        router_group_b[l])
    p = {
        'norm_mix_w': norm_mix_w[l].reshape(1, 1, d), 'w_cols': w_cols,
        'gdn_conv_w': gdn_conv_w[l], 'alog_rep': row(gdn_a_log[l][rep]), 'dtb_rep': row(gdn_dt_bias[l][rep]),
        'gdn_nw_rep': row(jnp.tile(gdn_norm_w[l], N_HEADS)),
        'rwkv_mu': row(rwkv_mu[l]),
        'rwkv_vecs': tuple(row(a) for a in (rwkv_w0[l], rwkv_a0[l], rwkv_k_k[l], rwkv_k_a[l], rwkv_r_k[l],
                                            rwkv_ln_w[l], rwkv_ln_b[l])),
        'rwkv_w_up': rwkv_w_up[l], 'rwkv_a_up': rwkv_a_up[l], 'rwkv_g_up': rwkv_g_up[l],
        'w_out_a': w_out[l][:D_MIX_HALF].astype(BF16), 'w_out_b': w_out[l][D_MIX_HALF:].astype(BF16),
        'norm_ffn_w': norm_ffn_w[l].reshape(1, 1, d), 'w_router': w_router, 'b_router': b_router,
        'w_gu': expert_w_gate_up[l].astype(BF16), 'w_down': expert_w_down[l].astype(BF16),
        'final_norm_w': final_norm_w.reshape(1, 1, d),
    }

    mod = _ada(jnp.concatenate([c_prompt, c_sample], axis=0), ada_w[l], ada_b[l])
    mod_p = tuple(m.reshape(b_p, 1, d) for m in jnp.split(mod[:b_p], 6, axis=-1))
    mod_s = tuple(m.reshape(b_s, 1, d) for m in jnp.split(mod[b_p:], 6, axis=-1))

    zc = jnp.zeros((b_p, CONV_W - 1, GDN_QKV), F32)
    zs = jnp.zeros((b_p, N_HEADS, HEAD_DIM, HEAD_DIM), F32)
    zsh = jnp.zeros((b_p, 1, RWKV_COLS), F32)
    y_p, conv_p, sg_p, shift_p, sr_p = _layer(x_prompt, mod_p, zc, zs, zsh, zs, p, t_valid=None,
                                              bb=1, tt=512, moe_tile=(1, 1024), bsz=4, rows_per_group=4,
                                              cd=CHUNK, chunk=CHUNK)

    t_pad = SUBLANES
    x_s = jnp.pad(x_sample, ((0, 0), (0, t_pad - t_s), (0, 0)))
    y_s, conv_s, sg_s, shift_s, sr_s = _layer(x_s, mod_s, state_gdn_conv[l], state_gdn[l],
                                              state_rwkv_shift[l].reshape(b_s, 1, RWKV_COLS), state_rwkv[l], p,
                                              t_valid=t_s, bb=64, tt=t_pad, moe_tile=(b_s, t_pad), bsz=8,
                                              rows_per_group=8, cd=t_pad,
                                              chunk=t_pad)
    y_s = y_s[:, :t_s]
    return (y_p, y_s, conv_p[None], sg_p[None], shift_p[None], sr_p[None],
            conv_s[None], sg_s[None], shift_s[None], sr_s[None])
```

```python
import functools

import jax
import jax.numpy as jnp
from jax import lax
from jax.experimental import pallas as pl
from jax.experimental.pallas import tpu as pltpu

F32 = jnp.float32
BF16 = jnp.bfloat16
HI = lax.Precision.HIGHEST

D_MODEL = 1024
HEAD_DIM = 64
N_HEADS = 8
D_MIX_HALF = N_HEADS * HEAD_DIM
CONV_W = 4
LORA_W = 64
LORA_A = 64
LORA_G = 128
N_GROUPS = 4
EXP_PER_GROUP = 8
N_EXPERTS = N_GROUPS * EXP_PER_GROUP
D_EXPERT = 256
EPS = 1e-6
GN_EPS = HEAD_DIM * 1e-5

LANES = 128
SUBLANES = 8
BF16_ROWS = 16
GDN_QKV = 3 * D_MIX_HALF
RWKV_RKV = 3 * D_MIX_HALF
RWKV_COLS = RWKV_RKV + LORA_W + LORA_A + LORA_G
LORA_COLS = LORA_W + LORA_A + LORA_G

COL_Z = GDN_QKV
COL_B = COL_Z + D_MIX_HALF
COL_A = COL_B + D_MIX_HALF
COL_RW = COL_A + D_MIX_HALF
COL_LORA = COL_RW + RWKV_RKV
N_COLS = COL_LORA + LORA_COLS
CHUNK = 64

VMEM_LIMIT = 48 * 1024 * 1024
MOE_VMEM_LIMIT = 58 * 1024 * 1024


def _dot(a, b, prec=HI):
    return jnp.dot(a, b, preferred_element_type=F32, precision=prec)


def _bf(a):
    return a.astype(BF16)


def _mm(a, b):
    return jnp.dot(_bf(a), _bf(b), preferred_element_type=F32)


def _mm_nt(a, b):
    return lax.dot_general(_bf(a), _bf(b), (((1,), (1,)), ((), ())), preferred_element_type=F32)


def _mm_tn(a, b):
    return lax.dot_general(_bf(a), _bf(b), (((0,), (0,)), ((), ())), preferred_element_type=F32)


def _split2(a):
    hi = a.astype(BF16)
    return hi, (a - hi.astype(F32)).astype(BF16)


def _sigmoid(x):
    return 1.0 / (1.0 + jnp.exp(-x))


def _silu(x):
    return x * _sigmoid(x)


def _softplus(x):
    return jnp.maximum(x, 0.0) + jnp.log1p(jnp.exp(-jnp.abs(x)))


def _iota2(shape, dim):
    return lax.broadcasted_iota(jnp.int32, shape, dim)


def _head_block_ones():
    r = _iota2((LANES, LANES), 0)
    c = _iota2((LANES, LANES), 1)
    sh = HEAD_DIM.bit_length() - 1
    return jnp.where((r >> sh) == (c >> sh), 1.0, 0.0).astype(F32)


def _head_sums(x):
    ones = _bf(_head_block_ones())
    hi, lo = _split2(x)
    parts = []
    for p in range(x.shape[1] // LANES):
        s = slice(p * LANES, (p + 1) * LANES)
        parts.append(jnp.dot(hi[:, s], ones, preferred_element_type=F32)
                     + jnp.dot(lo[:, s], ones, preferred_element_type=F32))
    return jnp.concatenate(parts, axis=1)


def _cumsum_rows(x):
    n = x.shape[0]
    r = _iota2((n, n), 0)
    c = _iota2((n, n), 1)
    tri = jnp.where(r >= c, 1.0, 0.0).astype(BF16)
    x1 = x.astype(BF16)
    rem = x - x1.astype(F32)
    x2 = rem.astype(BF16)
    x3 = (rem - x2.astype(F32)).astype(BF16)
    return (jnp.dot(tri, x1, preferred_element_type=F32) + jnp.dot(tri, x2, preferred_element_type=F32)
            + jnp.dot(tri, x3, preferred_element_type=F32))


def _unit_lower_inverses(lows, n, mm):
    r = _iota2((n, n), 0)
    c = _iota2((n, n), 1)
    eye = jnp.where(r == c, 1.0, 0.0)
    pair = (r >> 1) == (c >> 1)
    invs = [eye - jnp.where(pair, low, 0.0) for low in lows]
    s = 2
    while s < n:
        sh = s.bit_length()
        sel = ((r >> sh) == (c >> sh)) & ((r & (2 * s - 1)) >= s) & ((c & (2 * s - 1)) < s)
        prods = [mm(jnp.where(sel, low, 0.0), inv) for low, inv in zip(lows, invs)]
        invs = [inv - mm(inv, prod) for inv, prod in zip(invs, prods)]
        s *= 2
    return invs


def _pad_rows(a, n):
    if a.shape[0] == n:
        return a
    return jnp.concatenate([a, jnp.zeros((n - a.shape[0], a.shape[1]), a.dtype)], axis=0)


def _for_row_groups(n, group, body):
    if n == group:
        body(list(range(n)))
    else:
        def step(i, carry):
            body([i * group + j for j in range(group)])
            return carry
        lax.fori_loop(0, n // group, step, 0)


def _ada_kernel(c_ref, w_ref, b_ref, o_ref):
    o_ref[...] = _dot(_silu(c_ref[...]), w_ref[...]) + b_ref[...]


def _ada(c_all, ada_w, ada_b):
    n, d = c_all.shape
    nout = ada_w.shape[1]
    tn = 1536
    return pl.pallas_call(
        _ada_kernel,
        grid=(nout // tn,),
        in_specs=[pl.BlockSpec((n, d), lambda j: (0, 0)),
                  pl.BlockSpec((d, tn), lambda j: (0, j)),
                  pl.BlockSpec((1, tn), lambda j: (0, j))],
        out_specs=pl.BlockSpec((n, tn), lambda j: (0, j)),
        out_shape=jax.ShapeDtypeStruct((n, nout), F32),
        compiler_params=pltpu.CompilerParams(dimension_semantics=("arbitrary",), vmem_limit_bytes=VMEM_LIMIT),
        name="ada",
    )(c_all, ada_w, ada_b.reshape(1, nout))


def _modulated_norm(x, nw, sc, sh):
    ms = jnp.mean(x * x, axis=-1, keepdims=True)
    return (x * lax.rsqrt(ms + EPS) * nw) * (1.0 + sc) + sh


def _inproj_kernel(x_ref, sc_ref, sh_ref, nw_ref, w_ref, o_ref):
    bb, tt, d = x_ref.shape
    h = _modulated_norm(x_ref[...], nw_ref[...], sc_ref[...], sh_ref[...])
    h = h.reshape(bb * tt, d).astype(BF16)
    o = jnp.dot(h, w_ref[...], preferred_element_type=F32)
    o_ref[...] = o.reshape(bb, tt, o.shape[-1])


def _inproj(x, sc, sh, nw, w_cols, bb, tt):
    b, t, d = x.shape
    tn = N_COLS // 2
    grid = (2, b // bb, t // tt)
    return pl.pallas_call(
        _inproj_kernel,
        grid=grid,
        in_specs=[pl.BlockSpec((bb, tt, d), lambda n, i, j: (i, j, 0)),
                  pl.BlockSpec((bb, 1, d), lambda n, i, j: (i, 0, 0)),
                  pl.BlockSpec((bb, 1, d), lambda n, i, j: (i, 0, 0)),
                  pl.BlockSpec((1, 1, d), lambda n, i, j: (0, 0, 0)),
                  pl.BlockSpec((d, tn), lambda n, i, j: (0, n))],
        out_specs=pl.BlockSpec((bb, tt, tn), lambda n, i, j: (i, j, n)),
        out_shape=jax.ShapeDtypeStruct((b, t, N_COLS), F32),
        compiler_params=pltpu.CompilerParams(dimension_semantics=("arbitrary",) * 3, vmem_limit_bytes=VMEM_LIMIT),
        name="inproj",
    )(x, sc, sh, nw, w_cols)


def _gdn_kernel(qkv_ref, z_ref, b_ref, a_ref, c0_ref, cw_ref, alog_ref, dtb_ref, nw_ref, s0_ref,
                o_ref, s_ref, hist_scr, *, chunk, t_valid, rows_per_group):
    bsz, cd, _ = qkv_ref.shape
    c = pl.program_id(1)
    hist_lo = SUBLANES - (CONV_W - 1)

    @pl.when(c == 0)
    def _():
        s_ref[...] = s0_ref[...]
        hist_scr[:, hist_lo:SUBLANES, :] = c0_ref[...]

    row = _iota2((chunk, chunk), 0)
    col = _iota2((chunk, chunk), 1)
    causal = row >= col
    strict = row > col

    def prep(bi):
        u = qkv_ref[bi]
        hist_scr[bi, SUBLANES:SUBLANES + cd, :] = u
        y = u * cw_ref[CONV_W - 1:CONV_W, :]
        for i in range(CONV_W - 1):
            y = y + hist_scr[bi, hist_lo + i:hist_lo + i + cd, :] * cw_ref[i:i + 1, :]
        hist_scr[bi, hist_lo:SUBLANES, :] = hist_scr[bi, hist_lo + cd:SUBLANES + cd, :]
        qkv = _pad_rows(_silu(y), chunk)
        q, k, v = (qkv[:, i * D_MIX_HALF:(i + 1) * D_MIX_HALF] for i in range(3))
        q = q * lax.rsqrt(_head_sums(q * q) + EPS) * (HEAD_DIM ** -0.5)
        k = k * lax.rsqrt(_head_sums(k * k) + EPS)
        beta = _sigmoid(_pad_rows(b_ref[bi], chunk))
        g = -jnp.exp(alog_ref[...]) * _softplus(_pad_rows(a_ref[bi], chunk) + dtb_ref[...])
        if t_valid is not None:
            valid = (c * cd + _iota2((chunk, D_MIX_HALF), 0)) < t_valid
            q, k, v = (jnp.where(valid, x, 0.0) for x in (q, k, v))
            beta = jnp.where(valid, beta, 0.0)
            g = jnp.where(valid, g, 0.0)
        gcum = _cumsum_rows(g)
        eg = jnp.exp(gcum)
        g_last = gcum[chunk - 1:chunk, :]
        return dict(q=q, k=k, beta=beta, gcum=gcum, rhs_v=beta * v, rhs_k=beta * eg * k, q_dec=eg * q,
                    k_dec=jnp.exp(g_last - gcum) * k, eg_last=jnp.exp(g_last))

    def group(bis):
        pre = [prep(bi) for bi in bis]
        chains = [(r, h) for r in range(len(bis)) for h in range(N_HEADS)]
        n = range(len(chains))

        def head(name, i, width=HEAD_DIM):
            r, h = chains[i]
            return pre[r][name][:, h * HEAD_DIM:h * HEAD_DIM + width]

        g_is = [head('gcum', i, chunk) for i in n]
        decays = [jnp.where(causal, jnp.exp(jnp.minimum(g_i - g_i.T, 0.0)), 0.0) for g_i in g_is]
        qk_kks = [_mm_nt(jnp.concatenate([head('k', i), head('q', i)], axis=0), head('k', i)) for i in n]
        lows = [jnp.where(strict, head('beta', i, chunk) * decays[i] * qk_kks[i][:chunk], 0.0) for i in n]
        t_invs = _unit_lower_inverses(lows, chunk, _mm)
        sols = [_mm(t_invs[i], jnp.concatenate([head('rhs_v', i), head('rhs_k', i)], axis=1)) for i in n]
        states = [s_ref[bis[r], h] for r, h in chains]
        wss = [_mm(jnp.concatenate([sols[i][:, HEAD_DIM:], head('q_dec', i)], axis=0), states[i])
               for i in n]
        u_news = [sols[i][:, :HEAD_DIM] - wss[i][:chunk] for i in n]
        outs = [wss[i][chunk:] + _mm(qk_kks[i][chunk:] * decays[i], u_news[i]) for i in n]
        s_news = [head('eg_last', i) * states[i] + _mm_tn(head('k_dec', i), u_news[i]) for i in n]
        for i, (r, h) in enumerate(chains):
            s_ref[bis[r], h] = s_news[i]
        for r, bi in enumerate(bis):
            o = jnp.concatenate(outs[r * N_HEADS:(r + 1) * N_HEADS], axis=1)
            o = o * lax.rsqrt(_head_sums(o * o) * (1.0 / HEAD_DIM) + EPS) * nw_ref[...]
            o = o * _silu(_pad_rows(z_ref[bi], chunk))
            o_ref[bi] = o[:cd]

    _for_row_groups(bsz, rows_per_group, group)


def _gdn(cols, conv0, s0, conv_w, alog_rep, dtb_rep, nw_rep, bsz, rows_per_group, cd, chunk, t_valid):
    b, t, _ = cols.shape

    def col_spec(width, start):
        return pl.BlockSpec((bsz, cd, width), lambda i, c: (i, c, start // width))

    vec_spec = pl.BlockSpec((1, D_MIX_HALF), lambda i, c: (0, 0))
    state_spec = pl.BlockSpec((bsz, N_HEADS, HEAD_DIM, HEAD_DIM), lambda i, c: (i, 0, 0, 0))
    kern = functools.partial(_gdn_kernel, chunk=chunk, t_valid=t_valid, rows_per_group=rows_per_group)
    return pl.pallas_call(
        kern,
        grid=(b // bsz, t // cd),
        in_specs=[col_spec(GDN_QKV, 0), col_spec(D_MIX_HALF, COL_Z), col_spec(D_MIX_HALF, COL_B),
                  col_spec(D_MIX_HALF, COL_A),
                  pl.BlockSpec((bsz, CONV_W - 1, GDN_QKV), lambda i, c: (i, 0, 0)),
                  pl.BlockSpec((CONV_W, GDN_QKV), lambda i, c: (0, 0)),
                  vec_spec, vec_spec, vec_spec, state_spec],
        out_specs=[pl.BlockSpec((bsz, cd, D_MIX_HALF), lambda i, c: (i, c, 0)), state_spec],
        out_shape=[jax.ShapeDtypeStruct((b, t, D_MIX_HALF), F32),
                   jax.ShapeDtypeStruct((b, N_HEADS, HEAD_DIM, HEAD_DIM), F32)],
        scratch_shapes=[pltpu.VMEM((bsz, SUBLANES + cd, GDN_QKV), F32)],
        compiler_params=pltpu.CompilerParams(dimension_semantics=("arbitrary",) * 2, vmem_limit_bytes=VMEM_LIMIT),
        name="gdn",
    )(cols, cols, cols, cols, conv0, conv_w, alog_rep, dtb_rep, nw_rep, s0)


def _rwkv_kernel(rkv_ref, l_ref, p_ref, pl_ref, mu_ref, mul_ref, w0_ref, a0_ref, kk_ref, ka_ref, rk_ref,
                 lnw_ref, lnb_ref, wup_ref, aup_ref, gup_ref, s0_ref, o_ref, s_ref, hist_scr, histl_scr,
                 *, chunk, t_valid, rows_per_group):
    bsz, cd, _ = rkv_ref.shape
    c = pl.program_id(1)
    prev_row = SUBLANES - 1

    @pl.when(c == 0)
    def _():
        s_ref[...] = s0_ref[...]
        hist_scr[:, prev_row:SUBLANES, :] = p_ref[...]
        histl_scr[:, prev_row:SUBLANES, :] = pl_ref[...]

    row = _iota2((chunk, chunk), 0)
    col = _iota2((chunk, chunk), 1)
    causal = row >= col
    strict = row > col

    def shifted(bi, u_ref, scr, m_ref):
        u = u_ref[bi]
        scr[bi, SUBLANES:SUBLANES + cd, :] = u
        prev = scr[bi, prev_row:prev_row + cd, :]
        scr[bi, prev_row:SUBLANES, :] = scr[bi, prev_row + cd:SUBLANES + cd, :]
        return _pad_rows(u + (prev - u) * m_ref[...], chunk)

    def prep(bi):
        rkv = shifted(bi, rkv_ref, hist_scr, mu_ref)
        r, k, v = (rkv[:, i * D_MIX_HALF:(i + 1) * D_MIX_HALF] for i in range(3))
        xl = shifted(bi, l_ref, histl_scr, mul_ref)
        dw = xl[:, :LORA_W]
        da = xl[:, LORA_W:LORA_W + LORA_A]
        dg = xl[:, LORA_W + LORA_A:]
        w_log = -_softplus(-(w0_ref[...] + _mm(jnp.tanh(dw), wup_ref[...]))) - 0.5
        lw = -jnp.exp(w_log)
        a = _sigmoid(a0_ref[...] + _mm(da, aup_ref[...]))
        gate = _mm(_sigmoid(dg), gup_ref[...])
        kk = k * kk_ref[...]
        kk = kk * lax.rsqrt(_head_sums(kk * kk) + EPS)
        k = k * (1.0 + (a - 1.0) * ka_ref[...])
        if t_valid is not None:
            valid = (c * cd + _iota2((chunk, D_MIX_HALF), 0)) < t_valid
            r, k, v, kk, lw = (jnp.where(valid, x, 0.0) for x in (r, k, v, kk, lw))
        cum = _cumsum_rows(lw)
        e_out = jnp.exp(-cum)
        cum_last = cum[chunk - 1:chunk, :]
        e_rest = jnp.exp(cum_last - cum)
        return dict(r=r, k=k, v=v, gate=gate, a_t=-kk * jnp.exp(cum - lw), b_t=kk * a * e_out, k_t=k * e_out,
                    r_t=r * jnp.exp(cum), b_c=kk * a * e_rest, k_c=k * e_rest, e_last=jnp.exp(cum_last))

    def group(bis):
        pre = [prep(bi) for bi in bis]
        chains = [(r, h) for r in range(len(bis)) for h in range(N_HEADS)]
        n = range(len(chains))

        def head(name, i):
            r, h = chains[i]
            return pre[r][name][:, h * HEAD_DIM:(h + 1) * HEAD_DIM]

        ars = [jnp.concatenate([head('a_t', i), head('r_t', i)], axis=0) for i in n]
        bks = [jnp.concatenate([head('b_t', i), head('k_t', i)], axis=0) for i in n]
        ms = [_mm_nt(ars[i], bks[i]) for i in n]
        t_invs = _unit_lower_inverses([jnp.where(strict, -m[:chunk, :chunk], 0.0) for m in ms], chunk, _mm)
        states = [s_ref[bis[r], h] for r, h in chains]
        arss = [_mm_nt(ars[i], states[i]) for i in n]
        vhs = [head('v', i) for i in n]
        akvs = [_mm(jnp.where(strict, ms[i][:chunk, chunk:], 0.0), vhs[i]) for i in n]
        us = [_mm(t_invs[i], arss[i][:chunk] + akvs[i]) for i in n]
        uvs = [jnp.concatenate([us[i], vhs[i]], axis=0) for i in n]
        causal2 = _iota2((chunk, 2 * chunk), 0) >= (_iota2((chunk, 2 * chunk), 1) & (chunk - 1))
        ys = [arss[i][chunk:] + _mm(jnp.where(causal2, ms[i][chunk:], 0.0), uvs[i]) for i in n]
        s_news = [states[i] * head('e_last', i)
                  + _mm_tn(uvs[i], jnp.concatenate([head('b_c', i), head('k_c', i)], axis=0)) for i in n]
        for i, (r, h) in enumerate(chains):
            s_ref[bis[r], h] = s_news[i]
        for r, bi in enumerate(bis):
            p = pre[r]
            y = jnp.concatenate(ys[r * N_HEADS:(r + 1) * N_HEADS], axis=1)
            mean = _head_sums(y) * (1.0 / HEAD_DIM)
            dy = y - mean
            var = _head_sums(dy * dy) * (1.0 / HEAD_DIM)
            y = dy * lax.rsqrt(var + GN_EPS) * lnw_ref[...] + lnb_ref[...]
            bonus = _head_sums(p['r'] * p['k'] * rk_ref[...]) * p['v']
            o_ref[bi] = ((y + bonus) * p['gate'])[:cd]

    _for_row_groups(bsz, rows_per_group, group)


def _rwkv(cols, shift0, s0, mu, vecs, w_up, a_up, g_up, bsz, rows_per_group, cd, chunk, t_valid):
    b, t, _ = cols.shape

    def full2(shape):
        return pl.BlockSpec(shape, lambda i, c: (0, 0))

    state_spec = pl.BlockSpec((bsz, N_HEADS, HEAD_DIM, HEAD_DIM), lambda i, c: (i, 0, 0, 0))
    kern = functools.partial(_rwkv_kernel, chunk=chunk, t_valid=t_valid, rows_per_group=rows_per_group)
    return pl.pallas_call(
        kern,
        grid=(b // bsz, t // cd),
        in_specs=[pl.BlockSpec((bsz, cd, RWKV_RKV), lambda i, c: (i, c, COL_RW // RWKV_RKV)),
                  pl.BlockSpec((bsz, cd, LORA_COLS), lambda i, c: (i, c, COL_LORA // LORA_COLS)),
                  pl.BlockSpec((bsz, 1, RWKV_RKV), lambda i, c: (i, 0, 0)),
                  pl.BlockSpec((bsz, 1, LORA_COLS), lambda i, c: (i, 0, RWKV_RKV // LORA_COLS)),
                  pl.BlockSpec((1, RWKV_RKV), lambda i, c: (0, 0)),
                  pl.BlockSpec((1, LORA_COLS), lambda i, c: (0, RWKV_RKV // LORA_COLS))]
                 + [full2((1, D_MIX_HALF))] * 7
                 + [full2((LORA_W, D_MIX_HALF)), full2((LORA_A, D_MIX_HALF)), full2((LORA_G, D_MIX_HALF)),
                    state_spec],
        out_specs=[pl.BlockSpec((bsz, cd, D_MIX_HALF), lambda i, c: (i, c, 0)), state_spec],
        out_shape=[jax.ShapeDtypeStruct((b, t, D_MIX_HALF), F32),
                   jax.ShapeDtypeStruct((b, N_HEADS, HEAD_DIM, HEAD_DIM), F32)],
        scratch_shapes=[pltpu.VMEM((bsz, SUBLANES + cd, RWKV_RKV), F32),
                        pltpu.VMEM((bsz, SUBLANES + cd, LORA_COLS), F32)],
        compiler_params=pltpu.CompilerParams(dimension_semantics=("arbitrary",) * 2, vmem_limit_bytes=VMEM_LIMIT),
        name="rwkv",
    )(cols, cols, shift0, shift0, mu, mu, *vecs, w_up, a_up, g_up, s0)


def _outproj_kernel(oa_ref, ob_ref, x_ref, gt_ref, sc_ref, sh_ref, nw_ref, wa_ref, wb_ref, wr_ref, br_ref,
                    x1_ref, h_ref, wt_ref):
    bb, tt, d = x_ref.shape
    n = bb * tt
    oa = oa_ref[...].reshape(n, D_MIX_HALF).astype(BF16)
    ob = ob_ref[...].reshape(n, D_MIX_HALF).astype(BF16)
    mixed = (jnp.dot(oa, wa_ref[...], preferred_element_type=F32)
             + jnp.dot(ob, wb_ref[...], preferred_element_type=F32))
    x1 = x_ref[...] + gt_ref[...] * mixed.reshape(bb, tt, d)
    x1_ref[...] = x1
    h = _modulated_norm(x1, nw_ref[...], sc_ref[...], sh_ref[...]).reshape(n, d)
    h_ref[...] = h.reshape(bb, tt, d).astype(BF16)

    h_hi, h_lo = _split2(h)
    hw = jnp.dot(h_hi, wr_ref[...], preferred_element_type=F32)
    logits = (hw[:, :LANES] + hw[:, LANES:] + jnp.dot(h_lo, wr_ref[:, :LANES], preferred_element_type=F32)
              + br_ref[...])
    lane_i = _iota2((n, LANES), 1)
    lane = lane_i.astype(F32)
    lane_grp = (lane_i >> (EXP_PER_GROUP.bit_length() - 1)).astype(F32)
    neg = jnp.float32(-jnp.inf)
    big = jnp.float32(LANES)
    is_grp = (lane_i >= N_EXPERTS) & (lane_i < N_EXPERTS + N_GROUPS)
    gl = jnp.where(is_grp, logits, neg)
    gmax = jnp.max(gl, axis=-1, keepdims=True)
    grp = jnp.min(jnp.where(gl == gmax, lane, big), axis=-1, keepdims=True) - N_EXPERTS
    g_prob = 1.0 / jnp.sum(jnp.where(is_grp, jnp.exp(gl - gmax), 0.0), axis=-1, keepdims=True)
    in_grp = (lane_i < N_EXPERTS) & (lane_grp == grp)
    el = jnp.where(in_grp, logits, neg)
    m1 = jnp.max(el, axis=-1, keepdims=True)
    i1 = jnp.min(jnp.where(el == m1, lane, big), axis=-1, keepdims=True)
    el2 = jnp.where(lane == i1, neg, el)
    m2 = jnp.max(el2, axis=-1, keepdims=True)
    i2 = jnp.min(jnp.where(el2 == m2, lane, big), axis=-1, keepdims=True)
    e2 = jnp.exp(m2 - m1)
    p1 = g_prob / (1.0 + e2)
    p2 = g_prob * e2 / (1.0 + e2)
    wt = jnp.where(lane == i1, p1, 0.0) + jnp.where(lane == i2, p2, 0.0)
    wt_ref[...] = wt.reshape(bb, tt, LANES)


def _outproj(oa, ob, x, gt, sc, sh, nw, wa, wb, wr, br, bb, tt):
    b, t, d = x.shape
    grid = (b // bb, t // tt)

    def tok_spec(w):
        return pl.BlockSpec((bb, tt, w), lambda i, j: (i, j, 0))

    mod_spec = pl.BlockSpec((bb, 1, d), lambda i, j: (i, 0, 0))

    def full2(shape):
        return pl.BlockSpec(shape, lambda i, j: (0, 0))

    return pl.pallas_call(
        _outproj_kernel,
        grid=grid,
        in_specs=[tok_spec(D_MIX_HALF), tok_spec(D_MIX_HALF), tok_spec(d), mod_spec, mod_spec, mod_spec,
                  pl.BlockSpec((1, 1, d), lambda i, j: (0, 0, 0)),
                  full2((D_MIX_HALF, d)), full2((D_MIX_HALF, d)), full2((d, 2 * LANES)), full2((1, LANES))],
        out_specs=[tok_spec(d), tok_spec(d), tok_spec(LANES)],
        out_shape=[jax.ShapeDtypeStruct((b, t, d), F32), jax.ShapeDtypeStruct((b, t, d), BF16),
                   jax.ShapeDtypeStruct((b, t, LANES), F32)],
        compiler_params=pltpu.CompilerParams(dimension_semantics=("arbitrary",) * 2, vmem_limit_bytes=VMEM_LIMIT),
        name="outproj",
    )(oa, ob, x, gt, sc, sh, nw, wa, wb, wr, br)


MOE_ROWS = 128
MOE_EXPERTS_PER_STEP = 4


def _moe_kernel(h_ref, wt_ref, x1_ref, gt_ref, fw_ref, wgu_ref, wd_ref, y_ref,
                xg_scr, gtw_scr, yw_scr, cnt_scr, off_scr):
    bb, tt, d = x1_ref.shape
    n = bb * tt
    npos = 2 * n
    e = pl.program_id(2)

    @pl.when(e == 0)
    def _():
        wt_t = wt_ref[...].reshape(n, LANES).T
        member = wt_t > 0.0
        ones = jnp.where(member, 1.0, 0.0)
        upper = jnp.where(_iota2((n, n), 0) < _iota2((n, n), 1), 1.0, 0.0).astype(BF16)
        rank = jnp.dot(_bf(ones), upper, preferred_element_type=F32)
        cnt = jnp.broadcast_to(jnp.sum(ones, axis=1, keepdims=True), (LANES, LANES))
        lower = jnp.where(_iota2((LANES, LANES), 0) > _iota2((LANES, LANES), 1), 1.0, 0.0)
        off = _dot(lower, cnt)
        cnt_scr[...] = cnt
        off_scr[...] = off
        pos = off[:, :1] + rank
        pos1 = jnp.min(jnp.where(member, pos, jnp.float32(4 * n)), axis=0, keepdims=True)
        pos2 = jnp.max(jnp.where(member, pos, -1.0), axis=0, keepdims=True)
        w1 = jnp.sum(jnp.where(member & (pos == pos1), wt_t, 0.0), axis=0, keepdims=True)
        w2 = jnp.sum(jnp.where(member & (pos == pos2), wt_t, 0.0), axis=0, keepdims=True)
        w2 = jnp.where(pos2 != pos1, w2, 0.0)
        h = h_ref[...].reshape(n, d)

        def onehot_block(b, carry):
            p = (b * MOE_ROWS + _iota2((MOE_ROWS, n), 0)).astype(F32)
            rows = pl.ds(pl.multiple_of(b * MOE_ROWS, MOE_ROWS), MOE_ROWS)
            yw_scr[rows, :n] = jnp.where((pos1 == p) | (pos2 == p), 1.0, 0.0).astype(BF16)
            return carry

        lax.fori_loop(0, npos // MOE_ROWS, onehot_block, 0)
        xg_scr[:npos, :] = jnp.dot(yw_scr[:npos, :n], h, preferred_element_type=F32).astype(BF16)
        z_t = jnp.concatenate([pos1, pos2, w1, w2, jnp.zeros((LANES - 4, n), F32)], axis=0).T
        p1c, p2c, w1c, w2c = (z_t[:, i:i + 1] for i in range(4))
        for b in range(npos // LANES):
            p = (b * LANES + _iota2((n, LANES), 1)).astype(F32)
            gtw_scr[:, b * LANES:(b + 1) * LANES] = (jnp.where(p1c == p, w1c, 0.0)
                                                     + jnp.where(p2c == p, w2c, 0.0)).astype(BF16)
        xg_scr[npos:, :] = jnp.zeros((MOE_ROWS, d), BF16)
        yw_scr[...] = jnp.zeros_like(yw_scr)

    es = range(wgu_ref.shape[0])
    offs = [off_scr[pl.ds(e * len(es) + j, 1), 0:1] for j in es]
    firsts = [o.astype(jnp.int32)[0, 0] for o in offs]
    lasts = [(offs[j] + cnt_scr[pl.ds(e * len(es) + j, 1), 0:1]).astype(jnp.int32)[0, 0] for j in es]
    starts = [(f // BF16_ROWS) * BF16_ROWS for f in firsts]

    def window_rows(r0):
        return pl.ds(pl.multiple_of(r0, BF16_ROWS), MOE_ROWS)

    def store_rows(j, r0, y):
        p = r0 + _iota2((MOE_ROWS, 1), 0)
        rows = window_rows(r0)
        yw_scr[rows, :] = jnp.where((p >= firsts[j]) & (p < lasts[j]), y, yw_scr[rows, :])

    gus = [jnp.dot(xg_scr[window_rows(starts[j]), :], wgu_ref[j], preferred_element_type=F32) for j in es]
    acts = [(_silu(gu[:, :D_EXPERT]) * gu[:, D_EXPERT:]).astype(BF16) for gu in gus]
    ys = [jnp.dot(acts[j], wd_ref[j], preferred_element_type=F32).astype(BF16) for j in es]
    for j in es:
        store_rows(j, starts[j], ys[j])

    for j in es:
        def more_windows(w, carry, j=j):
            r0 = starts[j] + w * MOE_ROWS
            gu = jnp.dot(xg_scr[window_rows(r0), :], wgu_ref[j], preferred_element_type=F32)
            act = (_silu(gu[:, :D_EXPERT]) * gu[:, D_EXPERT:]).astype(BF16)
            store_rows(j, r0, jnp.dot(act, wd_ref[j], preferred_element_type=F32).astype(BF16))
            return carry

        lax.fori_loop(1, (lasts[j] - starts[j] + MOE_ROWS - 1) // MOE_ROWS, more_windows, 0)

    @pl.when(e == pl.num_programs(2) - 1)
    def _():
        moe = jnp.dot(gtw_scr[...], yw_scr[:npos, :], preferred_element_type=F32)
        x2 = x1_ref[...] + gt_ref[...] * moe.reshape(bb, tt, d)
        ms = jnp.mean(x2 * x2, axis=-1, keepdims=True)
        y_ref[...] = x2 * lax.rsqrt(ms + EPS) * fw_ref[...]


def _moe(h, wt, x1, gt, fw, w_gu, w_down, bb, tt):
    b, t, d = x1.shape
    es = MOE_EXPERTS_PER_STEP
    grid = (b // bb, t // tt, N_EXPERTS // es)
    npos = 2 * bb * tt

    def tok_spec(w):
        return pl.BlockSpec((bb, tt, w), lambda i, j, e: (i, j, 0))

    return pl.pallas_call(
        _moe_kernel,
        grid=grid,
        in_specs=[tok_spec(d), tok_spec(LANES), tok_spec(d),
                  pl.BlockSpec((bb, 1, d), lambda i, j, e: (i, 0, 0)),
                  pl.BlockSpec((1, 1, d), lambda i, j, e: (0, 0, 0)),
                  pl.BlockSpec((es, d, 2 * D_EXPERT), lambda i, j, e: (e, 0, 0)),
                  pl.BlockSpec((es, D_EXPERT, d), lambda i, j, e: (e, 0, 0))],
        out_specs=tok_spec(d),
        out_shape=jax.ShapeDtypeStruct((b, t, d), F32),
        scratch_shapes=[pltpu.VMEM((npos + MOE_ROWS, d), BF16), pltpu.VMEM((bb * tt, npos), BF16),
                        pltpu.VMEM((npos + MOE_ROWS, d), BF16), pltpu.VMEM((LANES, LANES), F32),
                        pltpu.VMEM((LANES, LANES), F32)],
        compiler_params=pltpu.CompilerParams(dimension_semantics=("arbitrary",) * 3,
                                             vmem_limit_bytes=MOE_VMEM_LIMIT),
        name="moe",
    )(h, wt, x1, gt, fw, w_gu, w_down)


def _layer(x, mod, conv0, s_gdn0, shift0, s_rwkv0, p, *, t_valid, bb, tt, moe_tile, bsz, rows_per_group, cd,
           chunk):
    sh_m, sc_m, gt_m, sh_f, sc_f, gt_f = mod
    cols = _inproj(x, sc_m, sh_m, p['norm_mix_w'], p['w_cols'], bb, tt)
    o_a, s_gdn = _gdn(cols, conv0, s_gdn0, p['gdn_conv_w'], p['alog_rep'], p['dtb_rep'], p['gdn_nw_rep'],
                      bsz, rows_per_group, cd, chunk, t_valid)
    o_b, s_rwkv = _rwkv(cols, shift0, s_rwkv0, p['rwkv_mu'], p['rwkv_vecs'], p['rwkv_w_up'], p['rwkv_a_up'],
                        p['rwkv_g_up'], bsz, rows_per_group, cd, chunk, t_valid)
    x1, h2, wt = _outproj(o_a, o_b, x, gt_m, sc_f, sh_f, p['norm_ffn_w'], p['w_out_a'], p['w_out_b'],
                          p['w_router'], p['b_router'], bb, tt)
    y = _moe(h2, wt, x1, gt_f, p['final_norm_w'], p['w_gu'], p['w_down'], *moe_tile)
    t_last = x.shape[1] if t_valid is None else t_valid
    conv_new = cols[:, t_last - (CONV_W - 1):t_last, :GDN_QKV]
    shift_new = cols[:, t_last - 1, COL_RW:]
    return y, conv_new, s_gdn, shift_new, s_rwkv


def kernel(x_prompt, x_sample, state_gdn_conv, state_gdn, state_rwkv_shift, state_rwkv, c_prompt, c_sample, ada_w, ada_b, norm_mix_w, w_in, gdn_conv_w, gdn_a_log, gdn_dt_bias, gdn_norm_w, rwkv_mu, rwkv_w0, rwkv_w_up, rwkv_a0, rwkv_a_up, rwkv_g_up, rwkv_k_k, rwkv_k_a, rwkv_r_k, rwkv_ln_w, rwkv_ln_b, w_out, norm_ffn_w, router_group_w, router_group_b, router_expert_w, router_expert_b, expert_w_gate_up, expert_w_down, final_norm_w):
    depth = ada_w.shape[0]
    assert depth == 1
    l = 0
    b_p, t_p, d = x_prompt.shape
    b_s, t_s, _ = x_sample.shape
    gdn_cols = GDN_QKV + D_MIX_HALF + 2 * N_HEADS

    w = w_in[l]
    beta0 = GDN_QKV + D_MIX_HALF
    w_bf = w.astype(BF16)
    w_cols = jnp.concatenate([w_bf[:, :beta0], jnp.repeat(w_bf[:, beta0:gdn_cols], HEAD_DIM, axis=1),
                              w_bf[:, gdn_cols:]], axis=1)
    row = lambda a: a.reshape(1, -1)
    n_route = N_EXPERTS + N_GROUPS
    w_router = jnp.concatenate([router_expert_w[l], router_group_w[l], jnp.zeros((d, LANES - n_route), F32)], axis=1)
    w_router = jnp.concatenate(_split2(w_router), axis=1)
    b_router = jnp.concatenate([router_expert_b[l], router_group_b[l], jnp.zeros((LANES - n_route,), F32)])[None]
    p = {
        'norm_mix_w': norm_mix_w[l].reshape(1, 1, d), 'w_cols': w_cols,
        'gdn_conv_w': gdn_conv_w[l], 'alog_rep': row(jnp.repeat(gdn_a_log[l], HEAD_DIM)),
        'dtb_rep': row(jnp.repeat(gdn_dt_bias[l], HEAD_DIM)),
        'gdn_nw_rep': row(jnp.tile(gdn_norm_w[l], N_HEADS)),
        'rwkv_mu': row(rwkv_mu[l]),
        'rwkv_vecs': tuple(row(a) for a in (rwkv_w0[l], rwkv_a0[l], rwkv_k_k[l], rwkv_k_a[l], rwkv_r_k[l],
                                            rwkv_ln_w[l], rwkv_ln_b[l])),
        'rwkv_w_up': rwkv_w_up[l], 'rwkv_a_up': rwkv_a_up[l], 'rwkv_g_up': rwkv_g_up[l],
        'w_out_a': w_out[l][:D_MIX_HALF].astype(BF16), 'w_out_b': w_out[l][D_MIX_HALF:].astype(BF16),
        'norm_ffn_w': norm_ffn_w[l].reshape(1, 1, d), 'w_router': w_router, 'b_router': b_router,
        'w_gu': expert_w_gate_up[l].astype(BF16), 'w_down': expert_w_down[l].astype(BF16),
        'final_norm_w': final_norm_w.reshape(1, 1, d),
    }

    mod = _ada(jnp.concatenate([c_prompt, c_sample], axis=0), ada_w[l], ada_b[l])
    mod_p = tuple(m.reshape(b_p, 1, d) for m in jnp.split(mod[:b_p], 6, axis=-1))
    mod_s = tuple(m.reshape(b_s, 1, d) for m in jnp.split(mod[b_p:], 6, axis=-1))

    zc = jnp.zeros((b_p, CONV_W - 1, GDN_QKV), F32)
    zs = jnp.zeros((b_p, N_HEADS, HEAD_DIM, HEAD_DIM), F32)
    zsh = jnp.zeros((b_p, 1, RWKV_COLS), F32)
    y_p, conv_p, sg_p, shift_p, sr_p = _layer(x_prompt, mod_p, zc, zs, zsh, zs, p, t_valid=None,
                                              bb=1, tt=512, moe_tile=(1, 1024), bsz=4, rows_per_group=4,
                                              cd=CHUNK, chunk=CHUNK)

    t_pad = SUBLANES
    x_s = jnp.pad(x_sample, ((0, 0), (0, t_pad - t_s), (0, 0)))
    y_s, conv_s, sg_s, shift_s, sr_s = _layer(x_s, mod_s, state_gdn_conv[l], state_gdn[l],
                                              state_rwkv_shift[l].reshape(b_s, 1, RWKV_COLS), state_rwkv[l], p,
                                              t_valid=t_s, bb=64, tt=t_pad, moe_tile=(b_s, t_pad), bsz=8,
                                              rows_per_group=8, cd=t_pad,
                                              chunk=t_pad)
    y_s = y_s[:, :t_s]
    return (y_p, y_s, conv_p[None], sg_p[None], shift_p[None], sr_p[None],
            conv_s[None], sg_s[None], shift_s[None], sr_s[None])
```

```python
import functools

import jax
import jax.numpy as jnp
from jax import lax
from jax.experimental import pallas as pl
from jax.experimental.pallas import tpu as pltpu

F32 = jnp.float32
BF16 = jnp.bfloat16
HI = lax.Precision.HIGHEST

D_MODEL = 1024
HEAD_DIM = 64
N_HEADS = 8
D_MIX_HALF = N_HEADS * HEAD_DIM
CONV_W = 4
LORA_W = 64
LORA_A = 64
LORA_G = 128
N_GROUPS = 4
EXP_PER_GROUP = 8
N_EXPERTS = N_GROUPS * EXP_PER_GROUP
D_EXPERT = 256
EPS = 1e-6
GN_EPS = HEAD_DIM * 1e-5

LANES = 128
SUBLANES = 8
BF16_ROWS = 16
GDN_QKV = 3 * D_MIX_HALF
RWKV_RKV = 3 * D_MIX_HALF
RWKV_COLS = RWKV_RKV + LORA_W + LORA_A + LORA_G
LORA_COLS = LORA_W + LORA_A + LORA_G

COL_Z = GDN_QKV
COL_B = COL_Z + D_MIX_HALF
COL_A = COL_B + D_MIX_HALF
COL_RW = COL_A + D_MIX_HALF
COL_LORA = COL_RW + RWKV_RKV
N_COLS = COL_LORA + LORA_COLS
CHUNK = 64

VMEM_LIMIT = 48 * 1024 * 1024
MOE_VMEM_LIMIT = 58 * 1024 * 1024


def _dot(a, b, prec=HI):
    return jnp.dot(a, b, preferred_element_type=F32, precision=prec)


def _bf(a):
    return a.astype(BF16)


def _mm(a, b):
    return jnp.dot(_bf(a), _bf(b), preferred_element_type=F32)


def _mm_nt(a, b):
    return lax.dot_general(_bf(a), _bf(b), (((1,), (1,)), ((), ())), preferred_element_type=F32)


def _mm_tn(a, b):
    return lax.dot_general(_bf(a), _bf(b), (((0,), (0,)), ((), ())), preferred_element_type=F32)


def _split2(a):
    hi = a.astype(BF16)
    return hi, (a - hi.astype(F32)).astype(BF16)


def _sigmoid(x):
    return 1.0 / (1.0 + jnp.exp(-x))


def _silu(x):
    return x * _sigmoid(x)


def _softplus(x):
    return jnp.maximum(x, 0.0) + jnp.log1p(jnp.exp(-jnp.abs(x)))


def _iota2(shape, dim):
    return lax.broadcasted_iota(jnp.int32, shape, dim)


def _head_block_ones():
    r = _iota2((LANES, LANES), 0)
    c = _iota2((LANES, LANES), 1)
    sh = HEAD_DIM.bit_length() - 1
    return jnp.where((r >> sh) == (c >> sh), 1.0, 0.0).astype(F32)


def _head_sums(x):
    ones = _bf(_head_block_ones())
    hi, lo = _split2(x)
    parts = []
    for p in range(x.shape[1] // LANES):
        s = slice(p * LANES, (p + 1) * LANES)
        parts.append(jnp.dot(hi[:, s], ones, preferred_element_type=F32)
                     + jnp.dot(lo[:, s], ones, preferred_element_type=F32))
    return jnp.concatenate(parts, axis=1)


def _cumsum_rows(x):
    n = x.shape[0]
    r = _iota2((n, n), 0)
    c = _iota2((n, n), 1)
    tri = jnp.where(r >= c, 1.0, 0.0).astype(BF16)
    x1 = x.astype(BF16)
    rem = x - x1.astype(F32)
    x2 = rem.astype(BF16)
    x3 = (rem - x2.astype(F32)).astype(BF16)
    return (jnp.dot(tri, x1, preferred_element_type=F32) + jnp.dot(tri, x2, preferred_element_type=F32)
            + jnp.dot(tri, x3, preferred_element_type=F32))


def _unit_lower_inverses(lows, n, mm):
    r = _iota2((n, n), 0)
    c = _iota2((n, n), 1)
    eye = jnp.where(r == c, 1.0, 0.0)
    pair = (r >> 1) == (c >> 1)
    invs = [eye - jnp.where(pair, low, 0.0) for low in lows]
    s = 2
    while s < n:
        sh = s.bit_length()
        sel = ((r >> sh) == (c >> sh)) & ((r & (2 * s - 1)) >= s) & ((c & (2 * s - 1)) < s)
        prods = [mm(jnp.where(sel, low, 0.0), inv) for low, inv in zip(lows, invs)]
        invs = [inv - mm(inv, prod) for inv, prod in zip(invs, prods)]
        s *= 2
    return invs


def _pad_rows(a, n):
    if a.shape[0] == n:
        return a
    return jnp.concatenate([a, jnp.zeros((n - a.shape[0], a.shape[1]), a.dtype)], axis=0)


def _for_row_groups(n, group, body):
    if n == group:
        body(list(range(n)))
    else:
        def step(i, carry):
            body([i * group + j for j in range(group)])
            return carry
        lax.fori_loop(0, n // group, step, 0)


def _ada_kernel(c_ref, w_ref, b_ref, o_ref):
    o_ref[...] = _dot(_silu(c_ref[...]), w_ref[...]) + b_ref[...]


def _ada(c_all, ada_w, ada_b):
    n, d = c_all.shape
    nout = ada_w.shape[1]
    tn = 1536
    return pl.pallas_call(
        _ada_kernel,
        grid=(nout // tn,),
        in_specs=[pl.BlockSpec((n, d), lambda j: (0, 0)),
                  pl.BlockSpec((d, tn), lambda j: (0, j)),
                  pl.BlockSpec((1, tn), lambda j: (0, j))],
        out_specs=pl.BlockSpec((n, tn), lambda j: (0, j)),
        out_shape=jax.ShapeDtypeStruct((n, nout), F32),
        compiler_params=pltpu.CompilerParams(dimension_semantics=("arbitrary",), vmem_limit_bytes=VMEM_LIMIT),
        name="ada",
    )(c_all, ada_w, ada_b.reshape(1, nout))


def _modulated_norm(x, nw, sc, sh):
    ms = jnp.mean(x * x, axis=-1, keepdims=True)
    return (x * lax.rsqrt(ms + EPS) * nw) * (1.0 + sc) + sh


def _inproj_kernel(x_ref, sc_ref, sh_ref, nw_ref, w_ref, o_ref):
    bb, tt, d = x_ref.shape
    h = _modulated_norm(x_ref[...], nw_ref[...], sc_ref[...], sh_ref[...])
    h = h.reshape(bb * tt, d).astype(BF16)
    o = jnp.dot(h, w_ref[...], preferred_element_type=F32)
    o_ref[...] = o.reshape(bb, tt, o.shape[-1])


def _inproj(x, sc, sh, nw, w_cols, bb, tt):
    b, t, d = x.shape
    tn = N_COLS // 2
    grid = (2, b // bb, t // tt)
    return pl.pallas_call(
        _inproj_kernel,
        grid=grid,
        in_specs=[pl.BlockSpec((bb, tt, d), lambda n, i, j: (i, j, 0)),
                  pl.BlockSpec((bb, 1, d), lambda n, i, j: (i, 0, 0)),
                  pl.BlockSpec((bb, 1, d), lambda n, i, j: (i, 0, 0)),
                  pl.BlockSpec((1, 1, d), lambda n, i, j: (0, 0, 0)),
                  pl.BlockSpec((d, tn), lambda n, i, j: (0, n))],
        out_specs=pl.BlockSpec((bb, tt, tn), lambda n, i, j: (i, j, n)),
        out_shape=jax.ShapeDtypeStruct((b, t, N_COLS), F32),
        compiler_params=pltpu.CompilerParams(dimension_semantics=("arbitrary",) * 3, vmem_limit_bytes=VMEM_LIMIT),
        name="inproj",
    )(x, sc, sh, nw, w_cols)


def _gdn_kernel(qkv_ref, z_ref, b_ref, a_ref, c0_ref, cw_ref, alog_ref, dtb_ref, nw_ref, s0_ref,
                o_ref, s_ref, hist_scr, *, chunk, t_valid, rows_per_group):
    bsz, cd, _ = qkv_ref.shape
    c = pl.program_id(1)
    hist_lo = SUBLANES - (CONV_W - 1)

    @pl.when(c == 0)
    def _():
        s_ref[...] = s0_ref[...]
        hist_scr[:, hist_lo:SUBLANES, :] = c0_ref[...]

    row = _iota2((chunk, chunk), 0)
    col = _iota2((chunk, chunk), 1)
    causal = row >= col
    strict = row > col

    def prep(bi):
        u = qkv_ref[bi]
        hist_scr[bi, SUBLANES:SUBLANES + cd, :] = u
        y = u * cw_ref[CONV_W - 1:CONV_W, :]
        for i in range(CONV_W - 1):
            y = y + hist_scr[bi, hist_lo + i:hist_lo + i + cd, :] * cw_ref[i:i + 1, :]
        hist_scr[bi, hist_lo:SUBLANES, :] = hist_scr[bi, hist_lo + cd:SUBLANES + cd, :]
        qkv = _pad_rows(_silu(y), chunk)
        q, k, v = (qkv[:, i * D_MIX_HALF:(i + 1) * D_MIX_HALF] for i in range(3))
        q = q * lax.rsqrt(_head_sums(q * q) + EPS) * (HEAD_DIM ** -0.5)
        k = k * lax.rsqrt(_head_sums(k * k) + EPS)
        beta = _sigmoid(_pad_rows(b_ref[bi], chunk))
        g = -jnp.exp(alog_ref[...]) * _softplus(_pad_rows(a_ref[bi], chunk) + dtb_ref[...])
        if t_valid is not None:
            valid = (c * cd + _iota2((chunk, D_MIX_HALF), 0)) < t_valid
            q, k, v = (jnp.where(valid, x, 0.0) for x in (q, k, v))
            beta = jnp.where(valid, beta, 0.0)
            g = jnp.where(valid, g, 0.0)
        gcum = _cumsum_rows(g)
        eg = jnp.exp(gcum)
        g_last = gcum[chunk - 1:chunk, :]
        return dict(q=q, k=k, beta=beta, gcum=gcum, rhs_v=beta * v, rhs_k=beta * eg * k, q_dec=eg * q,
                    k_dec=jnp.exp(g_last - gcum) * k, eg_last=jnp.exp(g_last))

    def group(bis):
        pre = [prep(bi) for bi in bis]
        chains = [(r, h) for r in range(len(bis)) for h in range(N_HEADS)]
        n = range(len(chains))

        def head(name, i, width=HEAD_DIM):
            r, h = chains[i]
            return pre[r][name][:, h * HEAD_DIM:h * HEAD_DIM + width]

        g_is = [head('gcum', i, chunk) for i in n]
        decays = [jnp.where(causal, jnp.exp(jnp.minimum(g_i - g_i.T, 0.0)), 0.0) for g_i in g_is]
        qk_kks = [_mm_nt(jnp.concatenate([head('k', i), head('q', i)], axis=0), head('k', i)) for i in n]
        lows = [jnp.where(strict, head('beta', i, chunk) * decays[i] * qk_kks[i][:chunk], 0.0) for i in n]
        t_invs = _unit_lower_inverses(lows, chunk, _mm)
        sols = [_mm(t_invs[i], jnp.concatenate([head('rhs_v', i), head('rhs_k', i)], axis=1)) for i in n]
        states = [s_ref[bis[r], h] for r, h in chains]
        wss = [_mm(jnp.concatenate([sols[i][:, HEAD_DIM:], head('q_dec', i)], axis=0), states[i])
               for i in n]
        u_news = [sols[i][:, :HEAD_DIM] - wss[i][:chunk] for i in n]
        outs = [wss[i][chunk:] + _mm(qk_kks[i][chunk:] * decays[i], u_news[i]) for i in n]
        s_news = [head('eg_last', i) * states[i] + _mm_tn(head('k_dec', i), u_news[i]) for i in n]
        for i, (r, h) in enumerate(chains):
            s_ref[bis[r], h] = s_news[i]
        for r, bi in enumerate(bis):
            o = jnp.concatenate(outs[r * N_HEADS:(r + 1) * N_HEADS], axis=1)
            o = o * lax.rsqrt(_head_sums(o * o) * (1.0 / HEAD_DIM) + EPS) * nw_ref[...]
            o = o * _silu(_pad_rows(z_ref[bi], chunk))
            o_ref[bi] = o[:cd]

    _for_row_groups(bsz, rows_per_group, group)


def _gdn(cols, conv0, s0, conv_w, alog_rep, dtb_rep, nw_rep, bsz, rows_per_group, cd, chunk, t_valid):
    b, t, _ = cols.shape

    def col_spec(width, start):
        return pl.BlockSpec((bsz, cd, width), lambda i, c: (i, c, start // width))

    vec_spec = pl.BlockSpec((1, D_MIX_HALF), lambda i, c: (0, 0))
    state_spec = pl.BlockSpec((bsz, N_HEADS, HEAD_DIM, HEAD_DIM), lambda i, c: (i, 0, 0, 0))
    kern = functools.partial(_gdn_kernel, chunk=chunk, t_valid=t_valid, rows_per_group=rows_per_group)
    return pl.pallas_call(
        kern,
        grid=(b // bsz, t // cd),
        in_specs=[col_spec(GDN_QKV, 0), col_spec(D_MIX_HALF, COL_Z), col_spec(D_MIX_HALF, COL_B),
                  col_spec(D_MIX_HALF, COL_A),
                  pl.BlockSpec((bsz, CONV_W - 1, GDN_QKV), lambda i, c: (i, 0, 0)),
                  pl.BlockSpec((CONV_W, GDN_QKV), lambda i, c: (0, 0)),
                  vec_spec, vec_spec, vec_spec, state_spec],
        out_specs=[pl.BlockSpec((bsz, cd, D_MIX_HALF), lambda i, c: (i, c, 0)), state_spec],
        out_shape=[jax.ShapeDtypeStruct((b, t, D_MIX_HALF), F32),
                   jax.ShapeDtypeStruct((b, N_HEADS, HEAD_DIM, HEAD_DIM), F32)],
        scratch_shapes=[pltpu.VMEM((bsz, SUBLANES + cd, GDN_QKV), F32)],
        compiler_params=pltpu.CompilerParams(dimension_semantics=("arbitrary",) * 2, vmem_limit_bytes=VMEM_LIMIT),
        name="gdn",
    )(cols, cols, cols, cols, conv0, conv_w, alog_rep, dtb_rep, nw_rep, s0)


def _rwkv_kernel(rkv_ref, l_ref, p_ref, pl_ref, mu_ref, mul_ref, w0_ref, a0_ref, kk_ref, ka_ref, rk_ref,
                 lnw_ref, lnb_ref, wup_ref, aup_ref, gup_ref, s0_ref, o_ref, s_ref, hist_scr, histl_scr,
                 *, chunk, t_valid, rows_per_group):
    bsz, cd, _ = rkv_ref.shape
    c = pl.program_id(1)
    prev_row = SUBLANES - 1

    @pl.when(c == 0)
    def _():
        s_ref[...] = s0_ref[...]
        hist_scr[:, prev_row:SUBLANES, :] = p_ref[...]
        histl_scr[:, prev_row:SUBLANES, :] = pl_ref[...]

    row = _iota2((chunk, chunk), 0)
    col = _iota2((chunk, chunk), 1)
    causal = row >= col
    strict = row > col

    def shifted(bi, u_ref, scr, m_ref):
        u = u_ref[bi]
        scr[bi, SUBLANES:SUBLANES + cd, :] = u
        prev = scr[bi, prev_row:prev_row + cd, :]
        scr[bi, prev_row:SUBLANES, :] = scr[bi, prev_row + cd:SUBLANES + cd, :]
        return _pad_rows(u + (prev - u) * m_ref[...], chunk)

    def prep(bi):
        rkv = shifted(bi, rkv_ref, hist_scr, mu_ref)
        r, k, v = (rkv[:, i * D_MIX_HALF:(i + 1) * D_MIX_HALF] for i in range(3))
        xl = shifted(bi, l_ref, histl_scr, mul_ref)
        dw = xl[:, :LORA_W]
        da = xl[:, LORA_W:LORA_W + LORA_A]
        dg = xl[:, LORA_W + LORA_A:]
        w_log = -_softplus(-(w0_ref[...] + _mm(jnp.tanh(dw), wup_ref[...]))) - 0.5
        lw = -jnp.exp(w_log)
        a = _sigmoid(a0_ref[...] + _mm(da, aup_ref[...]))
        gate = _mm(_sigmoid(dg), gup_ref[...])
        kk = k * kk_ref[...]
        kk = kk * lax.rsqrt(_head_sums(kk * kk) + EPS)
        k = k * (1.0 + (a - 1.0) * ka_ref[...])
        if t_valid is not None:
            valid = (c * cd + _iota2((chunk, D_MIX_HALF), 0)) < t_valid
            r, k, v, kk, lw = (jnp.where(valid, x, 0.0) for x in (r, k, v, kk, lw))
        cum = _cumsum_rows(lw)
        e_out = jnp.exp(-cum)
        cum_last = cum[chunk - 1:chunk, :]
        e_rest = jnp.exp(cum_last - cum)
        return dict(r=r, k=k, v=v, gate=gate, a_t=-kk * jnp.exp(cum - lw), b_t=kk * a * e_out, k_t=k * e_out,
                    r_t=r * jnp.exp(cum), b_c=kk * a * e_rest, k_c=k * e_rest, e_last=jnp.exp(cum_last))

    def group(bis):
        pre = [prep(bi) for bi in bis]
        chains = [(r, h) for r in range(len(bis)) for h in range(N_HEADS)]
        n = range(len(chains))

        def head(name, i):
            r, h = chains[i]
            return pre[r][name][:, h * HEAD_DIM:(h + 1) * HEAD_DIM]

        ars = [jnp.concatenate([head('a_t', i), head('r_t', i)], axis=0) for i in n]
        bks = [jnp.concatenate([head('b_t', i), head('k_t', i)], axis=0) for i in n]
        ms = [_mm_nt(ars[i], bks[i]) for i in n]
        t_invs = _unit_lower_inverses([jnp.where(strict, -m[:chunk, :chunk], 0.0) for m in ms], chunk, _mm)
        states = [s_ref[bis[r], h] for r, h in chains]
        arss = [_mm_nt(ars[i], states[i]) for i in n]
        vhs = [head('v', i) for i in n]
        akvs = [_mm(jnp.where(strict, ms[i][:chunk, chunk:], 0.0), vhs[i]) for i in n]
        us = [_mm(t_invs[i], arss[i][:chunk] + akvs[i]) for i in n]
        uvs = [jnp.concatenate([us[i], vhs[i]], axis=0) for i in n]
        causal2 = _iota2((chunk, 2 * chunk), 0) >= (_iota2((chunk, 2 * chunk), 1) & (chunk - 1))
        ys = [arss[i][chunk:] + _mm(jnp.where(causal2, ms[i][chunk:], 0.0), uvs[i]) for i in n]
        s_news = [states[i] * head('e_last', i)
                  + _mm_tn(uvs[i], jnp.concatenate([head('b_c', i), head('k_c', i)], axis=0)) for i in n]
        for i, (r, h) in enumerate(chains):
            s_ref[bis[r], h] = s_news[i]
        for r, bi in enumerate(bis):
            p = pre[r]
            y = jnp.concatenate(ys[r * N_HEADS:(r + 1) * N_HEADS], axis=1)
            mean = _head_sums(y) * (1.0 / HEAD_DIM)
            dy = y - mean
            var = _head_sums(dy * dy) * (1.0 / HEAD_DIM)
            y = dy * lax.rsqrt(var + GN_EPS) * lnw_ref[...] + lnb_ref[...]
            bonus = _head_sums(p['r'] * p['k'] * rk_ref[...]) * p['v']
            o_ref[bi] = ((y + bonus) * p['gate'])[:cd]

    _for_row_groups(bsz, rows_per_group, group)


def _rwkv(cols, shift0, s0, mu, vecs, w_up, a_up, g_up, bsz, rows_per_group, cd, chunk, t_valid):
    b, t, _ = cols.shape

    def full2(shape):
        return pl.BlockSpec(shape, lambda i, c: (0, 0))

    state_spec = pl.BlockSpec((bsz, N_HEADS, HEAD_DIM, HEAD_DIM), lambda i, c: (i, 0, 0, 0))
    kern = functools.partial(_rwkv_kernel, chunk=chunk, t_valid=t_valid, rows_per_group=rows_per_group)
    return pl.pallas_call(
        kern,
        grid=(b // bsz, t // cd),
        in_specs=[pl.BlockSpec((bsz, cd, RWKV_RKV), lambda i, c: (i, c, COL_RW // RWKV_RKV)),
                  pl.BlockSpec((bsz, cd, LORA_COLS), lambda i, c: (i, c, COL_LORA // LORA_COLS)),
                  pl.BlockSpec((bsz, 1, RWKV_RKV), lambda i, c: (i, 0, 0)),
                  pl.BlockSpec((bsz, 1, LORA_COLS), lambda i, c: (i, 0, RWKV_RKV // LORA_COLS)),
                  pl.BlockSpec((1, RWKV_RKV), lambda i, c: (0, 0)),
                  pl.BlockSpec((1, LORA_COLS), lambda i, c: (0, RWKV_RKV // LORA_COLS))]
                 + [full2((1, D_MIX_HALF))] * 7
                 + [full2((LORA_W, D_MIX_HALF)), full2((LORA_A, D_MIX_HALF)), full2((LORA_G, D_MIX_HALF)),
                    state_spec],
        out_specs=[pl.BlockSpec((bsz, cd, D_MIX_HALF), lambda i, c: (i, c, 0)), state_spec],
        out_shape=[jax.ShapeDtypeStruct((b, t, D_MIX_HALF), F32),
                   jax.ShapeDtypeStruct((b, N_HEADS, HEAD_DIM, HEAD_DIM), F32)],
        scratch_shapes=[pltpu.VMEM((bsz, SUBLANES + cd, RWKV_RKV), F32),
                        pltpu.VMEM((bsz, SUBLANES + cd, LORA_COLS), F32)],
        compiler_params=pltpu.CompilerParams(dimension_semantics=("arbitrary",) * 2, vmem_limit_bytes=VMEM_LIMIT),
        name="rwkv",
    )(cols, cols, shift0, shift0, mu, mu, *vecs, w_up, a_up, g_up, s0)


def _mod_spec(m, bb, tt):
    d = m.shape[-1]
    if m.shape[1] == 1:
        return pl.BlockSpec((bb, 1, d), lambda i, j, *_: (i, 0, 0))
    return pl.BlockSpec((1, tt, d), lambda i, j, *_: (0, j, 0))


def _outproj_kernel(oa_ref, ob_ref, x_ref, gt_ref, sc_ref, sh_ref, nw_ref, wa_ref, wb_ref, wr_ref, br_ref,
                    x1_ref, h_ref, wt_ref):
    bb, tt, d = x_ref.shape
    n = bb * tt
    if len(oa_ref.shape) == 2:
        mixed = _mm_tn(oa_ref[...], wa_ref[...]) + _mm_tn(ob_ref[...], wb_ref[...])
    else:
        oa = oa_ref[...].reshape(n, D_MIX_HALF).astype(BF16)
        ob = ob_ref[...].reshape(n, D_MIX_HALF).astype(BF16)
        mixed = (jnp.dot(oa, wa_ref[...], preferred_element_type=F32)
                 + jnp.dot(ob, wb_ref[...], preferred_element_type=F32))
    x1 = x_ref[...] + gt_ref[...] * mixed.reshape(bb, tt, d)
    x1_ref[...] = x1
    h = _modulated_norm(x1, nw_ref[...], sc_ref[...], sh_ref[...]).reshape(n, d)
    h_ref[...] = h.reshape(bb, tt, d).astype(BF16)

    h_hi, h_lo = _split2(h)
    hw = jnp.dot(h_hi, wr_ref[...], preferred_element_type=F32)
    logits = (hw[:, :LANES] + hw[:, LANES:] + jnp.dot(h_lo, wr_ref[:, :LANES], preferred_element_type=F32)
              + br_ref[...])
    lane_i = _iota2((n, LANES), 1)
    lane = lane_i.astype(F32)
    lane_grp = (lane_i >> (EXP_PER_GROUP.bit_length() - 1)).astype(F32)
    neg = jnp.float32(-jnp.inf)
    big = jnp.float32(LANES)
    is_grp = (lane_i >= N_EXPERTS) & (lane_i < N_EXPERTS + N_GROUPS)
    gl = jnp.where(is_grp, logits, neg)
    gmax = jnp.max(gl, axis=-1, keepdims=True)
    grp = jnp.min(jnp.where(gl == gmax, lane, big), axis=-1, keepdims=True) - N_EXPERTS
    g_prob = 1.0 / jnp.sum(jnp.where(is_grp, jnp.exp(gl - gmax), 0.0), axis=-1, keepdims=True)
    in_grp = (lane_i < N_EXPERTS) & (lane_grp == grp)
    el = jnp.where(in_grp, logits, neg)
    m1 = jnp.max(el, axis=-1, keepdims=True)
    i1 = jnp.min(jnp.where(el == m1, lane, big), axis=-1, keepdims=True)
    el2 = jnp.where(lane == i1, neg, el)
    m2 = jnp.max(el2, axis=-1, keepdims=True)
    i2 = jnp.min(jnp.where(el2 == m2, lane, big), axis=-1, keepdims=True)
    e2 = jnp.exp(m2 - m1)
    p1 = g_prob / (1.0 + e2)
    p2 = g_prob * e2 / (1.0 + e2)
    wt = jnp.where(lane == i1, p1, 0.0) + jnp.where(lane == i2, p2, 0.0)
    wt_ref[...] = wt.reshape(bb, tt, LANES)


def _outproj(oa, ob, x, gt, sc, sh, nw, wa, wb, wr, br, bb, tt):
    b, t, d = x.shape
    grid = (b // bb, t // tt)

    def tok_spec(w):
        return pl.BlockSpec((bb, tt, w), lambda i, j: (i, j, 0))

    def full2(shape):
        return pl.BlockSpec(shape, lambda i, j: (0, 0))

    mod_spec = _mod_spec(gt, bb, tt)
    if oa.ndim == 2:
        assert grid == (1, 1)
        mix_spec = full2(oa.shape)
    else:
        mix_spec = tok_spec(D_MIX_HALF)
    return pl.pallas_call(
        _outproj_kernel,
        grid=grid,
        in_specs=[mix_spec, mix_spec, tok_spec(d), mod_spec, mod_spec, mod_spec,
                  pl.BlockSpec((1, 1, d), lambda i, j: (0, 0, 0)),
                  full2((D_MIX_HALF, d)), full2((D_MIX_HALF, d)), full2((d, 2 * LANES)), full2((1, LANES))],
        out_specs=[tok_spec(d), tok_spec(d), tok_spec(LANES)],
        out_shape=[jax.ShapeDtypeStruct((b, t, d), F32), jax.ShapeDtypeStruct((b, t, d), BF16),
                   jax.ShapeDtypeStruct((b, t, LANES), F32)],
        compiler_params=pltpu.CompilerParams(dimension_semantics=("arbitrary",) * 2, vmem_limit_bytes=VMEM_LIMIT),
        name="outproj",
    )(oa, ob, x, gt, sc, sh, nw, wa, wb, wr, br)


MOE_ROWS = 128
MOE_EXPERTS_PER_STEP = 4


def _moe_kernel(h_ref, wt_ref, x1_ref, gt_ref, fw_ref, wgu_ref, wd_ref, y_ref,
                xg_scr, gtw_scr, yw_scr, cnt_scr, off_scr):
    bb, tt, d = x1_ref.shape
    n = bb * tt
    npos = 2 * n
    e = pl.program_id(2)

    @pl.when(e == 0)
    def _():
        wt_t = wt_ref[...].reshape(n, LANES).T
        member = wt_t > 0.0
        ones = jnp.where(member, 1.0, 0.0)
        upper = jnp.where(_iota2((n, n), 0) < _iota2((n, n), 1), 1.0, 0.0).astype(BF16)
        rank = jnp.dot(_bf(ones), upper, preferred_element_type=F32)
        cnt = jnp.broadcast_to(jnp.sum(ones, axis=1, keepdims=True), (LANES, LANES))
        lower = jnp.where(_iota2((LANES, LANES), 0) > _iota2((LANES, LANES), 1), 1.0, 0.0)
        off = _dot(lower, cnt)
        cnt_scr[...] = cnt
        off_scr[...] = off
        pos = off[:, :1] + rank
        pos1 = jnp.min(jnp.where(member, pos, jnp.float32(4 * n)), axis=0, keepdims=True)
        pos2 = jnp.max(jnp.where(member, pos, -1.0), axis=0, keepdims=True)
        w1 = jnp.sum(jnp.where(member & (pos == pos1), wt_t, 0.0), axis=0, keepdims=True)
        w2 = jnp.sum(jnp.where(member & (pos == pos2), wt_t, 0.0), axis=0, keepdims=True)
        w2 = jnp.where(pos2 != pos1, w2, 0.0)
        h = h_ref[...].reshape(n, d)

        def onehot_block(b, carry):
            p = (b * MOE_ROWS + _iota2((MOE_ROWS, n), 0)).astype(F32)
            rows = pl.ds(pl.multiple_of(b * MOE_ROWS, MOE_ROWS), MOE_ROWS)
            yw_scr[rows, :n] = jnp.where((pos1 == p) | (pos2 == p), 1.0, 0.0).astype(BF16)
            return carry

        lax.fori_loop(0, npos // MOE_ROWS, onehot_block, 0)
        xg_scr[:npos, :] = jnp.dot(yw_scr[:npos, :n], h, preferred_element_type=F32).astype(BF16)
        z_t = jnp.concatenate([pos1, pos2, w1, w2, jnp.zeros((LANES - 4, n), F32)], axis=0).T
        p1c, p2c, w1c, w2c = (z_t[:, i:i + 1] for i in range(4))
        for b in range(npos // LANES):
            p = (b * LANES + _iota2((n, LANES), 1)).astype(F32)
            gtw_scr[:, b * LANES:(b + 1) * LANES] = (jnp.where(p1c == p, w1c, 0.0)
                                                     + jnp.where(p2c == p, w2c, 0.0)).astype(BF16)
        xg_scr[npos:, :] = jnp.zeros((MOE_ROWS, d), BF16)
        yw_scr[...] = jnp.zeros_like(yw_scr)

    es = range(wgu_ref.shape[0])
    offs = [off_scr[pl.ds(e * len(es) + j, 1), 0:1] for j in es]
    firsts = [o.astype(jnp.int32)[0, 0] for o in offs]
    lasts = [(offs[j] + cnt_scr[pl.ds(e * len(es) + j, 1), 0:1]).astype(jnp.int32)[0, 0] for j in es]
    starts = [(f // BF16_ROWS) * BF16_ROWS for f in firsts]

    def window_rows(r0):
        return pl.ds(pl.multiple_of(r0, BF16_ROWS), MOE_ROWS)

    def store_rows(j, r0, y):
        p = r0 + _iota2((MOE_ROWS, 1), 0)
        rows = window_rows(r0)
        yw_scr[rows, :] = jnp.where((p >= firsts[j]) & (p < lasts[j]), y, yw_scr[rows, :])

    gus = [jnp.dot(xg_scr[window_rows(starts[j]), :], wgu_ref[j], preferred_element_type=F32) for j in es]
    acts = [(_silu(gu[:, :D_EXPERT]) * gu[:, D_EXPERT:]).astype(BF16) for gu in gus]
    ys = [jnp.dot(acts[j], wd_ref[j], preferred_element_type=F32).astype(BF16) for j in es]
    for j in es:
        store_rows(j, starts[j], ys[j])

    for j in es:
        def more_windows(w, carry, j=j):
            r0 = starts[j] + w * MOE_ROWS
            gu = jnp.dot(xg_scr[window_rows(r0), :], wgu_ref[j], preferred_element_type=F32)
            act = (_silu(gu[:, :D_EXPERT]) * gu[:, D_EXPERT:]).astype(BF16)
            store_rows(j, r0, jnp.dot(act, wd_ref[j], preferred_element_type=F32).astype(BF16))
            return carry

        lax.fori_loop(1, (lasts[j] - starts[j] + MOE_ROWS - 1) // MOE_ROWS, more_windows, 0)

    @pl.when(e == pl.num_programs(2) - 1)
    def _():
        moe = jnp.dot(gtw_scr[...], yw_scr[:npos, :], preferred_element_type=F32)
        x2 = x1_ref[...] + gt_ref[...] * moe.reshape(bb, tt, d)
        ms = jnp.mean(x2 * x2, axis=-1, keepdims=True)
        y_ref[...] = x2 * lax.rsqrt(ms + EPS) * fw_ref[...]


def _moe(h, wt, x1, gt, fw, w_gu, w_down, bb, tt):
    b, t, d = x1.shape
    es = MOE_EXPERTS_PER_STEP
    grid = (b // bb, t // tt, N_EXPERTS // es)
    npos = 2 * bb * tt

    def tok_spec(w):
        return pl.BlockSpec((bb, tt, w), lambda i, j, e: (i, j, 0))

    return pl.pallas_call(
        _moe_kernel,
        grid=grid,
        in_specs=[tok_spec(d), tok_spec(LANES), tok_spec(d),
                  _mod_spec(gt, bb, tt),
                  pl.BlockSpec((1, 1, d), lambda i, j, e: (0, 0, 0)),
                  pl.BlockSpec((es, d, 2 * D_EXPERT), lambda i, j, e: (e, 0, 0)),
                  pl.BlockSpec((es, D_EXPERT, d), lambda i, j, e: (e, 0, 0))],
        out_specs=tok_spec(d),
        out_shape=jax.ShapeDtypeStruct((b, t, d), F32),
        scratch_shapes=[pltpu.VMEM((npos + MOE_ROWS, d), BF16), pltpu.VMEM((bb * tt, npos), BF16),
                        pltpu.VMEM((npos + MOE_ROWS, d), BF16), pltpu.VMEM((LANES, LANES), F32),
                        pltpu.VMEM((LANES, LANES), F32)],
        compiler_params=pltpu.CompilerParams(dimension_semantics=("arbitrary",) * 3,
                                             vmem_limit_bytes=MOE_VMEM_LIMIT),
        name="moe",
    )(h, wt, x1, gt, fw, w_gu, w_down)


ROW_Z = GDN_QKV
ROW_R = ROW_Z + D_MIX_HALF
ROW_LORA = ROW_R + RWKV_RKV
ROW_BETA = ROW_LORA + LORA_COLS
ROW_A = ROW_BETA + N_HEADS
N_ROWS = ROW_A + N_HEADS


def _inproj_t_kernel(x_ref, sc_ref, sh_ref, nw_ref, w_ref, o_ref):
    t, b, d = x_ref.shape
    h = _modulated_norm(x_ref[...], nw_ref[...], sc_ref[...], sh_ref[...]).reshape(t * b, d)
    o_ref[...] = _mm_nt(w_ref[...], h)


def _inproj_t(x, sc, sh, nw, w_rows):
    t, b, d = x.shape
    nr = w_rows.shape[0]

    def full(shape):
        return pl.BlockSpec(shape, lambda i: (0,) * len(shape))

    return pl.pallas_call(
        _inproj_t_kernel,
        grid=(1,),
        in_specs=[full((t, b, d)), full((1, b, d)), full((1, b, d)), full((1, 1, d)), full((nr, d))],
        out_specs=full((nr, t * b)),
        out_shape=jax.ShapeDtypeStruct((nr, t * b), F32),
        compiler_params=pltpu.CompilerParams(dimension_semantics=("arbitrary",), vmem_limit_bytes=VMEM_LIMIT),
        name="inproj_t",
    )(x, sc, sh, nw, w_rows)


def _store_state_rows(s_scr, sout_ref):
    for j in range(HEAD_DIM // 2):
        pair = jnp.concatenate([s_scr[2 * j], s_scr[2 * j + 1]], axis=0)
        sout_ref[:, j * LANES:(j + 1) * LANES] = pair.T


def _gdn_s_kernel(q_ref, k_ref, v_ref, z_ref, b_ref, a_ref, cq_ref, ck_ref, cv_ref, wq_ref, wk_ref, wv_ref,
                  alog_ref, dtb_ref, nw_ref, s0_ref, o_ref, sout_ref, s_scr, kq_scr):
    nb = s0_ref.shape[-1]
    nt = q_ref.shape[1] // nb
    h = pl.program_id(0)

    def tok(ref, t):
        return ref[:, t * nb:(t + 1) * nb]

    def conv(u_ref, c_ref, w_ref):
        full = [c_ref[i] for i in range(CONV_W - 1)] + [tok(u_ref, t) for t in range(nt)]
        taps = [w_ref[:, i:i + 1] for i in range(CONV_W)]
        outs = []
        for t in range(nt):
            y = full[t + CONV_W - 1] * taps[CONV_W - 1]
            for i in range(CONV_W - 1):
                y = y + full[t + i] * taps[i]
            outs.append(_silu(y))
        return outs

    qs = conv(q_ref, cq_ref, wq_ref)
    ks = conv(k_ref, ck_ref, wk_ref)
    vs = conv(v_ref, cv_ref, wv_ref)
    neg_rate = -jnp.exp(alog_ref[pl.ds(h, 1), :])
    dtb = dtb_ref[pl.ds(h, 1), :]
    beta_in = b_ref[pl.ds(h, 1), :]
    a_in = a_ref[pl.ds(h, 1), :]
    s_scr[...] = s0_ref[0]
    zero = jnp.zeros((HEAD_DIM, nb), F32)
    for t in range(nt):
        q = qs[t] * lax.rsqrt(jnp.sum(qs[t] * qs[t], axis=0, keepdims=True) + EPS) * (HEAD_DIM ** -0.5)
        k = ks[t] * lax.rsqrt(jnp.sum(ks[t] * ks[t], axis=0, keepdims=True) + EPS)
        beta = _sigmoid(beta_in[:, t * nb:(t + 1) * nb])
        decay = jnp.exp(neg_rate * _softplus(a_in[:, t * nb:(t + 1) * nb] + dtb))
        kq_scr[0] = k
        kq_scr[1] = q

        def decay_and_project(i, acc):
            row = s_scr[i] * decay
            s_scr[i] = row
            return acc + kq_scr[0, pl.ds(i, 1), :] * row

        k_s = lax.fori_loop(0, HEAD_DIM, decay_and_project, zero, unroll=8)
        u = beta * (vs[t] - k_s)

        def update_and_read(i, acc):
            row = s_scr[i] + kq_scr[0, pl.ds(i, 1), :] * u
            s_scr[i] = row
            return acc + kq_scr[1, pl.ds(i, 1), :] * row

        o = lax.fori_loop(0, HEAD_DIM, update_and_read, zero, unroll=8)
        o = o * lax.rsqrt(jnp.mean(o * o, axis=0, keepdims=True) + EPS) * nw_ref[...]
        o_ref[:, t * nb:(t + 1) * nb] = o * _silu(tok(z_ref, t))
    _store_state_rows(s_scr, sout_ref)


def _gdn_s(cols_t, conv_t, s_t, cw_t, alog_b, dtb_b, nw_b):
    ntok = cols_t.shape[1]
    nb = s_t.shape[-1]

    def head_rows(base):
        return pl.BlockSpec((HEAD_DIM, ntok), lambda h: (base // HEAD_DIM + h, 0))

    def conv_rows(base):
        return pl.BlockSpec((CONV_W - 1, HEAD_DIM, nb), lambda h: (0, base // HEAD_DIM + h, 0))

    def tap_rows(base):
        return pl.BlockSpec((HEAD_DIM, CONV_W), lambda h: (base // HEAD_DIM + h, 0))

    def full2(shape):
        return pl.BlockSpec(shape, lambda h: (0, 0))

    return pl.pallas_call(
        _gdn_s_kernel,
        grid=(N_HEADS,),
        in_specs=[head_rows(0), head_rows(D_MIX_HALF), head_rows(2 * D_MIX_HALF), head_rows(ROW_Z),
                  pl.BlockSpec((N_HEADS, ntok), lambda h: (ROW_BETA // N_HEADS, 0)),
                  pl.BlockSpec((N_HEADS, ntok), lambda h: (ROW_A // N_HEADS, 0)),
                  conv_rows(0), conv_rows(D_MIX_HALF), conv_rows(2 * D_MIX_HALF),
                  tap_rows(0), tap_rows(D_MIX_HALF), tap_rows(2 * D_MIX_HALF),
                  full2((N_HEADS, nb)), full2((N_HEADS, nb)), full2((HEAD_DIM, nb)),
                  pl.BlockSpec((1, HEAD_DIM, HEAD_DIM, nb), lambda h: (h, 0, 0, 0))],
        out_specs=[pl.BlockSpec((HEAD_DIM, ntok), lambda h: (h, 0)),
                   pl.BlockSpec((nb, HEAD_DIM * HEAD_DIM), lambda h: (0, h))],
        out_shape=[jax.ShapeDtypeStruct((D_MIX_HALF, ntok), F32),
                   jax.ShapeDtypeStruct((nb, N_HEADS * HEAD_DIM * HEAD_DIM), F32)],
        scratch_shapes=[pltpu.VMEM((HEAD_DIM, HEAD_DIM, nb), F32), pltpu.VMEM((2, HEAD_DIM, nb), F32)],
        compiler_params=pltpu.CompilerParams(dimension_semantics=("arbitrary",), vmem_limit_bytes=VMEM_LIMIT),
        name="gdn_s",
    )(cols_t, cols_t, cols_t, cols_t, cols_t, cols_t, conv_t, conv_t, conv_t, cw_t, cw_t, cw_t,
      alog_b, dtb_b, nw_b, s_t)


def _rwkv_s_kernel(r_ref, k_ref, v_ref, l_ref, pr_ref, pk_ref, pv_ref, pl_ref, mr_ref, mk_ref, mv_ref, ml_ref,
                   vec_ref, wup_ref, aup_ref, gup_ref, s0_ref, o_ref, sout_ref, s_scr, y_scr, v_scr):
    nb = s0_ref.shape[-1]
    nt = r_ref.shape[1] // nb

    def mixed(u_ref, p_ref, m_ref):
        cur = [u_ref[:, t * nb:(t + 1) * nb] for t in range(nt)]
        prev = [p_ref[...]] + cur[:-1]
        mu = m_ref[...]
        return [c + (p - c) * mu for c, p in zip(cur, prev)]

    rs = mixed(r_ref, pr_ref, mr_ref)
    ks = mixed(k_ref, pk_ref, mk_ref)
    vs = mixed(v_ref, pv_ref, mv_ref)
    ls = mixed(l_ref, pl_ref, ml_ref)
    w0, a0, k_k, k_a, r_k, ln_w, ln_b = (vec_ref[:, i:i + 1] for i in range(7))
    s_scr[...] = s0_ref[0]
    for t in range(nt):
        dw = ls[t][:LORA_W]
        da = ls[t][LORA_W:LORA_W + LORA_A]
        dg = ls[t][LORA_W + LORA_A:]
        w_log = -_softplus(-(w0 + _mm(wup_ref[...], jnp.tanh(dw)))) - 0.5
        w = jnp.exp(-jnp.exp(w_log))
        a = _sigmoid(a0 + _mm(aup_ref[...], da))
        gate = _mm(gup_ref[...], _sigmoid(dg))
        r, v = rs[t], vs[t]
        kk = ks[t] * k_k
        kk = kk * lax.rsqrt(jnp.sum(kk * kk, axis=0, keepdims=True) + EPS)
        k = ks[t] * (1.0 + (a - 1.0) * k_a)
        neg_kk = -kk
        kk_a = kk * a
        v_scr[...] = v

        def state_row(i, carry):
            row = s_scr[i]
            sa = jnp.sum(row * neg_kk, axis=0, keepdims=True)
            row = row * w + sa * kk_a + v_scr[pl.ds(i, 1), :] * k
            s_scr[i] = row
            y_scr[pl.ds(i, 1), :] = jnp.sum(row * r, axis=0, keepdims=True)
            return carry

        lax.fori_loop(0, HEAD_DIM, state_row, 0, unroll=4)
        y = y_scr[...]
        dy = y - jnp.mean(y, axis=0, keepdims=True)
        y = dy * lax.rsqrt(jnp.mean(dy * dy, axis=0, keepdims=True) + GN_EPS) * ln_w + ln_b
        bonus = jnp.sum(r * k * r_k, axis=0, keepdims=True) * v
        o_ref[:, t * nb:(t + 1) * nb] = (y + bonus) * gate
    _store_state_rows(s_scr, sout_ref)


def _rwkv_s(cols_t, shift_t, s_t, mu_b, vecs_t, w_up_t, a_up_t, g_up_t):
    ntok = cols_t.shape[1]
    nb = s_t.shape[-1]

    def head_rows(width, base):
        return pl.BlockSpec((HEAD_DIM, width), lambda h: (base // HEAD_DIM + h, 0))

    def lora_rows(width, base):
        return pl.BlockSpec((LORA_COLS, width), lambda h: (base // LORA_COLS, 0))

    return pl.pallas_call(
        _rwkv_s_kernel,
        grid=(N_HEADS,),
        in_specs=[head_rows(ntok, ROW_R), head_rows(ntok, ROW_R + D_MIX_HALF), head_rows(ntok, ROW_R + 2 * D_MIX_HALF),
                  lora_rows(ntok, ROW_LORA),
                  head_rows(nb, 0), head_rows(nb, D_MIX_HALF), head_rows(nb, 2 * D_MIX_HALF), lora_rows(nb, RWKV_RKV),
                  head_rows(nb, 0), head_rows(nb, D_MIX_HALF), head_rows(nb, 2 * D_MIX_HALF), lora_rows(nb, RWKV_RKV),
                  head_rows(SUBLANES, 0), head_rows(LORA_W, 0), head_rows(LORA_A, 0), head_rows(LORA_G, 0),
                  pl.BlockSpec((1, HEAD_DIM, HEAD_DIM, nb), lambda h: (h, 0, 0, 0))],
        out_specs=[pl.BlockSpec((HEAD_DIM, ntok), lambda h: (h, 0)),
                   pl.BlockSpec((nb, HEAD_DIM * HEAD_DIM), lambda h: (0, h))],
        out_shape=[jax.ShapeDtypeStruct((D_MIX_HALF, ntok), F32),
                   jax.ShapeDtypeStruct((nb, N_HEADS * HEAD_DIM * HEAD_DIM), F32)],
        scratch_shapes=[pltpu.VMEM((HEAD_DIM, HEAD_DIM, nb), F32), pltpu.VMEM((HEAD_DIM, nb), F32),
                        pltpu.VMEM((HEAD_DIM, nb), F32)],
        compiler_params=pltpu.CompilerParams(dimension_semantics=("arbitrary",), vmem_limit_bytes=VMEM_LIMIT),
        name="rwkv_s",
    )(cols_t, cols_t, cols_t, cols_t, shift_t, shift_t, shift_t, shift_t, mu_b, mu_b, mu_b, mu_b,
      vecs_t, w_up_t, a_up_t, g_up_t, s_t)


def _layer(x, mod, conv0, s_gdn0, shift0, s_rwkv0, p, *, t_valid, bb, tt, moe_tile, bsz, rows_per_group, cd,
           chunk):
    sh_m, sc_m, gt_m, sh_f, sc_f, gt_f = mod
    cols = _inproj(x, sc_m, sh_m, p['norm_mix_w'], p['w_cols'], bb, tt)
    o_a, s_gdn = _gdn(cols, conv0, s_gdn0, p['gdn_conv_w'], p['alog_rep'], p['dtb_rep'], p['gdn_nw_rep'],
                      bsz, rows_per_group, cd, chunk, t_valid)
    o_b, s_rwkv = _rwkv(cols, shift0, s_rwkv0, p['rwkv_mu'], p['rwkv_vecs'], p['rwkv_w_up'], p['rwkv_a_up'],
                        p['rwkv_g_up'], bsz, rows_per_group, cd, chunk, t_valid)
    x1, h2, wt = _outproj(o_a, o_b, x, gt_m, sc_f, sh_f, p['norm_ffn_w'], p['w_out_a'], p['w_out_b'],
                          p['w_router'], p['b_router'], bb, tt)
    y = _moe(h2, wt, x1, gt_f, p['final_norm_w'], p['w_gu'], p['w_down'], *moe_tile)
    t_last = x.shape[1] if t_valid is None else t_valid
    conv_new = cols[:, t_last - (CONV_W - 1):t_last, :GDN_QKV]
    shift_new = cols[:, t_last - 1, COL_RW:]
    return y, conv_new, s_gdn, shift_new, s_rwkv


def _sample_layer(x, mod, conv0, s_gdn0, shift0, s_rwkv0, p, ps):
    b, t, d = x.shape
    assert t >= CONV_W - 1
    sh_m, sc_m, gt_m, sh_f, sc_f, gt_f = mod
    x_tm = jnp.transpose(x, (1, 0, 2))
    cols_t = _inproj_t(x_tm, sc_m, sh_m, p['norm_mix_w'], ps['w_rows'])
    seq_last = (1, 2, 3, 0)
    o_a, s_gdn = _gdn_s(cols_t, jnp.transpose(conv0, (1, 2, 0)), jnp.transpose(s_gdn0, seq_last),
                        ps['gdn_conv_w_t'], ps['alog_b'], ps['dtb_b'], ps['gdn_nw_b'])
    o_b, s_rwkv = _rwkv_s(cols_t, shift0.T, jnp.transpose(s_rwkv0, seq_last), ps['rwkv_mu_b'], ps['rwkv_vecs_t'],
                          ps['w_up_t'], ps['a_up_t'], ps['g_up_t'])
    x1, h2, wt = _outproj(o_a, o_b, x_tm, gt_m, sc_f, sh_f, p['norm_ffn_w'], p['w_out_a'], p['w_out_b'],
                          p['w_router'], p['b_router'], t, b)
    y = _moe(h2, wt, x1, gt_f, p['final_norm_w'], p['w_gu'], p['w_down'], t, b)
    keep = CONV_W - 1
    conv_new = jnp.transpose(cols_t[:GDN_QKV, (t - keep) * b:].reshape(GDN_QKV, keep, b), (2, 1, 0))
    shift_new = cols_t[ROW_R:ROW_BETA, (t - 1) * b:].T
    state_shape = (b, N_HEADS, HEAD_DIM, HEAD_DIM)
    return (jnp.transpose(y, (1, 0, 2)), conv_new, s_gdn.reshape(state_shape), shift_new,
            s_rwkv.reshape(state_shape))


def kernel(x_prompt, x_sample, state_gdn_conv, state_gdn, state_rwkv_shift, state_rwkv, c_prompt, c_sample, ada_w, ada_b, norm_mix_w, w_in, gdn_conv_w, gdn_a_log, gdn_dt_bias, gdn_norm_w, rwkv_mu, rwkv_w0, rwkv_w_up, rwkv_a0, rwkv_a_up, rwkv_g_up, rwkv_k_k, rwkv_k_a, rwkv_r_k, rwkv_ln_w, rwkv_ln_b, w_out, norm_ffn_w, router_group_w, router_group_b, router_expert_w, router_expert_b, expert_w_gate_up, expert_w_down, final_norm_w):
    depth = ada_w.shape[0]
    assert depth == 1
    l = 0
    b_p, t_p, d = x_prompt.shape
    b_s, t_s, _ = x_sample.shape
    gdn_cols = GDN_QKV + D_MIX_HALF + 2 * N_HEADS

    w = w_in[l]
    beta0 = GDN_QKV + D_MIX_HALF
    w_bf = w.astype(BF16)
    w_cols = jnp.concatenate([w_bf[:, :beta0], jnp.repeat(w_bf[:, beta0:gdn_cols], HEAD_DIM, axis=1),
                              w_bf[:, gdn_cols:]], axis=1)
    row = lambda a: a.reshape(1, -1)
    n_route = N_EXPERTS + N_GROUPS
    w_router = jnp.concatenate([router_expert_w[l], router_group_w[l], jnp.zeros((d, LANES - n_route), F32)], axis=1)
    w_router = jnp.concatenate(_split2(w_router), axis=1)
    b_router = jnp.concatenate([router_expert_b[l], router_group_b[l], jnp.zeros((LANES - n_route,), F32)])[None]
    p = {
        'norm_mix_w': norm_mix_w[l].reshape(1, 1, d), 'w_cols': w_cols,
        'gdn_conv_w': gdn_conv_w[l], 'alog_rep': row(jnp.repeat(gdn_a_log[l], HEAD_DIM)),
        'dtb_rep': row(jnp.repeat(gdn_dt_bias[l], HEAD_DIM)),
        'gdn_nw_rep': row(jnp.tile(gdn_norm_w[l], N_HEADS)),
        'rwkv_mu': row(rwkv_mu[l]),
        'rwkv_vecs': tuple(row(a) for a in (rwkv_w0[l], rwkv_a0[l], rwkv_k_k[l], rwkv_k_a[l], rwkv_r_k[l],
                                            rwkv_ln_w[l], rwkv_ln_b[l])),
        'rwkv_w_up': rwkv_w_up[l], 'rwkv_a_up': rwkv_a_up[l], 'rwkv_g_up': rwkv_g_up[l],
        'w_out_a': w_out[l][:D_MIX_HALF].astype(BF16), 'w_out_b': w_out[l][D_MIX_HALF:].astype(BF16),
        'norm_ffn_w': norm_ffn_w[l].reshape(1, 1, d), 'w_router': w_router, 'b_router': b_router,
        'w_gu': expert_w_gate_up[l].astype(BF16), 'w_down': expert_w_down[l].astype(BF16),
        'final_norm_w': final_norm_w.reshape(1, 1, d),
    }

    col = lambda a: a.reshape(-1, 1)
    lanes = lambda a: jnp.broadcast_to(col(a), (a.size, b_s))
    ps = {
        'w_rows': jnp.concatenate([w_bf[:, :beta0], w_bf[:, gdn_cols:], w_bf[:, beta0:gdn_cols]], axis=1).T,
        'gdn_conv_w_t': gdn_conv_w[l].T, 'alog_b': lanes(gdn_a_log[l]), 'dtb_b': lanes(gdn_dt_bias[l]),
        'gdn_nw_b': lanes(gdn_norm_w[l]), 'rwkv_mu_b': lanes(rwkv_mu[l]),
        'rwkv_vecs_t': jnp.concatenate([col(a) for a in (rwkv_w0[l], rwkv_a0[l], rwkv_k_k[l], rwkv_k_a[l],
                                                         rwkv_r_k[l], rwkv_ln_w[l], rwkv_ln_b[l], rwkv_ln_b[l])],
                                       axis=1),
        'w_up_t': rwkv_w_up[l].T, 'a_up_t': rwkv_a_up[l].T, 'g_up_t': rwkv_g_up[l].T,
    }

    mod = _ada(jnp.concatenate([c_prompt, c_sample], axis=0), ada_w[l], ada_b[l])
    mod_p = tuple(m.reshape(b_p, 1, d) for m in jnp.split(mod[:b_p], 6, axis=-1))
    mod_s = tuple(m.reshape(1, b_s, d) for m in jnp.split(mod[b_p:], 6, axis=-1))

    zc = jnp.zeros((b_p, CONV_W - 1, GDN_QKV), F32)
    zs = jnp.zeros((b_p, N_HEADS, HEAD_DIM, HEAD_DIM), F32)
    zsh = jnp.zeros((b_p, 1, RWKV_COLS), F32)
    y_p, conv_p, sg_p, shift_p, sr_p = _layer(x_prompt, mod_p, zc, zs, zsh, zs, p, t_valid=None,
                                              bb=1, tt=512, moe_tile=(1, 1024), bsz=4, rows_per_group=4,
                                              cd=CHUNK, chunk=CHUNK)

    y_s, conv_s, sg_s, shift_s, sr_s = _sample_layer(x_sample, mod_s, state_gdn_conv[l], state_gdn[l],
                                                     state_rwkv_shift[l], state_rwkv[l], p, ps)
    return (y_p, y_s, conv_p[None], sg_p[None], shift_p[None], sr_p[None],
            conv_s[None], sg_s[None], shift_s[None], sr_s[None])
```

```python
import jax
import jax.numpy as jnp
from jax import lax
from jax.experimental import pallas as pl
from jax.experimental.pallas import tpu as pltpu

F32 = jnp.float32
BF16 = jnp.bfloat16
HI = lax.Precision.HIGHEST

D_MODEL = 1024
HEAD_DIM = 64
N_HEADS = 8
D_MIX_HALF = N_HEADS * HEAD_DIM
CONV_W = 4
LORA_W = 64
LORA_A = 64
LORA_G = 128
N_GROUPS = 4
EXP_PER_GROUP = 8
N_EXPERTS = N_GROUPS * EXP_PER_GROUP
D_EXPERT = 256
EPS = 1e-6
GN_EPS = HEAD_DIM * 1e-5

LANES = 128
SUBLANES = 8
BF16_ROWS = 16
GDN_QKV = 3 * D_MIX_HALF
RWKV_RKV = 3 * D_MIX_HALF
RWKV_COLS = RWKV_RKV + LORA_W + LORA_A + LORA_G
LORA_COLS = LORA_W + LORA_A + LORA_G

COL_RKV = GDN_QKV
COL_Z = COL_RKV + RWKV_RKV
COL_LORA = COL_Z + D_MIX_HALF
COL_BA = COL_LORA + LORA_COLS
N_COLS = COL_BA + LANES
CHUNK = 64

VMEM_LIMIT = 48 * 1024 * 1024
MOE_VMEM_LIMIT = 58 * 1024 * 1024


def _dot(a, b, prec=HI):
    return jnp.dot(a, b, preferred_element_type=F32, precision=prec)


def _bf(a):
    return a.astype(BF16)


def _mm(a, b):
    return jnp.dot(_bf(a), _bf(b), preferred_element_type=F32)


def _mm_nt(a, b):
    return lax.dot_general(_bf(a), _bf(b), (((1,), (1,)), ((), ())), preferred_element_type=F32)


def _mm_tn(a, b):
    return lax.dot_general(_bf(a), _bf(b), (((0,), (0,)), ((), ())), preferred_element_type=F32)


def _split2(a):
    hi = a.astype(BF16)
    return hi, (a - hi.astype(F32)).astype(BF16)


def _sigmoid(x):
    return 1.0 / (1.0 + jnp.exp(-x))


def _silu(x):
    return x * _sigmoid(x)


def _softplus(x):
    return jnp.maximum(x, 0.0) + jnp.log1p(jnp.exp(-jnp.abs(x)))


def _iota2(shape, dim):
    return lax.broadcasted_iota(jnp.int32, shape, dim)


def _head_block_ones():
    r = _iota2((LANES, LANES), 0)
    c = _iota2((LANES, LANES), 1)
    sh = HEAD_DIM.bit_length() - 1
    return jnp.where((r >> sh) == (c >> sh), 1.0, 0.0).astype(F32)


def _head_sums(x):
    ones = _bf(_head_block_ones())
    xb = _bf(x)
    parts = [jnp.dot(xb[:, p * LANES:(p + 1) * LANES], ones, preferred_element_type=F32)
             for p in range(x.shape[1] // LANES)]
    return jnp.concatenate(parts, axis=1)


def _expand_heads(x, base, terms):
    lane = _iota2((LANES, D_MIX_HALF), 0)
    head = _iota2((LANES, D_MIX_HALF), 1) >> (HEAD_DIM.bit_length() - 1)
    select = jnp.where(lane == base + head, 1.0, 0.0).astype(BF16)
    out = None
    rem = x
    for _ in range(terms):
        piece = rem.astype(BF16)
        part = jnp.dot(piece, select, preferred_element_type=F32)
        out = part if out is None else out + part
        rem = rem - piece.astype(F32)
    return out


def _cumsum_rows(x):
    n = x.shape[0]
    r = _iota2((n, n), 0)
    c = _iota2((n, n), 1)
    tri = jnp.where(r >= c, 1.0, 0.0).astype(BF16)
    x1 = x.astype(BF16)
    rem = x - x1.astype(F32)
    x2 = rem.astype(BF16)
    x3 = (rem - x2.astype(F32)).astype(BF16)
    return (jnp.dot(tri, x1, preferred_element_type=F32) + jnp.dot(tri, x2, preferred_element_type=F32)
            + jnp.dot(tri, x3, preferred_element_type=F32))


def _unit_lower_inverses(lows, n, mm):
    r = _iota2((n, n), 0)
    c = _iota2((n, n), 1)
    eye = jnp.where(r == c, 1.0, 0.0)
    pair = (r >> 1) == (c >> 1)
    invs = [eye - jnp.where(pair, low, 0.0) for low in lows]
    s = 2
    while s < n:
        sh = s.bit_length()
        sel = ((r >> sh) == (c >> sh)) & ((r & (2 * s - 1)) >= s) & ((c & (2 * s - 1)) < s)
        prods = [mm(jnp.where(sel, low, 0.0), inv) for low, inv in zip(lows, invs)]
        invs = [inv - mm(inv, prod) for inv, prod in zip(invs, prods)]
        s *= 2
    return invs


def _ada_kernel(c_ref, w_ref, b_ref, o_ref):
    o_ref[...] = _dot(_silu(c_ref[...]), w_ref[...]) + b_ref[...]


def _ada(c_all, ada_w, ada_b):
    n, d = c_all.shape
    nout = ada_w.shape[1]
    tn = 1536
    return pl.pallas_call(
        _ada_kernel,
        grid=(nout // tn,),
        in_specs=[pl.BlockSpec((n, d), lambda j: (0, 0)),
                  pl.BlockSpec((d, tn), lambda j: (0, j)),
                  pl.BlockSpec((1, tn), lambda j: (0, j))],
        out_specs=pl.BlockSpec((n, tn), lambda j: (0, j)),
        out_shape=jax.ShapeDtypeStruct((n, nout), F32),
        compiler_params=pltpu.CompilerParams(dimension_semantics=("arbitrary",), vmem_limit_bytes=VMEM_LIMIT),
        name="ada",
    )(c_all, ada_w, ada_b.reshape(1, nout))


def _modulated_norm(x, nw, sc, sh):
    ms = jnp.mean(x * x, axis=-1, keepdims=True)
    return (x * lax.rsqrt(ms + EPS) * nw) * (1.0 + sc) + sh


def _inproj_kernel(x_ref, sc_ref, sh_ref, nw_ref, w_ref, o_ref):
    bb, tt, d = x_ref.shape
    h = _modulated_norm(x_ref[...], nw_ref[...], sc_ref[...], sh_ref[...])
    h = h.reshape(bb * tt, d).astype(BF16)
    o = jnp.dot(h, w_ref[...], preferred_element_type=F32)
    o_ref[...] = o.reshape(bb, tt, o.shape[-1])


def _inproj(x, sc, sh, nw, w_cols, bb, tt):
    b, t, d = x.shape
    nc = w_cols.shape[1]
    return pl.pallas_call(
        _inproj_kernel,
        grid=(b // bb, t // tt),
        in_specs=[pl.BlockSpec((bb, tt, d), lambda i, j: (i, j, 0)),
                  pl.BlockSpec((bb, 1, d), lambda i, j: (i, 0, 0)),
                  pl.BlockSpec((bb, 1, d), lambda i, j: (i, 0, 0)),
                  pl.BlockSpec((1, 1, d), lambda i, j: (0, 0, 0)),
                  pl.BlockSpec((d, nc), lambda i, j: (0, 0))],
        out_specs=pl.BlockSpec((bb, tt, nc), lambda i, j: (i, j, 0)),
        out_shape=jax.ShapeDtypeStruct((b, t, nc), F32),
        compiler_params=pltpu.CompilerParams(dimension_semantics=("arbitrary",) * 2, vmem_limit_bytes=VMEM_LIMIT),
        name="inproj",
    )(x, sc, sh, nw, w_cols)


def _gdn_kernel(qkv_ref, z_ref, ba_ref, c0_ref, cw_ref, alog_ref, dtb_ref, nw_ref, s0_ref, o_ref, s_ref, hist_scr):
    bsz, chunk, _ = qkv_ref.shape
    hist_lo = SUBLANES - (CONV_W - 1)

    @pl.when(pl.program_id(1) == 0)
    def _():
        s_ref[...] = s0_ref[...]
        hist_scr[:, hist_lo:SUBLANES, :] = c0_ref[...]

    row = _iota2((chunk, chunk), 0)
    col = _iota2((chunk, chunk), 1)
    causal = row >= col
    strict = row > col

    def prep(bi):
        u = qkv_ref[bi]
        hist_scr[bi, SUBLANES:SUBLANES + chunk, :] = u
        y = u * cw_ref[CONV_W - 1:CONV_W, :]
        for i in range(CONV_W - 1):
            y = y + hist_scr[bi, hist_lo + i:hist_lo + i + chunk, :] * cw_ref[i:i + 1, :]
        hist_scr[bi, hist_lo:SUBLANES, :] = hist_scr[bi, hist_lo + chunk:SUBLANES + chunk, :]
        qkv = _silu(y)
        q, k, v = (qkv[:, i * D_MIX_HALF:(i + 1) * D_MIX_HALF] for i in range(3))
        q = q * lax.rsqrt(_head_sums(q * q) + EPS) * (HEAD_DIM ** -0.5)
        k = k * lax.rsqrt(_head_sums(k * k) + EPS)
        ba = ba_ref[bi]
        beta = _expand_heads(_sigmoid(ba), 0, 2)
        g = -jnp.exp(alog_ref[...]) * _softplus(ba + dtb_ref[...])
        gcum = _expand_heads(_cumsum_rows(g), N_HEADS, 3)
        eg = jnp.exp(gcum)
        g_last = gcum[chunk - 1:chunk, :]
        return dict(q=q, k=k, beta=beta, gcum=gcum, rhs_v=beta * v, rhs_k=beta * eg * k, q_dec=eg * q,
                    k_dec=jnp.exp(g_last - gcum) * k, eg_last=jnp.exp(g_last))

    pre = [prep(bi) for bi in range(bsz)]
    chains = [(bi, h) for bi in range(bsz) for h in range(N_HEADS)]
    n = range(len(chains))

    def head(name, i, width=HEAD_DIM):
        bi, h = chains[i]
        return pre[bi][name][:, h * HEAD_DIM:h * HEAD_DIM + width]

    g_is = [head('gcum', i, chunk) for i in n]
    decays = [jnp.where(causal, jnp.exp(jnp.minimum(g_i - g_i.T, 0.0)), 0.0) for g_i in g_is]
    qk_kks = [_mm_nt(jnp.concatenate([head('k', i), head('q', i)], axis=0), head('k', i)) for i in n]
    lows = [jnp.where(strict, head('beta', i, chunk) * decays[i] * qk_kks[i][:chunk], 0.0) for i in n]
    t_invs = _unit_lower_inverses(lows, chunk, _mm)
    sols = [_mm(t_invs[i], jnp.concatenate([head('rhs_v', i), head('rhs_k', i)], axis=1)) for i in n]
    states = [s_ref[bi, h] for bi, h in chains]
    wss = [_mm(jnp.concatenate([sols[i][:, HEAD_DIM:], head('q_dec', i)], axis=0), states[i])
           for i in n]
    u_news = [sols[i][:, :HEAD_DIM] - wss[i][:chunk] for i in n]
    outs = [wss[i][chunk:] + _mm(qk_kks[i][chunk:] * decays[i], u_news[i]) for i in n]
    s_news = [head('eg_last', i) * states[i] + _mm_tn(head('k_dec', i), u_news[i]) for i in n]
    for i, (bi, h) in enumerate(chains):
        s_ref[bi, h] = s_news[i]
    for bi in range(bsz):
        o = jnp.concatenate(outs[bi * N_HEADS:(bi + 1) * N_HEADS], axis=1)
        o = o * lax.rsqrt(_head_sums(o * o) * (1.0 / HEAD_DIM) + EPS) * nw_ref[...]
        o_ref[bi] = o * _silu(z_ref[bi])


def _gdn(cols, conv0, s0, conv_w, alog_c, dtb_c, nw_rep, bsz, chunk):
    b, t, _ = cols.shape

    def col_spec(width, start):
        return pl.BlockSpec((bsz, chunk, width), lambda i, c: (i, c, start // width))

    state_spec = pl.BlockSpec((bsz, N_HEADS, HEAD_DIM, HEAD_DIM), lambda i, c: (i, 0, 0, 0))
    return pl.pallas_call(
        _gdn_kernel,
        grid=(b // bsz, t // chunk),
        in_specs=[col_spec(GDN_QKV, 0), col_spec(D_MIX_HALF, COL_Z), col_spec(LANES, COL_BA),
                  pl.BlockSpec((bsz, CONV_W - 1, GDN_QKV), lambda i, c: (i, 0, 0)),
                  pl.BlockSpec((CONV_W, GDN_QKV), lambda i, c: (0, 0)),
                  pl.BlockSpec((1, LANES), lambda i, c: (0, 0)), pl.BlockSpec((1, LANES), lambda i, c: (0, 0)),
                  pl.BlockSpec((1, D_MIX_HALF), lambda i, c: (0, 0)), state_spec],
        out_specs=[pl.BlockSpec((bsz, chunk, D_MIX_HALF), lambda i, c: (i, c, 0)), state_spec],
        out_shape=[jax.ShapeDtypeStruct((b, t, D_MIX_HALF), F32),
                   jax.ShapeDtypeStruct((b, N_HEADS, HEAD_DIM, HEAD_DIM), F32)],
        scratch_shapes=[pltpu.VMEM((bsz, SUBLANES + chunk, GDN_QKV), F32)],
        compiler_params=pltpu.CompilerParams(dimension_semantics=("arbitrary",) * 2, vmem_limit_bytes=VMEM_LIMIT),
        name="gdn",
    )(cols, cols, cols, conv0, conv_w, alog_c, dtb_c, nw_rep, s0)


def _rwkv_kernel(rkv_ref, l_ref, p_ref, pl_ref, mu_ref, mul_ref, w0_ref, a0_ref, kk_ref, ka_ref, rk_ref,
                 lnw_ref, lnb_ref, wup_ref, aup_ref, gup_ref, s0_ref, o_ref, s_ref, hist_scr, histl_scr):
    bsz, chunk, _ = rkv_ref.shape
    prev_row = SUBLANES - 1

    @pl.when(pl.program_id(1) == 0)
    def _():
        s_ref[...] = s0_ref[...]
        hist_scr[:, prev_row:SUBLANES, :] = p_ref[...]
        histl_scr[:, prev_row:SUBLANES, :] = pl_ref[...]

    row = _iota2((chunk, chunk), 0)
    col = _iota2((chunk, chunk), 1)
    causal = row >= col
    strict = row > col

    def shifted(bi, u_ref, scr, m_ref):
        u = u_ref[bi]
        scr[bi, SUBLANES:SUBLANES + chunk, :] = u
        prev = scr[bi, prev_row:prev_row + chunk, :]
        scr[bi, prev_row:SUBLANES, :] = scr[bi, prev_row + chunk:SUBLANES + chunk, :]
        return u + (prev - u) * m_ref[...]

    def prep(bi):
        rkv = shifted(bi, rkv_ref, hist_scr, mu_ref)
        r, k, v = (rkv[:, i * D_MIX_HALF:(i + 1) * D_MIX_HALF] for i in range(3))
        xl = shifted(bi, l_ref, histl_scr, mul_ref)
        dw = xl[:, :LORA_W]
        da = xl[:, LORA_W:LORA_W + LORA_A]
        dg = xl[:, LORA_W + LORA_A:]
        w_log = -_softplus(-(w0_ref[...] + _mm(jnp.tanh(dw), wup_ref[...]))) - 0.5
        lw = -jnp.exp(w_log)
        a = _sigmoid(a0_ref[...] + _mm(da, aup_ref[...]))
        gate = _mm(_sigmoid(dg), gup_ref[...])
        kk = k * kk_ref[...]
        kk = kk * lax.rsqrt(_head_sums(kk * kk) + EPS)
        k = k * (1.0 + (a - 1.0) * ka_ref[...])
        cum = _cumsum_rows(lw)
        e_out = jnp.exp(-cum)
        cum_last = cum[chunk - 1:chunk, :]
        e_rest = jnp.exp(cum_last - cum)
        return dict(r=r, k=k, v=v, gate=gate, a_t=-kk * jnp.exp(cum - lw), b_t=kk * a * e_out, k_t=k * e_out,
                    r_t=r * jnp.exp(cum), b_c=kk * a * e_rest, k_c=k * e_rest, e_last=jnp.exp(cum_last))

    pre = [prep(bi) for bi in range(bsz)]
    chains = [(bi, h) for bi in range(bsz) for h in range(N_HEADS)]
    n = range(len(chains))

    def head(name, i):
        bi, h = chains[i]
        return pre[bi][name][:, h * HEAD_DIM:(h + 1) * HEAD_DIM]

    ars = [jnp.concatenate([head('a_t', i), head('r_t', i)], axis=0) for i in n]
    bks = [jnp.concatenate([head('b_t', i), head('k_t', i)], axis=0) for i in n]
    ms = [_mm_nt(ars[i], bks[i]) for i in n]
    t_invs = _unit_lower_inverses([jnp.where(strict, -m[:chunk, :chunk], 0.0) for m in ms], chunk, _mm)
    states = [s_ref[bi, h] for bi, h in chains]
    arss = [_mm_nt(ars[i], states[i]) for i in n]
    vhs = [head('v', i) for i in n]
    akvs = [_mm(jnp.where(strict, ms[i][:chunk, chunk:], 0.0), vhs[i]) for i in n]
    us = [_mm(t_invs[i], arss[i][:chunk] + akvs[i]) for i in n]
    uvs = [jnp.concatenate([us[i], vhs[i]], axis=0) for i in n]
    causal2 = _iota2((chunk, 2 * chunk), 0) >= (_iota2((chunk, 2 * chunk), 1) & (chunk - 1))
    ys = [arss[i][chunk:] + _mm(jnp.where(causal2, ms[i][chunk:], 0.0), uvs[i]) for i in n]
    s_news = [states[i] * head('e_last', i)
              + _mm_tn(uvs[i], jnp.concatenate([head('b_c', i), head('k_c', i)], axis=0)) for i in n]
    for i, (bi, h) in enumerate(chains):
        s_ref[bi, h] = s_news[i]
    for bi in range(bsz):
        p = pre[bi]
        y = jnp.concatenate(ys[bi * N_HEADS:(bi + 1) * N_HEADS], axis=1)
        mean = _head_sums(y) * (1.0 / HEAD_DIM)
        dy = y - mean
        var = _head_sums(dy * dy) * (1.0 / HEAD_DIM)
        y = dy * lax.rsqrt(var + GN_EPS) * lnw_ref[...] + lnb_ref[...]
        bonus = _head_sums(p['r'] * p['k'] * rk_ref[...]) * p['v']
        o_ref[bi] = (y + bonus) * p['gate']


def _rwkv(cols, shift0, s0, mu, vecs, w_up, a_up, g_up, bsz, chunk):
    b, t, _ = cols.shape

    def full2(shape):
        return pl.BlockSpec(shape, lambda i, c: (0, 0))

    state_spec = pl.BlockSpec((bsz, N_HEADS, HEAD_DIM, HEAD_DIM), lambda i, c: (i, 0, 0, 0))
    return pl.pallas_call(
        _rwkv_kernel,
        grid=(b // bsz, t // chunk),
        in_specs=[pl.BlockSpec((bsz, chunk, RWKV_RKV), lambda i, c: (i, c, COL_RKV // RWKV_RKV)),
                  pl.BlockSpec((bsz, chunk, LORA_COLS), lambda i, c: (i, c, COL_LORA // LORA_COLS)),
                  pl.BlockSpec((bsz, 1, RWKV_RKV), lambda i, c: (i, 0, 0)),
                  pl.BlockSpec((bsz, 1, LORA_COLS), lambda i, c: (i, 0, RWKV_RKV // LORA_COLS)),
                  pl.BlockSpec((1, RWKV_RKV), lambda i, c: (0, 0)),
                  pl.BlockSpec((1, LORA_COLS), lambda i, c: (0, RWKV_RKV // LORA_COLS))]
                 + [full2((1, D_MIX_HALF))] * 7
                 + [full2((LORA_W, D_MIX_HALF)), full2((LORA_A, D_MIX_HALF)), full2((LORA_G, D_MIX_HALF)),
                    state_spec],
        out_specs=[pl.BlockSpec((bsz, chunk, D_MIX_HALF), lambda i, c: (i, c, 0)), state_spec],
        out_shape=[jax.ShapeDtypeStruct((b, t, D_MIX_HALF), F32),
                   jax.ShapeDtypeStruct((b, N_HEADS, HEAD_DIM, HEAD_DIM), F32)],
        scratch_shapes=[pltpu.VMEM((bsz, SUBLANES + chunk, RWKV_RKV), F32),
                        pltpu.VMEM((bsz, SUBLANES + chunk, LORA_COLS), F32)],
        compiler_params=pltpu.CompilerParams(dimension_semantics=("arbitrary",) * 2, vmem_limit_bytes=VMEM_LIMIT),
        name="rwkv",
    )(cols, cols, shift0, shift0, mu, mu, *vecs, w_up, a_up, g_up, s0)


def _mod_spec(m, bb, tt):
    d = m.shape[-1]
    if m.shape[1] == 1:
        return pl.BlockSpec((bb, 1, d), lambda i, j, *_: (i, 0, 0))
    return pl.BlockSpec((1, tt, d), lambda i, j, *_: (0, j, 0))


def _outproj_kernel(oa_ref, ob_ref, x_ref, gt_ref, sc_ref, sh_ref, nw_ref, wa_ref, wb_ref, wr_ref, br_ref,
                    x1_ref, h_ref, wt_ref):
    bb, tt, d = x_ref.shape
    n = bb * tt
    if len(oa_ref.shape) == 2:
        mixed = _mm_tn(oa_ref[...], wa_ref[...]) + _mm_tn(ob_ref[...], wb_ref[...])
    else:
        oa = oa_ref[...].reshape(n, D_MIX_HALF).astype(BF16)
        ob = ob_ref[...].reshape(n, D_MIX_HALF).astype(BF16)
        mixed = (jnp.dot(oa, wa_ref[...], preferred_element_type=F32)
                 + jnp.dot(ob, wb_ref[...], preferred_element_type=F32))
    x1 = x_ref[...] + gt_ref[...] * mixed.reshape(bb, tt, d)
    x1_ref[...] = x1
    h = _modulated_norm(x1, nw_ref[...], sc_ref[...], sh_ref[...]).reshape(n, d)
    h_ref[...] = h.reshape(bb, tt, d).astype(BF16)

    h_hi, h_lo = _split2(h)
    hw = jnp.dot(h_hi, wr_ref[...], preferred_element_type=F32)
    logits = (hw[:, :LANES] + hw[:, LANES:] + jnp.dot(h_lo, wr_ref[:, :LANES], preferred_element_type=F32)
              + br_ref[...])
    lane_i = _iota2((n, LANES), 1)
    lane = lane_i.astype(F32)
    lane_grp = (lane_i >> (EXP_PER_GROUP.bit_length() - 1)).astype(F32)
    neg = jnp.float32(-jnp.inf)
    big = jnp.float32(LANES)
    is_grp = (lane_i >= N_EXPERTS) & (lane_i < N_EXPERTS + N_GROUPS)
    gl = jnp.where(is_grp, logits, neg)
    gmax = jnp.max(gl, axis=-1, keepdims=True)
    grp = jnp.min(jnp.where(gl == gmax, lane, big), axis=-1, keepdims=True) - N_EXPERTS
    g_prob = 1.0 / jnp.sum(jnp.where(is_grp, jnp.exp(gl - gmax), 0.0), axis=-1, keepdims=True)
    in_grp = (lane_i < N_EXPERTS) & (lane_grp == grp)
    el = jnp.where(in_grp, logits, neg)
    m1 = jnp.max(el, axis=-1, keepdims=True)
    i1 = jnp.min(jnp.where(el == m1, lane, big), axis=-1, keepdims=True)
    el2 = jnp.where(lane == i1, neg, el)
    m2 = jnp.max(el2, axis=-1, keepdims=True)
    i2 = jnp.min(jnp.where(el2 == m2, lane, big), axis=-1, keepdims=True)
    e2 = jnp.exp(m2 - m1)
    p1 = g_prob / (1.0 + e2)
    p2 = g_prob * e2 / (1.0 + e2)
    wt = jnp.where(lane == i1, p1, 0.0) + jnp.where(lane == i2, p2, 0.0)
    wt_ref[...] = wt.reshape(bb, tt, LANES)


def _outproj(oa, ob, x, gt, sc, sh, nw, wa, wb, wr, br, bb, tt):
    b, t, d = x.shape
    grid = (b // bb, t // tt)

    def tok_spec(w):
        return pl.BlockSpec((bb, tt, w), lambda i, j: (i, j, 0))

    def full2(shape):
        return pl.BlockSpec(shape, lambda i, j: (0, 0))

    mod_spec = _mod_spec(gt, bb, tt)
    if oa.ndim == 2:
        assert grid == (1, 1)
        mix_spec = full2(oa.shape)
    else:
        mix_spec = tok_spec(D_MIX_HALF)
    return pl.pallas_call(
        _outproj_kernel,
        grid=grid,
        in_specs=[mix_spec, mix_spec, tok_spec(d), mod_spec, mod_spec, mod_spec,
                  pl.BlockSpec((1, 1, d), lambda i, j: (0, 0, 0)),
                  full2((D_MIX_HALF, d)), full2((D_MIX_HALF, d)), full2((d, 2 * LANES)), full2((1, LANES))],
        out_specs=[tok_spec(d), tok_spec(d), tok_spec(LANES)],
        out_shape=[jax.ShapeDtypeStruct((b, t, d), F32), jax.ShapeDtypeStruct((b, t, d), BF16),
                   jax.ShapeDtypeStruct((b, t, LANES), F32)],
        compiler_params=pltpu.CompilerParams(dimension_semantics=("arbitrary",) * 2, vmem_limit_bytes=VMEM_LIMIT),
        name="outproj",
    )(oa, ob, x, gt, sc, sh, nw, wa, wb, wr, br)


MOE_ROWS = 128
MOE_EXPERTS_PER_STEP = 4


def _moe_kernel(h_ref, wt_ref, x1_ref, gt_ref, fw_ref, wgu_ref, wd_ref, y_ref,
                xg_scr, gtw_scr, yw_scr, seg_scr):
    bb, tt, d = x1_ref.shape
    n = bb * tt
    npos = 2 * n
    e = pl.program_id(2)

    @pl.when(e == 0)
    def _():
        wt_t = wt_ref[...].reshape(n, LANES).T
        member = wt_t > 0.0
        ones = jnp.where(member, 1.0, 0.0)
        upper = jnp.where(_iota2((n, n), 0) < _iota2((n, n), 1), 1.0, 0.0).astype(BF16)
        rank = jnp.dot(_bf(ones), upper, preferred_element_type=F32)
        cnt = jnp.broadcast_to(jnp.sum(ones, axis=1, keepdims=True), (LANES, LANES))
        lower = jnp.where(_iota2((LANES, LANES), 0) > _iota2((LANES, LANES), 1), 1.0, 0.0)
        off = _dot(lower, cnt)
        first_i = off.astype(jnp.int32)
        last_i = (off + cnt).astype(jnp.int32)
        for x in range(N_EXPERTS):
            seg_scr[x] = first_i[x, 0]
            seg_scr[N_EXPERTS + x] = last_i[x, 0]
        pos = off[:, :1] + rank
        pos1 = jnp.min(jnp.where(member, pos, jnp.float32(4 * n)), axis=0, keepdims=True)
        pos2 = jnp.max(jnp.where(member, pos, -1.0), axis=0, keepdims=True)
        w1 = jnp.sum(jnp.where(member & (pos == pos1), wt_t, 0.0), axis=0, keepdims=True)
        w2 = jnp.sum(jnp.where(member & (pos == pos2), wt_t, 0.0), axis=0, keepdims=True)
        w2 = jnp.where(pos2 != pos1, w2, 0.0)

        def onehot_block(b, carry):
            p = (b * MOE_ROWS + _iota2((MOE_ROWS, n), 0)).astype(F32)
            rows = pl.ds(pl.multiple_of(b * MOE_ROWS, MOE_ROWS), MOE_ROWS)
            yw_scr[rows, :n] = jnp.where((pos1 == p) | (pos2 == p), 1.0, 0.0).astype(BF16)
            return carry

        lax.fori_loop(0, npos // MOE_ROWS, onehot_block, 0)
        xg_scr[:npos, :] = jnp.dot(yw_scr[:npos, :n], h_ref[...].reshape(n, d),
                                   preferred_element_type=F32).astype(BF16)
        xg_scr[npos:, :] = jnp.zeros((MOE_ROWS, d), BF16)
        z_t = jnp.concatenate([pos1, pos2, w1, w2, jnp.zeros((LANES - 4, n), F32)], axis=0).T
        p1c, p2c, w1c, w2c = (z_t[:, i:i + 1] for i in range(4))
        for b in range(npos // LANES):
            p = (b * LANES + _iota2((n, LANES), 1)).astype(F32)
            gtw_scr[:, b * LANES:(b + 1) * LANES] = (jnp.where(p1c == p, w1c, 0.0)
                                                     + jnp.where(p2c == p, w2c, 0.0)).astype(BF16)
        yw_scr[...] = jnp.zeros_like(yw_scr)

    es = range(wgu_ref.shape[0])
    firsts = [seg_scr[e * len(es) + j] for j in es]
    lasts = [seg_scr[N_EXPERTS + e * len(es) + j] for j in es]
    starts = [(f // BF16_ROWS) * BF16_ROWS for f in firsts]
    windows = [(lasts[j] - starts[j] + MOE_ROWS - 1) // MOE_ROWS for j in es]

    def window_rows(r0):
        return pl.ds(pl.multiple_of(r0, BF16_ROWS), MOE_ROWS)

    def store_rows(j, r0, y):
        p = r0 + _iota2((MOE_ROWS, 1), 0)
        rows = window_rows(r0)
        yw_scr[rows, :] = jnp.where((p >= firsts[j]) & (p < lasts[j]), y, yw_scr[rows, :])

    gus = [jnp.dot(xg_scr[window_rows(starts[j]), :], wgu_ref[j], preferred_element_type=F32) for j in es]
    acts = [(_silu(gu[:, :D_EXPERT]) * gu[:, D_EXPERT:]).astype(BF16) for gu in gus]
    ys = [jnp.dot(acts[j], wd_ref[j], preferred_element_type=F32).astype(BF16) for j in es]
    for j in es:
        store_rows(j, starts[j], ys[j])

    most = windows[0]
    for j in es[1:]:
        most = jnp.maximum(most, windows[j])

    @pl.when(most > 1)
    def _():
        for j in es:
            def more_windows(w, carry, j=j):
                r0 = starts[j] + w * MOE_ROWS
                gu = jnp.dot(xg_scr[window_rows(r0), :], wgu_ref[j], preferred_element_type=F32)
                act = (_silu(gu[:, :D_EXPERT]) * gu[:, D_EXPERT:]).astype(BF16)
                store_rows(j, r0, jnp.dot(act, wd_ref[j], preferred_element_type=F32).astype(BF16))
                return carry

            lax.fori_loop(1, windows[j], more_windows, 0)

    @pl.when(e == pl.num_programs(2) - 1)
    def _():
        moe = jnp.dot(gtw_scr[...], yw_scr[:npos, :], preferred_element_type=F32)
        x2 = x1_ref[...] + gt_ref[...] * moe.reshape(bb, tt, d)
        ms = jnp.mean(x2 * x2, axis=-1, keepdims=True)
        y_ref[...] = x2 * lax.rsqrt(ms + EPS) * fw_ref[...]


def _moe(h, wt, x1, gt, fw, w_gu, w_down, bb, tt):
    b, t, d = x1.shape
    es = MOE_EXPERTS_PER_STEP
    grid = (b // bb, t // tt, N_EXPERTS // es)
    npos = 2 * bb * tt

    def tok_spec(w):
        return pl.BlockSpec((bb, tt, w), lambda i, j, e: (i, j, 0))

    return pl.pallas_call(
        _moe_kernel,
        grid=grid,
        in_specs=[tok_spec(d), tok_spec(LANES), tok_spec(d),
                  _mod_spec(gt, bb, tt),
                  pl.BlockSpec((1, 1, d), lambda i, j, e: (0, 0, 0)),
                  pl.BlockSpec((es, d, 2 * D_EXPERT), lambda i, j, e: (e, 0, 0)),
                  pl.BlockSpec((es, D_EXPERT, d), lambda i, j, e: (e, 0, 0))],
        out_specs=tok_spec(d),
        out_shape=jax.ShapeDtypeStruct((b, t, d), F32),
        scratch_shapes=[pltpu.VMEM((npos + MOE_ROWS, d), BF16), pltpu.VMEM((bb * tt, npos), BF16),
                        pltpu.VMEM((npos + MOE_ROWS, d), BF16), pltpu.SMEM((2 * N_EXPERTS,), jnp.int32)],
        compiler_params=pltpu.CompilerParams(dimension_semantics=("arbitrary",) * 3,
                                             vmem_limit_bytes=MOE_VMEM_LIMIT),
        name="moe",
    )(h, wt, x1, gt, fw, w_gu, w_down)


ROW_Z = GDN_QKV
ROW_R = ROW_Z + D_MIX_HALF
ROW_LORA = ROW_R + RWKV_RKV
ROW_BETA = ROW_LORA + LORA_COLS
ROW_A = ROW_BETA + N_HEADS
N_ROWS = ROW_A + N_HEADS


def _inproj_t_kernel(x_ref, sc_ref, sh_ref, nw_ref, w_ref, o_ref):
    t, b, d = x_ref.shape
    h = _modulated_norm(x_ref[...], nw_ref[...], sc_ref[...], sh_ref[...]).reshape(t * b, d)
    o_ref[...] = _mm_nt(w_ref[...], h)


def _inproj_t(x, sc, sh, nw, w_rows):
    t, b, d = x.shape
    nr = w_rows.shape[0]

    def full(shape):
        return pl.BlockSpec(shape, lambda i: (0,) * len(shape))

    return pl.pallas_call(
        _inproj_t_kernel,
        grid=(1,),
        in_specs=[full((t, b, d)), full((1, b, d)), full((1, b, d)), full((1, 1, d)), full((nr, d))],
        out_specs=full((nr, t * b)),
        out_shape=jax.ShapeDtypeStruct((nr, t * b), F32),
        compiler_params=pltpu.CompilerParams(dimension_semantics=("arbitrary",), vmem_limit_bytes=VMEM_LIMIT),
        name="inproj_t",
    )(x, sc, sh, nw, w_rows)


def _store_state_rows(s_scr, sout_ref):
    for j in range(HEAD_DIM // 2):
        pair = jnp.concatenate([s_scr[2 * j], s_scr[2 * j + 1]], axis=0)
        sout_ref[:, j * LANES:(j + 1) * LANES] = pair.T


def _gdn_s_kernel(q_ref, k_ref, v_ref, z_ref, b_ref, a_ref, cq_ref, ck_ref, cv_ref, wq_ref, wk_ref, wv_ref,
                  alog_ref, dtb_ref, nw_ref, s0_ref, o_ref, sout_ref, s_scr, kq_scr):
    nb = s0_ref.shape[-1]
    nt = q_ref.shape[1] // nb
    h = pl.program_id(0)

    def tok(ref, t):
        return ref[:, t * nb:(t + 1) * nb]

    def conv(u_ref, c_ref, w_ref):
        full = [c_ref[i] for i in range(CONV_W - 1)] + [tok(u_ref, t) for t in range(nt)]
        taps = [w_ref[:, i:i + 1] for i in range(CONV_W)]
        outs = []
        for t in range(nt):
            y = full[t + CONV_W - 1] * taps[CONV_W - 1]
            for i in range(CONV_W - 1):
                y = y + full[t + i] * taps[i]
            outs.append(_silu(y))
        return outs

    qs = conv(q_ref, cq_ref, wq_ref)
    ks = conv(k_ref, ck_ref, wk_ref)
    vs = conv(v_ref, cv_ref, wv_ref)
    neg_rate = -jnp.exp(alog_ref[pl.ds(h, 1), :])
    dtb = dtb_ref[pl.ds(h, 1), :]
    beta_in = b_ref[pl.ds(h, 1), :]
    a_in = a_ref[pl.ds(h, 1), :]
    s_scr[...] = s0_ref[0]
    zero = jnp.zeros((HEAD_DIM, nb), F32)
    for t in range(nt):
        q = qs[t] * lax.rsqrt(jnp.sum(qs[t] * qs[t], axis=0, keepdims=True) + EPS) * (HEAD_DIM ** -0.5)
        k = ks[t] * lax.rsqrt(jnp.sum(ks[t] * ks[t], axis=0, keepdims=True) + EPS)
        beta = _sigmoid(beta_in[:, t * nb:(t + 1) * nb])
        decay = jnp.exp(neg_rate * _softplus(a_in[:, t * nb:(t + 1) * nb] + dtb))
        kq_scr[0] = k
        kq_scr[1] = q

        def decay_and_project(i, acc):
            row = s_scr[i] * decay
            s_scr[i] = row
            return acc + kq_scr[0, pl.ds(i, 1), :] * row

        k_s = lax.fori_loop(0, HEAD_DIM, decay_and_project, zero, unroll=8)
        u = beta * (vs[t] - k_s)

        def update_and_read(i, acc):
            row = s_scr[i] + kq_scr[0, pl.ds(i, 1), :] * u
            s_scr[i] = row
            return acc + kq_scr[1, pl.ds(i, 1), :] * row

        o = lax.fori_loop(0, HEAD_DIM, update_and_read, zero, unroll=8)
        o = o * lax.rsqrt(jnp.mean(o * o, axis=0, keepdims=True) + EPS) * nw_ref[...]
        o_ref[:, t * nb:(t + 1) * nb] = o * _silu(tok(z_ref, t))
    _store_state_rows(s_scr, sout_ref)


def _gdn_s(cols_t, conv_t, s_t, cw_t, alog_b, dtb_b, nw_b):
    ntok = cols_t.shape[1]
    nb = s_t.shape[-1]

    def head_rows(base):
        return pl.BlockSpec((HEAD_DIM, ntok), lambda h: (base // HEAD_DIM + h, 0))

    def conv_rows(base):
        return pl.BlockSpec((CONV_W - 1, HEAD_DIM, nb), lambda h: (0, base // HEAD_DIM + h, 0))

    def tap_rows(base):
        return pl.BlockSpec((HEAD_DIM, CONV_W), lambda h: (base // HEAD_DIM + h, 0))

    def full2(shape):
        return pl.BlockSpec(shape, lambda h: (0, 0))

    return pl.pallas_call(
        _gdn_s_kernel,
        grid=(N_HEADS,),
        in_specs=[head_rows(0), head_rows(D_MIX_HALF), head_rows(2 * D_MIX_HALF), head_rows(ROW_Z),
                  pl.BlockSpec((N_HEADS, ntok), lambda h: (ROW_BETA // N_HEADS, 0)),
                  pl.BlockSpec((N_HEADS, ntok), lambda h: (ROW_A // N_HEADS, 0)),
                  conv_rows(0), conv_rows(D_MIX_HALF), conv_rows(2 * D_MIX_HALF),
                  tap_rows(0), tap_rows(D_MIX_HALF), tap_rows(2 * D_MIX_HALF),
                  full2((N_HEADS, nb)), full2((N_HEADS, nb)), full2((HEAD_DIM, nb)),
                  pl.BlockSpec((1, HEAD_DIM, HEAD_DIM, nb), lambda h: (h, 0, 0, 0))],
        out_specs=[pl.BlockSpec((HEAD_DIM, ntok), lambda h: (h, 0)),
                   pl.BlockSpec((nb, HEAD_DIM * HEAD_DIM), lambda h: (0, h))],
        out_shape=[jax.ShapeDtypeStruct((D_MIX_HALF, ntok), F32),
                   jax.ShapeDtypeStruct((nb, N_HEADS * HEAD_DIM * HEAD_DIM), F32)],
        scratch_shapes=[pltpu.VMEM((HEAD_DIM, HEAD_DIM, nb), F32), pltpu.VMEM((2, HEAD_DIM, nb), F32)],
        compiler_params=pltpu.CompilerParams(dimension_semantics=("arbitrary",), vmem_limit_bytes=VMEM_LIMIT),
        name="gdn_s",
    )(cols_t, cols_t, cols_t, cols_t, cols_t, cols_t, conv_t, conv_t, conv_t, cw_t, cw_t, cw_t,
      alog_b, dtb_b, nw_b, s_t)


def _rwkv_s_kernel(r_ref, k_ref, v_ref, l_ref, pr_ref, pk_ref, pv_ref, pl_ref, mr_ref, mk_ref, mv_ref, ml_ref,
                   vec_ref, wup_ref, aup_ref, gup_ref, s0_ref, o_ref, sout_ref, s_scr, y_scr, v_scr):
    nb = s0_ref.shape[-1]
    nt = r_ref.shape[1] // nb

    def mixed(u_ref, p_ref, m_ref):
        cur = [u_ref[:, t * nb:(t + 1) * nb] for t in range(nt)]
        prev = [p_ref[...]] + cur[:-1]
        mu = m_ref[...]
        return [c + (p - c) * mu for c, p in zip(cur, prev)]

    rs = mixed(r_ref, pr_ref, mr_ref)
    ks = mixed(k_ref, pk_ref, mk_ref)
    vs = mixed(v_ref, pv_ref, mv_ref)
    ls = mixed(l_ref, pl_ref, ml_ref)
    w0, a0, k_k, k_a, r_k, ln_w, ln_b = (vec_ref[:, i:i + 1] for i in range(7))
    s_scr[...] = s0_ref[0]
    for t in range(nt):
        dw = ls[t][:LORA_W]
        da = ls[t][LORA_W:LORA_W + LORA_A]
        dg = ls[t][LORA_W + LORA_A:]
        w_log = -_softplus(-(w0 + _mm(wup_ref[...], jnp.tanh(dw)))) - 0.5
        w = jnp.exp(-jnp.exp(w_log))
        a = _sigmoid(a0 + _mm(aup_ref[...], da))
        gate = _mm(gup_ref[...], _sigmoid(dg))
        r, v = rs[t], vs[t]
        kk = ks[t] * k_k
        kk = kk * lax.rsqrt(jnp.sum(kk * kk, axis=0, keepdims=True) + EPS)
        k = ks[t] * (1.0 + (a - 1.0) * k_a)
        neg_kk = -kk
        kk_a = kk * a
        v_scr[...] = v

        def state_row(i, carry):
            row = s_scr[i]
            sa = jnp.sum(row * neg_kk, axis=0, keepdims=True)
            row = row * w + sa * kk_a + v_scr[pl.ds(i, 1), :] * k
            s_scr[i] = row
            y_scr[pl.ds(i, 1), :] = jnp.sum(row * r, axis=0, keepdims=True)
            return carry

        lax.fori_loop(0, HEAD_DIM, state_row, 0, unroll=4)
        y = y_scr[...]
        dy = y - jnp.mean(y, axis=0, keepdims=True)
        y = dy * lax.rsqrt(jnp.mean(dy * dy, axis=0, keepdims=True) + GN_EPS) * ln_w + ln_b
        bonus = jnp.sum(r * k * r_k, axis=0, keepdims=True) * v
        o_ref[:, t * nb:(t + 1) * nb] = (y + bonus) * gate
    _store_state_rows(s_scr, sout_ref)


def _rwkv_s(cols_t, shift_t, s_t, mu_b, vecs_t, w_up_t, a_up_t, g_up_t):
    ntok = cols_t.shape[1]
    nb = s_t.shape[-1]

    def head_rows(width, base):
        return pl.BlockSpec((HEAD_DIM, width), lambda h: (base // HEAD_DIM + h, 0))

    def lora_rows(width, base):
        return pl.BlockSpec((LORA_COLS, width), lambda h: (base // LORA_COLS, 0))

    return pl.pallas_call(
        _rwkv_s_kernel,
        grid=(N_HEADS,),
        in_specs=[head_rows(ntok, ROW_R), head_rows(ntok, ROW_R + D_MIX_HALF), head_rows(ntok, ROW_R + 2 * D_MIX_HALF),
                  lora_rows(ntok, ROW_LORA),
                  head_rows(nb, 0), head_rows(nb, D_MIX_HALF), head_rows(nb, 2 * D_MIX_HALF), lora_rows(nb, RWKV_RKV),
                  head_rows(nb, 0), head_rows(nb, D_MIX_HALF), head_rows(nb, 2 * D_MIX_HALF), lora_rows(nb, RWKV_RKV),
                  head_rows(SUBLANES, 0), head_rows(LORA_W, 0), head_rows(LORA_A, 0), head_rows(LORA_G, 0),
                  pl.BlockSpec((1, HEAD_DIM, HEAD_DIM, nb), lambda h: (h, 0, 0, 0))],
        out_specs=[pl.BlockSpec((HEAD_DIM, ntok), lambda h: (h, 0)),
                   pl.BlockSpec((nb, HEAD_DIM * HEAD_DIM), lambda h: (0, h))],
        out_shape=[jax.ShapeDtypeStruct((D_MIX_HALF, ntok), F32),
                   jax.ShapeDtypeStruct((nb, N_HEADS * HEAD_DIM * HEAD_DIM), F32)],
        scratch_shapes=[pltpu.VMEM((HEAD_DIM, HEAD_DIM, nb), F32), pltpu.VMEM((HEAD_DIM, nb), F32),
                        pltpu.VMEM((HEAD_DIM, nb), F32)],
        compiler_params=pltpu.CompilerParams(dimension_semantics=("arbitrary",), vmem_limit_bytes=VMEM_LIMIT),
        name="rwkv_s",
    )(cols_t, cols_t, cols_t, cols_t, shift_t, shift_t, shift_t, shift_t, mu_b, mu_b, mu_b, mu_b,
      vecs_t, w_up_t, a_up_t, g_up_t, s_t)


def _layer(x, mod, conv0, s_gdn0, shift0, s_rwkv0, p, *, bb, tt, moe_tile, bsz, chunk):
    sh_m, sc_m, gt_m, sh_f, sc_f, gt_f = mod
    cols = _inproj(x, sc_m, sh_m, p['norm_mix_w'], p['w_cols'], bb, tt)
    o_a, s_gdn = _gdn(cols, conv0, s_gdn0, p['gdn_conv_w'], p['alog_c'], p['dtb_c'], p['gdn_nw_rep'], bsz, chunk)
    o_b, s_rwkv = _rwkv(cols, shift0, s_rwkv0, p['rwkv_mu'], p['rwkv_vecs'], p['rwkv_w_up'], p['rwkv_a_up'],
                        p['rwkv_g_up'], bsz, chunk)
    x1, h2, wt = _outproj(o_a, o_b, x, gt_m, sc_f, sh_f, p['norm_ffn_w'], p['w_out_a'], p['w_out_b'],
                          p['w_router'], p['b_router'], *moe_tile)
    y = _moe(h2, wt, x1, gt_f, p['final_norm_w'], p['w_gu'], p['w_down'], *moe_tile)
    conv_new = cols[:, -(CONV_W - 1):, :GDN_QKV]
    shift_new = jnp.concatenate([cols[:, -1, COL_RKV:COL_Z], cols[:, -1, COL_LORA:COL_BA]], axis=-1)
    return y, conv_new, s_gdn, shift_new, s_rwkv


def _sample_layer(x, mod, conv0, s_gdn0, shift0, s_rwkv0, p, ps):
    b, t, d = x.shape
    assert t >= CONV_W - 1
    sh_m, sc_m, gt_m, sh_f, sc_f, gt_f = mod
    x_tm = jnp.transpose(x, (1, 0, 2))
    cols_t = _inproj_t(x_tm, sc_m, sh_m, p['norm_mix_w'], ps['w_rows'])
    seq_last = (1, 2, 3, 0)
    o_a, s_gdn = _gdn_s(cols_t, jnp.transpose(conv0, (1, 2, 0)), jnp.transpose(s_gdn0, seq_last),
                        ps['gdn_conv_w_t'], ps['alog_b'], ps['dtb_b'], ps['gdn_nw_b'])
    o_b, s_rwkv = _rwkv_s(cols_t, shift0.T, jnp.transpose(s_rwkv0, seq_last), ps['rwkv_mu_b'], ps['rwkv_vecs_t'],
                          ps['w_up_t'], ps['a_up_t'], ps['g_up_t'])
    x1, h2, wt = _outproj(o_a, o_b, x_tm, gt_m, sc_f, sh_f, p['norm_ffn_w'], p['w_out_a'], p['w_out_b'],
                          p['w_router'], p['b_router'], t, b)
    y = _moe(h2, wt, x1, gt_f, p['final_norm_w'], p['w_gu'], p['w_down'], t, b)
    keep = CONV_W - 1
    conv_new = jnp.transpose(cols_t[:GDN_QKV, (t - keep) * b:].reshape(GDN_QKV, keep, b), (2, 1, 0))
    shift_new = cols_t[ROW_R:ROW_BETA, (t - 1) * b:].T
    state_shape = (b, N_HEADS, HEAD_DIM, HEAD_DIM)
    return (jnp.transpose(y, (1, 0, 2)), conv_new, s_gdn.reshape(state_shape), shift_new,
            s_rwkv.reshape(state_shape))


def kernel(x_prompt, x_sample, state_gdn_conv, state_gdn, state_rwkv_shift, state_rwkv, c_prompt, c_sample, ada_w, ada_b, norm_mix_w, w_in, gdn_conv_w, gdn_a_log, gdn_dt_bias, gdn_norm_w, rwkv_mu, rwkv_w0, rwkv_w_up, rwkv_a0, rwkv_a_up, rwkv_g_up, rwkv_k_k, rwkv_k_a, rwkv_r_k, rwkv_ln_w, rwkv_ln_b, w_out, norm_ffn_w, router_group_w, router_group_b, router_expert_w, router_expert_b, expert_w_gate_up, expert_w_down, final_norm_w):
    depth = ada_w.shape[0]
    assert depth == 1
    l = 0
    b_p, t_p, d = x_prompt.shape
    b_s, t_s, _ = x_sample.shape
    gdn_cols = GDN_QKV + D_MIX_HALF + 2 * N_HEADS

    w = w_in[l]
    beta0 = GDN_QKV + D_MIX_HALF
    w_bf = w.astype(BF16)
    rkv0, lora0 = gdn_cols, gdn_cols + RWKV_RKV
    w_cols = jnp.concatenate([w_bf[:, :GDN_QKV], w_bf[:, rkv0:lora0], w_bf[:, GDN_QKV:beta0], w_bf[:, lora0:],
                              w_bf[:, beta0:gdn_cols], jnp.zeros((d, LANES - 2 * N_HEADS), BF16)], axis=1)
    head_lanes = lambda a, first: jnp.pad(a, (first, LANES - first - N_HEADS)).reshape(1, LANES)
    row = lambda a: a.reshape(1, -1)
    n_route = N_EXPERTS + N_GROUPS
    w_router = jnp.concatenate([router_expert_w[l], router_group_w[l], jnp.zeros((d, LANES - n_route), F32)], axis=1)
    w_router = jnp.concatenate(_split2(w_router), axis=1)
    b_router = jnp.concatenate([router_expert_b[l], router_group_b[l], jnp.zeros((LANES - n_route,), F32)])[None]
    p = {
        'norm_mix_w': norm_mix_w[l].reshape(1, 1, d), 'w_cols': w_cols,
        'gdn_conv_w': gdn_conv_w[l], 'alog_c': head_lanes(gdn_a_log[l], N_HEADS),
        'dtb_c': head_lanes(gdn_dt_bias[l], N_HEADS),
        'gdn_nw_rep': row(jnp.tile(gdn_norm_w[l], N_HEADS)),
        'rwkv_mu': row(rwkv_mu[l]),
        'rwkv_vecs': tuple(row(a) for a in (rwkv_w0[l], rwkv_a0[l], rwkv_k_k[l], rwkv_k_a[l], rwkv_r_k[l],
                                            rwkv_ln_w[l], rwkv_ln_b[l])),
        'rwkv_w_up': rwkv_w_up[l], 'rwkv_a_up': rwkv_a_up[l], 'rwkv_g_up': rwkv_g_up[l],
        'w_out_a': w_out[l][:D_MIX_HALF].astype(BF16), 'w_out_b': w_out[l][D_MIX_HALF:].astype(BF16),
        'norm_ffn_w': norm_ffn_w[l].reshape(1, 1, d), 'w_router': w_router, 'b_router': b_router,
        'w_gu': expert_w_gate_up[l].astype(BF16), 'w_down': expert_w_down[l].astype(BF16),
        'final_norm_w': final_norm_w.reshape(1, 1, d),
    }

    col = lambda a: a.reshape(-1, 1)
    lanes = lambda a: jnp.broadcast_to(col(a), (a.size, b_s))
    ps = {
        'w_rows': jnp.concatenate([w_bf[:, :beta0], w_bf[:, gdn_cols:], w_bf[:, beta0:gdn_cols]], axis=1).T,
        'gdn_conv_w_t': gdn_conv_w[l].T, 'alog_b': lanes(gdn_a_log[l]), 'dtb_b': lanes(gdn_dt_bias[l]),
        'gdn_nw_b': lanes(gdn_norm_w[l]), 'rwkv_mu_b': lanes(rwkv_mu[l]),
        'rwkv_vecs_t': jnp.concatenate([col(a) for a in (rwkv_w0[l], rwkv_a0[l], rwkv_k_k[l], rwkv_k_a[l],
                                                         rwkv_r_k[l], rwkv_ln_w[l], rwkv_ln_b[l], rwkv_ln_b[l])],
                                       axis=1),
        'w_up_t': rwkv_w_up[l].T, 'a_up_t': rwkv_a_up[l].T, 'g_up_t': rwkv_g_up[l].T,
    }

    mod = _ada(jnp.concatenate([c_prompt, c_sample], axis=0), ada_w[l], ada_b[l])
    mod_p = tuple(m.reshape(b_p, 1, d) for m in jnp.split(mod[:b_p], 6, axis=-1))
    mod_s = tuple(m.reshape(1, b_s, d) for m in jnp.split(mod[b_p:], 6, axis=-1))

    zc = jnp.zeros((b_p, CONV_W - 1, GDN_QKV), F32)
    zs = jnp.zeros((b_p, N_HEADS, HEAD_DIM, HEAD_DIM), F32)
    zsh = jnp.zeros((b_p, 1, RWKV_COLS), F32)
    y_p, conv_p, sg_p, shift_p, sr_p = _layer(x_prompt, mod_p, zc, zs, zsh, zs, p, bb=1, tt=512,
                                              moe_tile=(1, 1024), bsz=4, chunk=CHUNK)

    y_s, conv_s, sg_s, shift_s, sr_s = _sample_layer(x_sample, mod_s, state_gdn_conv[l], state_gdn[l],
                                                     state_rwkv_shift[l], state_rwkv[l], p, ps)
    return (y_p, y_s, conv_p[None], sg_p[None], shift_p[None], sr_p[None],
            conv_s[None], sg_s[None], shift_s[None], sr_s[None])
```

```python
import jax
import jax.numpy as jnp
from jax import lax
from jax.experimental import pallas as pl
from jax.experimental.pallas import tpu as pltpu

F32 = jnp.float32
BF16 = jnp.bfloat16
HI = lax.Precision.HIGHEST

HEAD_DIM = 64
N_HEADS = 8
D_MIX_HALF = N_HEADS * HEAD_DIM
CONV_W = 4
LORA_W = 64
LORA_A = 64
LORA_G = 128
N_GROUPS = 4
EXP_PER_GROUP = 8
N_EXPERTS = N_GROUPS * EXP_PER_GROUP
D_EXPERT = 256
EPS = 1e-6
GN_EPS = HEAD_DIM * 1e-5

LANES = 128
SUBLANES = 8
BF16_ROWS = 16
GDN_QKV = 3 * D_MIX_HALF
RWKV_RKV = 3 * D_MIX_HALF
RWKV_COLS = RWKV_RKV + LORA_W + LORA_A + LORA_G
LORA_COLS = LORA_W + LORA_A + LORA_G

COL_RKV = GDN_QKV
COL_Z = COL_RKV + RWKV_RKV
COL_LORA = COL_Z + D_MIX_HALF
COL_BA = COL_LORA + LORA_COLS
N_COLS = COL_BA + LANES
CHUNK = 64

ADA_COLS_PER_STEP = 1536
PROJ_ROWS = 512
MOE_TILE_ROWS = 1024
SEQS_PER_STEP = 4

VMEM_LIMIT = 48 * 1024 * 1024
MOE_VMEM_LIMIT = 58 * 1024 * 1024


def _dot(a, b, prec=HI):
    return jnp.dot(a, b, preferred_element_type=F32, precision=prec)


def _bf(a):
    return a.astype(BF16)


def _mm(a, b):
    return jnp.dot(_bf(a), _bf(b), preferred_element_type=F32)


def _mm_nt(a, b):
    return lax.dot_general(_bf(a), _bf(b), (((1,), (1,)), ((), ())), preferred_element_type=F32)


def _mm_tn(a, b):
    return lax.dot_general(_bf(a), _bf(b), (((0,), (0,)), ((), ())), preferred_element_type=F32)


def _split2(a):
    hi = a.astype(BF16)
    return hi, (a - hi.astype(F32)).astype(BF16)


def _sigmoid(x):
    return 1.0 / (1.0 + jnp.exp(-x))


def _silu(x):
    return x * _sigmoid(x)


def _softplus(x):
    return jnp.maximum(x, 0.0) + jnp.log1p(jnp.exp(-jnp.abs(x)))


def _iota2(shape, dim):
    return lax.broadcasted_iota(jnp.int32, shape, dim)


def _head_block_ones():
    r = _iota2((LANES, LANES), 0)
    c = _iota2((LANES, LANES), 1)
    sh = HEAD_DIM.bit_length() - 1
    return jnp.where((r >> sh) == (c >> sh), 1.0, 0.0).astype(F32)


def _head_sums(x):
    ones = _bf(_head_block_ones())
    xb = _bf(x)
    parts = [jnp.dot(xb[:, p * LANES:(p + 1) * LANES], ones, preferred_element_type=F32)
             for p in range(x.shape[1] // LANES)]
    return jnp.concatenate(parts, axis=1)


def _expand_heads(x, base, terms):
    lane = _iota2((LANES, D_MIX_HALF), 0)
    head = _iota2((LANES, D_MIX_HALF), 1) >> (HEAD_DIM.bit_length() - 1)
    select = jnp.where(lane == base + head, 1.0, 0.0).astype(BF16)
    out = None
    rem = x
    for _ in range(terms):
        piece = rem.astype(BF16)
        part = jnp.dot(piece, select, preferred_element_type=F32)
        out = part if out is None else out + part
        rem = rem - piece.astype(F32)
    return out


def _cumsum_rows(x):
    n = x.shape[0]
    r = _iota2((n, n), 0)
    c = _iota2((n, n), 1)
    tri = jnp.where(r >= c, 1.0, 0.0).astype(BF16)
    x1 = x.astype(BF16)
    rem = x - x1.astype(F32)
    x2 = rem.astype(BF16)
    x3 = (rem - x2.astype(F32)).astype(BF16)
    return (jnp.dot(tri, x1, preferred_element_type=F32) + jnp.dot(tri, x2, preferred_element_type=F32)
            + jnp.dot(tri, x3, preferred_element_type=F32))


def _unit_lower_inverses(lows, n, mm):
    r = _iota2((n, n), 0)
    c = _iota2((n, n), 1)
    eye = jnp.where(r == c, 1.0, 0.0)
    pair = (r >> 1) == (c >> 1)
    invs = [eye - jnp.where(pair, low, 0.0) for low in lows]
    s = 2
    while s < n:
        sh = s.bit_length()
        sel = ((r >> sh) == (c >> sh)) & ((r & (2 * s - 1)) >= s) & ((c & (2 * s - 1)) < s)
        prods = [mm(jnp.where(sel, low, 0.0), inv) for low, inv in zip(lows, invs)]
        invs = [inv - mm(inv, prod) for inv, prod in zip(invs, prods)]
        s *= 2
    return invs


def _ada_kernel(c_ref, w_ref, b_ref, o_ref):
    o_ref[...] = _dot(_silu(c_ref[...]), w_ref[...]) + b_ref[...]


def _ada(c_all, ada_w, ada_b):
    n, d = c_all.shape
    nout = ada_w.shape[1]
    tn = ADA_COLS_PER_STEP
    return pl.pallas_call(
        _ada_kernel,
        grid=(nout // tn,),
        in_specs=[pl.BlockSpec((n, d), lambda j: (0, 0)),
                  pl.BlockSpec((d, tn), lambda j: (0, j)),
                  pl.BlockSpec((1, tn), lambda j: (0, j))],
        out_specs=pl.BlockSpec((n, tn), lambda j: (0, j)),
        out_shape=jax.ShapeDtypeStruct((n, nout), F32),
        compiler_params=pltpu.CompilerParams(dimension_semantics=("arbitrary",), vmem_limit_bytes=VMEM_LIMIT),
        name="ada",
    )(c_all, ada_w, ada_b.reshape(1, nout))


def _modulated_norm(x, nw, sc, sh):
    ms = jnp.mean(x * x, axis=-1, keepdims=True)
    return (x * lax.rsqrt(ms + EPS) * nw) * (1.0 + sc) + sh


def _inproj_kernel(x_ref, sc_ref, sh_ref, nw_ref, w_ref, o_ref):
    bb, tt, d = x_ref.shape
    h = _modulated_norm(x_ref[...], nw_ref[...], sc_ref[...], sh_ref[...])
    h = h.reshape(bb * tt, d).astype(BF16)
    o = jnp.dot(h, w_ref[...], preferred_element_type=F32)
    o_ref[...] = o.reshape(bb, tt, o.shape[-1])


def _inproj(x, sc, sh, nw, w_cols, bb, tt):
    b, t, d = x.shape
    nc = w_cols.shape[1]
    return pl.pallas_call(
        _inproj_kernel,
        grid=(b // bb, t // tt),
        in_specs=[pl.BlockSpec((bb, tt, d), lambda i, j: (i, j, 0)),
                  pl.BlockSpec((bb, 1, d), lambda i, j: (i, 0, 0)),
                  pl.BlockSpec((bb, 1, d), lambda i, j: (i, 0, 0)),
                  pl.BlockSpec((1, 1, d), lambda i, j: (0, 0, 0)),
                  pl.BlockSpec((d, nc), lambda i, j: (0, 0))],
        out_specs=pl.BlockSpec((bb, tt, nc), lambda i, j: (i, j, 0)),
        out_shape=jax.ShapeDtypeStruct((b, t, nc), F32),
        compiler_params=pltpu.CompilerParams(dimension_semantics=("arbitrary",) * 2, vmem_limit_bytes=VMEM_LIMIT),
        name="inproj",
    )(x, sc, sh, nw, w_cols)


def _gdn_kernel(qkv_ref, z_ref, ba_ref, c0_ref, cw_ref, alog_ref, dtb_ref, nw_ref, s0_ref, o_ref, s_ref, hist_scr):
    bsz, chunk, _ = qkv_ref.shape
    hist_lo = SUBLANES - (CONV_W - 1)

    @pl.when(pl.program_id(1) == 0)
    def _():
        s_ref[...] = s0_ref[...]
        hist_scr[:, hist_lo:SUBLANES, :] = c0_ref[...]

    row = _iota2((chunk, chunk), 0)
    col = _iota2((chunk, chunk), 1)
    causal = row >= col
    strict = row > col

    def prep(bi):
        u = qkv_ref[bi]
        hist_scr[bi, SUBLANES:SUBLANES + chunk, :] = u
        y = u * cw_ref[CONV_W - 1:CONV_W, :]
        for i in range(CONV_W - 1):
            y = y + hist_scr[bi, hist_lo + i:hist_lo + i + chunk, :] * cw_ref[i:i + 1, :]
        hist_scr[bi, hist_lo:SUBLANES, :] = hist_scr[bi, hist_lo + chunk:SUBLANES + chunk, :]
        qkv = _silu(y)
        q, k, v = (qkv[:, i * D_MIX_HALF:(i + 1) * D_MIX_HALF] for i in range(3))
        q = q * lax.rsqrt(_head_sums(q * q) + EPS) * (HEAD_DIM ** -0.5)
        k = k * lax.rsqrt(_head_sums(k * k) + EPS)
        ba = ba_ref[bi]
        beta = _expand_heads(_sigmoid(ba), 0, 2)
        g = -jnp.exp(alog_ref[...]) * _softplus(ba + dtb_ref[...])
        gcum = _expand_heads(_cumsum_rows(g), N_HEADS, 3)
        eg = jnp.exp(gcum)
        g_last = gcum[chunk - 1:chunk, :]
        return dict(q=q, k=k, beta=beta, gcum=gcum, rhs_v=beta * v, rhs_k=beta * eg * k, q_dec=eg * q,
                    k_dec=jnp.exp(g_last - gcum) * k, eg_last=jnp.exp(g_last))

    pre = [prep(bi) for bi in range(bsz)]
    chains = [(bi, h) for bi in range(bsz) for h in range(N_HEADS)]
    n = range(len(chains))

    def head(name, i, width=HEAD_DIM):
        bi, h = chains[i]
        return pre[bi][name][:, h * HEAD_DIM:h * HEAD_DIM + width]

    g_is = [head('gcum', i, chunk) for i in n]
    decays = [jnp.where(causal, jnp.exp(jnp.minimum(g_i - g_i.T, 0.0)), 0.0) for g_i in g_is]
    qk_kks = [_mm_nt(jnp.concatenate([head('k', i), head('q', i)], axis=0), head('k', i)) for i in n]
    lows = [jnp.where(strict, head('beta', i, chunk) * decays[i] * qk_kks[i][:chunk], 0.0) for i in n]
    t_invs = _unit_lower_inverses(lows, chunk, _mm)
    sols = [_mm(t_invs[i], jnp.concatenate([head('rhs_v', i), head('rhs_k', i)], axis=1)) for i in n]
    states = [s_ref[bi, h] for bi, h in chains]
    wss = [_mm(jnp.concatenate([sols[i][:, HEAD_DIM:], head('q_dec', i)], axis=0), states[i])
           for i in n]
    u_news = [sols[i][:, :HEAD_DIM] - wss[i][:chunk] for i in n]
    outs = [wss[i][chunk:] + _mm(qk_kks[i][chunk:] * decays[i], u_news[i]) for i in n]
    s_news = [head('eg_last', i) * states[i] + _mm_tn(head('k_dec', i), u_news[i]) for i in n]
    for i, (bi, h) in enumerate(chains):
        s_ref[bi, h] = s_news[i]
    for bi in range(bsz):
        o = jnp.concatenate(outs[bi * N_HEADS:(bi + 1) * N_HEADS], axis=1)
        o = o * lax.rsqrt(_head_sums(o * o) * (1.0 / HEAD_DIM) + EPS) * nw_ref[...]
        o_ref[bi] = o * _silu(z_ref[bi])


def _gdn(cols, conv0, s0, conv_w, alog_c, dtb_c, nw_rep, bsz, chunk):
    b, t, _ = cols.shape

    def col_spec(width, start):
        return pl.BlockSpec((bsz, chunk, width), lambda i, c: (i, c, start // width))

    state_spec = pl.BlockSpec((bsz, N_HEADS, HEAD_DIM, HEAD_DIM), lambda i, c: (i, 0, 0, 0))
    return pl.pallas_call(
        _gdn_kernel,
        grid=(b // bsz, t // chunk),
        in_specs=[col_spec(GDN_QKV, 0), col_spec(D_MIX_HALF, COL_Z), col_spec(LANES, COL_BA),
                  pl.BlockSpec((bsz, CONV_W - 1, GDN_QKV), lambda i, c: (i, 0, 0)),
                  pl.BlockSpec((CONV_W, GDN_QKV), lambda i, c: (0, 0)),
                  pl.BlockSpec((1, LANES), lambda i, c: (0, 0)), pl.BlockSpec((1, LANES), lambda i, c: (0, 0)),
                  pl.BlockSpec((1, D_MIX_HALF), lambda i, c: (0, 0)), state_spec],
        out_specs=[pl.BlockSpec((bsz, chunk, D_MIX_HALF), lambda i, c: (i, c, 0)), state_spec],
        out_shape=[jax.ShapeDtypeStruct((b, t, D_MIX_HALF), F32),
                   jax.ShapeDtypeStruct((b, N_HEADS, HEAD_DIM, HEAD_DIM), F32)],
        scratch_shapes=[pltpu.VMEM((bsz, SUBLANES + chunk, GDN_QKV), F32)],
        compiler_params=pltpu.CompilerParams(dimension_semantics=("arbitrary",) * 2, vmem_limit_bytes=VMEM_LIMIT),
        name="gdn",
    )(cols, cols, cols, conv0, conv_w, alog_c, dtb_c, nw_rep, s0)


def _rwkv_kernel(rkv_ref, l_ref, p_ref, pl_ref, mu_ref, mul_ref, w0_ref, a0_ref, kk_ref, ka_ref, rk_ref,
                 lnw_ref, lnb_ref, wup_ref, aup_ref, gup_ref, s0_ref, o_ref, s_ref, hist_scr, histl_scr):
    bsz, chunk, _ = rkv_ref.shape
    prev_row = SUBLANES - 1

    @pl.when(pl.program_id(1) == 0)
    def _():
        s_ref[...] = s0_ref[...]
        hist_scr[:, prev_row:SUBLANES, :] = p_ref[...]
        histl_scr[:, prev_row:SUBLANES, :] = pl_ref[...]

    row = _iota2((chunk, chunk), 0)
    col = _iota2((chunk, chunk), 1)
    causal = row >= col
    strict = row > col

    def shifted(bi, u_ref, scr, m_ref):
        u = u_ref[bi]
        scr[bi, SUBLANES:SUBLANES + chunk, :] = u
        prev = scr[bi, prev_row:prev_row + chunk, :]
        scr[bi, prev_row:SUBLANES, :] = scr[bi, prev_row + chunk:SUBLANES + chunk, :]
        return u + (prev - u) * m_ref[...]

    def prep(bi):
        rkv = shifted(bi, rkv_ref, hist_scr, mu_ref)
        r, k, v = (rkv[:, i * D_MIX_HALF:(i + 1) * D_MIX_HALF] for i in range(3))
        xl = shifted(bi, l_ref, histl_scr, mul_ref)
        dw = xl[:, :LORA_W]
        da = xl[:, LORA_W:LORA_W + LORA_A]
        dg = xl[:, LORA_W + LORA_A:]
        w_log = -_softplus(-(w0_ref[...] + _mm(jnp.tanh(dw), wup_ref[...]))) - 0.5
        lw = -jnp.exp(w_log)
        a = _sigmoid(a0_ref[...] + _mm(da, aup_ref[...]))
        gate = _mm(_sigmoid(dg), gup_ref[...])
        kk = k * kk_ref[...]
        kk = kk * lax.rsqrt(_head_sums(kk * kk) + EPS)
        k = k * (1.0 + (a - 1.0) * ka_ref[...])
        cum = _cumsum_rows(lw)
        e_out = jnp.exp(-cum)
        cum_last = cum[chunk - 1:chunk, :]
        e_rest = jnp.exp(cum_last - cum)
        return dict(r=r, k=k, v=v, gate=gate, a_t=-kk * jnp.exp(cum - lw), b_t=kk * a * e_out, k_t=k * e_out,
                    r_t=r * jnp.exp(cum), b_c=kk * a * e_rest, k_c=k * e_rest, e_last=jnp.exp(cum_last))

    pre = [prep(bi) for bi in range(bsz)]
    chains = [(bi, h) for bi in range(bsz) for h in range(N_HEADS)]
    n = range(len(chains))

    def head(name, i):
        bi, h = chains[i]
        return pre[bi][name][:, h * HEAD_DIM:(h + 1) * HEAD_DIM]

    ars = [jnp.concatenate([head('a_t', i), head('r_t', i)], axis=0) for i in n]
    bks = [jnp.concatenate([head('b_t', i), head('k_t', i)], axis=0) for i in n]
    ms = [_mm_nt(ars[i], bks[i]) for i in n]
    t_invs = _unit_lower_inverses([jnp.where(strict, -m[:chunk, :chunk], 0.0) for m in ms], chunk, _mm)
    states = [s_ref[bi, h] for bi, h in chains]
    arss = [_mm_nt(ars[i], states[i]) for i in n]
    vhs = [head('v', i) for i in n]
    akvs = [_mm(jnp.where(strict, ms[i][:chunk, chunk:], 0.0), vhs[i]) for i in n]
    us = [_mm(t_invs[i], arss[i][:chunk] + akvs[i]) for i in n]
    uvs = [jnp.concatenate([us[i], vhs[i]], axis=0) for i in n]
    causal2 = _iota2((chunk, 2 * chunk), 0) >= (_iota2((chunk, 2 * chunk), 1) & (chunk - 1))
    ys = [arss[i][chunk:] + _mm(jnp.where(causal2, ms[i][chunk:], 0.0), uvs[i]) for i in n]
    s_news = [states[i] * head('e_last', i)
              + _mm_tn(uvs[i], jnp.concatenate([head('b_c', i), head('k_c', i)], axis=0)) for i in n]
    for i, (bi, h) in enumerate(chains):
        s_ref[bi, h] = s_news[i]
    for bi in range(bsz):
        p = pre[bi]
        y = jnp.concatenate(ys[bi * N_HEADS:(bi + 1) * N_HEADS], axis=1)
        mean = _head_sums(y) * (1.0 / HEAD_DIM)
        dy = y - mean
        var = _head_sums(dy * dy) * (1.0 / HEAD_DIM)
        y = dy * lax.rsqrt(var + GN_EPS) * lnw_ref[...] + lnb_ref[...]
        bonus = _head_sums(p['r'] * p['k'] * rk_ref[...]) * p['v']
        o_ref[bi] = (y + bonus) * p['gate']


def _rwkv(cols, shift0, s0, mu, vecs, w_up, a_up, g_up, bsz, chunk):
    b, t, _ = cols.shape

    def full2(shape):
        return pl.BlockSpec(shape, lambda i, c: (0, 0))

    state_spec = pl.BlockSpec((bsz, N_HEADS, HEAD_DIM, HEAD_DIM), lambda i, c: (i, 0, 0, 0))
    return pl.pallas_call(
        _rwkv_kernel,
        grid=(b // bsz, t // chunk),
        in_specs=[pl.BlockSpec((bsz, chunk, RWKV_RKV), lambda i, c: (i, c, COL_RKV // RWKV_RKV)),
                  pl.BlockSpec((bsz, chunk, LORA_COLS), lambda i, c: (i, c, COL_LORA // LORA_COLS)),
                  pl.BlockSpec((bsz, 1, RWKV_RKV), lambda i, c: (i, 0, 0)),
                  pl.BlockSpec((bsz, 1, LORA_COLS), lambda i, c: (i, 0, RWKV_RKV // LORA_COLS)),
                  pl.BlockSpec((1, RWKV_RKV), lambda i, c: (0, 0)),
                  pl.BlockSpec((1, LORA_COLS), lambda i, c: (0, RWKV_RKV // LORA_COLS))]
                 + [full2((1, D_MIX_HALF))] * 7
                 + [full2((LORA_W, D_MIX_HALF)), full2((LORA_A, D_MIX_HALF)), full2((LORA_G, D_MIX_HALF)),
                    state_spec],
        out_specs=[pl.BlockSpec((bsz, chunk, D_MIX_HALF), lambda i, c: (i, c, 0)), state_spec],
        out_shape=[jax.ShapeDtypeStruct((b, t, D_MIX_HALF), F32),
                   jax.ShapeDtypeStruct((b, N_HEADS, HEAD_DIM, HEAD_DIM), F32)],
        scratch_shapes=[pltpu.VMEM((bsz, SUBLANES + chunk, RWKV_RKV), F32),
                        pltpu.VMEM((bsz, SUBLANES + chunk, LORA_COLS), F32)],
        compiler_params=pltpu.CompilerParams(dimension_semantics=("arbitrary",) * 2, vmem_limit_bytes=VMEM_LIMIT),
        name="rwkv",
    )(cols, cols, shift0, shift0, mu, mu, *vecs, w_up, a_up, g_up, s0)


def _mod_spec(m, bb, tt):
    d = m.shape[-1]
    if m.shape[1] == 1:
        return pl.BlockSpec((bb, 1, d), lambda i, j, *_: (i, 0, 0))
    return pl.BlockSpec((1, tt, d), lambda i, j, *_: (0, j, 0))


def _outproj_kernel(oa_ref, ob_ref, x_ref, gt_ref, sc_ref, sh_ref, nw_ref, wa_ref, wb_ref, wr_ref, br_ref,
                    x1_ref, h_ref, wt_ref):
    bb, tt, d = x_ref.shape
    n = bb * tt
    if len(oa_ref.shape) == 2:
        mixed = _mm_tn(oa_ref[...], wa_ref[...]) + _mm_tn(ob_ref[...], wb_ref[...])
    else:
        oa = oa_ref[...].reshape(n, D_MIX_HALF).astype(BF16)
        ob = ob_ref[...].reshape(n, D_MIX_HALF).astype(BF16)
        mixed = (jnp.dot(oa, wa_ref[...], preferred_element_type=F32)
                 + jnp.dot(ob, wb_ref[...], preferred_element_type=F32))
    x1 = x_ref[...] + gt_ref[...] * mixed.reshape(bb, tt, d)
    x1_ref[...] = x1
    h = _modulated_norm(x1, nw_ref[...], sc_ref[...], sh_ref[...]).reshape(n, d)
    h_ref[...] = h.reshape(bb, tt, d).astype(BF16)

    h_hi, h_lo = _split2(h)
    hw = jnp.dot(h_hi, wr_ref[...], preferred_element_type=F32)
    logits = (hw[:, :LANES] + hw[:, LANES:] + jnp.dot(h_lo, wr_ref[:, :LANES], preferred_element_type=F32)
              + br_ref[...])
    lane_i = _iota2((n, LANES), 1)
    lane = lane_i.astype(F32)
    lane_grp = (lane_i >> (EXP_PER_GROUP.bit_length() - 1)).astype(F32)
    neg = jnp.float32(-jnp.inf)
    big = jnp.float32(LANES)
    is_grp = (lane_i >= N_EXPERTS) & (lane_i < N_EXPERTS + N_GROUPS)
    gl = jnp.where(is_grp, logits, neg)
    gmax = jnp.max(gl, axis=-1, keepdims=True)
    grp = jnp.min(jnp.where(gl == gmax, lane, big), axis=-1, keepdims=True) - N_EXPERTS
    g_prob = 1.0 / jnp.sum(jnp.where(is_grp, jnp.exp(gl - gmax), 0.0), axis=-1, keepdims=True)
    in_grp = (lane_i < N_EXPERTS) & (lane_grp == grp)
    el = jnp.where(in_grp, logits, neg)
    m1 = jnp.max(el, axis=-1, keepdims=True)
    i1 = jnp.min(jnp.where(el == m1, lane, big), axis=-1, keepdims=True)
    el2 = jnp.where(lane == i1, neg, el)
    m2 = jnp.max(el2, axis=-1, keepdims=True)
    i2 = jnp.min(jnp.where(el2 == m2, lane, big), axis=-1, keepdims=True)
    e2 = jnp.exp(m2 - m1)
    p1 = g_prob / (1.0 + e2)
    p2 = g_prob * e2 / (1.0 + e2)
    wt = jnp.where(lane == i1, p1, 0.0) + jnp.where(lane == i2, p2, 0.0)
    wt_ref[...] = wt.reshape(bb, tt, LANES)


def _outproj(oa, ob, x, gt, sc, sh, nw, wa, wb, wr, br, bb, tt):
    b, t, d = x.shape
    grid = (b // bb, t // tt)

    def tok_spec(w):
        return pl.BlockSpec((bb, tt, w), lambda i, j: (i, j, 0))

    def full2(shape):
        return pl.BlockSpec(shape, lambda i, j: (0, 0))

    mod_spec = _mod_spec(gt, bb, tt)
    if oa.ndim == 2:
        assert grid == (1, 1)
        mix_spec = full2(oa.shape)
    else:
        mix_spec = tok_spec(D_MIX_HALF)
    return pl.pallas_call(
        _outproj_kernel,
        grid=grid,
        in_specs=[mix_spec, mix_spec, tok_spec(d), mod_spec, mod_spec, mod_spec,
                  pl.BlockSpec((1, 1, d), lambda i, j: (0, 0, 0)),
                  full2((D_MIX_HALF, d)), full2((D_MIX_HALF, d)), full2((d, 2 * LANES)), full2((1, LANES))],
        out_specs=[tok_spec(d), tok_spec(d), tok_spec(LANES)],
        out_shape=[jax.ShapeDtypeStruct((b, t, d), F32), jax.ShapeDtypeStruct((b, t, d), BF16),
                   jax.ShapeDtypeStruct((b, t, LANES), F32)],
        compiler_params=pltpu.CompilerParams(dimension_semantics=("arbitrary",) * 2, vmem_limit_bytes=VMEM_LIMIT),
        name="outproj",
    )(oa, ob, x, gt, sc, sh, nw, wa, wb, wr, br)


MOE_ROWS = 128
MOE_EXPERTS_PER_STEP = 4


def _moe_kernel(h_ref, wt_ref, x1_ref, gt_ref, fw_ref, wgu_ref, wd_ref, y_ref,
                xg_scr, gtw_scr, yw_scr, seg_scr):
    bb, tt, d = x1_ref.shape
    n = bb * tt
    npos = 2 * n
    e = pl.program_id(2)

    @pl.when(e == 0)
    def _():
        wt_t = wt_ref[...].reshape(n, LANES).T
        member = wt_t > 0.0
        ones = jnp.where(member, 1.0, 0.0)
        upper = jnp.where(_iota2((n, n), 0) < _iota2((n, n), 1), 1.0, 0.0).astype(BF16)
        rank = jnp.dot(_bf(ones), upper, preferred_element_type=F32)
        cnt = jnp.broadcast_to(jnp.sum(ones, axis=1, keepdims=True), (LANES, LANES))
        lower = jnp.where(_iota2((LANES, LANES), 0) > _iota2((LANES, LANES), 1), 1.0, 0.0)
        off = _dot(lower, cnt)
        first_i = off.astype(jnp.int32)
        last_i = (off + cnt).astype(jnp.int32)
        for x in range(N_EXPERTS):
            seg_scr[x] = first_i[x, 0]
            seg_scr[N_EXPERTS + x] = last_i[x, 0]
        pos = off[:, :1] + rank
        pos1 = jnp.min(jnp.where(member, pos, jnp.float32(4 * n)), axis=0, keepdims=True)
        pos2 = jnp.max(jnp.where(member, pos, -1.0), axis=0, keepdims=True)
        w1 = jnp.sum(jnp.where(member & (pos == pos1), wt_t, 0.0), axis=0, keepdims=True)
        w2 = jnp.sum(jnp.where(member & (pos == pos2), wt_t, 0.0), axis=0, keepdims=True)
        w2 = jnp.where(pos2 != pos1, w2, 0.0)

        def onehot_block(b, carry):
            p = (b * MOE_ROWS + _iota2((MOE_ROWS, n), 0)).astype(F32)
            rows = pl.ds(pl.multiple_of(b * MOE_ROWS, MOE_ROWS), MOE_ROWS)
            yw_scr[rows, :n] = jnp.where((pos1 == p) | (pos2 == p), 1.0, 0.0).astype(BF16)
            return carry

        lax.fori_loop(0, npos // MOE_ROWS, onehot_block, 0)
        xg_scr[:npos, :] = jnp.dot(yw_scr[:npos, :n], h_ref[...].reshape(n, d),
                                   preferred_element_type=F32).astype(BF16)
        xg_scr[npos:, :] = jnp.zeros((MOE_ROWS, d), BF16)
        z_t = jnp.concatenate([pos1, pos2, w1, w2, jnp.zeros((LANES - 4, n), F32)], axis=0).T
        p1c, p2c, w1c, w2c = (z_t[:, i:i + 1] for i in range(4))
        for b in range(npos // LANES):
            p = (b * LANES + _iota2((n, LANES), 1)).astype(F32)
            gtw_scr[:, b * LANES:(b + 1) * LANES] = (jnp.where(p1c == p, w1c, 0.0)
                                                     + jnp.where(p2c == p, w2c, 0.0)).astype(BF16)
        yw_scr[...] = jnp.zeros_like(yw_scr)

    es = range(wgu_ref.shape[0])
    firsts = [seg_scr[e * len(es) + j] for j in es]
    lasts = [seg_scr[N_EXPERTS + e * len(es) + j] for j in es]
    starts = [(f // BF16_ROWS) * BF16_ROWS for f in firsts]
    windows = [(lasts[j] - starts[j] + MOE_ROWS - 1) // MOE_ROWS for j in es]

    def window_rows(r0):
        return pl.ds(pl.multiple_of(r0, BF16_ROWS), MOE_ROWS)

    def store_rows(j, r0, y):
        p = r0 + _iota2((MOE_ROWS, 1), 0)
        rows = window_rows(r0)
        yw_scr[rows, :] = jnp.where((p >= firsts[j]) & (p < lasts[j]), y, yw_scr[rows, :])

    gus = [jnp.dot(xg_scr[window_rows(starts[j]), :], wgu_ref[j], preferred_element_type=F32) for j in es]
    acts = [(_silu(gu[:, :D_EXPERT]) * gu[:, D_EXPERT:]).astype(BF16) for gu in gus]
    ys = [jnp.dot(acts[j], wd_ref[j], preferred_element_type=F32).astype(BF16) for j in es]
    for j in es:
        store_rows(j, starts[j], ys[j])

    most = windows[0]
    for j in es[1:]:
        most = jnp.maximum(most, windows[j])

    @pl.when(most > 1)
    def _():
        for j in es:
            def more_windows(w, carry, j=j):
                r0 = starts[j] + w * MOE_ROWS
                gu = jnp.dot(xg_scr[window_rows(r0), :], wgu_ref[j], preferred_element_type=F32)
                act = (_silu(gu[:, :D_EXPERT]) * gu[:, D_EXPERT:]).astype(BF16)
                store_rows(j, r0, jnp.dot(act, wd_ref[j], preferred_element_type=F32).astype(BF16))
                return carry

            lax.fori_loop(1, windows[j], more_windows, 0)

    @pl.when(e == pl.num_programs(2) - 1)
    def _():
        moe = jnp.dot(gtw_scr[...], yw_scr[:npos, :], preferred_element_type=F32)
        x2 = x1_ref[...] + gt_ref[...] * moe.reshape(bb, tt, d)
        ms = jnp.mean(x2 * x2, axis=-1, keepdims=True)
        y_ref[...] = x2 * lax.rsqrt(ms + EPS) * fw_ref[...]


def _moe(h, wt, x1, gt, fw, w_gu, w_down, bb, tt):
    b, t, d = x1.shape
    es = MOE_EXPERTS_PER_STEP
    grid = (b // bb, t // tt, N_EXPERTS // es)
    npos = 2 * bb * tt

    def tok_spec(w):
        return pl.BlockSpec((bb, tt, w), lambda i, j, e: (i, j, 0))

    return pl.pallas_call(
        _moe_kernel,
        grid=grid,
        in_specs=[tok_spec(d), tok_spec(LANES), tok_spec(d),
                  _mod_spec(gt, bb, tt),
                  pl.BlockSpec((1, 1, d), lambda i, j, e: (0, 0, 0)),
                  pl.BlockSpec((es, d, 2 * D_EXPERT), lambda i, j, e: (e, 0, 0)),
                  pl.BlockSpec((es, D_EXPERT, d), lambda i, j, e: (e, 0, 0))],
        out_specs=tok_spec(d),
        out_shape=jax.ShapeDtypeStruct((b, t, d), F32),
        scratch_shapes=[pltpu.VMEM((npos + MOE_ROWS, d), BF16), pltpu.VMEM((bb * tt, npos), BF16),
                        pltpu.VMEM((npos + MOE_ROWS, d), BF16), pltpu.SMEM((2 * N_EXPERTS,), jnp.int32)],
        compiler_params=pltpu.CompilerParams(dimension_semantics=("arbitrary",) * 3,
                                             vmem_limit_bytes=MOE_VMEM_LIMIT),
        name="moe",
    )(h, wt, x1, gt, fw, w_gu, w_down)


ROW_Z = GDN_QKV
ROW_R = ROW_Z + D_MIX_HALF
ROW_LORA = ROW_R + RWKV_RKV
ROW_BETA = ROW_LORA + LORA_COLS
ROW_A = ROW_BETA + N_HEADS
N_ROWS = ROW_A + N_HEADS


def _inproj_t_kernel(x_ref, sc_ref, sh_ref, nw_ref, w_ref, o_ref):
    t, b, d = x_ref.shape
    h = _modulated_norm(x_ref[...], nw_ref[...], sc_ref[...], sh_ref[...]).reshape(t * b, d)
    o_ref[...] = _mm_nt(w_ref[...], h)


def _inproj_t(x, sc, sh, nw, w_rows):
    t, b, d = x.shape
    nr = w_rows.shape[0]

    def full(shape):
        return pl.BlockSpec(shape, lambda i: (0,) * len(shape))

    return pl.pallas_call(
        _inproj_t_kernel,
        grid=(1,),
        in_specs=[full((t, b, d)), full((1, b, d)), full((1, b, d)), full((1, 1, d)), full((nr, d))],
        out_specs=full((nr, t * b)),
        out_shape=jax.ShapeDtypeStruct((nr, t * b), F32),
        compiler_params=pltpu.CompilerParams(dimension_semantics=("arbitrary",), vmem_limit_bytes=VMEM_LIMIT),
        name="inproj_t",
    )(x, sc, sh, nw, w_rows)


def _store_state_rows(s_scr, sout_ref):
    for j in range(HEAD_DIM // 2):
        pair = jnp.concatenate([s_scr[2 * j], s_scr[2 * j + 1]], axis=0)
        sout_ref[:, j * LANES:(j + 1) * LANES] = pair.T


def _gdn_s_kernel(q_ref, k_ref, v_ref, z_ref, b_ref, a_ref, cq_ref, ck_ref, cv_ref, wq_ref, wk_ref, wv_ref,
                  alog_ref, dtb_ref, nw_ref, s0_ref, o_ref, sout_ref, s_scr, kq_scr):
    nb = s0_ref.shape[-1]
    nt = q_ref.shape[1] // nb
    h = pl.program_id(0)

    def tok(ref, t):
        return ref[:, t * nb:(t + 1) * nb]

    def conv(u_ref, c_ref, w_ref):
        full = [c_ref[i] for i in range(CONV_W - 1)] + [tok(u_ref, t) for t in range(nt)]
        taps = [w_ref[:, i:i + 1] for i in range(CONV_W)]
        outs = []
        for t in range(nt):
            y = full[t + CONV_W - 1] * taps[CONV_W - 1]
            for i in range(CONV_W - 1):
                y = y + full[t + i] * taps[i]
            outs.append(_silu(y))
        return outs

    qs = conv(q_ref, cq_ref, wq_ref)
    ks = conv(k_ref, ck_ref, wk_ref)
    vs = conv(v_ref, cv_ref, wv_ref)
    neg_rate = -jnp.exp(alog_ref[pl.ds(h, 1), :])
    dtb = dtb_ref[pl.ds(h, 1), :]
    beta_in = b_ref[pl.ds(h, 1), :]
    a_in = a_ref[pl.ds(h, 1), :]
    s_scr[...] = s0_ref[0]
    zero = jnp.zeros((HEAD_DIM, nb), F32)
    for t in range(nt):
        q = qs[t] * lax.rsqrt(jnp.sum(qs[t] * qs[t], axis=0, keepdims=True) + EPS) * (HEAD_DIM ** -0.5)
        k = ks[t] * lax.rsqrt(jnp.sum(ks[t] * ks[t], axis=0, keepdims=True) + EPS)
        beta = _sigmoid(beta_in[:, t * nb:(t + 1) * nb])
        decay = jnp.exp(neg_rate * _softplus(a_in[:, t * nb:(t + 1) * nb] + dtb))
        kq_scr[0] = k
        kq_scr[1] = q

        def decay_and_project(i, acc):
            row = s_scr[i] * decay
            s_scr[i] = row
            return acc + kq_scr[0, pl.ds(i, 1), :] * row

        k_s = lax.fori_loop(0, HEAD_DIM, decay_and_project, zero, unroll=8)
        u = beta * (vs[t] - k_s)

        def update_and_read(i, acc):
            row = s_scr[i] + kq_scr[0, pl.ds(i, 1), :] * u
            s_scr[i] = row
            return acc + kq_scr[1, pl.ds(i, 1), :] * row

        o = lax.fori_loop(0, HEAD_DIM, update_and_read, zero, unroll=8)
        o = o * lax.rsqrt(jnp.mean(o * o, axis=0, keepdims=True) + EPS) * nw_ref[...]
        o_ref[:, t * nb:(t + 1) * nb] = o * _silu(tok(z_ref, t))
    _store_state_rows(s_scr, sout_ref)


def _gdn_s(cols_t, conv_t, s_t, cw_t, alog_b, dtb_b, nw_b):
    ntok = cols_t.shape[1]
    nb = s_t.shape[-1]

    def head_rows(base):
        return pl.BlockSpec((HEAD_DIM, ntok), lambda h: (base // HEAD_DIM + h, 0))

    def conv_rows(base):
        return pl.BlockSpec((CONV_W - 1, HEAD_DIM, nb), lambda h: (0, base // HEAD_DIM + h, 0))

    def tap_rows(base):
        return pl.BlockSpec((HEAD_DIM, CONV_W), lambda h: (base // HEAD_DIM + h, 0))

    def full2(shape):
        return pl.BlockSpec(shape, lambda h: (0, 0))

    return pl.pallas_call(
        _gdn_s_kernel,
        grid=(N_HEADS,),
        in_specs=[head_rows(0), head_rows(D_MIX_HALF), head_rows(2 * D_MIX_HALF), head_rows(ROW_Z),
                  pl.BlockSpec((N_HEADS, ntok), lambda h: (ROW_BETA // N_HEADS, 0)),
                  pl.BlockSpec((N_HEADS, ntok), lambda h: (ROW_A // N_HEADS, 0)),
                  conv_rows(0), conv_rows(D_MIX_HALF), conv_rows(2 * D_MIX_HALF),
                  tap_rows(0), tap_rows(D_MIX_HALF), tap_rows(2 * D_MIX_HALF),
                  full2((N_HEADS, nb)), full2((N_HEADS, nb)), full2((HEAD_DIM, nb)),
                  pl.BlockSpec((1, HEAD_DIM, HEAD_DIM, nb), lambda h: (h, 0, 0, 0))],
        out_specs=[pl.BlockSpec((HEAD_DIM, ntok), lambda h: (h, 0)),
                   pl.BlockSpec((nb, HEAD_DIM * HEAD_DIM), lambda h: (0, h))],
        out_shape=[jax.ShapeDtypeStruct((D_MIX_HALF, ntok), F32),
                   jax.ShapeDtypeStruct((nb, N_HEADS * HEAD_DIM * HEAD_DIM), F32)],
        scratch_shapes=[pltpu.VMEM((HEAD_DIM, HEAD_DIM, nb), F32), pltpu.VMEM((2, HEAD_DIM, nb), F32)],
        compiler_params=pltpu.CompilerParams(dimension_semantics=("arbitrary",), vmem_limit_bytes=VMEM_LIMIT),
        name="gdn_s",
    )(cols_t, cols_t, cols_t, cols_t, cols_t, cols_t, conv_t, conv_t, conv_t, cw_t, cw_t, cw_t,
      alog_b, dtb_b, nw_b, s_t)


def _rwkv_s_kernel(r_ref, k_ref, v_ref, l_ref, pr_ref, pk_ref, pv_ref, pl_ref, mr_ref, mk_ref, mv_ref, ml_ref,
                   vec_ref, wup_ref, aup_ref, gup_ref, s0_ref, o_ref, sout_ref, s_scr, y_scr, v_scr):
    nb = s0_ref.shape[-1]
    nt = r_ref.shape[1] // nb

    def mixed(u_ref, p_ref, m_ref):
        cur = [u_ref[:, t * nb:(t + 1) * nb] for t in range(nt)]
        prev = [p_ref[...]] + cur[:-1]
        mu = m_ref[...]
        return [c + (p - c) * mu for c, p in zip(cur, prev)]

    rs = mixed(r_ref, pr_ref, mr_ref)
    ks = mixed(k_ref, pk_ref, mk_ref)
    vs = mixed(v_ref, pv_ref, mv_ref)
    ls = mixed(l_ref, pl_ref, ml_ref)
    w0, a0, k_k, k_a, r_k, ln_w, ln_b = (vec_ref[:, i:i + 1] for i in range(7))
    s_scr[...] = s0_ref[0]
    for t in range(nt):
        dw = ls[t][:LORA_W]
        da = ls[t][LORA_W:LORA_W + LORA_A]
        dg = ls[t][LORA_W + LORA_A:]
        w_log = -_softplus(-(w0 + _mm(wup_ref[...], jnp.tanh(dw)))) - 0.5
        w = jnp.exp(-jnp.exp(w_log))
        a = _sigmoid(a0 + _mm(aup_ref[...], da))
        gate = _mm(gup_ref[...], _sigmoid(dg))
        r, v = rs[t], vs[t]
        kk = ks[t] * k_k
        kk = kk * lax.rsqrt(jnp.sum(kk * kk, axis=0, keepdims=True) + EPS)
        k = ks[t] * (1.0 + (a - 1.0) * k_a)
        neg_kk = -kk
        kk_a = kk * a
        v_scr[...] = v

        def state_row(i, carry):
            row = s_scr[i]
            sa = jnp.sum(row * neg_kk, axis=0, keepdims=True)
            row = row * w + sa * kk_a + v_scr[pl.ds(i, 1), :] * k
            s_scr[i] = row
            y_scr[pl.ds(i, 1), :] = jnp.sum(row * r, axis=0, keepdims=True)
            return carry

        lax.fori_loop(0, HEAD_DIM, state_row, 0, unroll=4)
        y = y_scr[...]
        dy = y - jnp.mean(y, axis=0, keepdims=True)
        y = dy * lax.rsqrt(jnp.mean(dy * dy, axis=0, keepdims=True) + GN_EPS) * ln_w + ln_b
        bonus = jnp.sum(r * k * r_k, axis=0, keepdims=True) * v
        o_ref[:, t * nb:(t + 1) * nb] = (y + bonus) * gate
    _store_state_rows(s_scr, sout_ref)


def _rwkv_s(cols_t, shift_t, s_t, mu_b, vecs_t, w_up_t, a_up_t, g_up_t):
    ntok = cols_t.shape[1]
    nb = s_t.shape[-1]

    def head_rows(width, base):
        return pl.BlockSpec((HEAD_DIM, width), lambda h: (base // HEAD_DIM + h, 0))

    def lora_rows(width, base):
        return pl.BlockSpec((LORA_COLS, width), lambda h: (base // LORA_COLS, 0))

    return pl.pallas_call(
        _rwkv_s_kernel,
        grid=(N_HEADS,),
        in_specs=[head_rows(ntok, ROW_R), head_rows(ntok, ROW_R + D_MIX_HALF), head_rows(ntok, ROW_R + 2 * D_MIX_HALF),
                  lora_rows(ntok, ROW_LORA),
                  head_rows(nb, 0), head_rows(nb, D_MIX_HALF), head_rows(nb, 2 * D_MIX_HALF), lora_rows(nb, RWKV_RKV),
                  head_rows(nb, 0), head_rows(nb, D_MIX_HALF), head_rows(nb, 2 * D_MIX_HALF), lora_rows(nb, RWKV_RKV),
                  head_rows(SUBLANES, 0), head_rows(LORA_W, 0), head_rows(LORA_A, 0), head_rows(LORA_G, 0),
                  pl.BlockSpec((1, HEAD_DIM, HEAD_DIM, nb), lambda h: (h, 0, 0, 0))],
        out_specs=[pl.BlockSpec((HEAD_DIM, ntok), lambda h: (h, 0)),
                   pl.BlockSpec((nb, HEAD_DIM * HEAD_DIM), lambda h: (0, h))],
        out_shape=[jax.ShapeDtypeStruct((D_MIX_HALF, ntok), F32),
                   jax.ShapeDtypeStruct((nb, N_HEADS * HEAD_DIM * HEAD_DIM), F32)],
        scratch_shapes=[pltpu.VMEM((HEAD_DIM, HEAD_DIM, nb), F32), pltpu.VMEM((HEAD_DIM, nb), F32),
                        pltpu.VMEM((HEAD_DIM, nb), F32)],
        compiler_params=pltpu.CompilerParams(dimension_semantics=("arbitrary",), vmem_limit_bytes=VMEM_LIMIT),
        name="rwkv_s",
    )(cols_t, cols_t, cols_t, cols_t, shift_t, shift_t, shift_t, shift_t, mu_b, mu_b, mu_b, mu_b,
      vecs_t, w_up_t, a_up_t, g_up_t, s_t)


def _layer(x, mod, conv0, s_gdn0, shift0, s_rwkv0, p, *, bb, tt, moe_tile, bsz, chunk):
    sh_m, sc_m, gt_m, sh_f, sc_f, gt_f = mod
    cols = _inproj(x, sc_m, sh_m, p['norm_mix_w'], p['w_cols'], bb, tt)
    o_a, s_gdn = _gdn(cols, conv0, s_gdn0, p['gdn_conv_w'], p['alog_c'], p['dtb_c'], p['gdn_nw_rep'], bsz, chunk)
    o_b, s_rwkv = _rwkv(cols, shift0, s_rwkv0, p['rwkv_mu'], p['rwkv_vecs'], p['rwkv_w_up'], p['rwkv_a_up'],
                        p['rwkv_g_up'], bsz, chunk)
    x1, h2, wt = _outproj(o_a, o_b, x, gt_m, sc_f, sh_f, p['norm_ffn_w'], p['w_out_a'], p['w_out_b'],
                          p['w_router'], p['b_router'], *moe_tile)
    y = _moe(h2, wt, x1, gt_f, p['final_norm_w'], p['w_gu'], p['w_down'], *moe_tile)
    conv_new = cols[:, -(CONV_W - 1):, :GDN_QKV]
    shift_new = jnp.concatenate([cols[:, -1, COL_RKV:COL_Z], cols[:, -1, COL_LORA:COL_BA]], axis=-1)
    return y, conv_new, s_gdn, shift_new, s_rwkv


def _sample_layer(x, mod, conv0, s_gdn0, shift0, s_rwkv0, p, ps):
    b, t, d = x.shape
    assert t >= CONV_W - 1
    sh_m, sc_m, gt_m, sh_f, sc_f, gt_f = mod
    x_tm = jnp.transpose(x, (1, 0, 2))
    cols_t = _inproj_t(x_tm, sc_m, sh_m, p['norm_mix_w'], ps['w_rows'])
    seq_last = (1, 2, 3, 0)
    o_a, s_gdn = _gdn_s(cols_t, jnp.transpose(conv0, (1, 2, 0)), jnp.transpose(s_gdn0, seq_last),
                        ps['gdn_conv_w_t'], ps['alog_b'], ps['dtb_b'], ps['gdn_nw_b'])
    o_b, s_rwkv = _rwkv_s(cols_t, shift0.T, jnp.transpose(s_rwkv0, seq_last), ps['rwkv_mu_b'], ps['rwkv_vecs_t'],
                          ps['w_up_t'], ps['a_up_t'], ps['g_up_t'])
    x1, h2, wt = _outproj(o_a, o_b, x_tm, gt_m, sc_f, sh_f, p['norm_ffn_w'], p['w_out_a'], p['w_out_b'],
                          p['w_router'], p['b_router'], t, b)
    y = _moe(h2, wt, x1, gt_f, p['final_norm_w'], p['w_gu'], p['w_down'], t, b)
    keep = CONV_W - 1
    conv_new = jnp.transpose(cols_t[:GDN_QKV, (t - keep) * b:].reshape(GDN_QKV, keep, b), (2, 1, 0))
    shift_new = cols_t[ROW_R:ROW_BETA, (t - 1) * b:].T
    state_shape = (b, N_HEADS, HEAD_DIM, HEAD_DIM)
    return (jnp.transpose(y, (1, 0, 2)), conv_new, s_gdn.reshape(state_shape), shift_new,
            s_rwkv.reshape(state_shape))


def kernel(x_prompt, x_sample, state_gdn_conv, state_gdn, state_rwkv_shift, state_rwkv, c_prompt, c_sample, ada_w, ada_b, norm_mix_w, w_in, gdn_conv_w, gdn_a_log, gdn_dt_bias, gdn_norm_w, rwkv_mu, rwkv_w0, rwkv_w_up, rwkv_a0, rwkv_a_up, rwkv_g_up, rwkv_k_k, rwkv_k_a, rwkv_r_k, rwkv_ln_w, rwkv_ln_b, w_out, norm_ffn_w, router_group_w, router_group_b, router_expert_w, router_expert_b, expert_w_gate_up, expert_w_down, final_norm_w):
    depth = ada_w.shape[0]
    assert depth == 1
    l = 0
    b_p, t_p, d = x_prompt.shape
    b_s, t_s, _ = x_sample.shape
    gdn_cols = GDN_QKV + D_MIX_HALF + 2 * N_HEADS

    w = w_in[l]
    beta0 = GDN_QKV + D_MIX_HALF
    w_bf = w.astype(BF16)
    rkv0, lora0 = gdn_cols, gdn_cols + RWKV_RKV
    w_cols = jnp.concatenate([w_bf[:, :GDN_QKV], w_bf[:, rkv0:lora0], w_bf[:, GDN_QKV:beta0], w_bf[:, lora0:],
                              w_bf[:, beta0:gdn_cols], jnp.zeros((d, LANES - 2 * N_HEADS), BF16)], axis=1)
    head_lanes = lambda a, first: jnp.pad(a, (first, LANES - first - N_HEADS)).reshape(1, LANES)
    row = lambda a: a.reshape(1, -1)
    n_route = N_EXPERTS + N_GROUPS
    w_router = jnp.concatenate([router_expert_w[l], router_group_w[l], jnp.zeros((d, LANES - n_route), F32)], axis=1)
    w_router = jnp.concatenate(_split2(w_router), axis=1)
    b_router = jnp.concatenate([router_expert_b[l], router_group_b[l], jnp.zeros((LANES - n_route,), F32)])[None]
    p = {
        'norm_mix_w': norm_mix_w[l].reshape(1, 1, d), 'w_cols': w_cols,
        'gdn_conv_w': gdn_conv_w[l], 'alog_c': head_lanes(gdn_a_log[l], N_HEADS),
        'dtb_c': head_lanes(gdn_dt_bias[l], N_HEADS),
        'gdn_nw_rep': row(jnp.tile(gdn_norm_w[l], N_HEADS)),
        'rwkv_mu': row(rwkv_mu[l]),
        'rwkv_vecs': tuple(row(a) for a in (rwkv_w0[l], rwkv_a0[l], rwkv_k_k[l], rwkv_k_a[l], rwkv_r_k[l],
                                            rwkv_ln_w[l], rwkv_ln_b[l])),
        'rwkv_w_up': rwkv_w_up[l], 'rwkv_a_up': rwkv_a_up[l], 'rwkv_g_up': rwkv_g_up[l],
        'w_out_a': w_out[l][:D_MIX_HALF].astype(BF16), 'w_out_b': w_out[l][D_MIX_HALF:].astype(BF16),
        'norm_ffn_w': norm_ffn_w[l].reshape(1, 1, d), 'w_router': w_router, 'b_router': b_router,
        'w_gu': expert_w_gate_up[l].astype(BF16), 'w_down': expert_w_down[l].astype(BF16),
        'final_norm_w': final_norm_w.reshape(1, 1, d),
    }

    col = lambda a: a.reshape(-1, 1)
    lanes = lambda a: jnp.broadcast_to(col(a), (a.size, b_s))
    ps = {
        'w_rows': jnp.concatenate([w_bf[:, :beta0], w_bf[:, gdn_cols:], w_bf[:, beta0:gdn_cols]], axis=1).T,
        'gdn_conv_w_t': gdn_conv_w[l].T, 'alog_b': lanes(gdn_a_log[l]), 'dtb_b': lanes(gdn_dt_bias[l]),
        'gdn_nw_b': lanes(gdn_norm_w[l]), 'rwkv_mu_b': lanes(rwkv_mu[l]),
        'rwkv_vecs_t': jnp.concatenate([col(a) for a in (rwkv_w0[l], rwkv_a0[l], rwkv_k_k[l], rwkv_k_a[l],
                                                         rwkv_r_k[l], rwkv_ln_w[l], rwkv_ln_b[l], rwkv_ln_b[l])],
                                       axis=1),
        'w_up_t': rwkv_w_up[l].T, 'a_up_t': rwkv_a_up[l].T, 'g_up_t': rwkv_g_up[l].T,
    }

    mod = _ada(jnp.concatenate([c_prompt, c_sample], axis=0), ada_w[l], ada_b[l])
    mod_p = tuple(m.reshape(b_p, 1, d) for m in jnp.split(mod[:b_p], 6, axis=-1))
    mod_s = tuple(m.reshape(1, b_s, d) for m in jnp.split(mod[b_p:], 6, axis=-1))

    zc = jnp.zeros((b_p, CONV_W - 1, GDN_QKV), F32)
    zs = jnp.zeros((b_p, N_HEADS, HEAD_DIM, HEAD_DIM), F32)
    zsh = jnp.zeros((b_p, 1, RWKV_COLS), F32)
    y_p, conv_p, sg_p, shift_p, sr_p = _layer(x_prompt, mod_p, zc, zs, zsh, zs, p, bb=1, tt=PROJ_ROWS,
                                              moe_tile=(1, MOE_TILE_ROWS), bsz=SEQS_PER_STEP, chunk=CHUNK)

    y_s, conv_s, sg_s, shift_s, sr_s = _sample_layer(x_sample, mod_s, state_gdn_conv[l], state_gdn[l],
                                                     state_rwkv_shift[l], state_rwkv[l], p, ps)
    return (y_p, y_s, conv_p[None], sg_p[None], shift_p[None], sr_p[None],
            conv_s[None], sg_s[None], shift_s[None], sr_s[None])
```

```python
import jax
import jax.numpy as jnp
from jax import lax
from jax.experimental import pallas as pl
from jax.experimental.pallas import tpu as pltpu

F32 = jnp.float32
BF16 = jnp.bfloat16
HI = lax.Precision.HIGHEST

HEAD_DIM = 64
N_HEADS = 8
D_MIX_HALF = N_HEADS * HEAD_DIM
CONV_W = 4
LORA_W = 64
LORA_A = 64
LORA_G = 128
N_GROUPS = 4
EXP_PER_GROUP = 8
N_EXPERTS = N_GROUPS * EXP_PER_GROUP
D_EXPERT = 256
EPS = 1e-6
GN_EPS = HEAD_DIM * 1e-5

LANES = 128
SUBLANES = 8
BF16_ROWS = 16
GDN_QKV = 3 * D_MIX_HALF
RWKV_RKV = 3 * D_MIX_HALF
RWKV_COLS = RWKV_RKV + LORA_W + LORA_A + LORA_G
LORA_COLS = LORA_W + LORA_A + LORA_G

COL_RKV = GDN_QKV
COL_Z = COL_RKV + RWKV_RKV
COL_LORA = COL_Z + D_MIX_HALF
COL_BA = COL_LORA + LORA_COLS
N_COLS = COL_BA + LANES
CHUNK = 64

ADA_COLS_PER_STEP = 1536
PROJ_ROWS = 512
MOE_TILE_ROWS = 1024
SEQS_PER_STEP = 4

VMEM_LIMIT = 48 * 1024 * 1024
MOE_VMEM_LIMIT = 58 * 1024 * 1024


def _dot(a, b, prec=HI):
    return jnp.dot(a, b, preferred_element_type=F32, precision=prec)


def _bf(a):
    return a.astype(BF16)


def _mm(a, b):
    return jnp.dot(_bf(a), _bf(b), preferred_element_type=F32)


def _mm_nt(a, b):
    return lax.dot_general(_bf(a), _bf(b), (((1,), (1,)), ((), ())), preferred_element_type=F32)


def _mm_tn(a, b):
    return lax.dot_general(_bf(a), _bf(b), (((0,), (0,)), ((), ())), preferred_element_type=F32)


def _split2(a):
    hi = a.astype(BF16)
    return hi, (a - hi.astype(F32)).astype(BF16)


def _sigmoid(x):
    return 1.0 / (1.0 + jnp.exp(-x))


def _silu(x):
    return x * _sigmoid(x)


def _softplus(x):
    return jnp.maximum(x, 0.0) + jnp.log1p(jnp.exp(-jnp.abs(x)))


def _iota2(shape, dim):
    return lax.broadcasted_iota(jnp.int32, shape, dim)


def _head_block_ones():
    r = _iota2((LANES, LANES), 0)
    c = _iota2((LANES, LANES), 1)
    sh = HEAD_DIM.bit_length() - 1
    return jnp.where((r >> sh) == (c >> sh), 1.0, 0.0).astype(F32)


def _head_sums(x):
    ones = _bf(_head_block_ones())
    xb = _bf(x)
    parts = [jnp.dot(xb[:, p * LANES:(p + 1) * LANES], ones, preferred_element_type=F32)
             for p in range(x.shape[1] // LANES)]
    return jnp.concatenate(parts, axis=1)


def _expand_heads(x, base, terms):
    lane = _iota2((LANES, D_MIX_HALF), 0)
    head = _iota2((LANES, D_MIX_HALF), 1) >> (HEAD_DIM.bit_length() - 1)
    select = jnp.where(lane == base + head, 1.0, 0.0).astype(BF16)
    out = None
    rem = x
    for _ in range(terms):
        piece = rem.astype(BF16)
        part = jnp.dot(piece, select, preferred_element_type=F32)
        out = part if out is None else out + part
        rem = rem - piece.astype(F32)
    return out


def _cumsum_rows(x):
    n = x.shape[0]
    r = _iota2((n, n), 0)
    c = _iota2((n, n), 1)
    tri = jnp.where(r >= c, 1.0, 0.0).astype(BF16)
    x1 = x.astype(BF16)
    rem = x - x1.astype(F32)
    x2 = rem.astype(BF16)
    x3 = (rem - x2.astype(F32)).astype(BF16)
    return (jnp.dot(tri, x1, preferred_element_type=F32) + jnp.dot(tri, x2, preferred_element_type=F32)
            + jnp.dot(tri, x3, preferred_element_type=F32))


def _unit_lower_inverses(lows, n, mm):
    r = _iota2((n, n), 0)
    c = _iota2((n, n), 1)
    eye = jnp.where(r == c, 1.0, 0.0)
    pair = (r >> 1) == (c >> 1)
    invs = [eye - jnp.where(pair, low, 0.0) for low in lows]
    s = 2
    while s < n:
        sh = s.bit_length()
        sel = ((r >> sh) == (c >> sh)) & ((r & (2 * s - 1)) >= s) & ((c & (2 * s - 1)) < s)
        prods = [mm(jnp.where(sel, low, 0.0), inv) for low, inv in zip(lows, invs)]
        invs = [inv - mm(inv, prod) for inv, prod in zip(invs, prods)]
        s *= 2
    return invs


def _ada_kernel(c_ref, w_ref, b_ref, o_ref):
    o_ref[...] = _dot(_silu(c_ref[...]), w_ref[...]) + b_ref[...]


def _ada(c_all, ada_w, ada_b):
    n, d = c_all.shape
    nout = ada_w.shape[1]
    tn = ADA_COLS_PER_STEP
    return pl.pallas_call(
        _ada_kernel,
        grid=(nout // tn,),
        in_specs=[pl.BlockSpec((n, d), lambda j: (0, 0)),
                  pl.BlockSpec((d, tn), lambda j: (0, j)),
                  pl.BlockSpec((1, tn), lambda j: (0, j))],
        out_specs=pl.BlockSpec((n, tn), lambda j: (0, j)),
        out_shape=jax.ShapeDtypeStruct((n, nout), F32),
        compiler_params=pltpu.CompilerParams(dimension_semantics=("arbitrary",), vmem_limit_bytes=VMEM_LIMIT),
        name="ada",
    )(c_all, ada_w, ada_b.reshape(1, nout))


def _modulated_norm(x, nw, sc, sh):
    ms = jnp.mean(x * x, axis=-1, keepdims=True)
    return (x * lax.rsqrt(ms + EPS) * nw) * (1.0 + sc) + sh


def _inproj_kernel(x_ref, sc_ref, sh_ref, nw_ref, w_ref, o_ref):
    bb, tt, d = x_ref.shape
    h = _modulated_norm(x_ref[...], nw_ref[...], sc_ref[...], sh_ref[...])
    h = h.reshape(bb * tt, d).astype(BF16)
    o = _mm_nt(h, w_ref[...])
    o_ref[...] = o.reshape(bb, tt, o.shape[-1])


def _inproj(x, sc, sh, nw, w_rows, bb, tt):
    b, t, d = x.shape
    nc = w_rows.shape[0]
    return pl.pallas_call(
        _inproj_kernel,
        grid=(b // bb, t // tt),
        in_specs=[pl.BlockSpec((bb, tt, d), lambda i, j: (i, j, 0)),
                  pl.BlockSpec((bb, 1, d), lambda i, j: (i, 0, 0)),
                  pl.BlockSpec((bb, 1, d), lambda i, j: (i, 0, 0)),
                  pl.BlockSpec((1, 1, d), lambda i, j: (0, 0, 0)),
                  pl.BlockSpec((nc, d), lambda i, j: (0, 0))],
        out_specs=pl.BlockSpec((bb, tt, nc), lambda i, j: (i, j, 0)),
        out_shape=jax.ShapeDtypeStruct((b, t, nc), F32),
        compiler_params=pltpu.CompilerParams(dimension_semantics=("arbitrary",) * 2, vmem_limit_bytes=VMEM_LIMIT),
        name="inproj",
    )(x, sc, sh, nw, w_rows)


def _gdn_kernel(qkv_ref, z_ref, ba_ref, c0_ref, cw_ref, alog_ref, dtb_ref, nw_ref, s0_ref, o_ref, s_ref, hist_scr):
    bsz, chunk, _ = qkv_ref.shape
    hist_lo = SUBLANES - (CONV_W - 1)

    @pl.when(pl.program_id(1) == 0)
    def _():
        s_ref[...] = s0_ref[...]
        hist_scr[:, hist_lo:SUBLANES, :] = c0_ref[...]

    row = _iota2((chunk, chunk), 0)
    col = _iota2((chunk, chunk), 1)
    causal = row >= col
    strict = row > col

    def prep(bi):
        u = qkv_ref[bi]
        hist_scr[bi, SUBLANES:SUBLANES + chunk, :] = u
        y = u * cw_ref[CONV_W - 1:CONV_W, :]
        for i in range(CONV_W - 1):
            y = y + hist_scr[bi, hist_lo + i:hist_lo + i + chunk, :] * cw_ref[i:i + 1, :]
        hist_scr[bi, hist_lo:SUBLANES, :] = hist_scr[bi, hist_lo + chunk:SUBLANES + chunk, :]
        qkv = _silu(y)
        q, k, v = (qkv[:, i * D_MIX_HALF:(i + 1) * D_MIX_HALF] for i in range(3))
        q = q * lax.rsqrt(_head_sums(q * q) + EPS) * (HEAD_DIM ** -0.5)
        k = k * lax.rsqrt(_head_sums(k * k) + EPS)
        ba = ba_ref[bi]
        beta = _expand_heads(_sigmoid(ba), 0, 2)
        g = -jnp.exp(alog_ref[...]) * _softplus(ba + dtb_ref[...])
        gcum = _expand_heads(_cumsum_rows(g), N_HEADS, 3)
        eg = jnp.exp(gcum)
        g_last = gcum[chunk - 1:chunk, :]
        return dict(q=q, k=k, beta=beta, gcum=gcum, rhs_v=beta * v, rhs_k=beta * eg * k, q_dec=eg * q,
                    k_dec=jnp.exp(g_last - gcum) * k, eg_last=jnp.exp(g_last))

    pre = [prep(bi) for bi in range(bsz)]
    chains = [(bi, h) for bi in range(bsz) for h in range(N_HEADS)]
    n = range(len(chains))

    def head(name, i, width=HEAD_DIM):
        bi, h = chains[i]
        return pre[bi][name][:, h * HEAD_DIM:h * HEAD_DIM + width]

    g_is = [head('gcum', i, chunk) for i in n]
    decays = [jnp.where(causal, jnp.exp(jnp.minimum(g_i - g_i.T, 0.0)), 0.0) for g_i in g_is]
    qk_kks = [_mm_nt(jnp.concatenate([head('k', i), head('q', i)], axis=0), head('k', i)) for i in n]
    lows = [jnp.where(strict, head('beta', i, chunk) * decays[i] * qk_kks[i][:chunk], 0.0) for i in n]
    t_invs = _unit_lower_inverses(lows, chunk, _mm)
    sols = [_mm(t_invs[i], jnp.concatenate([head('rhs_v', i), head('rhs_k', i)], axis=1)) for i in n]
    states = [s_ref[bi, h] for bi, h in chains]
    wss = [_mm(jnp.concatenate([sols[i][:, HEAD_DIM:], head('q_dec', i)], axis=0), states[i])
           for i in n]
    u_news = [sols[i][:, :HEAD_DIM] - wss[i][:chunk] for i in n]
    outs = [wss[i][chunk:] + _mm(qk_kks[i][chunk:] * decays[i], u_news[i]) for i in n]
    s_news = [head('eg_last', i) * states[i] + _mm_tn(head('k_dec', i), u_news[i]) for i in n]
    for i, (bi, h) in enumerate(chains):
        s_ref[bi, h] = s_news[i]
    for bi in range(bsz):
        o = jnp.concatenate(outs[bi * N_HEADS:(bi + 1) * N_HEADS], axis=1)
        o = o * lax.rsqrt(_head_sums(o * o) * (1.0 / HEAD_DIM) + EPS) * nw_ref[...]
        o_ref[bi] = o * _silu(z_ref[bi])


def _gdn(cols, conv0, s0, conv_w, alog_c, dtb_c, nw_rep, bsz, chunk):
    b, t, _ = cols.shape

    def col_spec(width, start):
        return pl.BlockSpec((bsz, chunk, width), lambda i, c: (i, c, start // width))

    state_spec = pl.BlockSpec((bsz, N_HEADS, HEAD_DIM, HEAD_DIM), lambda i, c: (i, 0, 0, 0))
    return pl.pallas_call(
        _gdn_kernel,
        grid=(b // bsz, t // chunk),
        in_specs=[col_spec(GDN_QKV, 0), col_spec(D_MIX_HALF, COL_Z), col_spec(LANES, COL_BA),
                  pl.BlockSpec((bsz, CONV_W - 1, GDN_QKV), lambda i, c: (i, 0, 0)),
                  pl.BlockSpec((CONV_W, GDN_QKV), lambda i, c: (0, 0)),
                  pl.BlockSpec((1, LANES), lambda i, c: (0, 0)), pl.BlockSpec((1, LANES), lambda i, c: (0, 0)),
                  pl.BlockSpec((1, D_MIX_HALF), lambda i, c: (0, 0)), state_spec],
        out_specs=[pl.BlockSpec((bsz, chunk, D_MIX_HALF), lambda i, c: (i, c, 0)), state_spec],
        out_shape=[jax.ShapeDtypeStruct((b, t, D_MIX_HALF), F32),
                   jax.ShapeDtypeStruct((b, N_HEADS, HEAD_DIM, HEAD_DIM), F32)],
        scratch_shapes=[pltpu.VMEM((bsz, SUBLANES + chunk, GDN_QKV), F32)],
        compiler_params=pltpu.CompilerParams(dimension_semantics=("arbitrary",) * 2, vmem_limit_bytes=VMEM_LIMIT),
        name="gdn",
    )(cols, cols, cols, conv0, conv_w, alog_c, dtb_c, nw_rep, s0)


def _rwkv_kernel(rkv_ref, l_ref, p_ref, pl_ref, mu_ref, mul_ref, w0_ref, a0_ref, kk_ref, ka_ref, rk_ref,
                 lnw_ref, lnb_ref, wup_ref, aup_ref, gup_ref, s0_ref, o_ref, s_ref, hist_scr, histl_scr):
    bsz, chunk, _ = rkv_ref.shape
    prev_row = SUBLANES - 1

    @pl.when(pl.program_id(1) == 0)
    def _():
        s_ref[...] = s0_ref[...]
        hist_scr[:, prev_row:SUBLANES, :] = p_ref[...]
        histl_scr[:, prev_row:SUBLANES, :] = pl_ref[...]

    row = _iota2((chunk, chunk), 0)
    col = _iota2((chunk, chunk), 1)
    causal = row >= col
    strict = row > col

    def shifted(bi, u_ref, scr, m_ref):
        u = u_ref[bi]
        scr[bi, SUBLANES:SUBLANES + chunk, :] = u
        prev = scr[bi, prev_row:prev_row + chunk, :]
        scr[bi, prev_row:SUBLANES, :] = scr[bi, prev_row + chunk:SUBLANES + chunk, :]
        return u + (prev - u) * m_ref[...]

    def prep(bi):
        rkv = shifted(bi, rkv_ref, hist_scr, mu_ref)
        r, k, v = (rkv[:, i * D_MIX_HALF:(i + 1) * D_MIX_HALF] for i in range(3))
        xl = shifted(bi, l_ref, histl_scr, mul_ref)
        dw = xl[:, :LORA_W]
        da = xl[:, LORA_W:LORA_W + LORA_A]
        dg = xl[:, LORA_W + LORA_A:]
        w_log = -_softplus(-(w0_ref[...] + _mm(jnp.tanh(dw), wup_ref[...]))) - 0.5
        lw = -jnp.exp(w_log)
        a = _sigmoid(a0_ref[...] + _mm(da, aup_ref[...]))
        gate = _mm(_sigmoid(dg), gup_ref[...])
        kk = k * kk_ref[...]
        kk = kk * lax.rsqrt(_head_sums(kk * kk) + EPS)
        k = k * (1.0 + (a - 1.0) * ka_ref[...])
        cum = _cumsum_rows(lw)
        e_out = jnp.exp(-cum)
        cum_last = cum[chunk - 1:chunk, :]
        e_rest = jnp.exp(cum_last - cum)
        return dict(r=r, k=k, v=v, gate=gate, a_t=-kk * jnp.exp(cum - lw), b_t=kk * a * e_out, k_t=k * e_out,
                    r_t=r * jnp.exp(cum), b_c=kk * a * e_rest, k_c=k * e_rest, e_last=jnp.exp(cum_last))

    pre = [prep(bi) for bi in range(bsz)]
    chains = [(bi, h) for bi in range(bsz) for h in range(N_HEADS)]
    n = range(len(chains))

    def head(name, i):
        bi, h = chains[i]
        return pre[bi][name][:, h * HEAD_DIM:(h + 1) * HEAD_DIM]

    ars = [jnp.concatenate([head('a_t', i), head('r_t', i)], axis=0) for i in n]
    bks = [jnp.concatenate([head('b_t', i), head('k_t', i)], axis=0) for i in n]
    ms = [_mm_nt(ars[i], bks[i]) for i in n]
    t_invs = _unit_lower_inverses([jnp.where(strict, -m[:chunk, :chunk], 0.0) for m in ms], chunk, _mm)
    states = [s_ref[bi, h] for bi, h in chains]
    arss = [_mm_nt(ars[i], states[i]) for i in n]
    vhs = [head('v', i) for i in n]
    akvs = [_mm(jnp.where(strict, ms[i][:chunk, chunk:], 0.0), vhs[i]) for i in n]
    us = [_mm(t_invs[i], arss[i][:chunk] + akvs[i]) for i in n]
    uvs = [jnp.concatenate([us[i], vhs[i]], axis=0) for i in n]
    causal2 = _iota2((chunk, 2 * chunk), 0) >= (_iota2((chunk, 2 * chunk), 1) & (chunk - 1))
    ys = [arss[i][chunk:] + _mm(jnp.where(causal2, ms[i][chunk:], 0.0), uvs[i]) for i in n]
    s_news = [states[i] * head('e_last', i)
              + _mm_tn(uvs[i], jnp.concatenate([head('b_c', i), head('k_c', i)], axis=0)) for i in n]
    for i, (bi, h) in enumerate(chains):
        s_ref[bi, h] = s_news[i]
    for bi in range(bsz):
        p = pre[bi]
        y = jnp.concatenate(ys[bi * N_HEADS:(bi + 1) * N_HEADS], axis=1)
        mean = _head_sums(y) * (1.0 / HEAD_DIM)
        dy = y - mean
        var = _head_sums(dy * dy) * (1.0 / HEAD_DIM)
        y = dy * lax.rsqrt(var + GN_EPS) * lnw_ref[...] + lnb_ref[...]
        bonus = _head_sums(p['r'] * p['k'] * rk_ref[...]) * p['v']
        o_ref[bi] = (y + bonus) * p['gate']


def _rwkv(cols, shift0, s0, mu, vecs, w_up, a_up, g_up, bsz, chunk):
    b, t, _ = cols.shape

    def full2(shape):
        return pl.BlockSpec(shape, lambda i, c: (0, 0))

    state_spec = pl.BlockSpec((bsz, N_HEADS, HEAD_DIM, HEAD_DIM), lambda i, c: (i, 0, 0, 0))
    return pl.pallas_call(
        _rwkv_kernel,
        grid=(b // bsz, t // chunk),
        in_specs=[pl.BlockSpec((bsz, chunk, RWKV_RKV), lambda i, c: (i, c, COL_RKV // RWKV_RKV)),
                  pl.BlockSpec((bsz, chunk, LORA_COLS), lambda i, c: (i, c, COL_LORA // LORA_COLS)),
                  pl.BlockSpec((bsz, 1, RWKV_RKV), lambda i, c: (i, 0, 0)),
                  pl.BlockSpec((bsz, 1, LORA_COLS), lambda i, c: (i, 0, RWKV_RKV // LORA_COLS)),
                  pl.BlockSpec((1, RWKV_RKV), lambda i, c: (0, 0)),
                  pl.BlockSpec((1, LORA_COLS), lambda i, c: (0, RWKV_RKV // LORA_COLS))]
                 + [full2((1, D_MIX_HALF))] * 7
                 + [full2((LORA_W, D_MIX_HALF)), full2((LORA_A, D_MIX_HALF)), full2((LORA_G, D_MIX_HALF)),
                    state_spec],
        out_specs=[pl.BlockSpec((bsz, chunk, D_MIX_HALF), lambda i, c: (i, c, 0)), state_spec],
        out_shape=[jax.ShapeDtypeStruct((b, t, D_MIX_HALF), F32),
                   jax.ShapeDtypeStruct((b, N_HEADS, HEAD_DIM, HEAD_DIM), F32)],
        scratch_shapes=[pltpu.VMEM((bsz, SUBLANES + chunk, RWKV_RKV), F32),
                        pltpu.VMEM((bsz, SUBLANES + chunk, LORA_COLS), F32)],
        compiler_params=pltpu.CompilerParams(dimension_semantics=("arbitrary",) * 2, vmem_limit_bytes=VMEM_LIMIT),
        name="rwkv",
    )(cols, cols, shift0, shift0, mu, mu, *vecs, w_up, a_up, g_up, s0)


def _mod_spec(m, bb, tt):
    d = m.shape[-1]
    if m.shape[1] == 1:
        return pl.BlockSpec((bb, 1, d), lambda i, j, *_: (i, 0, 0))
    return pl.BlockSpec((1, tt, d), lambda i, j, *_: (0, j, 0))


def _outproj_kernel(oa_ref, ob_ref, x_ref, gt_ref, sc_ref, sh_ref, nw_ref, wa_ref, wb_ref, wr_ref, br_ref,
                    x1_ref, h_ref, wt_ref):
    bb, tt, d = x_ref.shape
    n = bb * tt
    if len(oa_ref.shape) == 2:
        mixed = _mm_tn(oa_ref[...], wa_ref[...]) + _mm_tn(ob_ref[...], wb_ref[...])
    else:
        oa = oa_ref[...].reshape(n, D_MIX_HALF).astype(BF16)
        ob = ob_ref[...].reshape(n, D_MIX_HALF).astype(BF16)
        mixed = (jnp.dot(oa, wa_ref[...], preferred_element_type=F32)
                 + jnp.dot(ob, wb_ref[...], preferred_element_type=F32))
    x1 = x_ref[...] + gt_ref[...] * mixed.reshape(bb, tt, d)
    x1_ref[...] = x1
    h = _modulated_norm(x1, nw_ref[...], sc_ref[...], sh_ref[...]).reshape(n, d)
    h_ref[...] = h.reshape(bb, tt, d).astype(BF16)

    h_hi, h_lo = _split2(h)
    hw = jnp.dot(h_hi, wr_ref[...], preferred_element_type=F32)
    logits = (hw[:, :LANES] + hw[:, LANES:] + jnp.dot(h_lo, wr_ref[:, :LANES], preferred_element_type=F32)
              + br_ref[...])
    lane_i = _iota2((n, LANES), 1)
    lane = lane_i.astype(F32)
    lane_grp = (lane_i >> (EXP_PER_GROUP.bit_length() - 1)).astype(F32)
    neg = jnp.float32(-jnp.inf)
    big = jnp.float32(LANES)
    is_grp = (lane_i >= N_EXPERTS) & (lane_i < N_EXPERTS + N_GROUPS)
    gl = jnp.where(is_grp, logits, neg)
    gmax = jnp.max(gl, axis=-1, keepdims=True)
    grp = jnp.min(jnp.where(gl == gmax, lane, big), axis=-1, keepdims=True) - N_EXPERTS
    g_prob = 1.0 / jnp.sum(jnp.where(is_grp, jnp.exp(gl - gmax), 0.0), axis=-1, keepdims=True)
    in_grp = (lane_i < N_EXPERTS) & (lane_grp == grp)
    el = jnp.where(in_grp, logits, neg)
    m1 = jnp.max(el, axis=-1, keepdims=True)
    i1 = jnp.min(jnp.where(el == m1, lane, big), axis=-1, keepdims=True)
    el2 = jnp.where(lane == i1, neg, el)
    m2 = jnp.max(el2, axis=-1, keepdims=True)
    i2 = jnp.min(jnp.where(el2 == m2, lane, big), axis=-1, keepdims=True)
    e2 = jnp.exp(m2 - m1)
    p1 = g_prob / (1.0 + e2)
    p2 = g_prob * e2 / (1.0 + e2)
    wt = jnp.where(lane == i1, p1, 0.0) + jnp.where(lane == i2, p2, 0.0)
    wt_ref[...] = wt.reshape(bb, tt, LANES)


def _outproj(oa, ob, x, gt, sc, sh, nw, wa, wb, wr, br, bb, tt):
    b, t, d = x.shape
    grid = (b // bb, t // tt)

    def tok_spec(w):
        return pl.BlockSpec((bb, tt, w), lambda i, j: (i, j, 0))

    def full2(shape):
        return pl.BlockSpec(shape, lambda i, j: (0, 0))

    mod_spec = _mod_spec(gt, bb, tt)
    if oa.ndim == 2:
        assert grid == (1, 1)
        mix_spec = full2(oa.shape)
    else:
        mix_spec = tok_spec(D_MIX_HALF)
    return pl.pallas_call(
        _outproj_kernel,
        grid=grid,
        in_specs=[mix_spec, mix_spec, tok_spec(d), mod_spec, mod_spec, mod_spec,
                  pl.BlockSpec((1, 1, d), lambda i, j: (0, 0, 0)),
                  full2((D_MIX_HALF, d)), full2((D_MIX_HALF, d)), full2((d, 2 * LANES)), full2((1, LANES))],
        out_specs=[tok_spec(d), tok_spec(d), tok_spec(LANES)],
        out_shape=[jax.ShapeDtypeStruct((b, t, d), F32), jax.ShapeDtypeStruct((b, t, d), BF16),
                   jax.ShapeDtypeStruct((b, t, LANES), F32)],
        compiler_params=pltpu.CompilerParams(dimension_semantics=("arbitrary",) * 2, vmem_limit_bytes=VMEM_LIMIT),
        name="outproj",
    )(oa, ob, x, gt, sc, sh, nw, wa, wb, wr, br)


MOE_ROWS = 128
MOE_EXPERTS_PER_STEP = 4


def _moe_kernel(h_ref, wt_ref, x1_ref, gt_ref, fw_ref, wgu_ref, wd_ref, y_ref,
                xg_scr, gtw_scr, yw_scr, seg_scr):
    bb, tt, d = x1_ref.shape
    n = bb * tt
    npos = 2 * n
    e = pl.program_id(2)

    @pl.when(e == 0)
    def _():
        wt_t = wt_ref[...].reshape(n, LANES).T
        member = wt_t > 0.0
        ones = jnp.where(member, 1.0, 0.0)
        upper = jnp.where(_iota2((n, n), 0) < _iota2((n, n), 1), 1.0, 0.0).astype(BF16)
        rank = jnp.dot(_bf(ones), upper, preferred_element_type=F32)
        cnt = jnp.broadcast_to(jnp.sum(ones, axis=1, keepdims=True), (LANES, LANES))
        lower = jnp.where(_iota2((LANES, LANES), 0) > _iota2((LANES, LANES), 1), 1.0, 0.0)
        off = _dot(lower, cnt)
        first_i = off.astype(jnp.int32)
        last_i = (off + cnt).astype(jnp.int32)
        for x in range(N_EXPERTS):
            seg_scr[x] = first_i[x, 0]
            seg_scr[N_EXPERTS + x] = last_i[x, 0]
        pos = off[:, :1] + rank
        pos1 = jnp.min(jnp.where(member, pos, jnp.float32(4 * n)), axis=0, keepdims=True)
        pos2 = jnp.max(jnp.where(member, pos, -1.0), axis=0, keepdims=True)
        w1 = jnp.sum(jnp.where(member & (pos == pos1), wt_t, 0.0), axis=0, keepdims=True)
        w2 = jnp.sum(jnp.where(member & (pos == pos2), wt_t, 0.0), axis=0, keepdims=True)
        w2 = jnp.where(pos2 != pos1, w2, 0.0)

        def onehot_block(b, carry):
            p = (b * MOE_ROWS + _iota2((MOE_ROWS, n), 0)).astype(F32)
            rows = pl.ds(pl.multiple_of(b * MOE_ROWS, MOE_ROWS), MOE_ROWS)
            yw_scr[rows, :n] = jnp.where((pos1 == p) | (pos2 == p), 1.0, 0.0).astype(BF16)
            return carry

        lax.fori_loop(0, npos // MOE_ROWS, onehot_block, 0)
        xg_scr[:npos, :] = jnp.dot(yw_scr[:npos, :n], h_ref[...].reshape(n, d),
                                   preferred_element_type=F32).astype(BF16)
        xg_scr[npos:, :] = jnp.zeros((MOE_ROWS, d), BF16)
        z_t = jnp.concatenate([pos1, pos2, w1, w2, jnp.zeros((LANES - 4, n), F32)], axis=0).T
        p1c, p2c, w1c, w2c = (z_t[:, i:i + 1] for i in range(4))
        for b in range(npos // LANES):
            p = (b * LANES + _iota2((n, LANES), 1)).astype(F32)
            gtw_scr[:, b * LANES:(b + 1) * LANES] = (jnp.where(p1c == p, w1c, 0.0)
                                                     + jnp.where(p2c == p, w2c, 0.0)).astype(BF16)
        yw_scr[...] = jnp.zeros_like(yw_scr)

    es = range(wgu_ref.shape[0])
    firsts = [seg_scr[e * len(es) + j] for j in es]
    lasts = [seg_scr[N_EXPERTS + e * len(es) + j] for j in es]
    starts = [(f // BF16_ROWS) * BF16_ROWS for f in firsts]
    windows = [(lasts[j] - starts[j] + MOE_ROWS - 1) // MOE_ROWS for j in es]

    def window_rows(r0):
        return pl.ds(pl.multiple_of(r0, BF16_ROWS), MOE_ROWS)

    def store_rows(j, r0, y):
        p = r0 + _iota2((MOE_ROWS, 1), 0)
        rows = window_rows(r0)
        yw_scr[rows, :] = jnp.where((p >= firsts[j]) & (p < lasts[j]), y, yw_scr[rows, :])

    gus = [jnp.dot(xg_scr[window_rows(starts[j]), :], wgu_ref[j], preferred_element_type=F32) for j in es]
    acts = [(_silu(gu[:, :D_EXPERT]) * gu[:, D_EXPERT:]).astype(BF16) for gu in gus]
    ys = [jnp.dot(acts[j], wd_ref[j], preferred_element_type=F32).astype(BF16) for j in es]
    for j in es:
        store_rows(j, starts[j], ys[j])

    most = windows[0]
    for j in es[1:]:
        most = jnp.maximum(most, windows[j])

    @pl.when(most > 1)
    def _():
        for j in es:
            def more_windows(w, carry, j=j):
                r0 = starts[j] + w * MOE_ROWS
                gu = jnp.dot(xg_scr[window_rows(r0), :], wgu_ref[j], preferred_element_type=F32)
                act = (_silu(gu[:, :D_EXPERT]) * gu[:, D_EXPERT:]).astype(BF16)
                store_rows(j, r0, jnp.dot(act, wd_ref[j], preferred_element_type=F32).astype(BF16))
                return carry

            lax.fori_loop(1, windows[j], more_windows, 0)

    @pl.when(e == pl.num_programs(2) - 1)
    def _():
        moe = jnp.dot(gtw_scr[...], yw_scr[:npos, :], preferred_element_type=F32)
        x2 = x1_ref[...] + gt_ref[...] * moe.reshape(bb, tt, d)
        ms = jnp.mean(x2 * x2, axis=-1, keepdims=True)
        y_ref[...] = x2 * lax.rsqrt(ms + EPS) * fw_ref[...]


def _moe(h, wt, x1, gt, fw, w_gu, w_down, bb, tt):
    b, t, d = x1.shape
    es = MOE_EXPERTS_PER_STEP
    grid = (b // bb, t // tt, N_EXPERTS // es)
    npos = 2 * bb * tt

    def tok_spec(w):
        return pl.BlockSpec((bb, tt, w), lambda i, j, e: (i, j, 0))

    return pl.pallas_call(
        _moe_kernel,
        grid=grid,
        in_specs=[tok_spec(d), tok_spec(LANES), tok_spec(d),
                  _mod_spec(gt, bb, tt),
                  pl.BlockSpec((1, 1, d), lambda i, j, e: (0, 0, 0)),
                  pl.BlockSpec((es, d, 2 * D_EXPERT), lambda i, j, e: (e, 0, 0)),
                  pl.BlockSpec((es, D_EXPERT, d), lambda i, j, e: (e, 0, 0))],
        out_specs=tok_spec(d),
        out_shape=jax.ShapeDtypeStruct((b, t, d), F32),
        scratch_shapes=[pltpu.VMEM((npos + MOE_ROWS, d), BF16), pltpu.VMEM((bb * tt, npos), BF16),
                        pltpu.VMEM((npos + MOE_ROWS, d), BF16), pltpu.SMEM((2 * N_EXPERTS,), jnp.int32)],
        compiler_params=pltpu.CompilerParams(dimension_semantics=("arbitrary",) * 3,
                                             vmem_limit_bytes=MOE_VMEM_LIMIT),
        name="moe",
    )(h, wt, x1, gt, fw, w_gu, w_down)


ROW_Z = GDN_QKV
ROW_R = ROW_Z + D_MIX_HALF
ROW_LORA = ROW_R + RWKV_RKV
ROW_BETA = ROW_LORA + LORA_COLS
ROW_A = ROW_BETA + N_HEADS
N_ROWS = ROW_A + N_HEADS


def _inproj_t_kernel(x_ref, sc_ref, sh_ref, nw_ref, w_ref, o_ref):
    t, b, d = x_ref.shape
    h = _modulated_norm(x_ref[...], nw_ref[...], sc_ref[...], sh_ref[...]).reshape(t * b, d)
    o_ref[...] = _mm_nt(w_ref[...], h)


def _inproj_t(x, sc, sh, nw, w_rows):
    t, b, d = x.shape
    nr = w_rows.shape[0]

    def full(shape):
        return pl.BlockSpec(shape, lambda i: (0,) * len(shape))

    return pl.pallas_call(
        _inproj_t_kernel,
        grid=(1,),
        in_specs=[full((t, b, d)), full((1, b, d)), full((1, b, d)), full((1, 1, d)), full((nr, d))],
        out_specs=full((nr, t * b)),
        out_shape=jax.ShapeDtypeStruct((nr, t * b), F32),
        compiler_params=pltpu.CompilerParams(dimension_semantics=("arbitrary",), vmem_limit_bytes=VMEM_LIMIT),
        name="inproj_t",
    )(x, sc, sh, nw, w_rows)


def _store_state_rows(s_scr, sout_ref):
    for j in range(HEAD_DIM // 2):
        pair = jnp.concatenate([s_scr[2 * j], s_scr[2 * j + 1]], axis=0)
        sout_ref[:, j * LANES:(j + 1) * LANES] = pair.T


def _gdn_s_kernel(q_ref, k_ref, v_ref, z_ref, b_ref, a_ref, cq_ref, ck_ref, cv_ref, wq_ref, wk_ref, wv_ref,
                  alog_ref, dtb_ref, nw_ref, s0_ref, o_ref, sout_ref, s_scr, kq_scr):
    nb = s0_ref.shape[-1]
    nt = q_ref.shape[1] // nb
    h = pl.program_id(0)

    def tok(ref, t):
        return ref[:, t * nb:(t + 1) * nb]

    def conv(u_ref, c_ref, w_ref):
        full = [c_ref[i] for i in range(CONV_W - 1)] + [tok(u_ref, t) for t in range(nt)]
        taps = [w_ref[:, i:i + 1] for i in range(CONV_W)]
        outs = []
        for t in range(nt):
            y = full[t + CONV_W - 1] * taps[CONV_W - 1]
            for i in range(CONV_W - 1):
                y = y + full[t + i] * taps[i]
            outs.append(_silu(y))
        return outs

    qs = conv(q_ref, cq_ref, wq_ref)
    ks = conv(k_ref, ck_ref, wk_ref)
    vs = conv(v_ref, cv_ref, wv_ref)
    neg_rate = -jnp.exp(alog_ref[pl.ds(h, 1), :])
    dtb = dtb_ref[pl.ds(h, 1), :]
    beta_in = b_ref[pl.ds(h, 1), :]
    a_in = a_ref[pl.ds(h, 1), :]
    s_scr[...] = s0_ref[0]
    zero = jnp.zeros((HEAD_DIM, nb), F32)
    for t in range(nt):
        q = qs[t] * lax.rsqrt(jnp.sum(qs[t] * qs[t], axis=0, keepdims=True) + EPS) * (HEAD_DIM ** -0.5)
        k = ks[t] * lax.rsqrt(jnp.sum(ks[t] * ks[t], axis=0, keepdims=True) + EPS)
        beta = _sigmoid(beta_in[:, t * nb:(t + 1) * nb])
        decay = jnp.exp(neg_rate * _softplus(a_in[:, t * nb:(t + 1) * nb] + dtb))
        kq_scr[0] = k
        kq_scr[1] = q

        def decay_and_project(i, acc):
            row = s_scr[i] * decay
            s_scr[i] = row
            return acc + kq_scr[0, pl.ds(i, 1), :] * row

        k_s = lax.fori_loop(0, HEAD_DIM, decay_and_project, zero, unroll=8)
        u = beta * (vs[t] - k_s)

        def update_and_read(i, acc):
            row = s_scr[i] + kq_scr[0, pl.ds(i, 1), :] * u
            s_scr[i] = row
            return acc + kq_scr[1, pl.ds(i, 1), :] * row

        o = lax.fori_loop(0, HEAD_DIM, update_and_read, zero, unroll=8)
        o = o * lax.rsqrt(jnp.mean(o * o, axis=0, keepdims=True) + EPS) * nw_ref[...]
        o_ref[:, t * nb:(t + 1) * nb] = o * _silu(tok(z_ref, t))
    _store_state_rows(s_scr, sout_ref)


def _gdn_s(cols_t, conv_t, s_t, cw_t, alog_b, dtb_b, nw_b):
    ntok = cols_t.shape[1]
    nb = s_t.shape[-1]

    def head_rows(base):
        return pl.BlockSpec((HEAD_DIM, ntok), lambda h: (base // HEAD_DIM + h, 0))

    def conv_rows(base):
        return pl.BlockSpec((CONV_W - 1, HEAD_DIM, nb), lambda h: (0, base // HEAD_DIM + h, 0))

    def tap_rows(base):
        return pl.BlockSpec((HEAD_DIM, CONV_W), lambda h: (base // HEAD_DIM + h, 0))

    def full2(shape):
        return pl.BlockSpec(shape, lambda h: (0, 0))

    return pl.pallas_call(
        _gdn_s_kernel,
        grid=(N_HEADS,),
        in_specs=[head_rows(0), head_rows(D_MIX_HALF), head_rows(2 * D_MIX_HALF), head_rows(ROW_Z),
                  pl.BlockSpec((N_HEADS, ntok), lambda h: (ROW_BETA // N_HEADS, 0)),
                  pl.BlockSpec((N_HEADS, ntok), lambda h: (ROW_A // N_HEADS, 0)),
                  conv_rows(0), conv_rows(D_MIX_HALF), conv_rows(2 * D_MIX_HALF),
                  tap_rows(0), tap_rows(D_MIX_HALF), tap_rows(2 * D_MIX_HALF),
                  full2((N_HEADS, nb)), full2((N_HEADS, nb)), full2((HEAD_DIM, nb)),
                  pl.BlockSpec((1, HEAD_DIM, HEAD_DIM, nb), lambda h: (h, 0, 0, 0))],
        out_specs=[pl.BlockSpec((HEAD_DIM, ntok), lambda h: (h, 0)),
                   pl.BlockSpec((nb, HEAD_DIM * HEAD_DIM), lambda h: (0, h))],
        out_shape=[jax.ShapeDtypeStruct((D_MIX_HALF, ntok), F32),
                   jax.ShapeDtypeStruct((nb, N_HEADS * HEAD_DIM * HEAD_DIM), F32)],
        scratch_shapes=[pltpu.VMEM((HEAD_DIM, HEAD_DIM, nb), F32), pltpu.VMEM((2, HEAD_DIM, nb), F32)],
        compiler_params=pltpu.CompilerParams(dimension_semantics=("arbitrary",), vmem_limit_bytes=VMEM_LIMIT),
        name="gdn_s",
    )(cols_t, cols_t, cols_t, cols_t, cols_t, cols_t, conv_t, conv_t, conv_t, cw_t, cw_t, cw_t,
      alog_b, dtb_b, nw_b, s_t)


def _rwkv_s_kernel(r_ref, k_ref, v_ref, l_ref, pr_ref, pk_ref, pv_ref, pl_ref, mr_ref, mk_ref, mv_ref, ml_ref,
                   vec_ref, wup_ref, aup_ref, gup_ref, s0_ref, o_ref, sout_ref, s_scr, y_scr, v_scr):
    nb = s0_ref.shape[-1]
    nt = r_ref.shape[1] // nb

    def mixed(u_ref, p_ref, m_ref):
        cur = [u_ref[:, t * nb:(t + 1) * nb] for t in range(nt)]
        prev = [p_ref[...]] + cur[:-1]
        mu = m_ref[...]
        return [c + (p - c) * mu for c, p in zip(cur, prev)]

    rs = mixed(r_ref, pr_ref, mr_ref)
    ks = mixed(k_ref, pk_ref, mk_ref)
    vs = mixed(v_ref, pv_ref, mv_ref)
    ls = mixed(l_ref, pl_ref, ml_ref)
    w0, a0, k_k, k_a, r_k, ln_w, ln_b = (vec_ref[:, i:i + 1] for i in range(7))
    s_scr[...] = s0_ref[0]
    for t in range(nt):
        dw = ls[t][:LORA_W]
        da = ls[t][LORA_W:LORA_W + LORA_A]
        dg = ls[t][LORA_W + LORA_A:]
        w_log = -_softplus(-(w0 + _mm(wup_ref[...], jnp.tanh(dw)))) - 0.5
        w = jnp.exp(-jnp.exp(w_log))
        a = _sigmoid(a0 + _mm(aup_ref[...], da))
        gate = _mm(gup_ref[...], _sigmoid(dg))
        r, v = rs[t], vs[t]
        kk = ks[t] * k_k
        kk = kk * lax.rsqrt(jnp.sum(kk * kk, axis=0, keepdims=True) + EPS)
        k = ks[t] * (1.0 + (a - 1.0) * k_a)
        neg_kk = -kk
        kk_a = kk * a
        v_scr[...] = v

        def state_row(i, carry):
            row = s_scr[i]
            sa = jnp.sum(row * neg_kk, axis=0, keepdims=True)
            row = row * w + sa * kk_a + v_scr[pl.ds(i, 1), :] * k
            s_scr[i] = row
            y_scr[pl.ds(i, 1), :] = jnp.sum(row * r, axis=0, keepdims=True)
            return carry

        lax.fori_loop(0, HEAD_DIM, state_row, 0, unroll=4)
        y = y_scr[...]
        dy = y - jnp.mean(y, axis=0, keepdims=True)
        y = dy * lax.rsqrt(jnp.mean(dy * dy, axis=0, keepdims=True) + GN_EPS) * ln_w + ln_b
        bonus = jnp.sum(r * k * r_k, axis=0, keepdims=True) * v
        o_ref[:, t * nb:(t + 1) * nb] = (y + bonus) * gate
    _store_state_rows(s_scr, sout_ref)


def _rwkv_s(cols_t, shift_t, s_t, mu_b, vecs_t, w_up_t, a_up_t, g_up_t):
    ntok = cols_t.shape[1]
    nb = s_t.shape[-1]

    def head_rows(width, base):
        return pl.BlockSpec((HEAD_DIM, width), lambda h: (base // HEAD_DIM + h, 0))

    def lora_rows(width, base):
        return pl.BlockSpec((LORA_COLS, width), lambda h: (base // LORA_COLS, 0))

    return pl.pallas_call(
        _rwkv_s_kernel,
        grid=(N_HEADS,),
        in_specs=[head_rows(ntok, ROW_R), head_rows(ntok, ROW_R + D_MIX_HALF), head_rows(ntok, ROW_R + 2 * D_MIX_HALF),
                  lora_rows(ntok, ROW_LORA),
                  head_rows(nb, 0), head_rows(nb, D_MIX_HALF), head_rows(nb, 2 * D_MIX_HALF), lora_rows(nb, RWKV_RKV),
                  head_rows(nb, 0), head_rows(nb, D_MIX_HALF), head_rows(nb, 2 * D_MIX_HALF), lora_rows(nb, RWKV_RKV),
                  head_rows(SUBLANES, 0), head_rows(LORA_W, 0), head_rows(LORA_A, 0), head_rows(LORA_G, 0),
                  pl.BlockSpec((1, HEAD_DIM, HEAD_DIM, nb), lambda h: (h, 0, 0, 0))],
        out_specs=[pl.BlockSpec((HEAD_DIM, ntok), lambda h: (h, 0)),
                   pl.BlockSpec((nb, HEAD_DIM * HEAD_DIM), lambda h: (0, h))],
        out_shape=[jax.ShapeDtypeStruct((D_MIX_HALF, ntok), F32),
                   jax.ShapeDtypeStruct((nb, N_HEADS * HEAD_DIM * HEAD_DIM), F32)],
        scratch_shapes=[pltpu.VMEM((HEAD_DIM, HEAD_DIM, nb), F32), pltpu.VMEM((HEAD_DIM, nb), F32),
                        pltpu.VMEM((HEAD_DIM, nb), F32)],
        compiler_params=pltpu.CompilerParams(dimension_semantics=("arbitrary",), vmem_limit_bytes=VMEM_LIMIT),
        name="rwkv_s",
    )(cols_t, cols_t, cols_t, cols_t, shift_t, shift_t, shift_t, shift_t, mu_b, mu_b, mu_b, mu_b,
      vecs_t, w_up_t, a_up_t, g_up_t, s_t)


def _layer(x, mod, conv0, s_gdn0, shift0, s_rwkv0, p, *, bb, tt, moe_tile, bsz, chunk):
    sh_m, sc_m, gt_m, sh_f, sc_f, gt_f = mod
    cols = _inproj(x, sc_m, sh_m, p['norm_mix_w'], p['w_cols'], bb, tt)
    o_a, s_gdn = _gdn(cols, conv0, s_gdn0, p['gdn_conv_w'], p['alog_c'], p['dtb_c'], p['gdn_nw_rep'], bsz, chunk)
    o_b, s_rwkv = _rwkv(cols, shift0, s_rwkv0, p['rwkv_mu'], p['rwkv_vecs'], p['rwkv_w_up'], p['rwkv_a_up'],
                        p['rwkv_g_up'], bsz, chunk)
    x1, h2, wt = _outproj(o_a, o_b, x, gt_m, sc_f, sh_f, p['norm_ffn_w'], p['w_out_a'], p['w_out_b'],
                          p['w_router'], p['b_router'], *moe_tile)
    y = _moe(h2, wt, x1, gt_f, p['final_norm_w'], p['w_gu'], p['w_down'], *moe_tile)
    conv_new = cols[:, -(CONV_W - 1):, :GDN_QKV]
    shift_new = jnp.concatenate([cols[:, -1, COL_RKV:COL_Z], cols[:, -1, COL_LORA:COL_BA]], axis=-1)
    return y, conv_new, s_gdn, shift_new, s_rwkv


def _sample_layer(x, mod, conv0, s_gdn0, shift0, s_rwkv0, p, ps):
    b, t, d = x.shape
    assert t >= CONV_W - 1
    sh_m, sc_m, gt_m, sh_f, sc_f, gt_f = mod
    x_tm = jnp.transpose(x, (1, 0, 2))
    cols_t = _inproj_t(x_tm, sc_m, sh_m, p['norm_mix_w'], ps['w_rows'])
    seq_last = (1, 2, 3, 0)
    o_a, s_gdn = _gdn_s(cols_t, jnp.transpose(conv0, (1, 2, 0)), jnp.transpose(s_gdn0, seq_last),
                        ps['gdn_conv_w_t'], ps['alog_b'], ps['dtb_b'], ps['gdn_nw_b'])
    o_b, s_rwkv = _rwkv_s(cols_t, shift0.T, jnp.transpose(s_rwkv0, seq_last), ps['rwkv_mu_b'], ps['rwkv_vecs_t'],
                          ps['w_up_t'], ps['a_up_t'], ps['g_up_t'])
    x1, h2, wt = _outproj(o_a, o_b, x_tm, gt_m, sc_f, sh_f, p['norm_ffn_w'], p['w_out_a'], p['w_out_b'],
                          p['w_router'], p['b_router'], t, b)
    y = _moe(h2, wt, x1, gt_f, p['final_norm_w'], p['w_gu'], p['w_down'], t, b)
    keep = CONV_W - 1
    conv_new = jnp.transpose(cols_t[:GDN_QKV, (t - keep) * b:].reshape(GDN_QKV, keep, b), (2, 1, 0))
    shift_new = cols_t[ROW_R:ROW_BETA, (t - 1) * b:].T
    state_shape = (b, N_HEADS, HEAD_DIM, HEAD_DIM)
    return (jnp.transpose(y, (1, 0, 2)), conv_new, s_gdn.reshape(state_shape), shift_new,
            s_rwkv.reshape(state_shape))


def kernel(x_prompt, x_sample, state_gdn_conv, state_gdn, state_rwkv_shift, state_rwkv, c_prompt, c_sample, ada_w, ada_b, norm_mix_w, w_in, gdn_conv_w, gdn_a_log, gdn_dt_bias, gdn_norm_w, rwkv_mu, rwkv_w0, rwkv_w_up, rwkv_a0, rwkv_a_up, rwkv_g_up, rwkv_k_k, rwkv_k_a, rwkv_r_k, rwkv_ln_w, rwkv_ln_b, w_out, norm_ffn_w, router_group_w, router_group_b, router_expert_w, router_expert_b, expert_w_gate_up, expert_w_down, final_norm_w):
    depth = ada_w.shape[0]
    assert depth == 1
    l = 0
    b_p, t_p, d = x_prompt.shape
    b_s, t_s, _ = x_sample.shape
    gdn_cols = GDN_QKV + D_MIX_HALF + 2 * N_HEADS

    w = w_in[l]
    beta0 = GDN_QKV + D_MIX_HALF
    w_t = w.T.astype(BF16)
    rkv0, lora0 = gdn_cols, gdn_cols + RWKV_RKV
    w_cols = jnp.concatenate([w_t[:GDN_QKV], w_t[rkv0:lora0], w_t[GDN_QKV:beta0], w_t[lora0:],
                              w_t[beta0:gdn_cols], jnp.zeros((LANES - 2 * N_HEADS, d), BF16)], axis=0)
    head_lanes = lambda a, first: jnp.pad(a, (first, LANES - first - N_HEADS)).reshape(1, LANES)
    row = lambda a: a.reshape(1, -1)
    n_route = N_EXPERTS + N_GROUPS
    w_router = jnp.concatenate([router_expert_w[l], router_group_w[l], jnp.zeros((d, LANES - n_route), F32)], axis=1)
    w_router = jnp.concatenate(_split2(w_router), axis=1)
    b_router = jnp.concatenate([router_expert_b[l], router_group_b[l], jnp.zeros((LANES - n_route,), F32)])[None]
    p = {
        'norm_mix_w': norm_mix_w[l].reshape(1, 1, d), 'w_cols': w_cols,
        'gdn_conv_w': gdn_conv_w[l], 'alog_c': head_lanes(gdn_a_log[l], N_HEADS),
        'dtb_c': head_lanes(gdn_dt_bias[l], N_HEADS),
        'gdn_nw_rep': row(jnp.tile(gdn_norm_w[l], N_HEADS)),
        'rwkv_mu': row(rwkv_mu[l]),
        'rwkv_vecs': tuple(row(a) for a in (rwkv_w0[l], rwkv_a0[l], rwkv_k_k[l], rwkv_k_a[l], rwkv_r_k[l],
                                            rwkv_ln_w[l], rwkv_ln_b[l])),
        'rwkv_w_up': rwkv_w_up[l], 'rwkv_a_up': rwkv_a_up[l], 'rwkv_g_up': rwkv_g_up[l],
        'w_out_a': w_out[l][:D_MIX_HALF].astype(BF16), 'w_out_b': w_out[l][D_MIX_HALF:].astype(BF16),
        'norm_ffn_w': norm_ffn_w[l].reshape(1, 1, d), 'w_router': w_router, 'b_router': b_router,
        'w_gu': expert_w_gate_up[l].astype(BF16), 'w_down': expert_w_down[l].astype(BF16),
        'final_norm_w': final_norm_w.reshape(1, 1, d),
    }

    col = lambda a: a.reshape(-1, 1)
    lanes = lambda a: jnp.broadcast_to(col(a), (a.size, b_s))
    ps = {
        'w_rows': jnp.concatenate([w_t[:beta0], w_t[gdn_cols:], w_t[beta0:gdn_cols]], axis=0),
        'gdn_conv_w_t': gdn_conv_w[l].T, 'alog_b': lanes(gdn_a_log[l]), 'dtb_b': lanes(gdn_dt_bias[l]),
        'gdn_nw_b': lanes(gdn_norm_w[l]), 'rwkv_mu_b': lanes(rwkv_mu[l]),
        'rwkv_vecs_t': jnp.concatenate([col(a) for a in (rwkv_w0[l], rwkv_a0[l], rwkv_k_k[l], rwkv_k_a[l],
                                                         rwkv_r_k[l], rwkv_ln_w[l], rwkv_ln_b[l], rwkv_ln_b[l])],
                                       axis=1),
        'w_up_t': rwkv_w_up[l].T, 'a_up_t': rwkv_a_up[l].T, 'g_up_t': rwkv_g_up[l].T,
    }

    mod = _ada(jnp.concatenate([c_prompt, c_sample], axis=0), ada_w[l], ada_b[l])
    mod_p = tuple(m.reshape(b_p, 1, d) for m in jnp.split(mod[:b_p], 6, axis=-1))
    mod_s = tuple(m.reshape(1, b_s, d) for m in jnp.split(mod[b_p:], 6, axis=-1))

    zc = jnp.zeros((b_p, CONV_W - 1, GDN_QKV), F32)
    zs = jnp.zeros((b_p, N_HEADS, HEAD_DIM, HEAD_DIM), F32)
    zsh = jnp.zeros((b_p, 1, RWKV_COLS), F32)
    y_p, conv_p, sg_p, shift_p, sr_p = _layer(x_prompt, mod_p, zc, zs, zsh, zs, p, bb=1, tt=PROJ_ROWS,
                                              moe_tile=(1, MOE_TILE_ROWS), bsz=SEQS_PER_STEP, chunk=CHUNK)

    y_s, conv_s, sg_s, shift_s, sr_s = _sample_layer(x_sample, mod_s, state_gdn_conv[l], state_gdn[l],
                                                     state_rwkv_shift[l], state_rwkv[l], p, ps)
    return (y_p, y_s, conv_p[None], sg_p[None], shift_p[None], sr_p[None],
            conv_s[None], sg_s[None], shift_s[None], sr_s[None])
```

```python
import jax
import jax.numpy as jnp
from jax import lax
from jax.experimental import pallas as pl
from jax.experimental.pallas import tpu as pltpu

F32 = jnp.float32
BF16 = jnp.bfloat16
HI = lax.Precision.HIGHEST

HEAD_DIM = 64
N_HEADS = 8
D_MIX_HALF = N_HEADS * HEAD_DIM
CONV_W = 4
LORA_W = 64
LORA_A = 64
LORA_G = 128
N_GROUPS = 4
EXP_PER_GROUP = 8
N_EXPERTS = N_GROUPS * EXP_PER_GROUP
D_EXPERT = 256
EPS = 1e-6
GN_EPS = HEAD_DIM * 1e-5

LANES = 128
SUBLANES = 8
BF16_ROWS = 16
GDN_QKV = 3 * D_MIX_HALF
RWKV_RKV = 3 * D_MIX_HALF
RWKV_COLS = RWKV_RKV + LORA_W + LORA_A + LORA_G
LORA_COLS = LORA_W + LORA_A + LORA_G

COL_Z = GDN_QKV
COL_RKV = COL_Z + D_MIX_HALF
COL_LORA = COL_RKV + RWKV_RKV
COL_BA = COL_LORA + LORA_COLS
N_COLS = COL_BA + LANES
CHUNK = 64

ADA_COLS_PER_STEP = 1536
PROJ_ROWS = 512
MOE_TILE_ROWS = 1024
SEQS_PER_STEP = 4

VMEM_LIMIT = 48 * 1024 * 1024
MOE_VMEM_LIMIT = 58 * 1024 * 1024


def _dot(a, b, prec=HI):
    return jnp.dot(a, b, preferred_element_type=F32, precision=prec)


def _bf(a):
    return a.astype(BF16)


def _mm(a, b):
    return jnp.dot(_bf(a), _bf(b), preferred_element_type=F32)


def _mm_nt(a, b):
    return lax.dot_general(_bf(a), _bf(b), (((1,), (1,)), ((), ())), preferred_element_type=F32)


def _mm_tn(a, b):
    return lax.dot_general(_bf(a), _bf(b), (((0,), (0,)), ((), ())), preferred_element_type=F32)


def _split2(a):
    hi = a.astype(BF16)
    return hi, (a - hi.astype(F32)).astype(BF16)


def _sigmoid(x):
    return 1.0 / (1.0 + jnp.exp(-x))


def _silu(x):
    return x * _sigmoid(x)


def _softplus(x):
    return jnp.maximum(x, 0.0) + jnp.log1p(jnp.exp(-jnp.abs(x)))


def _iota2(shape, dim):
    return lax.broadcasted_iota(jnp.int32, shape, dim)


def _head_block_ones():
    r = _iota2((LANES, LANES), 0)
    c = _iota2((LANES, LANES), 1)
    sh = HEAD_DIM.bit_length() - 1
    return jnp.where((r >> sh) == (c >> sh), 1.0, 0.0).astype(F32)


def _head_sums(x):
    ones = _bf(_head_block_ones())
    xb = _bf(x)
    parts = [jnp.dot(xb[:, p * LANES:(p + 1) * LANES], ones, preferred_element_type=F32)
             for p in range(x.shape[1] // LANES)]
    return jnp.concatenate(parts, axis=1)


def _expand_heads(x, base, terms):
    lane = _iota2((LANES, D_MIX_HALF), 0)
    head = _iota2((LANES, D_MIX_HALF), 1) >> (HEAD_DIM.bit_length() - 1)
    select = jnp.where(lane == base + head, 1.0, 0.0).astype(BF16)
    out = None
    rem = x
    for _ in range(terms):
        piece = rem.astype(BF16)
        part = jnp.dot(piece, select, preferred_element_type=F32)
        out = part if out is None else out + part
        rem = rem - piece.astype(F32)
    return out


def _cumsum_rows(x):
    n = x.shape[0]
    r = _iota2((n, n), 0)
    c = _iota2((n, n), 1)
    tri = jnp.where(r >= c, 1.0, 0.0).astype(BF16)
    x1 = x.astype(BF16)
    rem = x - x1.astype(F32)
    x2 = rem.astype(BF16)
    x3 = (rem - x2.astype(F32)).astype(BF16)
    return (jnp.dot(tri, x1, preferred_element_type=F32) + jnp.dot(tri, x2, preferred_element_type=F32)
            + jnp.dot(tri, x3, preferred_element_type=F32))


def _unit_lower_inverses(lows, n, mm):
    r = _iota2((n, n), 0)
    c = _iota2((n, n), 1)
    eye = jnp.where(r == c, 1.0, 0.0)
    pair = (r >> 1) == (c >> 1)
    invs = [eye - jnp.where(pair, low, 0.0) for low in lows]
    s = 2
    while s < n:
        sh = s.bit_length()
        sel = ((r >> sh) == (c >> sh)) & ((r & (2 * s - 1)) >= s) & ((c & (2 * s - 1)) < s)
        prods = [mm(jnp.where(sel, low, 0.0), inv) for low, inv in zip(lows, invs)]
        invs = [inv - mm(inv, prod) for inv, prod in zip(invs, prods)]
        s *= 2
    return invs


def _ada_kernel(c_ref, w_ref, b_ref, o_ref):
    o_ref[...] = _dot(_silu(c_ref[...]), w_ref[...]) + b_ref[...]


def _ada(c_all, ada_w, ada_b):
    n, d = c_all.shape
    nout = ada_w.shape[1]
    tn = ADA_COLS_PER_STEP
    return pl.pallas_call(
        _ada_kernel,
        grid=(nout // tn,),
        in_specs=[pl.BlockSpec((n, d), lambda j: (0, 0)),
                  pl.BlockSpec((d, tn), lambda j: (0, j)),
                  pl.BlockSpec((1, tn), lambda j: (0, j))],
        out_specs=pl.BlockSpec((n, tn), lambda j: (0, j)),
        out_shape=jax.ShapeDtypeStruct((n, nout), F32),
        compiler_params=pltpu.CompilerParams(dimension_semantics=("arbitrary",), vmem_limit_bytes=VMEM_LIMIT),
        name="ada",
    )(c_all, ada_w, ada_b.reshape(1, nout))


def _modulated_norm(x, nw, sc, sh):
    ms = jnp.mean(x * x, axis=-1, keepdims=True)
    return (x * lax.rsqrt(ms + EPS) * nw) * (1.0 + sc) + sh


def _inproj_kernel(x_ref, sc_ref, sh_ref, nw_ref, w_ref, o_ref):
    bb, tt, d = x_ref.shape
    h = _modulated_norm(x_ref[...], nw_ref[...], sc_ref[...], sh_ref[...])
    h = h.reshape(bb * tt, d).astype(BF16)
    o = _mm_nt(h, w_ref[...])
    o_ref[...] = o.reshape(bb, tt, o.shape[-1])


def _inproj(x, sc, sh, nw, w_rows, bb, tt):
    b, t, d = x.shape
    nc = w_rows.shape[0]
    return pl.pallas_call(
        _inproj_kernel,
        grid=(b // bb, t // tt),
        in_specs=[pl.BlockSpec((bb, tt, d), lambda i, j: (i, j, 0)),
                  pl.BlockSpec((bb, 1, d), lambda i, j: (i, 0, 0)),
                  pl.BlockSpec((bb, 1, d), lambda i, j: (i, 0, 0)),
                  pl.BlockSpec((1, 1, d), lambda i, j: (0, 0, 0)),
                  pl.BlockSpec((nc, d), lambda i, j: (0, 0))],
        out_specs=pl.BlockSpec((bb, tt, nc), lambda i, j: (i, j, 0)),
        out_shape=jax.ShapeDtypeStruct((b, t, nc), F32),
        compiler_params=pltpu.CompilerParams(dimension_semantics=("arbitrary",) * 2, vmem_limit_bytes=VMEM_LIMIT),
        name="inproj",
    )(x, sc, sh, nw, w_rows)


def _gdn_kernel(qkv_ref, z_ref, ba_ref, c0_ref, cw_ref, alog_ref, dtb_ref, nw_ref, s0_ref, o_ref, s_ref, hist_scr):
    bsz, chunk, _ = qkv_ref.shape
    hist_lo = SUBLANES - (CONV_W - 1)

    @pl.when(pl.program_id(1) == 0)
    def _():
        s_ref[...] = s0_ref[...]
        hist_scr[:, hist_lo:SUBLANES, :] = c0_ref[...]

    row = _iota2((chunk, chunk), 0)
    col = _iota2((chunk, chunk), 1)
    causal = row >= col
    strict = row > col

    def prep(bi):
        u = qkv_ref[bi]
        hist_scr[bi, SUBLANES:SUBLANES + chunk, :] = u
        y = u * cw_ref[CONV_W - 1:CONV_W, :]
        for i in range(CONV_W - 1):
            y = y + hist_scr[bi, hist_lo + i:hist_lo + i + chunk, :] * cw_ref[i:i + 1, :]
        hist_scr[bi, hist_lo:SUBLANES, :] = hist_scr[bi, hist_lo + chunk:SUBLANES + chunk, :]
        qkv = _silu(y)
        q, k, v = (qkv[:, i * D_MIX_HALF:(i + 1) * D_MIX_HALF] for i in range(3))
        q = q * lax.rsqrt(_head_sums(q * q) + EPS) * (HEAD_DIM ** -0.5)
        k = k * lax.rsqrt(_head_sums(k * k) + EPS)
        ba = ba_ref[bi]
        beta = _expand_heads(_sigmoid(ba), 0, 2)
        g = -jnp.exp(alog_ref[...]) * _softplus(ba + dtb_ref[...])
        gcum = _expand_heads(_cumsum_rows(g), N_HEADS, 3)
        eg = jnp.exp(gcum)
        g_last = gcum[chunk - 1:chunk, :]
        return dict(q=q, k=k, beta=beta, gcum=gcum, rhs_v=beta * v, rhs_k=beta * eg * k, q_dec=eg * q,
                    k_dec=jnp.exp(g_last - gcum) * k, eg_last=jnp.exp(g_last))

    pre = [prep(bi) for bi in range(bsz)]
    chains = [(bi, h) for bi in range(bsz) for h in range(N_HEADS)]
    n = range(len(chains))

    def head(name, i, width=HEAD_DIM):
        bi, h = chains[i]
        return pre[bi][name][:, h * HEAD_DIM:h * HEAD_DIM + width]

    g_is = [head('gcum', i, chunk) for i in n]
    decays = [jnp.where(causal, jnp.exp(jnp.minimum(g_i - g_i.T, 0.0)), 0.0) for g_i in g_is]
    qk_kks = [_mm_nt(jnp.concatenate([head('k', i), head('q', i)], axis=0), head('k', i)) for i in n]
    lows = [jnp.where(strict, head('beta', i, chunk) * decays[i] * qk_kks[i][:chunk], 0.0) for i in n]
    t_invs = _unit_lower_inverses(lows, chunk, _mm)
    sols = [_mm(t_invs[i], jnp.concatenate([head('rhs_v', i), head('rhs_k', i)], axis=1)) for i in n]
    states = [s_ref[bi, h] for bi, h in chains]
    wss = [_mm(jnp.concatenate([sols[i][:, HEAD_DIM:], head('q_dec', i)], axis=0), states[i])
           for i in n]
    u_news = [sols[i][:, :HEAD_DIM] - wss[i][:chunk] for i in n]
    outs = [wss[i][chunk:] + _mm(qk_kks[i][chunk:] * decays[i], u_news[i]) for i in n]
    s_news = [head('eg_last', i) * states[i] + _mm_tn(head('k_dec', i), u_news[i]) for i in n]
    for i, (bi, h) in enumerate(chains):
        s_ref[bi, h] = s_news[i]
    for bi in range(bsz):
        o = jnp.concatenate(outs[bi * N_HEADS:(bi + 1) * N_HEADS], axis=1)
        o = o * lax.rsqrt(_head_sums(o * o) * (1.0 / HEAD_DIM) + EPS) * nw_ref[...]
        o_ref[bi] = o * _silu(z_ref[bi])


def _gdn(cols, conv0, s0, conv_w, alog_c, dtb_c, nw_rep, bsz, chunk):
    b, t, _ = cols.shape

    def col_spec(width, start):
        return pl.BlockSpec((bsz, chunk, width), lambda i, c: (i, c, start // width))

    state_spec = pl.BlockSpec((bsz, N_HEADS, HEAD_DIM, HEAD_DIM), lambda i, c: (i, 0, 0, 0))
    return pl.pallas_call(
        _gdn_kernel,
        grid=(b // bsz, t // chunk),
        in_specs=[col_spec(GDN_QKV, 0), col_spec(D_MIX_HALF, COL_Z), col_spec(LANES, COL_BA),
                  pl.BlockSpec((bsz, CONV_W - 1, GDN_QKV), lambda i, c: (i, 0, 0)),
                  pl.BlockSpec((CONV_W, GDN_QKV), lambda i, c: (0, 0)),
                  pl.BlockSpec((1, LANES), lambda i, c: (0, 0)), pl.BlockSpec((1, LANES), lambda i, c: (0, 0)),
                  pl.BlockSpec((1, D_MIX_HALF), lambda i, c: (0, 0)), state_spec],
        out_specs=[pl.BlockSpec((bsz, chunk, D_MIX_HALF), lambda i, c: (i, c, 0)), state_spec],
        out_shape=[jax.ShapeDtypeStruct((b, t, D_MIX_HALF), F32),
                   jax.ShapeDtypeStruct((b, N_HEADS, HEAD_DIM, HEAD_DIM), F32)],
        scratch_shapes=[pltpu.VMEM((bsz, SUBLANES + chunk, GDN_QKV), F32)],
        compiler_params=pltpu.CompilerParams(dimension_semantics=("arbitrary",) * 2, vmem_limit_bytes=VMEM_LIMIT),
        name="gdn",
    )(cols, cols, cols, conv0, conv_w, alog_c, dtb_c, nw_rep, s0)


def _rwkv_kernel(r_ref, k_ref, v_ref, l_ref, p_ref, pl_ref, mu_ref, mul_ref, w0_ref, a0_ref, kk_ref, ka_ref, rk_ref,
                 lnw_ref, lnb_ref, wup_ref, aup_ref, gup_ref, s0_ref, o_ref, s_ref, hist_scr, histl_scr):
    bsz, chunk, _ = r_ref.shape
    prev_row = SUBLANES - 1

    @pl.when(pl.program_id(1) == 0)
    def _():
        s_ref[...] = s0_ref[...]
        hist_scr[:, prev_row:SUBLANES, :] = p_ref[...]
        histl_scr[:, prev_row:SUBLANES, :] = pl_ref[...]

    row = _iota2((chunk, chunk), 0)
    col = _iota2((chunk, chunk), 1)
    causal = row >= col
    strict = row > col

    def shifted(bi, u, scr, m_ref):
        scr[bi, SUBLANES:SUBLANES + chunk, :] = u
        prev = scr[bi, prev_row:prev_row + chunk, :]
        scr[bi, prev_row:SUBLANES, :] = scr[bi, prev_row + chunk:SUBLANES + chunk, :]
        return u + (prev - u) * m_ref[...]

    def prep(bi):
        rkv = shifted(bi, jnp.concatenate([r_ref[bi], k_ref[bi], v_ref[bi]], axis=1), hist_scr, mu_ref)
        r, k, v = (rkv[:, i * D_MIX_HALF:(i + 1) * D_MIX_HALF] for i in range(3))
        xl = shifted(bi, l_ref[bi], histl_scr, mul_ref)
        dw = xl[:, :LORA_W]
        da = xl[:, LORA_W:LORA_W + LORA_A]
        dg = xl[:, LORA_W + LORA_A:]
        w_log = -_softplus(-(w0_ref[...] + _mm(jnp.tanh(dw), wup_ref[...]))) - 0.5
        lw = -jnp.exp(w_log)
        a = _sigmoid(a0_ref[...] + _mm(da, aup_ref[...]))
        gate = _mm(_sigmoid(dg), gup_ref[...])
        kk = k * kk_ref[...]
        kk = kk * lax.rsqrt(_head_sums(kk * kk) + EPS)
        k = k * (1.0 + (a - 1.0) * ka_ref[...])
        cum = _cumsum_rows(lw)
        e_out = jnp.exp(-cum)
        cum_last = cum[chunk - 1:chunk, :]
        e_rest = jnp.exp(cum_last - cum)
        return dict(r=r, k=k, v=v, gate=gate, a_t=-kk * jnp.exp(cum - lw), b_t=kk * a * e_out, k_t=k * e_out,
                    r_t=r * jnp.exp(cum), b_c=kk * a * e_rest, k_c=k * e_rest, e_last=jnp.exp(cum_last))

    pre = [prep(bi) for bi in range(bsz)]
    chains = [(bi, h) for bi in range(bsz) for h in range(N_HEADS)]
    n = range(len(chains))

    def head(name, i):
        bi, h = chains[i]
        return pre[bi][name][:, h * HEAD_DIM:(h + 1) * HEAD_DIM]

    ars = [jnp.concatenate([head('a_t', i), head('r_t', i)], axis=0) for i in n]
    bks = [jnp.concatenate([head('b_t', i), head('k_t', i)], axis=0) for i in n]
    ms = [_mm_nt(ars[i], bks[i]) for i in n]
    t_invs = _unit_lower_inverses([jnp.where(strict, -m[:chunk, :chunk], 0.0) for m in ms], chunk, _mm)
    states = [s_ref[bi, h] for bi, h in chains]
    arss = [_mm_nt(ars[i], states[i]) for i in n]
    vhs = [head('v', i) for i in n]
    akvs = [_mm(jnp.where(strict, ms[i][:chunk, chunk:], 0.0), vhs[i]) for i in n]
    us = [_mm(t_invs[i], arss[i][:chunk] + akvs[i]) for i in n]
    uvs = [jnp.concatenate([us[i], vhs[i]], axis=0) for i in n]
    causal2 = _iota2((chunk, 2 * chunk), 0) >= (_iota2((chunk, 2 * chunk), 1) & (chunk - 1))
    ys = [arss[i][chunk:] + _mm(jnp.where(causal2, ms[i][chunk:], 0.0), uvs[i]) for i in n]
    s_news = [states[i] * head('e_last', i)
              + _mm_tn(uvs[i], jnp.concatenate([head('b_c', i), head('k_c', i)], axis=0)) for i in n]
    for i, (bi, h) in enumerate(chains):
        s_ref[bi, h] = s_news[i]
    for bi in range(bsz):
        p = pre[bi]
        y = jnp.concatenate(ys[bi * N_HEADS:(bi + 1) * N_HEADS], axis=1)
        mean = _head_sums(y) * (1.0 / HEAD_DIM)
        dy = y - mean
        var = _head_sums(dy * dy) * (1.0 / HEAD_DIM)
        y = dy * lax.rsqrt(var + GN_EPS) * lnw_ref[...] + lnb_ref[...]
        bonus = _head_sums(p['r'] * p['k'] * rk_ref[...]) * p['v']
        o_ref[bi] = (y + bonus) * p['gate']


def _rwkv(cols, shift0, s0, mu, vecs, w_up, a_up, g_up, bsz, chunk):
    b, t, _ = cols.shape

    def full2(shape):
        return pl.BlockSpec(shape, lambda i, c: (0, 0))

    state_spec = pl.BlockSpec((bsz, N_HEADS, HEAD_DIM, HEAD_DIM), lambda i, c: (i, 0, 0, 0))
    return pl.pallas_call(
        _rwkv_kernel,
        grid=(b // bsz, t // chunk),
        in_specs=[pl.BlockSpec((bsz, chunk, D_MIX_HALF), lambda i, c: (i, c, COL_RKV // D_MIX_HALF)),
                  pl.BlockSpec((bsz, chunk, D_MIX_HALF), lambda i, c: (i, c, COL_RKV // D_MIX_HALF + 1)),
                  pl.BlockSpec((bsz, chunk, D_MIX_HALF), lambda i, c: (i, c, COL_RKV // D_MIX_HALF + 2)),
                  pl.BlockSpec((bsz, chunk, LORA_COLS), lambda i, c: (i, c, COL_LORA // LORA_COLS)),
                  pl.BlockSpec((bsz, 1, RWKV_RKV), lambda i, c: (i, 0, 0)),
                  pl.BlockSpec((bsz, 1, LORA_COLS), lambda i, c: (i, 0, RWKV_RKV // LORA_COLS)),
                  pl.BlockSpec((1, RWKV_RKV), lambda i, c: (0, 0)),
                  pl.BlockSpec((1, LORA_COLS), lambda i, c: (0, RWKV_RKV // LORA_COLS))]
                 + [full2((1, D_MIX_HALF))] * 7
                 + [full2((LORA_W, D_MIX_HALF)), full2((LORA_A, D_MIX_HALF)), full2((LORA_G, D_MIX_HALF)),
                    state_spec],
        out_specs=[pl.BlockSpec((bsz, chunk, D_MIX_HALF), lambda i, c: (i, c, 0)), state_spec],
        out_shape=[jax.ShapeDtypeStruct((b, t, D_MIX_HALF), F32),
                   jax.ShapeDtypeStruct((b, N_HEADS, HEAD_DIM, HEAD_DIM), F32)],
        scratch_shapes=[pltpu.VMEM((bsz, SUBLANES + chunk, RWKV_RKV), F32),
                        pltpu.VMEM((bsz, SUBLANES + chunk, LORA_COLS), F32)],
        compiler_params=pltpu.CompilerParams(dimension_semantics=("arbitrary",) * 2, vmem_limit_bytes=VMEM_LIMIT),
        name="rwkv",
    )(cols, cols, cols, cols, shift0, shift0, mu, mu, *vecs, w_up, a_up, g_up, s0)


def _mod_spec(m, bb, tt):
    d = m.shape[-1]
    if m.shape[1] == 1:
        return pl.BlockSpec((bb, 1, d), lambda i, j, *_: (i, 0, 0))
    return pl.BlockSpec((1, tt, d), lambda i, j, *_: (0, j, 0))


def _outproj_kernel(oa_ref, ob_ref, x_ref, gt_ref, sc_ref, sh_ref, nw_ref, wa_ref, wb_ref, wr_ref, br_ref,
                    x1_ref, h_ref, wt_ref):
    bb, tt, d = x_ref.shape
    n = bb * tt
    if len(oa_ref.shape) == 2:
        mixed = _mm_tn(oa_ref[...], wa_ref[...]) + _mm_tn(ob_ref[...], wb_ref[...])
    else:
        oa = oa_ref[...].reshape(n, D_MIX_HALF).astype(BF16)
        ob = ob_ref[...].reshape(n, D_MIX_HALF).astype(BF16)
        mixed = (jnp.dot(oa, wa_ref[...], preferred_element_type=F32)
                 + jnp.dot(ob, wb_ref[...], preferred_element_type=F32))
    x1 = x_ref[...] + gt_ref[...] * mixed.reshape(bb, tt, d)
    x1_ref[...] = x1
    h = _modulated_norm(x1, nw_ref[...], sc_ref[...], sh_ref[...]).reshape(n, d)
    h_ref[...] = h.reshape(bb, tt, d).astype(BF16)

    h_hi, h_lo = _split2(h)
    hw = jnp.dot(h_hi, wr_ref[...], preferred_element_type=F32)
    logits = (hw[:, :LANES] + hw[:, LANES:] + jnp.dot(h_lo, wr_ref[:, :LANES], preferred_element_type=F32)
              + br_ref[...])
    lane_i = _iota2((n, LANES), 1)
    lane = lane_i.astype(F32)
    lane_grp = (lane_i >> (EXP_PER_GROUP.bit_length() - 1)).astype(F32)
    neg = jnp.float32(-jnp.inf)
    big = jnp.float32(LANES)
    is_grp = (lane_i >= N_EXPERTS) & (lane_i < N_EXPERTS + N_GROUPS)
    gl = jnp.where(is_grp, logits, neg)
    gmax = jnp.max(gl, axis=-1, keepdims=True)
    grp = jnp.min(jnp.where(gl == gmax, lane, big), axis=-1, keepdims=True) - N_EXPERTS
    g_prob = 1.0 / jnp.sum(jnp.where(is_grp, jnp.exp(gl - gmax), 0.0), axis=-1, keepdims=True)
    in_grp = (lane_i < N_EXPERTS) & (lane_grp == grp)
    el = jnp.where(in_grp, logits, neg)
    m1 = jnp.max(el, axis=-1, keepdims=True)
    i1 = jnp.min(jnp.where(el == m1, lane, big), axis=-1, keepdims=True)
    el2 = jnp.where(lane == i1, neg, el)
    m2 = jnp.max(el2, axis=-1, keepdims=True)
    i2 = jnp.min(jnp.where(el2 == m2, lane, big), axis=-1, keepdims=True)
    e2 = jnp.exp(m2 - m1)
    p1 = g_prob / (1.0 + e2)
    p2 = g_prob * e2 / (1.0 + e2)
    wt = jnp.where(lane == i1, p1, 0.0) + jnp.where(lane == i2, p2, 0.0)
    wt_ref[...] = wt.reshape(bb, tt, LANES)


def _outproj(oa, ob, x, gt, sc, sh, nw, wa, wb, wr, br, bb, tt):
    b, t, d = x.shape
    grid = (b // bb, t // tt)

    def tok_spec(w):
        return pl.BlockSpec((bb, tt, w), lambda i, j: (i, j, 0))

    def full2(shape):
        return pl.BlockSpec(shape, lambda i, j: (0, 0))

    mod_spec = _mod_spec(gt, bb, tt)
    if oa.ndim == 2:
        assert grid == (1, 1)
        mix_spec = full2(oa.shape)
    else:
        mix_spec = tok_spec(D_MIX_HALF)
    return pl.pallas_call(
        _outproj_kernel,
        grid=grid,
        in_specs=[mix_spec, mix_spec, tok_spec(d), mod_spec, mod_spec, mod_spec,
                  pl.BlockSpec((1, 1, d), lambda i, j: (0, 0, 0)),
                  full2((D_MIX_HALF, d)), full2((D_MIX_HALF, d)), full2((d, 2 * LANES)), full2((1, LANES))],
        out_specs=[tok_spec(d), tok_spec(d), tok_spec(LANES)],
        out_shape=[jax.ShapeDtypeStruct((b, t, d), F32), jax.ShapeDtypeStruct((b, t, d), BF16),
                   jax.ShapeDtypeStruct((b, t, LANES), F32)],
        compiler_params=pltpu.CompilerParams(dimension_semantics=("arbitrary",) * 2, vmem_limit_bytes=VMEM_LIMIT),
        name="outproj",
    )(oa, ob, x, gt, sc, sh, nw, wa, wb, wr, br)


MOE_ROWS = 128
MOE_EXPERTS_PER_STEP = 4


def _moe_kernel(h_ref, wt_ref, x1_ref, gt_ref, fw_ref, wgu_ref, wd_ref, y_ref,
                xg_scr, gtw_scr, yw_scr, seg_scr):
    bb, tt, d = x1_ref.shape
    n = bb * tt
    npos = 2 * n
    e = pl.program_id(2)

    @pl.when(e == 0)
    def _():
        wt_t = wt_ref[...].reshape(n, LANES).T
        member = wt_t > 0.0
        ones = jnp.where(member, 1.0, 0.0)
        upper = jnp.where(_iota2((n, n), 0) < _iota2((n, n), 1), 1.0, 0.0).astype(BF16)
        rank = jnp.dot(_bf(ones), upper, preferred_element_type=F32)
        cnt = jnp.broadcast_to(jnp.sum(ones, axis=1, keepdims=True), (LANES, LANES))
        lower = jnp.where(_iota2((LANES, LANES), 0) > _iota2((LANES, LANES), 1), 1.0, 0.0)
        off = _dot(lower, cnt)
        first_i = off.astype(jnp.int32)
        last_i = (off + cnt).astype(jnp.int32)
        for x in range(N_EXPERTS):
            seg_scr[x] = first_i[x, 0]
            seg_scr[N_EXPERTS + x] = last_i[x, 0]
        pos = off[:, :1] + rank
        pos1 = jnp.min(jnp.where(member, pos, jnp.float32(4 * n)), axis=0, keepdims=True)
        pos2 = jnp.max(jnp.where(member, pos, -1.0), axis=0, keepdims=True)
        w1 = jnp.sum(jnp.where(member & (pos == pos1), wt_t, 0.0), axis=0, keepdims=True)
        w2 = jnp.sum(jnp.where(member & (pos == pos2), wt_t, 0.0), axis=0, keepdims=True)
        w2 = jnp.where(pos2 != pos1, w2, 0.0)

        def onehot_block(b, carry):
            p = (b * MOE_ROWS + _iota2((MOE_ROWS, n), 0)).astype(F32)
            rows = pl.ds(pl.multiple_of(b * MOE_ROWS, MOE_ROWS), MOE_ROWS)
            yw_scr[rows, :n] = jnp.where((pos1 == p) | (pos2 == p), 1.0, 0.0).astype(BF16)
            return carry

        lax.fori_loop(0, npos // MOE_ROWS, onehot_block, 0)
        xg_scr[:npos, :] = jnp.dot(yw_scr[:npos, :n], h_ref[...].reshape(n, d),
                                   preferred_element_type=F32).astype(BF16)
        xg_scr[npos:, :] = jnp.zeros((MOE_ROWS, d), BF16)
        z_t = jnp.concatenate([pos1, pos2, w1, w2, jnp.zeros((LANES - 4, n), F32)], axis=0).T
        p1c, p2c, w1c, w2c = (z_t[:, i:i + 1] for i in range(4))
        for b in range(npos // LANES):
            p = (b * LANES + _iota2((n, LANES), 1)).astype(F32)
            gtw_scr[:, b * LANES:(b + 1) * LANES] = (jnp.where(p1c == p, w1c, 0.0)
                                                     + jnp.where(p2c == p, w2c, 0.0)).astype(BF16)
        yw_scr[...] = jnp.zeros_like(yw_scr)

    es = range(wgu_ref.shape[0])
    firsts = [seg_scr[e * len(es) + j] for j in es]
    lasts = [seg_scr[N_EXPERTS + e * len(es) + j] for j in es]
    starts = [(f // BF16_ROWS) * BF16_ROWS for f in firsts]
    windows = [(lasts[j] - starts[j] + MOE_ROWS - 1) // MOE_ROWS for j in es]

    def window_rows(r0):
        return pl.ds(pl.multiple_of(r0, BF16_ROWS), MOE_ROWS)

    def store_rows(j, r0, y):
        p = r0 + _iota2((MOE_ROWS, 1), 0)
        rows = window_rows(r0)
        yw_scr[rows, :] = jnp.where((p >= firsts[j]) & (p < lasts[j]), y, yw_scr[rows, :])

    gus = [jnp.dot(xg_scr[window_rows(starts[j]), :], wgu_ref[j], preferred_element_type=F32) for j in es]
    acts = [(_silu(gu[:, :D_EXPERT]) * gu[:, D_EXPERT:]).astype(BF16) for gu in gus]
    ys = [jnp.dot(acts[j], wd_ref[j], preferred_element_type=F32).astype(BF16) for j in es]
    for j in es:
        store_rows(j, starts[j], ys[j])

    most = windows[0]
    for j in es[1:]:
        most = jnp.maximum(most, windows[j])

    @pl.when(most > 1)
    def _():
        for j in es:
            def more_windows(w, carry, j=j):
                r0 = starts[j] + w * MOE_ROWS
                gu = jnp.dot(xg_scr[window_rows(r0), :], wgu_ref[j], preferred_element_type=F32)
                act = (_silu(gu[:, :D_EXPERT]) * gu[:, D_EXPERT:]).astype(BF16)
                store_rows(j, r0, jnp.dot(act, wd_ref[j], preferred_element_type=F32).astype(BF16))
                return carry

            lax.fori_loop(1, windows[j], more_windows, 0)

    @pl.when(e == pl.num_programs(2) - 1)
    def _():
        moe = jnp.dot(gtw_scr[...], yw_scr[:npos, :], preferred_element_type=F32)
        x2 = x1_ref[...] + gt_ref[...] * moe.reshape(bb, tt, d)
        ms = jnp.mean(x2 * x2, axis=-1, keepdims=True)
        y_ref[...] = x2 * lax.rsqrt(ms + EPS) * fw_ref[...]


def _moe(h, wt, x1, gt, fw, w_gu, w_down, bb, tt):
    b, t, d = x1.shape
    es = MOE_EXPERTS_PER_STEP
    grid = (b // bb, t // tt, N_EXPERTS // es)
    npos = 2 * bb * tt

    def tok_spec(w):
        return pl.BlockSpec((bb, tt, w), lambda i, j, e: (i, j, 0))

    return pl.pallas_call(
        _moe_kernel,
        grid=grid,
        in_specs=[tok_spec(d), tok_spec(LANES), tok_spec(d),
                  _mod_spec(gt, bb, tt),
                  pl.BlockSpec((1, 1, d), lambda i, j, e: (0, 0, 0)),
                  pl.BlockSpec((es, d, 2 * D_EXPERT), lambda i, j, e: (e, 0, 0)),
                  pl.BlockSpec((es, D_EXPERT, d), lambda i, j, e: (e, 0, 0))],
        out_specs=tok_spec(d),
        out_shape=jax.ShapeDtypeStruct((b, t, d), F32),
        scratch_shapes=[pltpu.VMEM((npos + MOE_ROWS, d), BF16), pltpu.VMEM((bb * tt, npos), BF16),
                        pltpu.VMEM((npos + MOE_ROWS, d), BF16), pltpu.SMEM((2 * N_EXPERTS,), jnp.int32)],
        compiler_params=pltpu.CompilerParams(dimension_semantics=("arbitrary",) * 3,
                                             vmem_limit_bytes=MOE_VMEM_LIMIT),
        name="moe",
    )(h, wt, x1, gt, fw, w_gu, w_down)


ROW_Z = COL_Z
ROW_R = COL_RKV
ROW_LORA = COL_LORA
ROW_BETA = COL_BA
ROW_A = COL_BA + N_HEADS


def _inproj_t_kernel(x_ref, sc_ref, sh_ref, nw_ref, w_ref, o_ref):
    t, b, d = x_ref.shape
    h = _modulated_norm(x_ref[...], nw_ref[...], sc_ref[...], sh_ref[...]).reshape(t * b, d)
    o_ref[...] = _mm_nt(w_ref[...], h)


def _inproj_t(x, sc, sh, nw, w_rows):
    t, b, d = x.shape
    nr = w_rows.shape[0]

    def full(shape):
        return pl.BlockSpec(shape, lambda i: (0,) * len(shape))

    return pl.pallas_call(
        _inproj_t_kernel,
        grid=(1,),
        in_specs=[full((t, b, d)), full((1, b, d)), full((1, b, d)), full((1, 1, d)), full((nr, d))],
        out_specs=full((nr, t * b)),
        out_shape=jax.ShapeDtypeStruct((nr, t * b), F32),
        compiler_params=pltpu.CompilerParams(dimension_semantics=("arbitrary",), vmem_limit_bytes=VMEM_LIMIT),
        name="inproj_t",
    )(x, sc, sh, nw, w_rows)


def _store_state_rows(s_scr, sout_ref):
    for j in range(HEAD_DIM // 2):
        pair = jnp.concatenate([s_scr[2 * j], s_scr[2 * j + 1]], axis=0)
        sout_ref[:, j * LANES:(j + 1) * LANES] = pair.T


def _gdn_s_kernel(q_ref, k_ref, v_ref, z_ref, b_ref, a_ref, cq_ref, ck_ref, cv_ref, wq_ref, wk_ref, wv_ref,
                  alog_ref, dtb_ref, nw_ref, s0_ref, o_ref, sout_ref, s_scr, kq_scr):
    nb = s0_ref.shape[-1]
    nt = q_ref.shape[1] // nb
    h = pl.program_id(0)

    def tok(ref, t):
        return ref[:, t * nb:(t + 1) * nb]

    def conv(u_ref, c_ref, w_ref):
        full = [c_ref[i] for i in range(CONV_W - 1)] + [tok(u_ref, t) for t in range(nt)]
        taps = [w_ref[:, i:i + 1] for i in range(CONV_W)]
        outs = []
        for t in range(nt):
            y = full[t + CONV_W - 1] * taps[CONV_W - 1]
            for i in range(CONV_W - 1):
                y = y + full[t + i] * taps[i]
            outs.append(_silu(y))
        return outs

    qs = conv(q_ref, cq_ref, wq_ref)
    ks = conv(k_ref, ck_ref, wk_ref)
    vs = conv(v_ref, cv_ref, wv_ref)
    neg_rate = -jnp.exp(alog_ref[pl.ds(h, 1), :])
    dtb = dtb_ref[pl.ds(h, 1), :]
    beta_in = b_ref[pl.ds(h, 1), :]
    a_in = a_ref[pl.ds(h, 1), :]
    s_scr[...] = s0_ref[0]
    zero = jnp.zeros((HEAD_DIM, nb), F32)
    for t in range(nt):
        q = qs[t] * lax.rsqrt(jnp.sum(qs[t] * qs[t], axis=0, keepdims=True) + EPS) * (HEAD_DIM ** -0.5)
        k = ks[t] * lax.rsqrt(jnp.sum(ks[t] * ks[t], axis=0, keepdims=True) + EPS)
        beta = _sigmoid(beta_in[:, t * nb:(t + 1) * nb])
        decay = jnp.exp(neg_rate * _softplus(a_in[:, t * nb:(t + 1) * nb] + dtb))
        kq_scr[0] = k
        kq_scr[1] = q

        def decay_and_project(i, acc):
            row = s_scr[i] * decay
            s_scr[i] = row
            return acc + kq_scr[0, pl.ds(i, 1), :] * row

        k_s = lax.fori_loop(0, HEAD_DIM, decay_and_project, zero, unroll=8)
        u = beta * (vs[t] - k_s)

        def update_and_read(i, acc):
            row = s_scr[i] + kq_scr[0, pl.ds(i, 1), :] * u
            s_scr[i] = row
            return acc + kq_scr[1, pl.ds(i, 1), :] * row

        o = lax.fori_loop(0, HEAD_DIM, update_and_read, zero, unroll=8)
        o = o * lax.rsqrt(jnp.mean(o * o, axis=0, keepdims=True) + EPS) * nw_ref[...]
        o_ref[:, t * nb:(t + 1) * nb] = o * _silu(tok(z_ref, t))
    _store_state_rows(s_scr, sout_ref)


def _gdn_s(cols_t, conv_t, s_t, cw_t, alog_b, dtb_b, nw_b):
    ntok = cols_t.shape[1]
    nb = s_t.shape[-1]

    def head_rows(base):
        return pl.BlockSpec((HEAD_DIM, ntok), lambda h: (base // HEAD_DIM + h, 0))

    def conv_rows(base):
        return pl.BlockSpec((CONV_W - 1, HEAD_DIM, nb), lambda h: (0, base // HEAD_DIM + h, 0))

    def tap_rows(base):
        return pl.BlockSpec((HEAD_DIM, CONV_W), lambda h: (base // HEAD_DIM + h, 0))

    def full2(shape):
        return pl.BlockSpec(shape, lambda h: (0, 0))

    return pl.pallas_call(
        _gdn_s_kernel,
        grid=(N_HEADS,),
        in_specs=[head_rows(0), head_rows(D_MIX_HALF), head_rows(2 * D_MIX_HALF), head_rows(ROW_Z),
                  pl.BlockSpec((N_HEADS, ntok), lambda h: (ROW_BETA // N_HEADS, 0)),
                  pl.BlockSpec((N_HEADS, ntok), lambda h: (ROW_A // N_HEADS, 0)),
                  conv_rows(0), conv_rows(D_MIX_HALF), conv_rows(2 * D_MIX_HALF),
                  tap_rows(0), tap_rows(D_MIX_HALF), tap_rows(2 * D_MIX_HALF),
                  full2((N_HEADS, nb)), full2((N_HEADS, nb)), full2((HEAD_DIM, nb)),
                  pl.BlockSpec((1, HEAD_DIM, HEAD_DIM, nb), lambda h: (h, 0, 0, 0))],
        out_specs=[pl.BlockSpec((HEAD_DIM, ntok), lambda h: (h, 0)),
                   pl.BlockSpec((nb, HEAD_DIM * HEAD_DIM), lambda h: (0, h))],
        out_shape=[jax.ShapeDtypeStruct((D_MIX_HALF, ntok), F32),
                   jax.ShapeDtypeStruct((nb, N_HEADS * HEAD_DIM * HEAD_DIM), F32)],
        scratch_shapes=[pltpu.VMEM((HEAD_DIM, HEAD_DIM, nb), F32), pltpu.VMEM((2, HEAD_DIM, nb), F32)],
        compiler_params=pltpu.CompilerParams(dimension_semantics=("arbitrary",), vmem_limit_bytes=VMEM_LIMIT),
        name="gdn_s",
    )(cols_t, cols_t, cols_t, cols_t, cols_t, cols_t, conv_t, conv_t, conv_t, cw_t, cw_t, cw_t,
      alog_b, dtb_b, nw_b, s_t)


def _rwkv_s_kernel(r_ref, k_ref, v_ref, l_ref, pr_ref, pk_ref, pv_ref, pl_ref, mr_ref, mk_ref, mv_ref, ml_ref,
                   vec_ref, wup_ref, aup_ref, gup_ref, s0_ref, o_ref, sout_ref, s_scr, y_scr, v_scr):
    nb = s0_ref.shape[-1]
    nt = r_ref.shape[1] // nb

    def mixed(u_ref, p_ref, m_ref):
        cur = [u_ref[:, t * nb:(t + 1) * nb] for t in range(nt)]
        prev = [p_ref[...]] + cur[:-1]
        mu = m_ref[...]
        return [c + (p - c) * mu for c, p in zip(cur, prev)]

    rs = mixed(r_ref, pr_ref, mr_ref)
    ks = mixed(k_ref, pk_ref, mk_ref)
    vs = mixed(v_ref, pv_ref, mv_ref)
    ls = mixed(l_ref, pl_ref, ml_ref)
    w0, a0, k_k, k_a, r_k, ln_w, ln_b = (vec_ref[:, i:i + 1] for i in range(7))
    s_scr[...] = s0_ref[0]
    for t in range(nt):
        dw = ls[t][:LORA_W]
        da = ls[t][LORA_W:LORA_W + LORA_A]
        dg = ls[t][LORA_W + LORA_A:]
        w_log = -_softplus(-(w0 + _mm(wup_ref[...], jnp.tanh(dw)))) - 0.5
        w = jnp.exp(-jnp.exp(w_log))
        a = _sigmoid(a0 + _mm(aup_ref[...], da))
        gate = _mm(gup_ref[...], _sigmoid(dg))
        r, v = rs[t], vs[t]
        kk = ks[t] * k_k
        kk = kk * lax.rsqrt(jnp.sum(kk * kk, axis=0, keepdims=True) + EPS)
        k = ks[t] * (1.0 + (a - 1.0) * k_a)
        neg_kk = -kk
        kk_a = kk * a
        v_scr[...] = v

        def state_row(i, carry):
            row = s_scr[i]
            sa = jnp.sum(row * neg_kk, axis=0, keepdims=True)
            row = row * w + sa * kk_a + v_scr[pl.ds(i, 1), :] * k
            s_scr[i] = row
            y_scr[pl.ds(i, 1), :] = jnp.sum(row * r, axis=0, keepdims=True)
            return carry

        lax.fori_loop(0, HEAD_DIM, state_row, 0, unroll=4)
        y = y_scr[...]
        dy = y - jnp.mean(y, axis=0, keepdims=True)
        y = dy * lax.rsqrt(jnp.mean(dy * dy, axis=0, keepdims=True) + GN_EPS) * ln_w + ln_b
        bonus = jnp.sum(r * k * r_k, axis=0, keepdims=True) * v
        o_ref[:, t * nb:(t + 1) * nb] = (y + bonus) * gate
    _store_state_rows(s_scr, sout_ref)


def _rwkv_s(cols_t, shift_t, s_t, mu_b, vecs_t, w_up_t, a_up_t, g_up_t):
    ntok = cols_t.shape[1]
    nb = s_t.shape[-1]

    def head_rows(width, base):
        return pl.BlockSpec((HEAD_DIM, width), lambda h: (base // HEAD_DIM + h, 0))

    def lora_rows(width, base):
        return pl.BlockSpec((LORA_COLS, width), lambda h: (base // LORA_COLS, 0))

    return pl.pallas_call(
        _rwkv_s_kernel,
        grid=(N_HEADS,),
        in_specs=[head_rows(ntok, ROW_R), head_rows(ntok, ROW_R + D_MIX_HALF), head_rows(ntok, ROW_R + 2 * D_MIX_HALF),
                  lora_rows(ntok, ROW_LORA),
                  head_rows(nb, 0), head_rows(nb, D_MIX_HALF), head_rows(nb, 2 * D_MIX_HALF), lora_rows(nb, RWKV_RKV),
                  head_rows(nb, 0), head_rows(nb, D_MIX_HALF), head_rows(nb, 2 * D_MIX_HALF), lora_rows(nb, RWKV_RKV),
                  head_rows(SUBLANES, 0), head_rows(LORA_W, 0), head_rows(LORA_A, 0), head_rows(LORA_G, 0),
                  pl.BlockSpec((1, HEAD_DIM, HEAD_DIM, nb), lambda h: (h, 0, 0, 0))],
        out_specs=[pl.BlockSpec((HEAD_DIM, ntok), lambda h: (h, 0)),
                   pl.BlockSpec((nb, HEAD_DIM * HEAD_DIM), lambda h: (0, h))],
        out_shape=[jax.ShapeDtypeStruct((D_MIX_HALF, ntok), F32),
                   jax.ShapeDtypeStruct((nb, N_HEADS * HEAD_DIM * HEAD_DIM), F32)],
        scratch_shapes=[pltpu.VMEM((HEAD_DIM, HEAD_DIM, nb), F32), pltpu.VMEM((HEAD_DIM, nb), F32),
                        pltpu.VMEM((HEAD_DIM, nb), F32)],
        compiler_params=pltpu.CompilerParams(dimension_semantics=("arbitrary",), vmem_limit_bytes=VMEM_LIMIT),
        name="rwkv_s",
    )(cols_t, cols_t, cols_t, cols_t, shift_t, shift_t, shift_t, shift_t, mu_b, mu_b, mu_b, mu_b,
      vecs_t, w_up_t, a_up_t, g_up_t, s_t)


def _layer(x, mod, conv0, s_gdn0, shift0, s_rwkv0, p, *, bb, tt, moe_tile, bsz, chunk):
    sh_m, sc_m, gt_m, sh_f, sc_f, gt_f = mod
    cols = _inproj(x, sc_m, sh_m, p['norm_mix_w'], p['w_cols'], bb, tt)
    o_a, s_gdn = _gdn(cols, conv0, s_gdn0, p['gdn_conv_w'], p['alog_c'], p['dtb_c'], p['gdn_nw_rep'], bsz, chunk)
    o_b, s_rwkv = _rwkv(cols, shift0, s_rwkv0, p['rwkv_mu'], p['rwkv_vecs'], p['rwkv_w_up'], p['rwkv_a_up'],
                        p['rwkv_g_up'], bsz, chunk)
    x1, h2, wt = _outproj(o_a, o_b, x, gt_m, sc_f, sh_f, p['norm_ffn_w'], p['w_out_a'], p['w_out_b'],
                          p['w_router'], p['b_router'], *moe_tile)
    y = _moe(h2, wt, x1, gt_f, p['final_norm_w'], p['w_gu'], p['w_down'], *moe_tile)
    conv_new = cols[:, -(CONV_W - 1):, :GDN_QKV]
    shift_new = cols[:, -1, COL_RKV:COL_BA]
    return y, conv_new, s_gdn, shift_new, s_rwkv


def _sample_layer(x, mod, conv0, s_gdn0, shift0, s_rwkv0, p, ps):
    b, t, d = x.shape
    assert t >= CONV_W - 1
    sh_m, sc_m, gt_m, sh_f, sc_f, gt_f = mod
    x_tm = jnp.transpose(x, (1, 0, 2))
    cols_t = _inproj_t(x_tm, sc_m, sh_m, p['norm_mix_w'], p['w_cols'])
    seq_last = (1, 2, 3, 0)
    o_a, s_gdn = _gdn_s(cols_t, jnp.transpose(conv0, (1, 2, 0)), jnp.transpose(s_gdn0, seq_last),
                        ps['gdn_conv_w_t'], ps['alog_b'], ps['dtb_b'], ps['gdn_nw_b'])
    o_b, s_rwkv = _rwkv_s(cols_t, shift0.T, jnp.transpose(s_rwkv0, seq_last), ps['rwkv_mu_b'], ps['rwkv_vecs_t'],
                          ps['w_up_t'], ps['a_up_t'], ps['g_up_t'])
    x1, h2, wt = _outproj(o_a, o_b, x_tm, gt_m, sc_f, sh_f, p['norm_ffn_w'], p['w_out_a'], p['w_out_b'],
                          p['w_router'], p['b_router'], t, b)
    y = _moe(h2, wt, x1, gt_f, p['final_norm_w'], p['w_gu'], p['w_down'], t, b)
    keep = CONV_W - 1
    conv_new = jnp.transpose(cols_t[:GDN_QKV, (t - keep) * b:].reshape(GDN_QKV, keep, b), (2, 1, 0))
    shift_new = cols_t[ROW_R:ROW_BETA, (t - 1) * b:].T
    state_shape = (b, N_HEADS, HEAD_DIM, HEAD_DIM)
    return (jnp.transpose(y, (1, 0, 2)), conv_new, s_gdn.reshape(state_shape), shift_new,
            s_rwkv.reshape(state_shape))


def kernel(x_prompt, x_sample, state_gdn_conv, state_gdn, state_rwkv_shift, state_rwkv, c_prompt, c_sample, ada_w, ada_b, norm_mix_w, w_in, gdn_conv_w, gdn_a_log, gdn_dt_bias, gdn_norm_w, rwkv_mu, rwkv_w0, rwkv_w_up, rwkv_a0, rwkv_a_up, rwkv_g_up, rwkv_k_k, rwkv_k_a, rwkv_r_k, rwkv_ln_w, rwkv_ln_b, w_out, norm_ffn_w, router_group_w, router_group_b, router_expert_w, router_expert_b, expert_w_gate_up, expert_w_down, final_norm_w):
    depth = ada_w.shape[0]
    assert depth == 1
    l = 0
    b_p, t_p, d = x_prompt.shape
    b_s, t_s, _ = x_sample.shape
    gdn_cols = GDN_QKV + D_MIX_HALF + 2 * N_HEADS

    w = w_in[l]
    beta0 = GDN_QKV + D_MIX_HALF
    w_t = w.T.astype(BF16)
    w_cols = jnp.concatenate([w_t[:beta0], w_t[gdn_cols:], w_t[beta0:gdn_cols],
                              jnp.zeros((LANES - 2 * N_HEADS, d), BF16)], axis=0)
    head_lanes = lambda a, first: jnp.pad(a, (first, LANES - first - N_HEADS)).reshape(1, LANES)
    row = lambda a: a.reshape(1, -1)
    n_route = N_EXPERTS + N_GROUPS
    w_router = jnp.concatenate([router_expert_w[l], router_group_w[l], jnp.zeros((d, LANES - n_route), F32)], axis=1)
    w_router = jnp.concatenate(_split2(w_router), axis=1)
    b_router = jnp.concatenate([router_expert_b[l], router_group_b[l], jnp.zeros((LANES - n_route,), F32)])[None]
    p = {
        'norm_mix_w': norm_mix_w[l].reshape(1, 1, d), 'w_cols': w_cols,
        'gdn_conv_w': gdn_conv_w[l], 'alog_c': head_lanes(gdn_a_log[l], N_HEADS),
        'dtb_c': head_lanes(gdn_dt_bias[l], N_HEADS),
        'gdn_nw_rep': row(jnp.tile(gdn_norm_w[l], N_HEADS)),
        'rwkv_mu': row(rwkv_mu[l]),
        'rwkv_vecs': tuple(row(a) for a in (rwkv_w0[l], rwkv_a0[l], rwkv_k_k[l], rwkv_k_a[l], rwkv_r_k[l],
                                            rwkv_ln_w[l], rwkv_ln_b[l])),
        'rwkv_w_up': rwkv_w_up[l], 'rwkv_a_up': rwkv_a_up[l], 'rwkv_g_up': rwkv_g_up[l],
        'w_out_a': w_out[l][:D_MIX_HALF].astype(BF16), 'w_out_b': w_out[l][D_MIX_HALF:].astype(BF16),
        'norm_ffn_w': norm_ffn_w[l].reshape(1, 1, d), 'w_router': w_router, 'b_router': b_router,
        'w_gu': expert_w_gate_up[l].astype(BF16), 'w_down': expert_w_down[l].astype(BF16),
        'final_norm_w': final_norm_w.reshape(1, 1, d),
    }

    col = lambda a: a.reshape(-1, 1)
    lanes = lambda a: jnp.broadcast_to(col(a), (a.size, b_s))
    ps = {
        'gdn_conv_w_t': gdn_conv_w[l].T, 'alog_b': lanes(gdn_a_log[l]), 'dtb_b': lanes(gdn_dt_bias[l]),
        'gdn_nw_b': lanes(gdn_norm_w[l]), 'rwkv_mu_b': lanes(rwkv_mu[l]),
        'rwkv_vecs_t': jnp.concatenate([col(a) for a in (rwkv_w0[l], rwkv_a0[l], rwkv_k_k[l], rwkv_k_a[l],
                                                         rwkv_r_k[l], rwkv_ln_w[l], rwkv_ln_b[l], rwkv_ln_b[l])],
                                       axis=1),
        'w_up_t': rwkv_w_up[l].T, 'a_up_t': rwkv_a_up[l].T, 'g_up_t': rwkv_g_up[l].T,
    }

    mod = _ada(jnp.concatenate([c_prompt, c_sample], axis=0), ada_w[l], ada_b[l])
    mod_p = tuple(m.reshape(b_p, 1, d) for m in jnp.split(mod[:b_p], 6, axis=-1))
    mod_s = tuple(m.reshape(1, b_s, d) for m in jnp.split(mod[b_p:], 6, axis=-1))

    zc = jnp.zeros((b_p, CONV_W - 1, GDN_QKV), F32)
    zs = jnp.zeros((b_p, N_HEADS, HEAD_DIM, HEAD_DIM), F32)
    zsh = jnp.zeros((b_p, 1, RWKV_COLS), F32)
    y_p, conv_p, sg_p, shift_p, sr_p = _layer(x_prompt, mod_p, zc, zs, zsh, zs, p, bb=1, tt=PROJ_ROWS,
                                              moe_tile=(1, MOE_TILE_ROWS), bsz=SEQS_PER_STEP, chunk=CHUNK)

    y_s, conv_s, sg_s, shift_s, sr_s = _sample_layer(x_sample, mod_s, state_gdn_conv[l], state_gdn[l],
                                                     state_rwkv_shift[l], state_rwkv[l], p, ps)
    return (y_p, y_s, conv_p[None], sg_p[None], shift_p[None], sr_p[None],
            conv_s[None], sg_s[None], shift_s[None], sr_s[None])
```

```python
import jax
import jax.numpy as jnp
from jax import lax
from jax.experimental import pallas as pl
from jax.experimental.pallas import tpu as pltpu

F32 = jnp.float32
BF16 = jnp.bfloat16
HI = lax.Precision.HIGHEST

HEAD_DIM = 64
N_HEADS = 8
D_MIX_HALF = N_HEADS * HEAD_DIM
CONV_W = 4
LORA_W = 64
LORA_A = 64
LORA_G = 128
N_GROUPS = 4
EXP_PER_GROUP = 8
N_EXPERTS = N_GROUPS * EXP_PER_GROUP
D_EXPERT = 256
EPS = 1e-6
GN_EPS = HEAD_DIM * 1e-5

LANES = 128
SUBLANES = 8
BF16_ROWS = 16
GDN_QKV = 3 * D_MIX_HALF
RWKV_RKV = 3 * D_MIX_HALF
RWKV_COLS = RWKV_RKV + LORA_W + LORA_A + LORA_G
LORA_COLS = LORA_W + LORA_A + LORA_G

COL_Z = GDN_QKV
COL_RKV = COL_Z + D_MIX_HALF
COL_LORA = COL_RKV + RWKV_RKV
COL_BA = COL_LORA + LORA_COLS
N_COLS = COL_BA + LANES
CHUNK = 128

ADA_COLS_PER_STEP = 1536
PROJ_ROWS = 512
MOE_TILE_ROWS = 1024
SEQS_PER_STEP = 4

VMEM_LIMIT = 48 * 1024 * 1024
MOE_VMEM_LIMIT = 58 * 1024 * 1024


def _dot(a, b, prec=HI):
    return jnp.dot(a, b, preferred_element_type=F32, precision=prec)


def _bf(a):
    return a.astype(BF16)


def _mm(a, b):
    return jnp.dot(_bf(a), _bf(b), preferred_element_type=F32)


def _mm_nt(a, b):
    return lax.dot_general(_bf(a), _bf(b), (((1,), (1,)), ((), ())), preferred_element_type=F32)


def _mm_tn(a, b):
    return lax.dot_general(_bf(a), _bf(b), (((0,), (0,)), ((), ())), preferred_element_type=F32)


def _split2(a):
    hi = a.astype(BF16)
    return hi, (a - hi.astype(F32)).astype(BF16)


def _sigmoid(x):
    return 1.0 / (1.0 + jnp.exp(-x))


def _silu(x):
    return x * _sigmoid(x)


def _softplus(x):
    return jnp.maximum(x, 0.0) + jnp.log1p(jnp.exp(-jnp.abs(x)))


def _iota2(shape, dim):
    return lax.broadcasted_iota(jnp.int32, shape, dim)


def _head_block_ones():
    r = _iota2((LANES, LANES), 0)
    c = _iota2((LANES, LANES), 1)
    sh = HEAD_DIM.bit_length() - 1
    return jnp.where((r >> sh) == (c >> sh), 1.0, 0.0).astype(F32)


def _head_sums(x):
    ones = _bf(_head_block_ones())
    xb = _bf(x)
    parts = [jnp.dot(xb[:, p * LANES:(p + 1) * LANES], ones, preferred_element_type=F32)
             for p in range(x.shape[1] // LANES)]
    return jnp.concatenate(parts, axis=1)


def _expand_heads(x, base, terms):
    lane = _iota2((LANES, D_MIX_HALF), 0)
    head = _iota2((LANES, D_MIX_HALF), 1) >> (HEAD_DIM.bit_length() - 1)
    select = jnp.where(lane == base + head, 1.0, 0.0).astype(BF16)
    out = None
    rem = x
    for _ in range(terms):
        piece = rem.astype(BF16)
        part = jnp.dot(piece, select, preferred_element_type=F32)
        out = part if out is None else out + part
        rem = rem - piece.astype(F32)
    return out


def _cumsum_rows(x):
    n = x.shape[0]
    r = _iota2((n, n), 0)
    c = _iota2((n, n), 1)
    tri = jnp.where(r >= c, 1.0, 0.0).astype(BF16)
    x1 = x.astype(BF16)
    rem = x - x1.astype(F32)
    x2 = rem.astype(BF16)
    x3 = (rem - x2.astype(F32)).astype(BF16)
    return (jnp.dot(tri, x1, preferred_element_type=F32) + jnp.dot(tri, x2, preferred_element_type=F32)
            + jnp.dot(tri, x3, preferred_element_type=F32))


def _unit_lower_inverses(lows, n, mm):
    r = _iota2((n, n), 0)
    c = _iota2((n, n), 1)
    eye = jnp.where(r == c, 1.0, 0.0)
    pair = (r >> 1) == (c >> 1)
    invs = [eye - jnp.where(pair, low, 0.0) for low in lows]
    s = 2
    while s < n:
        sh = s.bit_length()
        sel = ((r >> sh) == (c >> sh)) & ((r & (2 * s - 1)) >= s) & ((c & (2 * s - 1)) < s)
        prods = [mm(jnp.where(sel, low, 0.0), inv) for low, inv in zip(lows, invs)]
        invs = [inv - mm(inv, prod) for inv, prod in zip(invs, prods)]
        s *= 2
    return invs


def _ada_kernel(c_ref, w_ref, b_ref, o_ref):
    o_ref[...] = _dot(_silu(c_ref[...]), w_ref[...]) + b_ref[...]


def _ada(c_all, ada_w, ada_b):
    n, d = c_all.shape
    nout = ada_w.shape[1]
    tn = ADA_COLS_PER_STEP
    return pl.pallas_call(
        _ada_kernel,
        grid=(nout // tn,),
        in_specs=[pl.BlockSpec((n, d), lambda j: (0, 0)),
                  pl.BlockSpec((d, tn), lambda j: (0, j)),
                  pl.BlockSpec((1, tn), lambda j: (0, j))],
        out_specs=pl.BlockSpec((n, tn), lambda j: (0, j)),
        out_shape=jax.ShapeDtypeStruct((n, nout), F32),
        compiler_params=pltpu.CompilerParams(dimension_semantics=("arbitrary",), vmem_limit_bytes=VMEM_LIMIT),
        name="ada",
    )(c_all, ada_w, ada_b.reshape(1, nout))


def _modulated_norm(x, nw, sc, sh):
    ms = jnp.mean(x * x, axis=-1, keepdims=True)
    return (x * lax.rsqrt(ms + EPS) * nw) * (1.0 + sc) + sh


def _inproj_kernel(x_ref, sc_ref, sh_ref, nw_ref, w_ref, o_ref):
    bb, tt, d = x_ref.shape
    h = _modulated_norm(x_ref[...], nw_ref[...], sc_ref[...], sh_ref[...])
    h = h.reshape(bb * tt, d).astype(BF16)
    o = _mm_nt(h, w_ref[...])
    o_ref[...] = o.reshape(bb, tt, o.shape[-1])


def _inproj(x, sc, sh, nw, w_rows, bb, tt):
    b, t, d = x.shape
    nc = w_rows.shape[0]
    return pl.pallas_call(
        _inproj_kernel,
        grid=(b // bb, t // tt),
        in_specs=[pl.BlockSpec((bb, tt, d), lambda i, j: (i, j, 0)),
                  pl.BlockSpec((bb, 1, d), lambda i, j: (i, 0, 0)),
                  pl.BlockSpec((bb, 1, d), lambda i, j: (i, 0, 0)),
                  pl.BlockSpec((1, 1, d), lambda i, j: (0, 0, 0)),
                  pl.BlockSpec((nc, d), lambda i, j: (0, 0))],
        out_specs=pl.BlockSpec((bb, tt, nc), lambda i, j: (i, j, 0)),
        out_shape=jax.ShapeDtypeStruct((b, t, nc), F32),
        compiler_params=pltpu.CompilerParams(dimension_semantics=("arbitrary",) * 2, vmem_limit_bytes=VMEM_LIMIT),
        name="inproj",
    )(x, sc, sh, nw, w_rows)


def _gdn_kernel(qkv_ref, z_ref, ba_ref, c0_ref, cw_ref, alog_ref, dtb_ref, nw_ref, s0_ref, o_ref, s_ref, hist_scr):
    bsz, chunk, _ = qkv_ref.shape
    hist_lo = SUBLANES - (CONV_W - 1)

    @pl.when(pl.program_id(1) == 0)
    def _():
        s_ref[...] = s0_ref[...]
        hist_scr[:, hist_lo:SUBLANES, :] = c0_ref[...]

    row = _iota2((chunk, chunk), 0)
    col = _iota2((chunk, chunk), 1)
    causal = row >= col
    strict = row > col

    def prep(bi):
        u = qkv_ref[bi]
        hist_scr[bi, SUBLANES:SUBLANES + chunk, :] = u
        y = u * cw_ref[CONV_W - 1:CONV_W, :]
        for i in range(CONV_W - 1):
            y = y + hist_scr[bi, hist_lo + i:hist_lo + i + chunk, :] * cw_ref[i:i + 1, :]
        hist_scr[bi, hist_lo:SUBLANES, :] = hist_scr[bi, hist_lo + chunk:SUBLANES + chunk, :]
        qkv = _silu(y)
        q, k, v = (qkv[:, i * D_MIX_HALF:(i + 1) * D_MIX_HALF] for i in range(3))
        q = q * lax.rsqrt(_head_sums(q * q) + EPS) * (HEAD_DIM ** -0.5)
        k = k * lax.rsqrt(_head_sums(k * k) + EPS)
        ba = ba_ref[bi]
        beta = _expand_heads(_sigmoid(ba), 0, 2)
        g = -jnp.exp(alog_ref[...]) * _softplus(ba + dtb_ref[...])
        gcum = _expand_heads(_cumsum_rows(g), N_HEADS, 3)
        eg = jnp.exp(gcum)
        g_last = gcum[chunk - 1:chunk, :]
        return dict(q=q, k=k, beta=beta, gcum=gcum, rhs_v=beta * v, rhs_k=beta * eg * k, q_dec=eg * q,
                    k_dec=jnp.exp(g_last - gcum) * k, eg_last=jnp.exp(g_last))

    pre = [prep(bi) for bi in range(bsz)]
    chains = [(bi, h) for bi in range(bsz) for h in range(N_HEADS)]
    n = range(len(chains))

    def head(name, i, width=HEAD_DIM):
        bi, h = chains[i]
        lanes = pre[bi][name][:, h * HEAD_DIM:(h + 1) * HEAD_DIM]
        return lanes if width == HEAD_DIM else jnp.concatenate([lanes] * (width // HEAD_DIM), axis=1)

    g_is = [head('gcum', i, chunk) for i in n]
    decays = [jnp.where(causal, jnp.exp(jnp.minimum(g_i - g_i.T, 0.0)), 0.0) for g_i in g_is]
    qk_kks = [_mm_nt(jnp.concatenate([head('k', i), head('q', i)], axis=0), head('k', i)) for i in n]
    lows = [jnp.where(strict, head('beta', i, chunk) * decays[i] * qk_kks[i][:chunk], 0.0) for i in n]
    t_invs = _unit_lower_inverses(lows, chunk, _mm)
    sols = [_mm(t_invs[i], jnp.concatenate([head('rhs_v', i), head('rhs_k', i)], axis=1)) for i in n]
    states = [s_ref[bi, h] for bi, h in chains]
    wss = [_mm(jnp.concatenate([sols[i][:, HEAD_DIM:], head('q_dec', i)], axis=0), states[i])
           for i in n]
    u_news = [sols[i][:, :HEAD_DIM] - wss[i][:chunk] for i in n]
    outs = [wss[i][chunk:] + _mm(qk_kks[i][chunk:] * decays[i], u_news[i]) for i in n]
    s_news = [head('eg_last', i) * states[i] + _mm_tn(head('k_dec', i), u_news[i]) for i in n]
    for i, (bi, h) in enumerate(chains):
        s_ref[bi, h] = s_news[i]
    for bi in range(bsz):
        o = jnp.concatenate(outs[bi * N_HEADS:(bi + 1) * N_HEADS], axis=1)
        o = o * lax.rsqrt(_head_sums(o * o) * (1.0 / HEAD_DIM) + EPS) * nw_ref[...]
        o_ref[bi] = o * _silu(z_ref[bi])


def _gdn(cols, conv0, s0, conv_w, alog_c, dtb_c, nw_rep, bsz, chunk):
    b, t, _ = cols.shape

    def col_spec(width, start):
        return pl.BlockSpec((bsz, chunk, width), lambda i, c: (i, c, start // width))

    state_spec = pl.BlockSpec((bsz, N_HEADS, HEAD_DIM, HEAD_DIM), lambda i, c: (i, 0, 0, 0))
    return pl.pallas_call(
        _gdn_kernel,
        grid=(b // bsz, t // chunk),
        in_specs=[col_spec(GDN_QKV, 0), col_spec(D_MIX_HALF, COL_Z), col_spec(LANES, COL_BA),
                  pl.BlockSpec((bsz, CONV_W - 1, GDN_QKV), lambda i, c: (i, 0, 0)),
                  pl.BlockSpec((CONV_W, GDN_QKV), lambda i, c: (0, 0)),
                  pl.BlockSpec((1, LANES), lambda i, c: (0, 0)), pl.BlockSpec((1, LANES), lambda i, c: (0, 0)),
                  pl.BlockSpec((1, D_MIX_HALF), lambda i, c: (0, 0)), state_spec],
        out_specs=[pl.BlockSpec((bsz, chunk, D_MIX_HALF), lambda i, c: (i, c, 0)), state_spec],
        out_shape=[jax.ShapeDtypeStruct((b, t, D_MIX_HALF), F32),
                   jax.ShapeDtypeStruct((b, N_HEADS, HEAD_DIM, HEAD_DIM), F32)],
        scratch_shapes=[pltpu.VMEM((bsz, SUBLANES + chunk, GDN_QKV), F32)],
        compiler_params=pltpu.CompilerParams(dimension_semantics=("arbitrary",) * 2, vmem_limit_bytes=VMEM_LIMIT),
        name="gdn",
    )(cols, cols, cols, conv0, conv_w, alog_c, dtb_c, nw_rep, s0)


def _rwkv_kernel(r_ref, k_ref, v_ref, l_ref, p_ref, pl_ref, mu_ref, mul_ref, w0_ref, a0_ref, kk_ref, ka_ref, rk_ref,
                 lnw_ref, lnb_ref, wup_ref, aup_ref, gup_ref, s0_ref, o_ref, s_ref, hist_scr, histl_scr):
    bsz, chunk, _ = r_ref.shape
    prev_row = SUBLANES - 1

    @pl.when(pl.program_id(1) == 0)
    def _():
        s_ref[...] = s0_ref[...]
        hist_scr[:, prev_row:SUBLANES, :] = p_ref[...]
        histl_scr[:, prev_row:SUBLANES, :] = pl_ref[...]

    row = _iota2((chunk, chunk), 0)
    col = _iota2((chunk, chunk), 1)
    causal = row >= col
    strict = row > col

    def shifted(bi, u, scr, m_ref):
        scr[bi, SUBLANES:SUBLANES + chunk, :] = u
        prev = scr[bi, prev_row:prev_row + chunk, :]
        scr[bi, prev_row:SUBLANES, :] = scr[bi, prev_row + chunk:SUBLANES + chunk, :]
        return u + (prev - u) * m_ref[...]

    def prep(bi):
        rkv = shifted(bi, jnp.concatenate([r_ref[bi], k_ref[bi], v_ref[bi]], axis=1), hist_scr, mu_ref)
        r, k, v = (rkv[:, i * D_MIX_HALF:(i + 1) * D_MIX_HALF] for i in range(3))
        xl = shifted(bi, l_ref[bi], histl_scr, mul_ref)
        dw = xl[:, :LORA_W]
        da = xl[:, LORA_W:LORA_W + LORA_A]
        dg = xl[:, LORA_W + LORA_A:]
        w_log = -_softplus(-(w0_ref[...] + _mm(jnp.tanh(dw), wup_ref[...]))) - 0.5
        lw = -jnp.exp(w_log)
        a = _sigmoid(a0_ref[...] + _mm(da, aup_ref[...]))
        gate = _mm(_sigmoid(dg), gup_ref[...])
        kk = k * kk_ref[...]
        kk = kk * lax.rsqrt(_head_sums(kk * kk) + EPS)
        k = k * (1.0 + (a - 1.0) * ka_ref[...])
        cum = _cumsum_rows(lw)
        e_out = jnp.exp(-cum)
        cum_last = cum[chunk - 1:chunk, :]
        e_rest = jnp.exp(cum_last - cum)
        return dict(r=r, k=k, v=v, gate=gate, a_t=-kk * jnp.exp(cum - lw), b_t=kk * a * e_out, k_t=k * e_out,
                    r_t=r * jnp.exp(cum), b_c=kk * a * e_rest, k_c=k * e_rest, e_last=jnp.exp(cum_last))

    pre = [prep(bi) for bi in range(bsz)]
    chains = [(bi, h) for bi in range(bsz) for h in range(N_HEADS)]
    n = range(len(chains))

    def head(name, i):
        bi, h = chains[i]
        return pre[bi][name][:, h * HEAD_DIM:(h + 1) * HEAD_DIM]

    ars = [jnp.concatenate([head('a_t', i), head('r_t', i)], axis=0) for i in n]
    bks = [jnp.concatenate([head('b_t', i), head('k_t', i)], axis=0) for i in n]
    ms = [_mm_nt(ars[i], bks[i]) for i in n]
    t_invs = _unit_lower_inverses([jnp.where(strict, -m[:chunk, :chunk], 0.0) for m in ms], chunk, _mm)
    states = [s_ref[bi, h] for bi, h in chains]
    arss = [_mm_nt(ars[i], states[i]) for i in n]
    vhs = [head('v', i) for i in n]
    akvs = [_mm(jnp.where(strict, ms[i][:chunk, chunk:], 0.0), vhs[i]) for i in n]
    us = [_mm(t_invs[i], arss[i][:chunk] + akvs[i]) for i in n]
    uvs = [jnp.concatenate([us[i], vhs[i]], axis=0) for i in n]
    causal2 = _iota2((chunk, 2 * chunk), 0) >= (_iota2((chunk, 2 * chunk), 1) & (chunk - 1))
    ys = [arss[i][chunk:] + _mm(jnp.where(causal2, ms[i][chunk:], 0.0), uvs[i]) for i in n]
    s_news = [states[i] * head('e_last', i)
              + _mm_tn(uvs[i], jnp.concatenate([head('b_c', i), head('k_c', i)], axis=0)) for i in n]
    for i, (bi, h) in enumerate(chains):
        s_ref[bi, h] = s_news[i]
    for bi in range(bsz):
        p = pre[bi]
        y = jnp.concatenate(ys[bi * N_HEADS:(bi + 1) * N_HEADS], axis=1)
        mean = _head_sums(y) * (1.0 / HEAD_DIM)
        dy = y - mean
        var = _head_sums(dy * dy) * (1.0 / HEAD_DIM)
        y = dy * lax.rsqrt(var + GN_EPS) * lnw_ref[...] + lnb_ref[...]
        bonus = _head_sums(p['r'] * p['k'] * rk_ref[...]) * p['v']
        o_ref[bi] = (y + bonus) * p['gate']


def _rwkv(cols, shift0, s0, mu, vecs, w_up, a_up, g_up, bsz, chunk):
    b, t, _ = cols.shape

    def full2(shape):
        return pl.BlockSpec(shape, lambda i, c: (0, 0))

    state_spec = pl.BlockSpec((bsz, N_HEADS, HEAD_DIM, HEAD_DIM), lambda i, c: (i, 0, 0, 0))
    return pl.pallas_call(
        _rwkv_kernel,
        grid=(b // bsz, t // chunk),
        in_specs=[pl.BlockSpec((bsz, chunk, D_MIX_HALF), lambda i, c: (i, c, COL_RKV // D_MIX_HALF)),
                  pl.BlockSpec((bsz, chunk, D_MIX_HALF), lambda i, c: (i, c, COL_RKV // D_MIX_HALF + 1)),
                  pl.BlockSpec((bsz, chunk, D_MIX_HALF), lambda i, c: (i, c, COL_RKV // D_MIX_HALF + 2)),
                  pl.BlockSpec((bsz, chunk, LORA_COLS), lambda i, c: (i, c, COL_LORA // LORA_COLS)),
                  pl.BlockSpec((bsz, 1, RWKV_RKV), lambda i, c: (i, 0, 0)),
                  pl.BlockSpec((bsz, 1, LORA_COLS), lambda i, c: (i, 0, RWKV_RKV // LORA_COLS)),
                  pl.BlockSpec((1, RWKV_RKV), lambda i, c: (0, 0)),
                  pl.BlockSpec((1, LORA_COLS), lambda i, c: (0, RWKV_RKV // LORA_COLS))]
                 + [full2((1, D_MIX_HALF))] * 7
                 + [full2((LORA_W, D_MIX_HALF)), full2((LORA_A, D_MIX_HALF)), full2((LORA_G, D_MIX_HALF)),
                    state_spec],
        out_specs=[pl.BlockSpec((bsz, chunk, D_MIX_HALF), lambda i, c: (i, c, 0)), state_spec],
        out_shape=[jax.ShapeDtypeStruct((b, t, D_MIX_HALF), F32),
                   jax.ShapeDtypeStruct((b, N_HEADS, HEAD_DIM, HEAD_DIM), F32)],
        scratch_shapes=[pltpu.VMEM((bsz, SUBLANES + chunk, RWKV_RKV), F32),
                        pltpu.VMEM((bsz, SUBLANES + chunk, LORA_COLS), F32)],
        compiler_params=pltpu.CompilerParams(dimension_semantics=("arbitrary",) * 2, vmem_limit_bytes=VMEM_LIMIT),
        name="rwkv",
    )(cols, cols, cols, cols, shift0, shift0, mu, mu, *vecs, w_up, a_up, g_up, s0)


def _mod_spec(m, bb, tt):
    d = m.shape[-1]
    if m.shape[1] == 1:
        return pl.BlockSpec((bb, 1, d), lambda i, j, *_: (i, 0, 0))
    return pl.BlockSpec((1, tt, d), lambda i, j, *_: (0, j, 0))


def _outproj_kernel(oa_ref, ob_ref, x_ref, gt_ref, sc_ref, sh_ref, nw_ref, wa_ref, wb_ref, wr_ref, br_ref,
                    x1_ref, h_ref, wt_ref):
    bb, tt, d = x_ref.shape
    n = bb * tt
    if len(oa_ref.shape) == 2:
        mixed = _mm_tn(oa_ref[...], wa_ref[...]) + _mm_tn(ob_ref[...], wb_ref[...])
    else:
        oa = oa_ref[...].reshape(n, D_MIX_HALF).astype(BF16)
        ob = ob_ref[...].reshape(n, D_MIX_HALF).astype(BF16)
        mixed = (jnp.dot(oa, wa_ref[...], preferred_element_type=F32)
                 + jnp.dot(ob, wb_ref[...], preferred_element_type=F32))
    x1 = x_ref[...] + gt_ref[...] * mixed.reshape(bb, tt, d)
    x1_ref[...] = x1
    h = _modulated_norm(x1, nw_ref[...], sc_ref[...], sh_ref[...]).reshape(n, d)
    h_ref[...] = h.reshape(bb, tt, d).astype(BF16)

    h_hi, h_lo = _split2(h)
    hw = jnp.dot(h_hi, wr_ref[...], preferred_element_type=F32)
    logits = (hw[:, :LANES] + hw[:, LANES:] + jnp.dot(h_lo, wr_ref[:, :LANES], preferred_element_type=F32)
              + br_ref[...])
    lane_i = _iota2((n, LANES), 1)
    lane = lane_i.astype(F32)
    lane_grp = (lane_i >> (EXP_PER_GROUP.bit_length() - 1)).astype(F32)
    neg = jnp.float32(-jnp.inf)
    big = jnp.float32(LANES)
    is_grp = (lane_i >= N_EXPERTS) & (lane_i < N_EXPERTS + N_GROUPS)
    gl = jnp.where(is_grp, logits, neg)
    gmax = jnp.max(gl, axis=-1, keepdims=True)
    grp = jnp.min(jnp.where(gl == gmax, lane, big), axis=-1, keepdims=True) - N_EXPERTS
    g_prob = 1.0 / jnp.sum(jnp.where(is_grp, jnp.exp(gl - gmax), 0.0), axis=-1, keepdims=True)
    in_grp = (lane_i < N_EXPERTS) & (lane_grp == grp)
    el = jnp.where(in_grp, logits, neg)
    m1 = jnp.max(el, axis=-1, keepdims=True)
    i1 = jnp.min(jnp.where(el == m1, lane, big), axis=-1, keepdims=True)
    el2 = jnp.where(lane == i1, neg, el)
    m2 = jnp.max(el2, axis=-1, keepdims=True)
    i2 = jnp.min(jnp.where(el2 == m2, lane, big), axis=-1, keepdims=True)
    e2 = jnp.exp(m2 - m1)
    p1 = g_prob / (1.0 + e2)
    p2 = g_prob * e2 / (1.0 + e2)
    wt = jnp.where(lane == i1, p1, 0.0) + jnp.where(lane == i2, p2, 0.0)
    wt_ref[...] = wt.reshape(bb, tt, LANES)


def _outproj(oa, ob, x, gt, sc, sh, nw, wa, wb, wr, br, bb, tt):
    b, t, d = x.shape
    grid = (b // bb, t // tt)

    def tok_spec(w):
        return pl.BlockSpec((bb, tt, w), lambda i, j: (i, j, 0))

    def full2(shape):
        return pl.BlockSpec(shape, lambda i, j: (0, 0))

    mod_spec = _mod_spec(gt, bb, tt)
    if oa.ndim == 2:
        assert grid == (1, 1)
        mix_spec = full2(oa.shape)
    else:
        mix_spec = tok_spec(D_MIX_HALF)
    return pl.pallas_call(
        _outproj_kernel,
        grid=grid,
        in_specs=[mix_spec, mix_spec, tok_spec(d), mod_spec, mod_spec, mod_spec,
                  pl.BlockSpec((1, 1, d), lambda i, j: (0, 0, 0)),
                  full2((D_MIX_HALF, d)), full2((D_MIX_HALF, d)), full2((d, 2 * LANES)), full2((1, LANES))],
        out_specs=[tok_spec(d), tok_spec(d), tok_spec(LANES)],
        out_shape=[jax.ShapeDtypeStruct((b, t, d), F32), jax.ShapeDtypeStruct((b, t, d), BF16),
                   jax.ShapeDtypeStruct((b, t, LANES), F32)],
        compiler_params=pltpu.CompilerParams(dimension_semantics=("arbitrary",) * 2, vmem_limit_bytes=VMEM_LIMIT),
        name="outproj",
    )(oa, ob, x, gt, sc, sh, nw, wa, wb, wr, br)


MOE_ROWS = 128
MOE_EXPERTS_PER_STEP = 4


def _moe_kernel(h_ref, wt_ref, x1_ref, gt_ref, fw_ref, wgu_ref, wd_ref, y_ref,
                xg_scr, gtw_scr, yw_scr, seg_scr):
    bb, tt, d = x1_ref.shape
    n = bb * tt
    npos = 2 * n
    e = pl.program_id(2)

    @pl.when(e == 0)
    def _():
        wt_t = wt_ref[...].reshape(n, LANES).T
        member = wt_t > 0.0
        ones = jnp.where(member, 1.0, 0.0)
        upper = jnp.where(_iota2((n, n), 0) < _iota2((n, n), 1), 1.0, 0.0).astype(BF16)
        rank = jnp.dot(_bf(ones), upper, preferred_element_type=F32)
        cnt = jnp.broadcast_to(jnp.sum(ones, axis=1, keepdims=True), (LANES, LANES))
        lower = jnp.where(_iota2((LANES, LANES), 0) > _iota2((LANES, LANES), 1), 1.0, 0.0)
        off = _dot(lower, cnt)
        first_i = off.astype(jnp.int32)
        last_i = (off + cnt).astype(jnp.int32)
        for x in range(N_EXPERTS):
            seg_scr[x] = first_i[x, 0]
            seg_scr[N_EXPERTS + x] = last_i[x, 0]
        pos = off[:, :1] + rank
        pos1 = jnp.min(jnp.where(member, pos, jnp.float32(4 * n)), axis=0, keepdims=True)
        pos2 = jnp.max(jnp.where(member, pos, -1.0), axis=0, keepdims=True)
        w1 = jnp.sum(jnp.where(member & (pos == pos1), wt_t, 0.0), axis=0, keepdims=True)
        w2 = jnp.sum(jnp.where(member & (pos == pos2), wt_t, 0.0), axis=0, keepdims=True)
        w2 = jnp.where(pos2 != pos1, w2, 0.0)

        def onehot_block(b, carry):
            p = (b * MOE_ROWS + _iota2((MOE_ROWS, n), 0)).astype(F32)
            rows = pl.ds(pl.multiple_of(b * MOE_ROWS, MOE_ROWS), MOE_ROWS)
            yw_scr[rows, :n] = jnp.where((pos1 == p) | (pos2 == p), 1.0, 0.0).astype(BF16)
            return carry

        lax.fori_loop(0, npos // MOE_ROWS, onehot_block, 0)
        xg_scr[:npos, :] = jnp.dot(yw_scr[:npos, :n], h_ref[...].reshape(n, d),
                                   preferred_element_type=F32).astype(BF16)
        xg_scr[npos:, :] = jnp.zeros((MOE_ROWS, d), BF16)
        z_t = jnp.concatenate([pos1, pos2, w1, w2, jnp.zeros((LANES - 4, n), F32)], axis=0).T
        p1c, p2c, w1c, w2c = (z_t[:, i:i + 1] for i in range(4))
        for b in range(npos // LANES):
            p = (b * LANES + _iota2((n, LANES), 1)).astype(F32)
            gtw_scr[:, b * LANES:(b + 1) * LANES] = (jnp.where(p1c == p, w1c, 0.0)
                                                     + jnp.where(p2c == p, w2c, 0.0)).astype(BF16)
        yw_scr[...] = jnp.zeros_like(yw_scr)

    es = range(wgu_ref.shape[0])
    firsts = [seg_scr[e * len(es) + j] for j in es]
    lasts = [seg_scr[N_EXPERTS + e * len(es) + j] for j in es]
    starts = [(f // BF16_ROWS) * BF16_ROWS for f in firsts]
    windows = [(lasts[j] - starts[j] + MOE_ROWS - 1) // MOE_ROWS for j in es]

    def window_rows(r0):
        return pl.ds(pl.multiple_of(r0, BF16_ROWS), MOE_ROWS)

    def store_rows(j, r0, y):
        p = r0 + _iota2((MOE_ROWS, 1), 0)
        rows = window_rows(r0)
        yw_scr[rows, :] = jnp.where((p >= firsts[j]) & (p < lasts[j]), y, yw_scr[rows, :])

    gus = [jnp.dot(xg_scr[window_rows(starts[j]), :], wgu_ref[j], preferred_element_type=F32) for j in es]
    acts = [(_silu(gu[:, :D_EXPERT]) * gu[:, D_EXPERT:]).astype(BF16) for gu in gus]
    ys = [jnp.dot(acts[j], wd_ref[j], preferred_element_type=F32).astype(BF16) for j in es]
    for j in es:
        store_rows(j, starts[j], ys[j])

    most = windows[0]
    for j in es[1:]:
        most = jnp.maximum(most, windows[j])

    @pl.when(most > 1)
    def _():
        for j in es:
            def more_windows(w, carry, j=j):
                r0 = starts[j] + w * MOE_ROWS
                gu = jnp.dot(xg_scr[window_rows(r0), :], wgu_ref[j], preferred_element_type=F32)
                act = (_silu(gu[:, :D_EXPERT]) * gu[:, D_EXPERT:]).astype(BF16)
                store_rows(j, r0, jnp.dot(act, wd_ref[j], preferred_element_type=F32).astype(BF16))
                return carry

            lax.fori_loop(1, windows[j], more_windows, 0)

    @pl.when(e == pl.num_programs(2) - 1)
    def _():
        moe = jnp.dot(gtw_scr[...], yw_scr[:npos, :], preferred_element_type=F32)
        x2 = x1_ref[...] + gt_ref[...] * moe.reshape(bb, tt, d)
        ms = jnp.mean(x2 * x2, axis=-1, keepdims=True)
        y_ref[...] = x2 * lax.rsqrt(ms + EPS) * fw_ref[...]


def _moe(h, wt, x1, gt, fw, w_gu, w_down, bb, tt):
    b, t, d = x1.shape
    es = MOE_EXPERTS_PER_STEP
    grid = (b // bb, t // tt, N_EXPERTS // es)
    npos = 2 * bb * tt

    def tok_spec(w):
        return pl.BlockSpec((bb, tt, w), lambda i, j, e: (i, j, 0))

    return pl.pallas_call(
        _moe_kernel,
        grid=grid,
        in_specs=[tok_spec(d), tok_spec(LANES), tok_spec(d),
                  _mod_spec(gt, bb, tt),
                  pl.BlockSpec((1, 1, d), lambda i, j, e: (0, 0, 0)),
                  pl.BlockSpec((es, d, 2 * D_EXPERT), lambda i, j, e: (e, 0, 0)),
                  pl.BlockSpec((es, D_EXPERT, d), lambda i, j, e: (e, 0, 0))],
        out_specs=tok_spec(d),
        out_shape=jax.ShapeDtypeStruct((b, t, d), F32),
        scratch_shapes=[pltpu.VMEM((npos + MOE_ROWS, d), BF16), pltpu.VMEM((bb * tt, npos), BF16),
                        pltpu.VMEM((npos + MOE_ROWS, d), BF16), pltpu.SMEM((2 * N_EXPERTS,), jnp.int32)],
        compiler_params=pltpu.CompilerParams(dimension_semantics=("arbitrary",) * 3,
                                             vmem_limit_bytes=MOE_VMEM_LIMIT),
        name="moe",
    )(h, wt, x1, gt, fw, w_gu, w_down)


ROW_Z = COL_Z
ROW_R = COL_RKV
ROW_LORA = COL_LORA
ROW_BETA = COL_BA
ROW_A = COL_BA + N_HEADS


def _inproj_t_kernel(x_ref, sc_ref, sh_ref, nw_ref, w_ref, o_ref):
    t, b, d = x_ref.shape
    h = _modulated_norm(x_ref[...], nw_ref[...], sc_ref[...], sh_ref[...]).reshape(t * b, d)
    o_ref[...] = _mm_nt(w_ref[...], h)


def _inproj_t(x, sc, sh, nw, w_rows):
    t, b, d = x.shape
    nr = w_rows.shape[0]

    def full(shape):
        return pl.BlockSpec(shape, lambda i: (0,) * len(shape))

    return pl.pallas_call(
        _inproj_t_kernel,
        grid=(1,),
        in_specs=[full((t, b, d)), full((1, b, d)), full((1, b, d)), full((1, 1, d)), full((nr, d))],
        out_specs=full((nr, t * b)),
        out_shape=jax.ShapeDtypeStruct((nr, t * b), F32),
        compiler_params=pltpu.CompilerParams(dimension_semantics=("arbitrary",), vmem_limit_bytes=VMEM_LIMIT),
        name="inproj_t",
    )(x, sc, sh, nw, w_rows)


def _store_state_rows(s_scr, sout_ref):
    for j in range(HEAD_DIM // 2):
        pair = jnp.concatenate([s_scr[2 * j], s_scr[2 * j + 1]], axis=0)
        sout_ref[:, j * LANES:(j + 1) * LANES] = pair.T


def _gdn_s_kernel(q_ref, k_ref, v_ref, z_ref, b_ref, a_ref, cq_ref, ck_ref, cv_ref, wq_ref, wk_ref, wv_ref,
                  alog_ref, dtb_ref, nw_ref, s0_ref, o_ref, sout_ref, s_scr, kq_scr):
    nb = s0_ref.shape[-1]
    nt = q_ref.shape[1] // nb
    h = pl.program_id(0)

    def tok(ref, t):
        return ref[:, t * nb:(t + 1) * nb]

    def conv(u_ref, c_ref, w_ref):
        full = [c_ref[i] for i in range(CONV_W - 1)] + [tok(u_ref, t) for t in range(nt)]
        taps = [w_ref[:, i:i + 1] for i in range(CONV_W)]
        outs = []
        for t in range(nt):
            y = full[t + CONV_W - 1] * taps[CONV_W - 1]
            for i in range(CONV_W - 1):
                y = y + full[t + i] * taps[i]
            outs.append(_silu(y))
        return outs

    qs = conv(q_ref, cq_ref, wq_ref)
    ks = conv(k_ref, ck_ref, wk_ref)
    vs = conv(v_ref, cv_ref, wv_ref)
    neg_rate = -jnp.exp(alog_ref[pl.ds(h, 1), :])
    dtb = dtb_ref[pl.ds(h, 1), :]
    beta_in = b_ref[pl.ds(h, 1), :]
    a_in = a_ref[pl.ds(h, 1), :]
    s_scr[...] = s0_ref[0]
    zero = jnp.zeros((HEAD_DIM, nb), F32)
    for t in range(nt):
        q = qs[t] * lax.rsqrt(jnp.sum(qs[t] * qs[t], axis=0, keepdims=True) + EPS) * (HEAD_DIM ** -0.5)
        k = ks[t] * lax.rsqrt(jnp.sum(ks[t] * ks[t], axis=0, keepdims=True) + EPS)
        beta = _sigmoid(beta_in[:, t * nb:(t + 1) * nb])
        decay = jnp.exp(neg_rate * _softplus(a_in[:, t * nb:(t + 1) * nb] + dtb))
        kq_scr[0] = k
        kq_scr[1] = q

        def decay_and_project(i, acc):
            row = s_scr[i] * decay
            s_scr[i] = row
            return acc + kq_scr[0, pl.ds(i, 1), :] * row

        k_s = lax.fori_loop(0, HEAD_DIM, decay_and_project, zero, unroll=8)
        u = beta * (vs[t] - k_s)

        def update_and_read(i, acc):
            row = s_scr[i] + kq_scr[0, pl.ds(i, 1), :] * u
            s_scr[i] = row
            return acc + kq_scr[1, pl.ds(i, 1), :] * row

        o = lax.fori_loop(0, HEAD_DIM, update_and_read, zero, unroll=8)
        o = o * lax.rsqrt(jnp.mean(o * o, axis=0, keepdims=True) + EPS) * nw_ref[...]
        o_ref[:, t * nb:(t + 1) * nb] = o * _silu(tok(z_ref, t))
    _store_state_rows(s_scr, sout_ref)


def _gdn_s(cols_t, conv_t, s_t, cw_t, alog_b, dtb_b, nw_b):
    ntok = cols_t.shape[1]
    nb = s_t.shape[-1]

    def head_rows(base):
        return pl.BlockSpec((HEAD_DIM, ntok), lambda h: (base // HEAD_DIM + h, 0))

    def conv_rows(base):
        return pl.BlockSpec((CONV_W - 1, HEAD_DIM, nb), lambda h: (0, base // HEAD_DIM + h, 0))

    def tap_rows(base):
        return pl.BlockSpec((HEAD_DIM, CONV_W), lambda h: (base // HEAD_DIM + h, 0))

    def full2(shape):
        return pl.BlockSpec(shape, lambda h: (0, 0))

    return pl.pallas_call(
        _gdn_s_kernel,
        grid=(N_HEADS,),
        in_specs=[head_rows(0), head_rows(D_MIX_HALF), head_rows(2 * D_MIX_HALF), head_rows(ROW_Z),
                  pl.BlockSpec((N_HEADS, ntok), lambda h: (ROW_BETA // N_HEADS, 0)),
                  pl.BlockSpec((N_HEADS, ntok), lambda h: (ROW_A // N_HEADS, 0)),
                  conv_rows(0), conv_rows(D_MIX_HALF), conv_rows(2 * D_MIX_HALF),
                  tap_rows(0), tap_rows(D_MIX_HALF), tap_rows(2 * D_MIX_HALF),
                  full2((N_HEADS, nb)), full2((N_HEADS, nb)), full2((HEAD_DIM, nb)),
                  pl.BlockSpec((1, HEAD_DIM, HEAD_DIM, nb), lambda h: (h, 0, 0, 0))],
        out_specs=[pl.BlockSpec((HEAD_DIM, ntok), lambda h: (h, 0)),
                   pl.BlockSpec((nb, HEAD_DIM * HEAD_DIM), lambda h: (0, h))],
        out_shape=[jax.ShapeDtypeStruct((D_MIX_HALF, ntok), F32),
                   jax.ShapeDtypeStruct((nb, N_HEADS * HEAD_DIM * HEAD_DIM), F32)],
        scratch_shapes=[pltpu.VMEM((HEAD_DIM, HEAD_DIM, nb), F32), pltpu.VMEM((2, HEAD_DIM, nb), F32)],
        compiler_params=pltpu.CompilerParams(dimension_semantics=("arbitrary",), vmem_limit_bytes=VMEM_LIMIT),
        name="gdn_s",
    )(cols_t, cols_t, cols_t, cols_t, cols_t, cols_t, conv_t, conv_t, conv_t, cw_t, cw_t, cw_t,
      alog_b, dtb_b, nw_b, s_t)


def _rwkv_s_kernel(r_ref, k_ref, v_ref, l_ref, pr_ref, pk_ref, pv_ref, pl_ref, mr_ref, mk_ref, mv_ref, ml_ref,
                   vec_ref, wup_ref, aup_ref, gup_ref, s0_ref, o_ref, sout_ref, s_scr, y_scr, v_scr):
    nb = s0_ref.shape[-1]
    nt = r_ref.shape[1] // nb

    def mixed(u_ref, p_ref, m_ref):
        cur = [u_ref[:, t * nb:(t + 1) * nb] for t in range(nt)]
        prev = [p_ref[...]] + cur[:-1]
        mu = m_ref[...]
        return [c + (p - c) * mu for c, p in zip(cur, prev)]

    rs = mixed(r_ref, pr_ref, mr_ref)
    ks = mixed(k_ref, pk_ref, mk_ref)
    vs = mixed(v_ref, pv_ref, mv_ref)
    ls = mixed(l_ref, pl_ref, ml_ref)
    w0, a0, k_k, k_a, r_k, ln_w, ln_b = (vec_ref[:, i:i + 1] for i in range(7))
    s_scr[...] = s0_ref[0]
    for t in range(nt):
        dw = ls[t][:LORA_W]
        da = ls[t][LORA_W:LORA_W + LORA_A]
        dg = ls[t][LORA_W + LORA_A:]
        w_log = -_softplus(-(w0 + _mm(wup_ref[...], jnp.tanh(dw)))) - 0.5
        w = jnp.exp(-jnp.exp(w_log))
        a = _sigmoid(a0 + _mm(aup_ref[...], da))
        gate = _mm(gup_ref[...], _sigmoid(dg))
        r, v = rs[t], vs[t]
        kk = ks[t] * k_k
        kk = kk * lax.rsqrt(jnp.sum(kk * kk, axis=0, keepdims=True) + EPS)
        k = ks[t] * (1.0 + (a - 1.0) * k_a)
        neg_kk = -kk
        kk_a = kk * a
        v_scr[...] = v

        def state_row(i, carry):
            row = s_scr[i]
            sa = jnp.sum(row * neg_kk, axis=0, keepdims=True)
            row = row * w + sa * kk_a + v_scr[pl.ds(i, 1), :] * k
            s_scr[i] = row
            y_scr[pl.ds(i, 1), :] = jnp.sum(row * r, axis=0, keepdims=True)
            return carry

        lax.fori_loop(0, HEAD_DIM, state_row, 0, unroll=4)
        y = y_scr[...]
        dy = y - jnp.mean(y, axis=0, keepdims=True)
        y = dy * lax.rsqrt(jnp.mean(dy * dy, axis=0, keepdims=True) + GN_EPS) * ln_w + ln_b
        bonus = jnp.sum(r * k * r_k, axis=0, keepdims=True) * v
        o_ref[:, t * nb:(t + 1) * nb] = (y + bonus) * gate
    _store_state_rows(s_scr, sout_ref)


def _rwkv_s(cols_t, shift_t, s_t, mu_b, vecs_t, w_up_t, a_up_t, g_up_t):
    ntok = cols_t.shape[1]
    nb = s_t.shape[-1]

    def head_rows(width, base):
        return pl.BlockSpec((HEAD_DIM, width), lambda h: (base // HEAD_DIM + h, 0))

    def lora_rows(width, base):
        return pl.BlockSpec((LORA_COLS, width), lambda h: (base // LORA_COLS, 0))

    return pl.pallas_call(
        _rwkv_s_kernel,
        grid=(N_HEADS,),
        in_specs=[head_rows(ntok, ROW_R), head_rows(ntok, ROW_R + D_MIX_HALF), head_rows(ntok, ROW_R + 2 * D_MIX_HALF),
                  lora_rows(ntok, ROW_LORA),
                  head_rows(nb, 0), head_rows(nb, D_MIX_HALF), head_rows(nb, 2 * D_MIX_HALF), lora_rows(nb, RWKV_RKV),
                  head_rows(nb, 0), head_rows(nb, D_MIX_HALF), head_rows(nb, 2 * D_MIX_HALF), lora_rows(nb, RWKV_RKV),
                  head_rows(SUBLANES, 0), head_rows(LORA_W, 0), head_rows(LORA_A, 0), head_rows(LORA_G, 0),
                  pl.BlockSpec((1, HEAD_DIM, HEAD_DIM, nb), lambda h: (h, 0, 0, 0))],
        out_specs=[pl.BlockSpec((HEAD_DIM, ntok), lambda h: (h, 0)),
                   pl.BlockSpec((nb, HEAD_DIM * HEAD_DIM), lambda h: (0, h))],
        out_shape=[jax.ShapeDtypeStruct((D_MIX_HALF, ntok), F32),
                   jax.ShapeDtypeStruct((nb, N_HEADS * HEAD_DIM * HEAD_DIM), F32)],
        scratch_shapes=[pltpu.VMEM((HEAD_DIM, HEAD_DIM, nb), F32), pltpu.VMEM((HEAD_DIM, nb), F32),
                        pltpu.VMEM((HEAD_DIM, nb), F32)],
        compiler_params=pltpu.CompilerParams(dimension_semantics=("arbitrary",), vmem_limit_bytes=VMEM_LIMIT),
        name="rwkv_s",
    )(cols_t, cols_t, cols_t, cols_t, shift_t, shift_t, shift_t, shift_t, mu_b, mu_b, mu_b, mu_b,
      vecs_t, w_up_t, a_up_t, g_up_t, s_t)


def _layer(x, mod, conv0, s_gdn0, shift0, s_rwkv0, p, *, bb, tt, moe_tile, bsz, chunk):
    sh_m, sc_m, gt_m, sh_f, sc_f, gt_f = mod
    cols = _inproj(x, sc_m, sh_m, p['norm_mix_w'], p['w_cols'], bb, tt)
    o_a, s_gdn = _gdn(cols, conv0, s_gdn0, p['gdn_conv_w'], p['alog_c'], p['dtb_c'], p['gdn_nw_rep'], bsz, chunk)
    o_b, s_rwkv = _rwkv(cols, shift0, s_rwkv0, p['rwkv_mu'], p['rwkv_vecs'], p['rwkv_w_up'], p['rwkv_a_up'],
                        p['rwkv_g_up'], bsz, chunk)
    x1, h2, wt = _outproj(o_a, o_b, x, gt_m, sc_f, sh_f, p['norm_ffn_w'], p['w_out_a'], p['w_out_b'],
                          p['w_router'], p['b_router'], *moe_tile)
    y = _moe(h2, wt, x1, gt_f, p['final_norm_w'], p['w_gu'], p['w_down'], *moe_tile)
    conv_new = cols[:, -(CONV_W - 1):, :GDN_QKV]
    shift_new = cols[:, -1, COL_RKV:COL_BA]
    return y, conv_new, s_gdn, shift_new, s_rwkv


def _sample_layer(x, mod, conv0, s_gdn0, shift0, s_rwkv0, p, ps):
    b, t, d = x.shape
    assert t >= CONV_W - 1
    sh_m, sc_m, gt_m, sh_f, sc_f, gt_f = mod
    x_tm = jnp.transpose(x, (1, 0, 2))
    cols_t = _inproj_t(x_tm, sc_m, sh_m, p['norm_mix_w'], p['w_cols'])
    seq_last = (1, 2, 3, 0)
    o_a, s_gdn = _gdn_s(cols_t, jnp.transpose(conv0, (1, 2, 0)), jnp.transpose(s_gdn0, seq_last),
                        ps['gdn_conv_w_t'], ps['alog_b'], ps['dtb_b'], ps['gdn_nw_b'])
    o_b, s_rwkv = _rwkv_s(cols_t, shift0.T, jnp.transpose(s_rwkv0, seq_last), ps['rwkv_mu_b'], ps['rwkv_vecs_t'],
                          ps['w_up_t'], ps['a_up_t'], ps['g_up_t'])
    x1, h2, wt = _outproj(o_a, o_b, x_tm, gt_m, sc_f, sh_f, p['norm_ffn_w'], p['w_out_a'], p['w_out_b'],
                          p['w_router'], p['b_router'], t, b)
    y = _moe(h2, wt, x1, gt_f, p['final_norm_w'], p['w_gu'], p['w_down'], t, b)
    keep = CONV_W - 1
    conv_new = jnp.transpose(cols_t[:GDN_QKV, (t - keep) * b:].reshape(GDN_QKV, keep, b), (2, 1, 0))
    shift_new = cols_t[ROW_R:ROW_BETA, (t - 1) * b:].T
    state_shape = (b, N_HEADS, HEAD_DIM, HEAD_DIM)
    return (jnp.transpose(y, (1, 0, 2)), conv_new, s_gdn.reshape(state_shape), shift_new,
            s_rwkv.reshape(state_shape))


def kernel(x_prompt, x_sample, state_gdn_conv, state_gdn, state_rwkv_shift, state_rwkv, c_prompt, c_sample, ada_w, ada_b, norm_mix_w, w_in, gdn_conv_w, gdn_a_log, gdn_dt_bias, gdn_norm_w, rwkv_mu, rwkv_w0, rwkv_w_up, rwkv_a0, rwkv_a_up, rwkv_g_up, rwkv_k_k, rwkv_k_a, rwkv_r_k, rwkv_ln_w, rwkv_ln_b, w_out, norm_ffn_w, router_group_w, router_group_b, router_expert_w, router_expert_b, expert_w_gate_up, expert_w_down, final_norm_w):
    depth = ada_w.shape[0]
    assert depth == 1
    l = 0
    b_p, t_p, d = x_prompt.shape
    b_s, t_s, _ = x_sample.shape
    gdn_cols = GDN_QKV + D_MIX_HALF + 2 * N_HEADS

    w = w_in[l]
    beta0 = GDN_QKV + D_MIX_HALF
    w_t = w.T.astype(BF16)
    w_cols = jnp.concatenate([w_t[:beta0], w_t[gdn_cols:], w_t[beta0:gdn_cols],
                              jnp.zeros((LANES - 2 * N_HEADS, d), BF16)], axis=0)
    head_lanes = lambda a, first: jnp.pad(a, (first, LANES - first - N_HEADS)).reshape(1, LANES)
    row = lambda a: a.reshape(1, -1)
    n_route = N_EXPERTS + N_GROUPS
    w_router = jnp.concatenate([router_expert_w[l], router_group_w[l], jnp.zeros((d, LANES - n_route), F32)], axis=1)
    w_router = jnp.concatenate(_split2(w_router), axis=1)
    b_router = jnp.concatenate([router_expert_b[l], router_group_b[l], jnp.zeros((LANES - n_route,), F32)])[None]
    p = {
        'norm_mix_w': norm_mix_w[l].reshape(1, 1, d), 'w_cols': w_cols,
        'gdn_conv_w': gdn_conv_w[l], 'alog_c': head_lanes(gdn_a_log[l], N_HEADS),
        'dtb_c': head_lanes(gdn_dt_bias[l], N_HEADS),
        'gdn_nw_rep': row(jnp.tile(gdn_norm_w[l], N_HEADS)),
        'rwkv_mu': row(rwkv_mu[l]),
        'rwkv_vecs': tuple(row(a) for a in (rwkv_w0[l], rwkv_a0[l], rwkv_k_k[l], rwkv_k_a[l], rwkv_r_k[l],
                                            rwkv_ln_w[l], rwkv_ln_b[l])),
        'rwkv_w_up': rwkv_w_up[l], 'rwkv_a_up': rwkv_a_up[l], 'rwkv_g_up': rwkv_g_up[l],
        'w_out_a': w_out[l][:D_MIX_HALF].astype(BF16), 'w_out_b': w_out[l][D_MIX_HALF:].astype(BF16),
        'norm_ffn_w': norm_ffn_w[l].reshape(1, 1, d), 'w_router': w_router, 'b_router': b_router,
        'w_gu': expert_w_gate_up[l].astype(BF16), 'w_down': expert_w_down[l].astype(BF16),
        'final_norm_w': final_norm_w.reshape(1, 1, d),
    }

    col = lambda a: a.reshape(-1, 1)
    lanes = lambda a: jnp.broadcast_to(col(a), (a.size, b_s))
    ps = {
        'gdn_conv_w_t': gdn_conv_w[l].T, 'alog_b': lanes(gdn_a_log[l]), 'dtb_b': lanes(gdn_dt_bias[l]),
        'gdn_nw_b': lanes(gdn_norm_w[l]), 'rwkv_mu_b': lanes(rwkv_mu[l]),
        'rwkv_vecs_t': jnp.concatenate([col(a) for a in (rwkv_w0[l], rwkv_a0[l], rwkv_k_k[l], rwkv_k_a[l],
                                                         rwkv_r_k[l], rwkv_ln_w[l], rwkv_ln_b[l], rwkv_ln_b[l])],
                                       axis=1),
        'w_up_t': rwkv_w_up[l].T, 'a_up_t': rwkv_a_up[l].T, 'g_up_t': rwkv_g_up[l].T,
    }

    mod = _ada(jnp.concatenate([c_prompt, c_sample], axis=0), ada_w[l], ada_b[l])
    mod_p = tuple(m.reshape(b_p, 1, d) for m in jnp.split(mod[:b_p], 6, axis=-1))
    mod_s = tuple(m.reshape(1, b_s, d) for m in jnp.split(mod[b_p:], 6, axis=-1))

    zc = jnp.zeros((b_p, CONV_W - 1, GDN_QKV), F32)
    zs = jnp.zeros((b_p, N_HEADS, HEAD_DIM, HEAD_DIM), F32)
    zsh = jnp.zeros((b_p, 1, RWKV_COLS), F32)
    y_p, conv_p, sg_p, shift_p, sr_p = _layer(x_prompt, mod_p, zc, zs, zsh, zs, p, bb=1, tt=PROJ_ROWS,
                                              moe_tile=(1, MOE_TILE_ROWS), bsz=SEQS_PER_STEP, chunk=CHUNK)

    y_s, conv_s, sg_s, shift_s, sr_s = _sample_layer(x_sample, mod_s, state_gdn_conv[l], state_gdn[l],
                                                     state_rwkv_shift[l], state_rwkv[l], p, ps)
    return (y_p, y_s, conv_p[None], sg_p[None], shift_p[None], sr_p[None],
            conv_s[None], sg_s[None], shift_s[None], sr_s[None])
```

```python
import jax
import jax.numpy as jnp
from jax import lax
from jax.experimental import pallas as pl
from jax.experimental.pallas import tpu as pltpu

F32 = jnp.float32
BF16 = jnp.bfloat16
HI = lax.Precision.HIGHEST

HEAD_DIM = 64
N_HEADS = 8
D_MIX_HALF = N_HEADS * HEAD_DIM
CONV_W = 4
LORA_W = 64
LORA_A = 64
LORA_G = 128
N_GROUPS = 4
EXP_PER_GROUP = 8
N_EXPERTS = N_GROUPS * EXP_PER_GROUP
D_EXPERT = 256
EPS = 1e-6
GN_EPS = HEAD_DIM * 1e-5

LANES = 128
SUBLANES = 8
BF16_ROWS = 16
GDN_QKV = 3 * D_MIX_HALF
RWKV_RKV = 3 * D_MIX_HALF
RWKV_COLS = RWKV_RKV + LORA_W + LORA_A + LORA_G
LORA_COLS = LORA_W + LORA_A + LORA_G

COL_Z = GDN_QKV
COL_RKV = COL_Z + D_MIX_HALF
COL_LORA = COL_RKV + RWKV_RKV
COL_BA = COL_LORA + LORA_COLS
N_COLS = COL_BA + LANES
CHUNK = 128

ADA_COLS_PER_STEP = 1536
PROJ_ROWS = 512
MOE_TILE_ROWS = 1024
SEQS_PER_STEP = 4

VMEM_LIMIT = 48 * 1024 * 1024
MOE_VMEM_LIMIT = 58 * 1024 * 1024


def _dot(a, b, prec=HI):
    return jnp.dot(a, b, preferred_element_type=F32, precision=prec)


def _bf(a):
    return a.astype(BF16)


def _mm(a, b):
    return jnp.dot(_bf(a), _bf(b), preferred_element_type=F32)


def _mm_nt(a, b):
    return lax.dot_general(_bf(a), _bf(b), (((1,), (1,)), ((), ())), preferred_element_type=F32)


def _mm_tn(a, b):
    return lax.dot_general(_bf(a), _bf(b), (((0,), (0,)), ((), ())), preferred_element_type=F32)


def _split2(a):
    hi = a.astype(BF16)
    return hi, (a - hi.astype(F32)).astype(BF16)


def _sigmoid(x):
    return 1.0 / (1.0 + jnp.exp(-x))


def _silu(x):
    return x * _sigmoid(x)


def _softplus(x):
    return jnp.maximum(x, 0.0) + jnp.log1p(jnp.exp(-jnp.abs(x)))


def _iota2(shape, dim):
    return lax.broadcasted_iota(jnp.int32, shape, dim)


def _head_block_ones():
    r = _iota2((LANES, LANES), 0)
    c = _iota2((LANES, LANES), 1)
    sh = HEAD_DIM.bit_length() - 1
    return jnp.where((r >> sh) == (c >> sh), 1.0, 0.0).astype(F32)


def _head_sums(x):
    ones = _bf(_head_block_ones())
    xb = _bf(x)
    parts = [jnp.dot(xb[:, p * LANES:(p + 1) * LANES], ones, preferred_element_type=F32)
             for p in range(x.shape[1] // LANES)]
    return jnp.concatenate(parts, axis=1)


def _expand_heads(x, base, terms):
    lane = _iota2((LANES, D_MIX_HALF), 0)
    head = _iota2((LANES, D_MIX_HALF), 1) >> (HEAD_DIM.bit_length() - 1)
    select = jnp.where(lane == base + head, 1.0, 0.0).astype(BF16)
    out = None
    rem = x
    for _ in range(terms):
        piece = rem.astype(BF16)
        part = jnp.dot(piece, select, preferred_element_type=F32)
        out = part if out is None else out + part
        rem = rem - piece.astype(F32)
    return out


def _cumsum_rows(x):
    n = x.shape[0]
    r = _iota2((n, n), 0)
    c = _iota2((n, n), 1)
    tri = jnp.where(r >= c, 1.0, 0.0).astype(BF16)
    x1 = x.astype(BF16)
    rem = x - x1.astype(F32)
    x2 = rem.astype(BF16)
    x3 = (rem - x2.astype(F32)).astype(BF16)
    return (jnp.dot(tri, x1, preferred_element_type=F32) + jnp.dot(tri, x2, preferred_element_type=F32)
            + jnp.dot(tri, x3, preferred_element_type=F32))


def _unit_lower_inverses(lows, n, mm):
    r = _iota2((n, n), 0)
    c = _iota2((n, n), 1)
    eye = jnp.where(r == c, 1.0, 0.0)
    pair = (r >> 1) == (c >> 1)
    invs = [eye - jnp.where(pair, low, 0.0) for low in lows]
    s = 2
    while s < n:
        sh = s.bit_length()
        sel = ((r >> sh) == (c >> sh)) & ((r & (2 * s - 1)) >= s) & ((c & (2 * s - 1)) < s)
        prods = [mm(jnp.where(sel, low, 0.0), inv) for low, inv in zip(lows, invs)]
        invs = [inv - mm(inv, prod) for inv, prod in zip(invs, prods)]
        s *= 2
    return invs


def _ada_kernel(c_ref, w_ref, b_ref, o_ref):
    o_ref[...] = _dot(_silu(c_ref[...]), w_ref[...]) + b_ref[...]


def _ada(c_all, ada_w, ada_b):
    n, d = c_all.shape
    nout = ada_w.shape[1]
    tn = ADA_COLS_PER_STEP
    return pl.pallas_call(
        _ada_kernel,
        grid=(nout // tn,),
        in_specs=[pl.BlockSpec((n, d), lambda j: (0, 0)),
                  pl.BlockSpec((d, tn), lambda j: (0, j)),
                  pl.BlockSpec((1, tn), lambda j: (0, j))],
        out_specs=pl.BlockSpec((n, tn), lambda j: (0, j)),
        out_shape=jax.ShapeDtypeStruct((n, nout), F32),
        compiler_params=pltpu.CompilerParams(dimension_semantics=("arbitrary",), vmem_limit_bytes=VMEM_LIMIT),
        name="ada",
    )(c_all, ada_w, ada_b.reshape(1, nout))


def _modulated_norm(x, nw, sc, sh):
    ms = jnp.mean(x * x, axis=-1, keepdims=True)
    return (x * lax.rsqrt(ms + EPS) * nw) * (1.0 + sc) + sh


def _inproj_kernel(x_ref, sc_ref, sh_ref, nw_ref, w_ref, o_ref):
    bb, tt, d = x_ref.shape
    h = _modulated_norm(x_ref[...], nw_ref[...], sc_ref[...], sh_ref[...])
    h = h.reshape(bb * tt, d).astype(BF16)
    o = _mm_nt(h, w_ref[...])
    o_ref[...] = o.reshape(bb, tt, o.shape[-1])


def _inproj(x, sc, sh, nw, w_rows, bb, tt):
    b, t, d = x.shape
    nc = w_rows.shape[0]
    return pl.pallas_call(
        _inproj_kernel,
        grid=(b // bb, t // tt),
        in_specs=[pl.BlockSpec((bb, tt, d), lambda i, j: (i, j, 0)),
                  pl.BlockSpec((bb, 1, d), lambda i, j: (i, 0, 0)),
                  pl.BlockSpec((bb, 1, d), lambda i, j: (i, 0, 0)),
                  pl.BlockSpec((1, 1, d), lambda i, j: (0, 0, 0)),
                  pl.BlockSpec((nc, d), lambda i, j: (0, 0))],
        out_specs=pl.BlockSpec((bb, tt, nc), lambda i, j: (i, j, 0)),
        out_shape=jax.ShapeDtypeStruct((b, t, nc), F32),
        compiler_params=pltpu.CompilerParams(dimension_semantics=("arbitrary",) * 2, vmem_limit_bytes=VMEM_LIMIT),
        name="inproj",
    )(x, sc, sh, nw, w_rows)


def _gdn_kernel(qkv_ref, z_ref, ba_ref, c0_ref, cw_ref, alog_ref, dtb_ref, nw_ref, s0_ref, o_ref, s_ref, hist_scr):
    bsz, chunk, _ = qkv_ref.shape
    hist_lo = SUBLANES - (CONV_W - 1)

    @pl.when(pl.program_id(1) == 0)
    def _():
        s_ref[...] = s0_ref[...]
        hist_scr[:, hist_lo:SUBLANES, :] = c0_ref[...]

    row = _iota2((chunk, chunk), 0)
    col = _iota2((chunk, chunk), 1)
    causal = row >= col
    strict = row > col

    def prep(bi):
        u = qkv_ref[bi]
        hist_scr[bi, SUBLANES:SUBLANES + chunk, :] = u
        y = u * cw_ref[CONV_W - 1:CONV_W, :]
        for i in range(CONV_W - 1):
            y = y + hist_scr[bi, hist_lo + i:hist_lo + i + chunk, :] * cw_ref[i:i + 1, :]
        hist_scr[bi, hist_lo:SUBLANES, :] = hist_scr[bi, hist_lo + chunk:SUBLANES + chunk, :]
        qkv = _silu(y)
        q, k, v = (qkv[:, i * D_MIX_HALF:(i + 1) * D_MIX_HALF] for i in range(3))
        q = q * lax.rsqrt(_head_sums(q * q) + EPS) * (HEAD_DIM ** -0.5)
        k = k * lax.rsqrt(_head_sums(k * k) + EPS)
        ba = ba_ref[bi]
        beta = _expand_heads(_sigmoid(ba), 0, 2)
        g = -jnp.exp(alog_ref[...]) * _softplus(ba + dtb_ref[...])
        gcum = _expand_heads(_cumsum_rows(g), N_HEADS, 3)
        eg = jnp.exp(gcum)
        g_last = gcum[chunk - 1:chunk, :]
        return dict(q=q, k=k, beta=beta, gcum=gcum, rhs_v=beta * v, rhs_k=beta * eg * k, q_dec=eg * q,
                    k_dec=jnp.exp(g_last - gcum) * k, eg_last=jnp.exp(g_last))

    pre = [prep(bi) for bi in range(bsz)]
    chains = [(bi, h) for bi in range(bsz) for h in range(N_HEADS)]
    n = range(len(chains))

    def head(name, i, width=HEAD_DIM):
        bi, h = chains[i]
        lanes = pre[bi][name][:, h * HEAD_DIM:(h + 1) * HEAD_DIM]
        return lanes if width == HEAD_DIM else jnp.concatenate([lanes] * (width // HEAD_DIM), axis=1)

    g_is = [head('gcum', i, chunk) for i in n]
    decays = [jnp.where(causal, jnp.exp(jnp.minimum(g_i - g_i.T, 0.0)), 0.0) for g_i in g_is]
    qk_kks = [_mm_nt(jnp.concatenate([head('k', i), head('q', i)], axis=0), head('k', i)) for i in n]
    lows = [jnp.where(strict, head('beta', i, chunk) * decays[i] * qk_kks[i][:chunk], 0.0) for i in n]
    t_invs = _unit_lower_inverses(lows, chunk, _mm)
    sols = [_mm(t_invs[i], jnp.concatenate([head('rhs_v', i), head('rhs_k', i)], axis=1)) for i in n]
    states = [s_ref[bi, h] for bi, h in chains]
    wss = [_mm(jnp.concatenate([sols[i][:, HEAD_DIM:], head('q_dec', i)], axis=0), states[i])
           for i in n]
    u_news = [sols[i][:, :HEAD_DIM] - wss[i][:chunk] for i in n]
    outs = [wss[i][chunk:] + _mm(qk_kks[i][chunk:] * decays[i], u_news[i]) for i in n]
    s_news = [head('eg_last', i) * states[i] + _mm_tn(head('k_dec', i), u_news[i]) for i in n]
    for i, (bi, h) in enumerate(chains):
        s_ref[bi, h] = s_news[i]
    for bi in range(bsz):
        o = jnp.concatenate(outs[bi * N_HEADS:(bi + 1) * N_HEADS], axis=1)
        o = o * lax.rsqrt(_head_sums(o * o) * (1.0 / HEAD_DIM) + EPS) * nw_ref[...]
        o_ref[bi] = o * _silu(z_ref[bi])


def _gdn(cols, conv0, s0, conv_w, alog_c, dtb_c, nw_rep, bsz, chunk):
    b, t, _ = cols.shape

    def col_spec(width, start):
        return pl.BlockSpec((bsz, chunk, width), lambda i, c: (i, c, start // width))

    state_spec = pl.BlockSpec((bsz, N_HEADS, HEAD_DIM, HEAD_DIM), lambda i, c: (i, 0, 0, 0))
    return pl.pallas_call(
        _gdn_kernel,
        grid=(b // bsz, t // chunk),
        in_specs=[col_spec(GDN_QKV, 0), col_spec(D_MIX_HALF, COL_Z), col_spec(LANES, COL_BA),
                  pl.BlockSpec((bsz, CONV_W - 1, GDN_QKV), lambda i, c: (i, 0, 0)),
                  pl.BlockSpec((CONV_W, GDN_QKV), lambda i, c: (0, 0)),
                  pl.BlockSpec((1, LANES), lambda i, c: (0, 0)), pl.BlockSpec((1, LANES), lambda i, c: (0, 0)),
                  pl.BlockSpec((1, D_MIX_HALF), lambda i, c: (0, 0)), state_spec],
        out_specs=[pl.BlockSpec((bsz, chunk, D_MIX_HALF), lambda i, c: (i, c, 0)), state_spec],
        out_shape=[jax.ShapeDtypeStruct((b, t, D_MIX_HALF), F32),
                   jax.ShapeDtypeStruct((b, N_HEADS, HEAD_DIM, HEAD_DIM), F32)],
        scratch_shapes=[pltpu.VMEM((bsz, SUBLANES + chunk, GDN_QKV), F32)],
        compiler_params=pltpu.CompilerParams(dimension_semantics=("arbitrary",) * 2, vmem_limit_bytes=VMEM_LIMIT),
        name="gdn",
    )(cols, cols, cols, conv0, conv_w, alog_c, dtb_c, nw_rep, s0)


def _rwkv_kernel(r_ref, k_ref, v_ref, l_ref, p_ref, pl_ref, mu_ref, mul_ref, w0_ref, a0_ref, kk_ref, ka_ref, rk_ref,
                 lnw_ref, lnb_ref, wup_ref, aup_ref, gup_ref, s0_ref, o_ref, s_ref, hist_scr, histl_scr):
    bsz, chunk, _ = r_ref.shape
    prev_row = SUBLANES - 1

    @pl.when(pl.program_id(1) == 0)
    def _():
        s_ref[...] = s0_ref[...]
        hist_scr[:, prev_row:SUBLANES, :] = p_ref[...]
        histl_scr[:, prev_row:SUBLANES, :] = pl_ref[...]

    row = _iota2((chunk, chunk), 0)
    col = _iota2((chunk, chunk), 1)
    causal = row >= col
    strict = row > col

    def shifted(bi, u, scr, m_ref):
        scr[bi, SUBLANES:SUBLANES + chunk, :] = u
        prev = scr[bi, prev_row:prev_row + chunk, :]
        scr[bi, prev_row:SUBLANES, :] = scr[bi, prev_row + chunk:SUBLANES + chunk, :]
        return u + (prev - u) * m_ref[...]

    def prep(bi):
        rkv = shifted(bi, jnp.concatenate([r_ref[bi], k_ref[bi], v_ref[bi]], axis=1), hist_scr, mu_ref)
        r, k, v = (rkv[:, i * D_MIX_HALF:(i + 1) * D_MIX_HALF] for i in range(3))
        xl = shifted(bi, l_ref[bi], histl_scr, mul_ref)
        dw = xl[:, :LORA_W]
        da = xl[:, LORA_W:LORA_W + LORA_A]
        dg = xl[:, LORA_W + LORA_A:]
        w_log = -_softplus(-(w0_ref[...] + _mm(jnp.tanh(dw), wup_ref[...]))) - 0.5
        lw = -jnp.exp(w_log)
        a = _sigmoid(a0_ref[...] + _mm(da, aup_ref[...]))
        gate = _mm(_sigmoid(dg), gup_ref[...])
        kk = k * kk_ref[...]
        kk = kk * lax.rsqrt(_head_sums(kk * kk) + EPS)
        k = k * (1.0 + (a - 1.0) * ka_ref[...])
        cum = _cumsum_rows(lw)
        e_out = jnp.exp(-cum)
        cum_last = cum[chunk - 1:chunk, :]
        e_rest = jnp.exp(cum_last - cum)
        return dict(r=r, k=k, v=v, gate=gate, a_t=-kk * jnp.exp(cum - lw), b_t=kk * a * e_out, k_t=k * e_out,
                    r_t=r * jnp.exp(cum), b_c=kk * a * e_rest, k_c=k * e_rest, e_last=jnp.exp(cum_last))

    pre = [prep(bi) for bi in range(bsz)]
    chains = [(bi, h) for bi in range(bsz) for h in range(N_HEADS)]
    n = range(len(chains))

    def head(name, i):
        bi, h = chains[i]
        return pre[bi][name][:, h * HEAD_DIM:(h + 1) * HEAD_DIM]

    ars = [jnp.concatenate([head('a_t', i), head('r_t', i)], axis=0) for i in n]
    bks = [jnp.concatenate([head('b_t', i), head('k_t', i)], axis=0) for i in n]
    ms = [_mm_nt(ars[i], bks[i]) for i in n]
    t_invs = _unit_lower_inverses([jnp.where(strict, -m[:chunk, :chunk], 0.0) for m in ms], chunk, _mm)
    states = [s_ref[bi, h] for bi, h in chains]
    arss = [_mm_nt(ars[i], states[i]) for i in n]
    vhs = [head('v', i) for i in n]
    akvs = [_mm(jnp.where(strict, ms[i][:chunk, chunk:], 0.0), vhs[i]) for i in n]
    us = [_mm(t_invs[i], arss[i][:chunk] + akvs[i]) for i in n]
    uvs = [jnp.concatenate([us[i], vhs[i]], axis=0) for i in n]
    causal2 = _iota2((chunk, 2 * chunk), 0) >= (_iota2((chunk, 2 * chunk), 1) & (chunk - 1))
    ys = [arss[i][chunk:] + _mm(jnp.where(causal2, ms[i][chunk:], 0.0), uvs[i]) for i in n]
    s_news = [states[i] * head('e_last', i)
              + _mm_tn(uvs[i], jnp.concatenate([head('b_c', i), head('k_c', i)], axis=0)) for i in n]
    for i, (bi, h) in enumerate(chains):
        s_ref[bi, h] = s_news[i]
    for bi in range(bsz):
        p = pre[bi]
        y = jnp.concatenate(ys[bi * N_HEADS:(bi + 1) * N_HEADS], axis=1)
        mean = _head_sums(y) * (1.0 / HEAD_DIM)
        dy = y - mean
        var = _head_sums(dy * dy) * (1.0 / HEAD_DIM)
        y = dy * lax.rsqrt(var + GN_EPS) * lnw_ref[...] + lnb_ref[...]
        bonus = _head_sums(p['r'] * p['k'] * rk_ref[...]) * p['v']
        o_ref[bi] = (y + bonus) * p['gate']


def _rwkv(cols, shift0, s0, mu, vecs, w_up, a_up, g_up, bsz, chunk):
    b, t, _ = cols.shape

    def full2(shape):
        return pl.BlockSpec(shape, lambda i, c: (0, 0))

    state_spec = pl.BlockSpec((bsz, N_HEADS, HEAD_DIM, HEAD_DIM), lambda i, c: (i, 0, 0, 0))
    return pl.pallas_call(
        _rwkv_kernel,
        grid=(b // bsz, t // chunk),
        in_specs=[pl.BlockSpec((bsz, chunk, D_MIX_HALF), lambda i, c: (i, c, COL_RKV // D_MIX_HALF)),
                  pl.BlockSpec((bsz, chunk, D_MIX_HALF), lambda i, c: (i, c, COL_RKV // D_MIX_HALF + 1)),
                  pl.BlockSpec((bsz, chunk, D_MIX_HALF), lambda i, c: (i, c, COL_RKV // D_MIX_HALF + 2)),
                  pl.BlockSpec((bsz, chunk, LORA_COLS), lambda i, c: (i, c, COL_LORA // LORA_COLS)),
                  pl.BlockSpec((bsz, 1, RWKV_RKV), lambda i, c: (i, 0, 0)),
                  pl.BlockSpec((bsz, 1, LORA_COLS), lambda i, c: (i, 0, RWKV_RKV // LORA_COLS)),
                  pl.BlockSpec((1, RWKV_RKV), lambda i, c: (0, 0)),
                  pl.BlockSpec((1, LORA_COLS), lambda i, c: (0, RWKV_RKV // LORA_COLS))]
                 + [full2((1, D_MIX_HALF))] * 7
                 + [full2((LORA_W, D_MIX_HALF)), full2((LORA_A, D_MIX_HALF)), full2((LORA_G, D_MIX_HALF)),
                    state_spec],
        out_specs=[pl.BlockSpec((bsz, chunk, D_MIX_HALF), lambda i, c: (i, c, 0)), state_spec],
        out_shape=[jax.ShapeDtypeStruct((b, t, D_MIX_HALF), F32),
                   jax.ShapeDtypeStruct((b, N_HEADS, HEAD_DIM, HEAD_DIM), F32)],
        scratch_shapes=[pltpu.VMEM((bsz, SUBLANES + chunk, RWKV_RKV), F32),
                        pltpu.VMEM((bsz, SUBLANES + chunk, LORA_COLS), F32)],
        compiler_params=pltpu.CompilerParams(dimension_semantics=("arbitrary",) * 2, vmem_limit_bytes=VMEM_LIMIT),
        name="rwkv",
    )(cols, cols, cols, cols, shift0, shift0, mu, mu, *vecs, w_up, a_up, g_up, s0)


def _mod_spec(m, bb, tt):
    d = m.shape[-1]
    if m.shape[1] == 1:
        return pl.BlockSpec((bb, 1, d), lambda i, j, *_: (i, 0, 0))
    return pl.BlockSpec((1, tt, d), lambda i, j, *_: (0, j, 0))


def _outproj_kernel(oa_ref, ob_ref, x_ref, gt_ref, sc_ref, sh_ref, nw_ref, wa_ref, wb_ref, wr_ref, br_ref,
                    x1_ref, h_ref, wt_ref):
    bb, tt, d = x_ref.shape
    n = bb * tt
    if len(oa_ref.shape) == 2:
        mixed = _mm_tn(oa_ref[...], wa_ref[...]) + _mm_tn(ob_ref[...], wb_ref[...])
    else:
        oa = oa_ref[...].reshape(n, D_MIX_HALF).astype(BF16)
        ob = ob_ref[...].reshape(n, D_MIX_HALF).astype(BF16)
        mixed = (jnp.dot(oa, wa_ref[...], preferred_element_type=F32)
                 + jnp.dot(ob, wb_ref[...], preferred_element_type=F32))
    x1 = x_ref[...] + gt_ref[...] * mixed.reshape(bb, tt, d)
    x1_ref[...] = x1
    h = _modulated_norm(x1, nw_ref[...], sc_ref[...], sh_ref[...]).reshape(n, d)
    h_ref[...] = h.reshape(bb, tt, d).astype(BF16)

    h_hi, h_lo = _split2(h)
    hw = jnp.dot(h_hi, wr_ref[...], preferred_element_type=F32)
    logits = (hw[:, :LANES] + hw[:, LANES:] + jnp.dot(h_lo, wr_ref[:, :LANES], preferred_element_type=F32)
              + br_ref[...])
    lane_i = _iota2((n, LANES), 1)
    lane = lane_i.astype(F32)
    lane_grp = (lane_i >> (EXP_PER_GROUP.bit_length() - 1)).astype(F32)
    neg = jnp.float32(-jnp.inf)
    big = jnp.float32(LANES)
    is_grp = (lane_i >= N_EXPERTS) & (lane_i < N_EXPERTS + N_GROUPS)
    gl = jnp.where(is_grp, logits, neg)
    gmax = jnp.max(gl, axis=-1, keepdims=True)
    grp = jnp.min(jnp.where(gl == gmax, lane, big), axis=-1, keepdims=True) - N_EXPERTS
    g_prob = 1.0 / jnp.sum(jnp.where(is_grp, jnp.exp(gl - gmax), 0.0), axis=-1, keepdims=True)
    in_grp = (lane_i < N_EXPERTS) & (lane_grp == grp)
    el = jnp.where(in_grp, logits, neg)
    m1 = jnp.max(el, axis=-1, keepdims=True)
    i1 = jnp.min(jnp.where(el == m1, lane, big), axis=-1, keepdims=True)
    el2 = jnp.where(lane == i1, neg, el)
    m2 = jnp.max(el2, axis=-1, keepdims=True)
    i2 = jnp.min(jnp.where(el2 == m2, lane, big), axis=-1, keepdims=True)
    e2 = jnp.exp(m2 - m1)
    p1 = g_prob / (1.0 + e2)
    p2 = g_prob * e2 / (1.0 + e2)
    wt = jnp.where(lane == i1, p1, 0.0) + jnp.where(lane == i2, p2, 0.0)
    wt_ref[...] = wt.reshape(bb, tt, LANES)


def _outproj(oa, ob, x, gt, sc, sh, nw, wa, wb, wr, br, bb, tt):
    b, t, d = x.shape
    grid = (b // bb, t // tt)

    def tok_spec(w):
        return pl.BlockSpec((bb, tt, w), lambda i, j: (i, j, 0))

    def full2(shape):
        return pl.BlockSpec(shape, lambda i, j: (0, 0))

    mod_spec = _mod_spec(gt, bb, tt)
    if oa.ndim == 2:
        assert grid == (1, 1)
        mix_spec = full2(oa.shape)
    else:
        mix_spec = tok_spec(D_MIX_HALF)
    return pl.pallas_call(
        _outproj_kernel,
        grid=grid,
        in_specs=[mix_spec, mix_spec, tok_spec(d), mod_spec, mod_spec, mod_spec,
                  pl.BlockSpec((1, 1, d), lambda i, j: (0, 0, 0)),
                  full2((D_MIX_HALF, d)), full2((D_MIX_HALF, d)), full2((d, 2 * LANES)), full2((1, LANES))],
        out_specs=[tok_spec(d), tok_spec(d), tok_spec(LANES)],
        out_shape=[jax.ShapeDtypeStruct((b, t, d), F32), jax.ShapeDtypeStruct((b, t, d), BF16),
                   jax.ShapeDtypeStruct((b, t, LANES), F32)],
        compiler_params=pltpu.CompilerParams(dimension_semantics=("arbitrary",) * 2, vmem_limit_bytes=VMEM_LIMIT),
        name="outproj",
    )(oa, ob, x, gt, sc, sh, nw, wa, wb, wr, br)


MOE_ROWS = 128
MOE_EXPERTS_PER_STEP = 4
MOE_WEIGHT_SLOTS = 3


def _moe_kernel(h_ref, wt_ref, x1_ref, gt_ref, fw_ref, wgu_hbm, wd_hbm, y_ref,
                xg_scr, gtw_scr, yw_scr, seg_scr, wgu_buf, wd_buf, w_sem):
    bb, tt, d = x1_ref.shape
    n = bb * tt
    npos = 2 * n
    e = pl.program_id(2)

    n_slots, per_step = wgu_buf.shape[0], wgu_buf.shape[1]
    n_esteps = pl.num_programs(2)
    step = (pl.program_id(0) * pl.num_programs(1) + pl.program_id(1)) * n_esteps + e
    n_steps = pl.num_programs(0) * pl.num_programs(1) * n_esteps

    def weight_copies(s):
        slot = s % n_slots
        first = (s % n_esteps) * per_step
        return (pltpu.make_async_copy(wgu_hbm.at[pl.ds(first, per_step)], wgu_buf.at[slot], w_sem.at[slot, 0]),
                pltpu.make_async_copy(wd_hbm.at[pl.ds(first, per_step)], wd_buf.at[slot], w_sem.at[slot, 1]))

    @pl.when(step == 0)
    def _():
        for s in range(n_slots - 1):
            @pl.when(s < n_steps)
            def _():
                for c in weight_copies(s):
                    c.start()

    @pl.when(step + n_slots - 1 < n_steps)
    def _():
        for c in weight_copies(step + n_slots - 1):
            c.start()

    @pl.when(e == 0)
    def _():
        wt_t = wt_ref[...].reshape(n, LANES).T
        member = wt_t > 0.0
        ones = jnp.where(member, 1.0, 0.0)
        upper = jnp.where(_iota2((n, n), 0) < _iota2((n, n), 1), 1.0, 0.0).astype(BF16)
        rank = jnp.dot(_bf(ones), upper, preferred_element_type=F32)
        cnt = jnp.broadcast_to(jnp.sum(ones, axis=1, keepdims=True), (LANES, LANES))
        lower = jnp.where(_iota2((LANES, LANES), 0) > _iota2((LANES, LANES), 1), 1.0, 0.0)
        off = _dot(lower, cnt)
        first_i = off.astype(jnp.int32)
        last_i = (off + cnt).astype(jnp.int32)
        for x in range(N_EXPERTS):
            seg_scr[x] = first_i[x, 0]
            seg_scr[N_EXPERTS + x] = last_i[x, 0]
        pos = off[:, :1] + rank
        pos1 = jnp.min(jnp.where(member, pos, jnp.float32(4 * n)), axis=0, keepdims=True)
        pos2 = jnp.max(jnp.where(member, pos, -1.0), axis=0, keepdims=True)
        w1 = jnp.sum(jnp.where(member & (pos == pos1), wt_t, 0.0), axis=0, keepdims=True)
        w2 = jnp.sum(jnp.where(member & (pos == pos2), wt_t, 0.0), axis=0, keepdims=True)
        w2 = jnp.where(pos2 != pos1, w2, 0.0)

        def onehot_block(b, carry):
            p = (b * MOE_ROWS + _iota2((MOE_ROWS, n), 0)).astype(F32)
            rows = pl.ds(pl.multiple_of(b * MOE_ROWS, MOE_ROWS), MOE_ROWS)
            yw_scr[rows, :n] = jnp.where((pos1 == p) | (pos2 == p), 1.0, 0.0).astype(BF16)
            return carry

        lax.fori_loop(0, npos // MOE_ROWS, onehot_block, 0)
        xg_scr[:npos, :] = jnp.dot(yw_scr[:npos, :n], h_ref[...].reshape(n, d),
                                   preferred_element_type=F32).astype(BF16)
        xg_scr[npos:, :] = jnp.zeros((MOE_ROWS, d), BF16)
        z_t = jnp.concatenate([pos1, pos2, w1, w2, jnp.zeros((LANES - 4, n), F32)], axis=0).T
        p1c, p2c, w1c, w2c = (z_t[:, i:i + 1] for i in range(4))
        for b in range(npos // LANES):
            p = (b * LANES + _iota2((n, LANES), 1)).astype(F32)
            gtw_scr[:, b * LANES:(b + 1) * LANES] = (jnp.where(p1c == p, w1c, 0.0)
                                                     + jnp.where(p2c == p, w2c, 0.0)).astype(BF16)
        yw_scr[...] = jnp.zeros_like(yw_scr)

    slot = step % n_slots
    for c in weight_copies(step):
        c.wait()
    es = range(per_step)
    firsts = [seg_scr[e * len(es) + j] for j in es]
    lasts = [seg_scr[N_EXPERTS + e * len(es) + j] for j in es]
    starts = [(f // BF16_ROWS) * BF16_ROWS for f in firsts]
    windows = [(lasts[j] - starts[j] + MOE_ROWS - 1) // MOE_ROWS for j in es]

    def window_rows(r0):
        return pl.ds(pl.multiple_of(r0, BF16_ROWS), MOE_ROWS)

    def store_rows(j, r0, y):
        p = r0 + _iota2((MOE_ROWS, 1), 0)
        rows = window_rows(r0)
        yw_scr[rows, :] = jnp.where((p >= firsts[j]) & (p < lasts[j]), y, yw_scr[rows, :])

    gus = [jnp.dot(xg_scr[window_rows(starts[j]), :], wgu_buf[slot, j], preferred_element_type=F32) for j in es]
    acts = [(_silu(gu[:, :D_EXPERT]) * gu[:, D_EXPERT:]).astype(BF16) for gu in gus]
    ys = [jnp.dot(acts[j], wd_buf[slot, j], preferred_element_type=F32).astype(BF16) for j in es]
    for j in es:
        store_rows(j, starts[j], ys[j])

    most = windows[0]
    for j in es[1:]:
        most = jnp.maximum(most, windows[j])

    @pl.when(most > 1)
    def _():
        for j in es:
            def more_windows(w, carry, j=j):
                r0 = starts[j] + w * MOE_ROWS
                gu = jnp.dot(xg_scr[window_rows(r0), :], wgu_buf[slot, j], preferred_element_type=F32)
                act = (_silu(gu[:, :D_EXPERT]) * gu[:, D_EXPERT:]).astype(BF16)
                store_rows(j, r0, jnp.dot(act, wd_buf[slot, j], preferred_element_type=F32).astype(BF16))
                return carry

            lax.fori_loop(1, windows[j], more_windows, 0)

    @pl.when(e == pl.num_programs(2) - 1)
    def _():
        moe = jnp.dot(gtw_scr[...], yw_scr[:npos, :], preferred_element_type=F32)
        x2 = x1_ref[...] + gt_ref[...] * moe.reshape(bb, tt, d)
        ms = jnp.mean(x2 * x2, axis=-1, keepdims=True)
        y_ref[...] = x2 * lax.rsqrt(ms + EPS) * fw_ref[...]


def _moe(h, wt, x1, gt, fw, w_gu, w_down, bb, tt):
    b, t, d = x1.shape
    es = MOE_EXPERTS_PER_STEP
    grid = (b // bb, t // tt, N_EXPERTS // es)
    npos = 2 * bb * tt

    def tok_spec(w):
        return pl.BlockSpec((bb, tt, w), lambda i, j, e: (i, j, 0))

    return pl.pallas_call(
        _moe_kernel,
        grid=grid,
        in_specs=[tok_spec(d), pl.BlockSpec((bb, tt, LANES), lambda i, j, e: (i, j, 0), pipeline_mode=pl.Buffered(1)),
                  pl.BlockSpec((bb, tt, d), lambda i, j, e: (i, j, 0), pipeline_mode=pl.Buffered(1)),
                  _mod_spec(gt, bb, tt),
                  pl.BlockSpec((1, 1, d), lambda i, j, e: (0, 0, 0)),
                  pl.BlockSpec(memory_space=pl.ANY), pl.BlockSpec(memory_space=pl.ANY)],
        out_specs=tok_spec(d),
        out_shape=jax.ShapeDtypeStruct((b, t, d), F32),
        scratch_shapes=[pltpu.VMEM((npos + MOE_ROWS, d), BF16), pltpu.VMEM((bb * tt, npos), BF16),
                        pltpu.VMEM((npos + MOE_ROWS, d), BF16), pltpu.SMEM((2 * N_EXPERTS,), jnp.int32),
                        pltpu.VMEM((MOE_WEIGHT_SLOTS, es, d, 2 * D_EXPERT), BF16),
                        pltpu.VMEM((MOE_WEIGHT_SLOTS, es, D_EXPERT, d), BF16),
                        pltpu.SemaphoreType.DMA((MOE_WEIGHT_SLOTS, 2))],
        compiler_params=pltpu.CompilerParams(dimension_semantics=("arbitrary",) * 3,
                                             vmem_limit_bytes=MOE_VMEM_LIMIT),
        name="moe",
    )(h, wt, x1, gt, fw, w_gu, w_down)


ROW_Z = COL_Z
ROW_R = COL_RKV
ROW_LORA = COL_LORA
ROW_BETA = COL_BA
ROW_A = COL_BA + N_HEADS


def _inproj_t_kernel(x_ref, sc_ref, sh_ref, nw_ref, w_ref, o_ref):
    t, b, d = x_ref.shape
    h = _modulated_norm(x_ref[...], nw_ref[...], sc_ref[...], sh_ref[...]).reshape(t * b, d)
    o_ref[...] = _mm_nt(w_ref[...], h)


def _inproj_t(x, sc, sh, nw, w_rows):
    t, b, d = x.shape
    nr = w_rows.shape[0]

    def full(shape):
        return pl.BlockSpec(shape, lambda i: (0,) * len(shape))

    return pl.pallas_call(
        _inproj_t_kernel,
        grid=(1,),
        in_specs=[full((t, b, d)), full((1, b, d)), full((1, b, d)), full((1, 1, d)), full((nr, d))],
        out_specs=full((nr, t * b)),
        out_shape=jax.ShapeDtypeStruct((nr, t * b), F32),
        compiler_params=pltpu.CompilerParams(dimension_semantics=("arbitrary",), vmem_limit_bytes=VMEM_LIMIT),
        name="inproj_t",
    )(x, sc, sh, nw, w_rows)


def _store_state_rows(s_scr, sout_ref):
    for j in range(HEAD_DIM // 2):
        pair = jnp.concatenate([s_scr[2 * j], s_scr[2 * j + 1]], axis=0)
        sout_ref[:, j * LANES:(j + 1) * LANES] = pair.T


def _gdn_s_kernel(q_ref, k_ref, v_ref, z_ref, b_ref, a_ref, cq_ref, ck_ref, cv_ref, wq_ref, wk_ref, wv_ref,
                  alog_ref, dtb_ref, nw_ref, s0_ref, o_ref, sout_ref, s_scr, kq_scr):
    nb = s0_ref.shape[-1]
    nt = q_ref.shape[1] // nb
    h = pl.program_id(0)

    def tok(ref, t):
        return ref[:, t * nb:(t + 1) * nb]

    def conv(u_ref, c_ref, w_ref):
        full = [c_ref[i] for i in range(CONV_W - 1)] + [tok(u_ref, t) for t in range(nt)]
        taps = [w_ref[:, i:i + 1] for i in range(CONV_W)]
        outs = []
        for t in range(nt):
            y = full[t + CONV_W - 1] * taps[CONV_W - 1]
            for i in range(CONV_W - 1):
                y = y + full[t + i] * taps[i]
            outs.append(_silu(y))
        return outs

    qs = conv(q_ref, cq_ref, wq_ref)
    ks = conv(k_ref, ck_ref, wk_ref)
    vs = conv(v_ref, cv_ref, wv_ref)
    neg_rate = -jnp.exp(alog_ref[pl.ds(h, 1), :])
    dtb = dtb_ref[pl.ds(h, 1), :]
    beta_in = b_ref[pl.ds(h, 1), :]
    a_in = a_ref[pl.ds(h, 1), :]
    s_scr[...] = s0_ref[0]
    zero = jnp.zeros((HEAD_DIM, nb), F32)
    for t in range(nt):
        q = qs[t] * lax.rsqrt(jnp.sum(qs[t] * qs[t], axis=0, keepdims=True) + EPS) * (HEAD_DIM ** -0.5)
        k = ks[t] * lax.rsqrt(jnp.sum(ks[t] * ks[t], axis=0, keepdims=True) + EPS)
        beta = _sigmoid(beta_in[:, t * nb:(t + 1) * nb])
        decay = jnp.exp(neg_rate * _softplus(a_in[:, t * nb:(t + 1) * nb] + dtb))
        kq_scr[0] = k
        kq_scr[1] = q

        def decay_and_project(i, acc):
            row = s_scr[i] * decay
            s_scr[i] = row
            return acc + kq_scr[0, pl.ds(i, 1), :] * row

        k_s = lax.fori_loop(0, HEAD_DIM, decay_and_project, zero, unroll=8)
        u = beta * (vs[t] - k_s)

        def update_and_read(i, acc):
            row = s_scr[i] + kq_scr[0, pl.ds(i, 1), :] * u
            s_scr[i] = row
            return acc + kq_scr[1, pl.ds(i, 1), :] * row

        o = lax.fori_loop(0, HEAD_DIM, update_and_read, zero, unroll=8)
        o = o * lax.rsqrt(jnp.mean(o * o, axis=0, keepdims=True) + EPS) * nw_ref[...]
        o_ref[:, t * nb:(t + 1) * nb] = o * _silu(tok(z_ref, t))
    _store_state_rows(s_scr, sout_ref)


def _gdn_s(cols_t, conv_t, s_t, cw_t, alog_b, dtb_b, nw_b):
    ntok = cols_t.shape[1]
    nb = s_t.shape[-1]

    def head_rows(base):
        return pl.BlockSpec((HEAD_DIM, ntok), lambda h: (base // HEAD_DIM + h, 0))

    def conv_rows(base):
        return pl.BlockSpec((CONV_W - 1, HEAD_DIM, nb), lambda h: (0, base // HEAD_DIM + h, 0))

    def tap_rows(base):
        return pl.BlockSpec((HEAD_DIM, CONV_W), lambda h: (base // HEAD_DIM + h, 0))

    def full2(shape):
        return pl.BlockSpec(shape, lambda h: (0, 0))

    return pl.pallas_call(
        _gdn_s_kernel,
        grid=(N_HEADS,),
        in_specs=[head_rows(0), head_rows(D_MIX_HALF), head_rows(2 * D_MIX_HALF), head_rows(ROW_Z),
                  pl.BlockSpec((N_HEADS, ntok), lambda h: (ROW_BETA // N_HEADS, 0)),
                  pl.BlockSpec((N_HEADS, ntok), lambda h: (ROW_A // N_HEADS, 0)),
                  conv_rows(0), conv_rows(D_MIX_HALF), conv_rows(2 * D_MIX_HALF),
                  tap_rows(0), tap_rows(D_MIX_HALF), tap_rows(2 * D_MIX_HALF),
                  full2((N_HEADS, nb)), full2((N_HEADS, nb)), full2((HEAD_DIM, nb)),
                  pl.BlockSpec((1, HEAD_DIM, HEAD_DIM, nb), lambda h: (h, 0, 0, 0))],
        out_specs=[pl.BlockSpec((HEAD_DIM, ntok), lambda h: (h, 0)),
                   pl.BlockSpec((nb, HEAD_DIM * HEAD_DIM), lambda h: (0, h))],
        out_shape=[jax.ShapeDtypeStruct((D_MIX_HALF, ntok), F32),
                   jax.ShapeDtypeStruct((nb, N_HEADS * HEAD_DIM * HEAD_DIM), F32)],
        scratch_shapes=[pltpu.VMEM((HEAD_DIM, HEAD_DIM, nb), F32), pltpu.VMEM((2, HEAD_DIM, nb), F32)],
        compiler_params=pltpu.CompilerParams(dimension_semantics=("arbitrary",), vmem_limit_bytes=VMEM_LIMIT),
        name="gdn_s",
    )(cols_t, cols_t, cols_t, cols_t, cols_t, cols_t, conv_t, conv_t, conv_t, cw_t, cw_t, cw_t,
      alog_b, dtb_b, nw_b, s_t)


def _rwkv_s_kernel(r_ref, k_ref, v_ref, l_ref, pr_ref, pk_ref, pv_ref, pl_ref, mr_ref, mk_ref, mv_ref, ml_ref,
                   vec_ref, wup_ref, aup_ref, gup_ref, s0_ref, o_ref, sout_ref, s_scr, y_scr, v_scr):
    nb = s0_ref.shape[-1]
    nt = r_ref.shape[1] // nb

    def mixed(u_ref, p_ref, m_ref):
        cur = [u_ref[:, t * nb:(t + 1) * nb] for t in range(nt)]
        prev = [p_ref[...]] + cur[:-1]
        mu = m_ref[...]
        return [c + (p - c) * mu for c, p in zip(cur, prev)]

    rs = mixed(r_ref, pr_ref, mr_ref)
    ks = mixed(k_ref, pk_ref, mk_ref)
    vs = mixed(v_ref, pv_ref, mv_ref)
    ls = mixed(l_ref, pl_ref, ml_ref)
    w0, a0, k_k, k_a, r_k, ln_w, ln_b = (vec_ref[:, i:i + 1] for i in range(7))
    s_scr[...] = s0_ref[0]
    for t in range(nt):
        dw = ls[t][:LORA_W]
        da = ls[t][LORA_W:LORA_W + LORA_A]
        dg = ls[t][LORA_W + LORA_A:]
        w_log = -_softplus(-(w0 + _mm(wup_ref[...], jnp.tanh(dw)))) - 0.5
        w = jnp.exp(-jnp.exp(w_log))
        a = _sigmoid(a0 + _mm(aup_ref[...], da))
        gate = _mm(gup_ref[...], _sigmoid(dg))
        r, v = rs[t], vs[t]
        kk = ks[t] * k_k
        kk = kk * lax.rsqrt(jnp.sum(kk * kk, axis=0, keepdims=True) + EPS)
        k = ks[t] * (1.0 + (a - 1.0) * k_a)
        neg_kk = -kk
        kk_a = kk * a
        v_scr[...] = v

        def state_row(i, carry):
            row = s_scr[i]
            sa = jnp.sum(row * neg_kk, axis=0, keepdims=True)
            row = row * w + sa * kk_a + v_scr[pl.ds(i, 1), :] * k
            s_scr[i] = row
            y_scr[pl.ds(i, 1), :] = jnp.sum(row * r, axis=0, keepdims=True)
            return carry

        lax.fori_loop(0, HEAD_DIM, state_row, 0, unroll=4)
        y = y_scr[...]
        dy = y - jnp.mean(y, axis=0, keepdims=True)
        y = dy * lax.rsqrt(jnp.mean(dy * dy, axis=0, keepdims=True) + GN_EPS) * ln_w + ln_b
        bonus = jnp.sum(r * k * r_k, axis=0, keepdims=True) * v
        o_ref[:, t * nb:(t + 1) * nb] = (y + bonus) * gate
    _store_state_rows(s_scr, sout_ref)


def _rwkv_s(cols_t, shift_t, s_t, mu_b, vecs_t, w_up_t, a_up_t, g_up_t):
    ntok = cols_t.shape[1]
    nb = s_t.shape[-1]

    def head_rows(width, base):
        return pl.BlockSpec((HEAD_DIM, width), lambda h: (base // HEAD_DIM + h, 0))

    def lora_rows(width, base):
        return pl.BlockSpec((LORA_COLS, width), lambda h: (base // LORA_COLS, 0))

    return pl.pallas_call(
        _rwkv_s_kernel,
        grid=(N_HEADS,),
        in_specs=[head_rows(ntok, ROW_R), head_rows(ntok, ROW_R + D_MIX_HALF), head_rows(ntok, ROW_R + 2 * D_MIX_HALF),
                  lora_rows(ntok, ROW_LORA),
                  head_rows(nb, 0), head_rows(nb, D_MIX_HALF), head_rows(nb, 2 * D_MIX_HALF), lora_rows(nb, RWKV_RKV),
                  head_rows(nb, 0), head_rows(nb, D_MIX_HALF), head_rows(nb, 2 * D_MIX_HALF), lora_rows(nb, RWKV_RKV),
                  head_rows(SUBLANES, 0), head_rows(LORA_W, 0), head_rows(LORA_A, 0), head_rows(LORA_G, 0),
                  pl.BlockSpec((1, HEAD_DIM, HEAD_DIM, nb), lambda h: (h, 0, 0, 0))],
        out_specs=[pl.BlockSpec((HEAD_DIM, ntok), lambda h: (h, 0)),
                   pl.BlockSpec((nb, HEAD_DIM * HEAD_DIM), lambda h: (0, h))],
        out_shape=[jax.ShapeDtypeStruct((D_MIX_HALF, ntok), F32),
                   jax.ShapeDtypeStruct((nb, N_HEADS * HEAD_DIM * HEAD_DIM), F32)],
        scratch_shapes=[pltpu.VMEM((HEAD_DIM, HEAD_DIM, nb), F32), pltpu.VMEM((HEAD_DIM, nb), F32),
                        pltpu.VMEM((HEAD_DIM, nb), F32)],
        compiler_params=pltpu.CompilerParams(dimension_semantics=("arbitrary",), vmem_limit_bytes=VMEM_LIMIT),
        name="rwkv_s",
    )(cols_t, cols_t, cols_t, cols_t, shift_t, shift_t, shift_t, shift_t, mu_b, mu_b, mu_b, mu_b,
      vecs_t, w_up_t, a_up_t, g_up_t, s_t)


def _layer(x, mod, conv0, s_gdn0, shift0, s_rwkv0, p, *, bb, tt, moe_tile, bsz, chunk):
    sh_m, sc_m, gt_m, sh_f, sc_f, gt_f = mod
    cols = _inproj(x, sc_m, sh_m, p['norm_mix_w'], p['w_cols'], bb, tt)
    o_a, s_gdn = _gdn(cols, conv0, s_gdn0, p['gdn_conv_w'], p['alog_c'], p['dtb_c'], p['gdn_nw_rep'], bsz, chunk)
    o_b, s_rwkv = _rwkv(cols, shift0, s_rwkv0, p['rwkv_mu'], p['rwkv_vecs'], p['rwkv_w_up'], p['rwkv_a_up'],
                        p['rwkv_g_up'], bsz, chunk)
    x1, h2, wt = _outproj(o_a, o_b, x, gt_m, sc_f, sh_f, p['norm_ffn_w'], p['w_out_a'], p['w_out_b'],
                          p['w_router'], p['b_router'], *moe_tile)
    y = _moe(h2, wt, x1, gt_f, p['final_norm_w'], p['w_gu'], p['w_down'], *moe_tile)
    conv_new = cols[:, -(CONV_W - 1):, :GDN_QKV]
    shift_new = cols[:, -1, COL_RKV:COL_BA]
    return y, conv_new, s_gdn, shift_new, s_rwkv


def _sample_layer(x, mod, conv0, s_gdn0, shift0, s_rwkv0, p, ps):
    b, t, d = x.shape
    assert t >= CONV_W - 1
    sh_m, sc_m, gt_m, sh_f, sc_f, gt_f = mod
    x_tm = jnp.transpose(x, (1, 0, 2))
    cols_t = _inproj_t(x_tm, sc_m, sh_m, p['norm_mix_w'], p['w_cols'])
    seq_last = (1, 2, 3, 0)
    o_a, s_gdn = _gdn_s(cols_t, jnp.transpose(conv0, (1, 2, 0)), jnp.transpose(s_gdn0, seq_last),
                        ps['gdn_conv_w_t'], ps['alog_b'], ps['dtb_b'], ps['gdn_nw_b'])
    o_b, s_rwkv = _rwkv_s(cols_t, shift0.T, jnp.transpose(s_rwkv0, seq_last), ps['rwkv_mu_b'], ps['rwkv_vecs_t'],
                          ps['w_up_t'], ps['a_up_t'], ps['g_up_t'])
    x1, h2, wt = _outproj(o_a, o_b, x_tm, gt_m, sc_f, sh_f, p['norm_ffn_w'], p['w_out_a'], p['w_out_b'],
                          p['w_router'], p['b_router'], t, b)
    y = _moe(h2, wt, x1, gt_f, p['final_norm_w'], p['w_gu'], p['w_down'], t, b)
    keep = CONV_W - 1
    conv_new = jnp.transpose(cols_t[:GDN_QKV, (t - keep) * b:].reshape(GDN_QKV, keep, b), (2, 1, 0))
    shift_new = cols_t[ROW_R:ROW_BETA, (t - 1) * b:].T
    state_shape = (b, N_HEADS, HEAD_DIM, HEAD_DIM)
    return (jnp.transpose(y, (1, 0, 2)), conv_new, s_gdn.reshape(state_shape), shift_new,
            s_rwkv.reshape(state_shape))


def kernel(x_prompt, x_sample, state_gdn_conv, state_gdn, state_rwkv_shift, state_rwkv, c_prompt, c_sample, ada_w, ada_b, norm_mix_w, w_in, gdn_conv_w, gdn_a_log, gdn_dt_bias, gdn_norm_w, rwkv_mu, rwkv_w0, rwkv_w_up, rwkv_a0, rwkv_a_up, rwkv_g_up, rwkv_k_k, rwkv_k_a, rwkv_r_k, rwkv_ln_w, rwkv_ln_b, w_out, norm_ffn_w, router_group_w, router_group_b, router_expert_w, router_expert_b, expert_w_gate_up, expert_w_down, final_norm_w):
    depth = ada_w.shape[0]
    assert depth == 1
    l = 0
    b_p, t_p, d = x_prompt.shape
    b_s, t_s, _ = x_sample.shape
    gdn_cols = GDN_QKV + D_MIX_HALF + 2 * N_HEADS

    w = w_in[l]
    beta0 = GDN_QKV + D_MIX_HALF
    w_t = w.T.astype(BF16)
    w_cols = jnp.concatenate([w_t[:beta0], w_t[gdn_cols:], w_t[beta0:gdn_cols],
                              jnp.zeros((LANES - 2 * N_HEADS, d), BF16)], axis=0)
    head_lanes = lambda a, first: jnp.pad(a, (first, LANES - first - N_HEADS)).reshape(1, LANES)
    row = lambda a: a.reshape(1, -1)
    n_route = N_EXPERTS + N_GROUPS
    w_router = jnp.concatenate([router_expert_w[l], router_group_w[l], jnp.zeros((d, LANES - n_route), F32)], axis=1)
    w_router = jnp.concatenate(_split2(w_router), axis=1)
    b_router = jnp.concatenate([router_expert_b[l], router_group_b[l], jnp.zeros((LANES - n_route,), F32)])[None]
    p = {
        'norm_mix_w': norm_mix_w[l].reshape(1, 1, d), 'w_cols': w_cols,
        'gdn_conv_w': gdn_conv_w[l], 'alog_c': head_lanes(gdn_a_log[l], N_HEADS),
        'dtb_c': head_lanes(gdn_dt_bias[l], N_HEADS),
        'gdn_nw_rep': row(jnp.tile(gdn_norm_w[l], N_HEADS)),
        'rwkv_mu': row(rwkv_mu[l]),
        'rwkv_vecs': tuple(row(a) for a in (rwkv_w0[l], rwkv_a0[l], rwkv_k_k[l], rwkv_k_a[l], rwkv_r_k[l],
                                            rwkv_ln_w[l], rwkv_ln_b[l])),
        'rwkv_w_up': rwkv_w_up[l], 'rwkv_a_up': rwkv_a_up[l], 'rwkv_g_up': rwkv_g_up[l],
        'w_out_a': w_out[l][:D_MIX_HALF].astype(BF16), 'w_out_b': w_out[l][D_MIX_HALF:].astype(BF16),
        'norm_ffn_w': norm_ffn_w[l].reshape(1, 1, d), 'w_router': w_router, 'b_router': b_router,
        'w_gu': expert_w_gate_up[l].astype(BF16), 'w_down': expert_w_down[l].astype(BF16),
        'final_norm_w': final_norm_w.reshape(1, 1, d),
    }

    col = lambda a: a.reshape(-1, 1)
    lanes = lambda a: jnp.broadcast_to(col(a), (a.size, b_s))
    ps = {
        'gdn_conv_w_t': gdn_conv_w[l].T, 'alog_b': lanes(gdn_a_log[l]), 'dtb_b': lanes(gdn_dt_bias[l]),
        'gdn_nw_b': lanes(gdn_norm_w[l]), 'rwkv_mu_b': lanes(rwkv_mu[l]),
        'rwkv_vecs_t': jnp.concatenate([col(a) for a in (rwkv_w0[l], rwkv_a0[l], rwkv_k_k[l], rwkv_k_a[l],
                                                         rwkv_r_k[l], rwkv_ln_w[l], rwkv_ln_b[l], rwkv_ln_b[l])],
                                       axis=1),
        'w_up_t': rwkv_w_up[l].T, 'a_up_t': rwkv_a_up[l].T, 'g_up_t': rwkv_g_up[l].T,
    }

    mod = _ada(jnp.concatenate([c_prompt, c_sample], axis=0), ada_w[l], ada_b[l])
    mod_p = tuple(m.reshape(b_p, 1, d) for m in jnp.split(mod[:b_p], 6, axis=-1))
    mod_s = tuple(m.reshape(1, b_s, d) for m in jnp.split(mod[b_p:], 6, axis=-1))

    zc = jnp.zeros((b_p, CONV_W - 1, GDN_QKV), F32)
    zs = jnp.zeros((b_p, N_HEADS, HEAD_DIM, HEAD_DIM), F32)
    zsh = jnp.zeros((b_p, 1, RWKV_COLS), F32)
    y_p, conv_p, sg_p, shift_p, sr_p = _layer(x_prompt, mod_p, zc, zs, zsh, zs, p, bb=1, tt=PROJ_ROWS,
                                              moe_tile=(1, MOE_TILE_ROWS), bsz=SEQS_PER_STEP, chunk=CHUNK)

    y_s, conv_s, sg_s, shift_s, sr_s = _sample_layer(x_sample, mod_s, state_gdn_conv[l], state_gdn[l],
                                                     state_rwkv_shift[l], state_rwkv[l], p, ps)
    return (y_p, y_s, conv_p[None], sg_p[None], shift_p[None], sr_p[None],
            conv_s[None], sg_s[None], shift_s[None], sr_s[None])
```
